```python
import math
import jax, jax.numpy as jnp
from jax import lax
import numpy as np

D_MODEL = 1024
BATCH = 32
SEQ = 256
DEPTH = 4
DEC_BATCH = 2
DEC_SEQ = 4096
PAST_LEN = 256

GRID_W = 64
EPS = 1e-6
ROPE_BASE = 10000.0
BRANCH_W = D_MODEL
M_HEADDIM = 64
M_HEADS = BRANCH_W // M_HEADDIM
M_INNER = M_HEADS * M_HEADDIM
M_GROUPS = 2
M_STATE = 64
M_CONV = 3
M_CHUNK = 128
M_CONV_CH = M_INNER + 2 * M_GROUPS * M_STATE
DA_QK = 64
DA_VD = 2 * DA_QK
DA_HEADS = BRANCH_W // DA_VD
DA_INNER = DA_HEADS * DA_VD
Q_BLOCK = 128
SG_WIDTH = BRANCH_W
SG_GROUPS = 8
SG_GDIM = SG_WIDTH // SG_GROUPS
SG_CHUNK = 128
SPLIT_SIZES = (M_INNER, M_CONV_CH, 2 * M_HEADS,
               DA_INNER, DA_INNER, DA_INNER, DA_INNER,
               SG_WIDTH, SG_WIDTH, SG_WIDTH,
               3 * D_MODEL)
N_IN = sum(SPLIT_SIZES)

kernel_name = "hybrid_diffusion_ssd_diffattn_sgmlp_step"

F32 = jnp.float32


def rmsnorm(x, w):
    xf = x.astype(F32)
    y = xf * lax.rsqrt(jnp.mean(xf * xf, -1, keepdims=True) + EPS)
    return (y * w.astype(F32)).astype(x.dtype)


def layernorm(x, w):
    xf = x.astype(F32)
    xc = xf - jnp.mean(xf, -1, keepdims=True)
    y = xc * lax.rsqrt(jnp.mean(xc * xc, -1, keepdims=True) + EPS)
    return (y * w.astype(F32)).astype(x.dtype)


def axial_rope(n_tokens):
    n_rows = n_tokens // GRID_W
    rows = jnp.repeat(jnp.arange(n_rows, dtype=F32), GRID_W)
    cols = jnp.tile(jnp.arange(GRID_W, dtype=F32), n_rows)
    n_freq = DA_QK // 4
    inv = ROPE_BASE ** (-jnp.arange(n_freq, dtype=F32) / n_freq)
    ang = jnp.concatenate([rows[:, None] * inv, cols[:, None] * inv], -1)
    return jnp.cos(ang), jnp.sin(ang)


def apply_rope(x, cos, sin):
    x1, x2 = jnp.split(x.astype(F32), 2, axis=-1)
    c = cos[:, None, None, :]
    s = sin[:, None, None, :]
    return jnp.concatenate([x1 * c - x2 * s, x1 * s + x2 * c], -1).astype(x.dtype)


def depthwise_conv(x, w, b):
    kern = jnp.transpose(w)[:, None, :].astype(x.dtype)
    y = lax.conv_general_dilated(x, kern, window_strides=(1,), padding=[(M_CONV // 2, M_CONV // 2)],
                                 dimension_numbers=("NWC", "WIO", "NWC"), feature_group_count=x.shape[-1])
    return y + b.astype(x.dtype)


def segsum(a):
    T = a.shape[-1]
    cs = jnp.cumsum(a, -1)
    s = cs[..., :, None] - cs[..., None, :]
    return jnp.where(jnp.tril(jnp.ones((T, T), bool)), s, -jnp.inf)


def ssd_scan(xdt, a, b, c, init_state):
    bt, T, H, P = xdt.shape
    N = b.shape[-1]
    nc = T // M_CHUNK
    xdt = xdt.astype(F32).reshape(bt, nc, M_CHUNK, H, P)
    b = b.astype(F32).reshape(bt, nc, M_CHUNK, H, N)
    c = c.astype(F32).reshape(bt, nc, M_CHUNK, H, N)
    a = a.astype(F32).reshape(bt, nc, M_CHUNK, H).transpose(0, 1, 3, 2)
    a_cs = jnp.cumsum(a, -1)
    decay_in = jnp.exp(segsum(a))
    cb = jnp.einsum("bclhn,bcshn->bchls", c, b)
    y_diag = jnp.einsum("bchls,bcshp->bclhp", cb * decay_in, xdt)
    decay_to_end = jnp.exp(a_cs[..., -1:] - a_cs).transpose(0, 1, 3, 2)
    chunk_states = jnp.einsum("bclhn,bclhp->bchpn", b * decay_to_end[..., None], xdt)
    states = jnp.concatenate([init_state.astype(F32)[:, None], chunk_states], 1)
    chunk_a = jnp.pad(a_cs[..., -1].transpose(0, 2, 1), ((0, 0), (0, 0), (1, 0)))
    decay_chunk = jnp.exp(segsum(chunk_a))
    states = jnp.einsum("bhzc,bchpn->bzhpn", decay_chunk, states)
    decay_from_start = jnp.exp(a_cs).transpose(0, 1, 3, 2)
    y_off = jnp.einsum("bclhn,bchpn->bclhp", c * decay_from_start[..., None], states[:, :-1])
    return (y_diag + y_off).reshape(bt, T, H, P), states[:, -1]


def _orient(t, d):
    return jnp.flip(t, 1) if d == 1 else t


def ssd_mixer(z, xbc, dt_raw, conv_w, conv_b, a_log, dt_bias, d_skip, norm_w, init_states):
    bt, T, _ = xbc.shape
    xbc = jax.nn.silu(depthwise_conv(xbc, conv_w, conv_b))
    xs, bs, cs = jnp.split(xbc, [M_INNER, M_INNER + M_GROUPS * M_STATE], -1)
    xh = xs.reshape(bt, T, M_HEADS, M_HEADDIM).astype(F32)
    rep = M_HEADS // M_GROUPS
    bh = jnp.repeat(bs.reshape(bt, T, M_GROUPS, M_STATE), rep, axis=2)
    ch = jnp.repeat(cs.reshape(bt, T, M_GROUPS, M_STATE), rep, axis=2)
    dt = jax.nn.softplus(dt_raw.reshape(bt, T, 2, M_HEADS).astype(F32) + dt_bias.astype(F32))
    A = -jnp.exp(a_log.astype(F32))
    ys, finals = [], []
    for d in range(2):
        dt_d = dt[:, :, d]
        y_d, fin = ssd_scan(_orient(xh * dt_d[..., None], d), _orient(dt_d * A[d], d),
                            _orient(bh, d), _orient(ch, d), init_states[:, d])
        ys.append(_orient(y_d, d) + d_skip[d].astype(F32)[:, None] * xh)
        finals.append(fin)
    y = (ys[0] + ys[1]).reshape(bt, T, M_INNER)
    y = rmsnorm(y * jax.nn.silu(z.astype(F32)), norm_w).astype(z.dtype)
    return y, jnp.stack(finals, 1).astype(z.dtype)


def diff_attention(q, k, v, lam):
    bt, Tq = q.shape[:2]
    nb = Tq // Q_BLOCK
    qb = q.reshape(bt, nb, Q_BLOCK, DA_HEADS, 2, DA_QK).transpose(1, 0, 2, 3, 4, 5)
    scale = DA_QK ** -0.5
    vf = v.astype(F32)

    def block(qi):
        s = jnp.einsum("bqhmd,bkhmd->bhmqk", qi, k).astype(F32) * scale
        p = jax.nn.softmax(s, axis=-1)
        w = p[:, :, 0] - lam * p[:, :, 1]
        return jnp.einsum("bhqk,bkhd->bqhd", w, vf)

    o = lax.map(block, qb)
    return o.transpose(1, 0, 2, 3, 4).reshape(bt, Tq, DA_HEADS, DA_VD).astype(q.dtype)


def chunk_sgmlp(u, v, vnorm_w, ws, bs):
    bt, T, _ = v.shape
    u = jax.nn.gelu(u)
    v = layernorm(jax.nn.gelu(v), vnorm_w)
    vc = v.reshape(bt, T // SG_CHUNK, SG_CHUNK, SG_GROUPS, SG_GDIM)
    mixed = jnp.einsum("gts,bcsge->bctge", ws, vc) + jnp.transpose(bs)[None, None, :, :, None]
    return u * mixed.reshape(bt, T, SG_WIDTH)


def trunk_layer(x, shift, scale, gate, p, layer_idx, rope, ctx_k, ctx_v, ssm_init):
    (pre_w, post_w, w_in, conv_w, conv_b, a_log, dt_bias, d_skip, m_norm_w,
     lam_vecs, head_norm_w, vnorm_w, ws, bs, w_branch, w_out) = p
    bt, T, _ = x.shape
    latent = ctx_k is not None
    h = rmsnorm(x, pre_w) * (1 + scale) + shift
    proj = h @ w_in
    idx = [int(i) for i in np.cumsum(SPLIT_SIZES)[:-1]]
    z, xbc, dt_raw, q, k, v, g_b, u, sv, g_c, mg = jnp.split(proj, idx, -1)

    if ssm_init is None:
        ssm_init = jnp.zeros((bt, 2, M_HEADS, M_HEADDIM, M_STATE), x.dtype)
    y_a, ssm_final = ssd_mixer(z, xbc, dt_raw, conv_w, conv_b, a_log, dt_bias, d_skip, m_norm_w, ssm_init)

    q = q.reshape(bt, T, DA_HEADS, 2, DA_QK)
    k = k.reshape(bt, T, DA_HEADS, 2, DA_QK)
    v = v.reshape(bt, T, DA_HEADS, DA_VD)
    if latent:
        cos, sin = rope
        L = ctx_k.shape[1]
        k_all = jnp.concatenate([ctx_k.reshape(bt, L, DA_HEADS, 2, DA_QK).astype(k.dtype), apply_rope(k, cos, sin)], 1)
        v_all = jnp.concatenate([ctx_v.astype(v.dtype), v], 1)
        q = apply_rope(q, cos, sin)
    else:
        k_all, v_all = k, v
    lam_init = 0.8 - 0.6 * math.exp(-0.3 * layer_idx)
    lv = lam_vecs.astype(F32)
    lam = jnp.exp(jnp.sum(lv[0] * lv[1])) - jnp.exp(jnp.sum(lv[2] * lv[3])) + lam_init
    o = diff_attention(q, k_all, v_all, lam)
    o = rmsnorm(o, head_norm_w) * (1 - lam_init)
    y_b = o.reshape(bt, T, DA_INNER) * jax.nn.silu(g_b)

    y_c = chunk_sgmlp(u, sv, vnorm_w, ws, bs) * jax.nn.silu(g_c)

    ga, gb, gc = jnp.split(jax.nn.sigmoid(mg), 3, -1)
    merged = ga * (y_a @ w_branch[0]) + gb * (y_b @ w_branch[1]) + gc * (y_c @ w_branch[2])
    out = rmsnorm(merged @ w_out, post_w)
    x = x + gate * out
    return x, k.reshape(bt, T, DA_HEADS, 2 * DA_QK), v, ssm_final


def setup_inputs(seed: int = 0) -> dict:
    key = jax.random.key(seed)
    ks = jax.random.split(key, 32)

    def nrm(k, shape, s):
        return jax.random.normal(k, shape, F32) * s

    dt0 = jnp.exp(jax.random.uniform(ks[15], (DEPTH, 2, M_HEADS), F32, math.log(1e-3), math.log(1e-1)))
    return {
        "x_prompt": nrm(ks[0], (BATCH, SEQ, D_MODEL), 1.0),
        "x_sample": nrm(ks[1], (DEC_BATCH, DEC_SEQ, D_MODEL), 1.0),
        "cache_k": nrm(ks[2], (DEC_BATCH, DEPTH, PAST_LEN, DA_HEADS, 2 * DA_QK), 1.0),
        "cache_v": nrm(ks[3], (DEC_BATCH, DEPTH, PAST_LEN, DA_HEADS, DA_VD), 1.0),
        "state_ssm": nrm(ks[4], (DEC_BATCH, DEPTH, 2, M_HEADS, M_HEADDIM, M_STATE), 0.5),
        "c": nrm(ks[5], (DEC_BATCH, D_MODEL), 1.0),
        "c_ctx": nrm(ks[6], (D_MODEL,), 1.0),
        "pre_norm_w": 1.0 + nrm(ks[7], (DEPTH, D_MODEL), 0.02),
        "post_norm_w": 1.0 + nrm(ks[8], (DEPTH, D_MODEL), 0.02),
        "w_mod": nrm(ks[9], (DEPTH, D_MODEL, 3 * D_MODEL), 0.5 * D_MODEL ** -0.5),
        "b_mod": nrm(ks[10], (DEPTH, 3 * D_MODEL), 0.01),
        "w_in": nrm(ks[11], (DEPTH, D_MODEL, N_IN), D_MODEL ** -0.5),
        "m_conv_w": nrm(ks[12], (DEPTH, M_CONV_CH, M_CONV), M_CONV ** -0.5),
        "m_conv_b": nrm(ks[13], (DEPTH, M_CONV_CH), 0.01),
        "m_A_log": jnp.log(jax.random.uniform(ks[14], (DEPTH, 2, M_HEADS), F32, 1.0, 16.0)),
        "m_dt_bias": dt0 + jnp.log(-jnp.expm1(-dt0)),
        "m_D": 1.0 + nrm(ks[16], (DEPTH, 2, M_HEADS), 0.1),
        "m_norm_w": 1.0 + nrm(ks[17], (DEPTH, M_INNER), 0.02),
        "da_lambda": nrm(ks[18], (DEPTH, 4, DA_QK), 0.1),
        "da_head_norm_w": 1.0 + nrm(ks[19], (DEPTH, DA_VD), 0.02),
        "sg_vnorm_w": 1.0 + nrm(ks[20], (DEPTH, SG_WIDTH), 0.02),
        "sg_spatial_w": nrm(ks[21], (DEPTH, SG_GROUPS, SG_CHUNK, SG_CHUNK), SG_CHUNK ** -0.5),
        "sg_spatial_b": nrm(ks[22], (DEPTH, SG_GROUPS, SG_CHUNK), 0.01),
        "w_branch": nrm(ks[23], (DEPTH, 3, BRANCH_W, D_MODEL), BRANCH_W ** -0.5),
        "w_out": nrm(ks[24], (DEPTH, D_MODEL, D_MODEL), D_MODEL ** -0.5),
    }


def reference(x_prompt, x_sample, cache_k, cache_v, state_ssm, c, c_ctx, pre_norm_w, post_norm_w,
              w_mod, b_mod, w_in, m_conv_w, m_conv_b, m_A_log, m_dt_bias, m_D, m_norm_w,
              da_lambda, da_head_norm_w, sg_vnorm_w, sg_spatial_w, sg_spatial_b, w_branch, w_out):
    rope = axial_rope(x_sample.shape[1])
    silu_ctx = jax.nn.silu(c_ctx)
    silu_lat = jax.nn.silu(c)
    xp, xs = x_prompt, x_sample
    ks_out, vs_out, ss_out = [], [], []
    for l in range(DEPTH):
        p = (pre_norm_w[l], post_norm_w[l], w_in[l], m_conv_w[l], m_conv_b[l], m_A_log[l], m_dt_bias[l],
             m_D[l], m_norm_w[l], da_lambda[l], da_head_norm_w[l], sg_vnorm_w[l], sg_spatial_w[l],
             sg_spatial_b[l], w_branch[l], w_out[l])
        sh, sc, gt = jnp.split(silu_ctx @ w_mod[l] + b_mod[l], 3, -1)
        xp, k_l, v_l, s_l = trunk_layer(xp, sh, sc, gt, p, l, None, None, None, None)
        ks_out.append(k_l)
        vs_out.append(v_l)
        ss_out.append(s_l)
        sh, sc, gt = jnp.split(silu_lat @ w_mod[l] + b_mod[l], 3, -1)
        xs = trunk_layer(xs, sh[:, None], sc[:, None], gt[:, None], p, l, rope,
                         cache_k[:, l], cache_v[:, l], state_ssm[:, l])[0]
    new_cache_k = jnp.stack(ks_out, 1)
    new_cache_v = jnp.stack(vs_out, 1)
    new_state_ssm = jnp.stack(ss_out, 1)
    return (xp, xs, new_cache_k, new_cache_v, new_state_ssm)
```

```python
import functools
import math

import jax
import jax.numpy as jnp
from jax import lax
from jax.experimental import pallas as pl
from jax.experimental.pallas import tpu as pltpu

F32 = jnp.float32
BF16 = jnp.bfloat16

D_MODEL = 1024
EPS = 1e-6
GRID_W = 64
ROPE_BASE = 10000.0
M_HEADS = 16
M_HEADDIM = 64
M_STATE = 64
M_GROUPS = 2
M_CHUNK = 128
M_PAIRS = M_HEADS // 2
DA_HEADS = 8
DA_QK = 64
DA_VD = 128
SG_GROUPS = 8
SG_CHUNK = 128

SEG_Z, SEG_XS, SEG_Q, SEG_K, SEG_V, SEG_GB, SEG_U, SEG_SV, SEG_GC, SEG_MGA, SEG_MGB, SEG_MGC = range(12)
N_SEG = 12
MISC_W = 384

VMEM_LIMIT = 56 * 1024 * 1024
NEG_BIG = -1e30


def _silu(x):
    return x * (1.0 / (1.0 + jnp.exp(-x)))


def _sigmoid(x):
    return 1.0 / (1.0 + jnp.exp(-x))


def _gelu_tanh(x):
    return 0.5 * x * (1.0 + jnp.tanh(math.sqrt(2.0 / math.pi) * (x + 0.044715 * (x * x * x))))


def _cparams(sem):
    return pltpu.CompilerParams(dimension_semantics=sem, vmem_limit_bytes=VMEM_LIMIT)


def _mod_kernel(c_ref, w_ref, b_ref, o_ref):
    c = c_ref[...]
    s = _silu(c).astype(BF16)
    o_ref[...] = jnp.dot(s, w_ref[...].astype(BF16), preferred_element_type=F32) + b_ref[...]


def _modulation(cvec, w_mod, b_mod):
    depth = w_mod.shape[0]
    nt = 3
    return pl.pallas_call(
        _mod_kernel,
        out_shape=jax.ShapeDtypeStruct((depth, 8, 3 * D_MODEL), F32),
        grid=(depth, nt),
        in_specs=[
            pl.BlockSpec((8, D_MODEL), lambda l, j: (0, 0)),
            pl.BlockSpec((None, D_MODEL, D_MODEL), lambda l, j: (l, 0, j)),
            pl.BlockSpec((None, 1, D_MODEL), lambda l, j: (l, 0, j)),
        ],
        out_specs=pl.BlockSpec((None, 8, D_MODEL), lambda l, j: (l, 0, j)),
        compiler_params=_cparams(("arbitrary", "arbitrary")),
        name="modulation",
    )(cvec, w_mod, b_mod.reshape(depth, 1, 3 * D_MODEL))


def _inproj_kernel(*refs, rope, emit_kv):
    it = iter(refs)
    x_ref, mod_ref, prew_ref, w_ref, wm_ref = next(it), next(it), next(it), next(it), next(it)
    if rope:
        cos_ref, sin_ref = next(it), next(it)
    main_ref, misc_ref = next(it), next(it)
    if emit_kv:
        kf_ref, vf_ref = next(it), next(it)
    h_scr = next(it)

    j = pl.program_id(1)

    @pl.when(j == 0)
    def _():
        x = x_ref[...]
        ms = jnp.mean(x * x, axis=-1, keepdims=True)
        y = x * lax.rsqrt(ms + EPS) * prew_ref[...]
        h = y * (1.0 + mod_ref[1:2, :]) + mod_ref[0:1, :]
        hb = h.astype(BF16)
        h_scr[...] = hb
        misc_ref[...] = jnp.dot(hb, wm_ref[...], preferred_element_type=F32)

    acc = jnp.dot(h_scr[...], w_ref[...], preferred_element_type=F32)

    if rope:
        is_rope = jnp.logical_or(j == SEG_Q, j == SEG_K)

        @pl.when(is_rope)
        def _():
            tm, n = acc.shape
            cos = jnp.concatenate([cos_ref[...]] * (n // 128), axis=1)
            sin = jnp.concatenate([sin_ref[...]] * (n // 128), axis=1)
            lane = lax.broadcasted_iota(jnp.int32, (tm, n), 1)
            first_half = (lane & (DA_QK - 1)) < (DA_QK // 2)
            rot = jnp.where(first_half, pltpu.roll(acc, n - DA_QK // 2, 1), pltpu.roll(acc, DA_QK // 2, 1))
            main_ref[...] = (acc * cos + rot * sin).astype(main_ref.dtype)

        @pl.when(jnp.logical_not(is_rope))
        def _():
            main_ref[...] = acc.astype(main_ref.dtype)
    else:
        main_ref[...] = acc.astype(main_ref.dtype)

    if emit_kv:
        @pl.when(j == SEG_K)
        def _():
            kf_ref[...] = acc

        @pl.when(j == SEG_V)
        def _():
            vf_ref[...] = acc


def _inproj(x, mod, pre_w, w_main, w_misc, rows_per_mod, rope_tabs, emit_kv, tm=512):
    R = x.shape[0]
    ni = R // tm
    rope = rope_tabs is not None
    in_specs = [
        pl.BlockSpec((tm, D_MODEL), lambda i, j: (i, 0)),
        pl.BlockSpec((None, 3, D_MODEL), lambda i, j: ((i * tm) // rows_per_mod, 0, 0)),
        pl.BlockSpec((1, D_MODEL), lambda i, j: (0, 0)),
        pl.BlockSpec((D_MODEL, D_MODEL), lambda i, j: (0, j)),
        pl.BlockSpec((D_MODEL, MISC_W), lambda i, j: (0, 0)),
    ]
    args = [x, mod, pre_w, w_main, w_misc]
    if rope:
        cos, sin = rope_tabs
        nt = cos.shape[0] // tm
        in_specs += [pl.BlockSpec((tm, 128), lambda i, j: (i % nt, 0)),
                     pl.BlockSpec((tm, 128), lambda i, j: (i % nt, 0))]
        args += [cos, sin]
    out_shape = [jax.ShapeDtypeStruct((R, N_SEG * D_MODEL), BF16), jax.ShapeDtypeStruct((R, MISC_W), F32)]
    out_specs = [pl.BlockSpec((tm, D_MODEL), lambda i, j: (i, j)), pl.BlockSpec((tm, MISC_W), lambda i, j: (i, 0))]
    if emit_kv:
        out_shape += [jax.ShapeDtypeStruct((R, D_MODEL), F32)] * 2
        out_specs += [pl.BlockSpec((tm, D_MODEL), lambda i, j: (i, 0))] * 2
    return pl.pallas_call(
        functools.partial(_inproj_kernel, rope=rope, emit_kv=emit_kv),
        out_shape=out_shape,
        grid=(ni, N_SEG),
        in_specs=in_specs,
        out_specs=out_specs,
        scratch_shapes=[pltpu.VMEM((tm, D_MODEL), BF16)],
        compiler_params=_cparams(("parallel", "arbitrary")),
        name="inproj",
    )(*args)


def _split3(a):
    hi = a.astype(BF16)
    r1 = a - hi.astype(F32)
    mid = r1.astype(BF16)
    lo = (r1 - mid.astype(F32)).astype(BF16)
    return hi, mid, lo


def _ssd_kernel(*refs, nc, has_init):
    it = iter(refs)
    xs_ref, xsp_ref, xsn_ref = next(it), next(it), next(it)
    mi_ref, mip_ref, min_ref = next(it), next(it), next(it)
    z_ref = next(it)
    cwx_ref, cbx_ref, cwm_ref, cbm_ref = next(it), next(it), next(it), next(it)
    dtb_ref, alog_ref, dsum_ref, nw_ref = next(it), next(it), next(it), next(it)
    if has_init:
        init_ref = next(it)
    y_ref, sto_ref = next(it), next(it)
    xc_scr, bc_scr, yf_scr, st_scr = next(it), next(it), next(it), next(it)

    L = M_CHUNK
    j = pl.program_id(1)
    fwd = j < nc
    c = jnp.where(fwd, j, 2 * nc - 1 - j)
    row0 = pl.multiple_of(c * L, L)

    rid = lax.broadcasted_iota(jnp.int32, (L, L), 0)
    cid = lax.broadcasted_iota(jnp.int32, (L, L), 1)

    def conv_silu(x, prow, nrow, w_ref, b_ref):
        n = x.shape[1]
        r = lax.broadcasted_iota(jnp.int32, (L, n), 0)
        xm = jnp.where(r == 0, prow, pltpu.roll(x, 1, 0))
        xp = jnp.where(r == L - 1, nrow, pltpu.roll(x, L - 1, 0))
        y = w_ref[0:1, :] * xm + w_ref[1:2, :] * x + w_ref[2:3, :] * xp + b_ref[...]
        return _silu(y)

    def load_state(d):
        if has_init:
            for i in range(M_PAIRS):
                st_scr[i] = init_ref[d, i]
        else:
            st_scr[...] = jnp.zeros_like(st_scr)

    def chunk(d, x_bf, bc, dt_raw):
        if d == 0:
            tri = (cid <= rid)
        else:
            tri = (cid >= rid)
        tri_bf = jnp.where(tri, 1.0, 0.0).astype(BF16)
        dt = dt_raw + dtb_ref[...]
        dt = jnp.maximum(dt, 0.0) + jnp.log1p(jnp.exp(-jnp.abs(dt)))
        a = dt * (-jnp.exp(alog_ref[...]))
        a_hi, a_mid, a_lo = _split3(a)
        p_col = (jnp.dot(tri_bf, a_hi, preferred_element_type=F32)
                 + jnp.dot(tri_bf, a_mid, preferred_element_type=F32)
                 + jnp.dot(tri_bf, a_lo, preferred_element_type=F32))
        p_row = p_col.T[d * M_HEADS:(d + 1) * M_HEADS, :]
        dt_row = dt.T[d * M_HEADS:(d + 1) * M_HEADS, :]
        tot = p_row[:, L - 1:L] if d == 0 else p_row[:, 0:1]
        w_row = dt_row * jnp.exp(tot - p_row)
        etot = jnp.exp(tot)
        ep_col = jnp.exp(p_col)

        b_all = bc[:, 0:M_GROUPS * M_STATE]
        c_all = bc[:, M_GROUPS * M_STATE:2 * M_GROUPS * M_STATE]
        bt_all = b_all.T
        b_bf = b_all.astype(BF16)

        lane_lo = cid < M_HEADDIM
        lane_lo_s = lax.broadcasted_iota(jnp.int32, (M_STATE, 2 * M_HEADDIM), 1) < M_HEADDIM
        outs = []
        heads_per_group = M_HEADS // M_GROUPS
        g_mats = []
        for g in range(M_GROUPS):
            cg = jnp.where(lane_lo if g == 0 else jnp.logical_not(lane_lo), c_all, 0.0).astype(BF16)
            g_mats.append(lax.dot_general(cg, b_bf, (((1,), (1,)), ((), ())), preferred_element_type=F32))

        zeros_s = jnp.zeros((M_STATE, 2 * M_HEADDIM), BF16)
        for i in range(M_PAIRS):
            g = (2 * i) // heads_per_group
            btg = bt_all[g * M_STATE:(g + 1) * M_STATE, :]
            lhs_rows = []
            bw_rows = []
            for hh in range(2):
                h = 2 * i + hh
                col = d * M_HEADS + h
                pc = jnp.broadcast_to(p_col[:, col:col + 1], (L, L))
                diff = pc - p_row[h:h + 1, :]
                dm = jnp.exp(jnp.where(tri, diff, NEG_BIG))
                m_h = g_mats[g] * dm * dt_row[h:h + 1, :]
                ce_h = c_all * jnp.broadcast_to(ep_col[:, col:col + 1], (L, L))
                lhs_rows.append(jnp.concatenate([m_h, ce_h], axis=1).astype(BF16))
                bw = (btg * w_row[h:h + 1, :]).astype(BF16)
                bw_rows.append(jnp.concatenate([bw, zeros_s], axis=1))
            lhs = jnp.concatenate(lhs_rows + bw_rows, axis=0)
            x_pair = x_bf[:, i * 128:(i + 1) * 128]
            st_pair = st_scr[i]
            st_bf = st_pair.astype(BF16)
            rhs = jnp.concatenate([x_pair] + ([st_bf, zeros_s] if g == 0 else [zeros_s, st_bf]), axis=0)
            res = jnp.dot(lhs, rhs, preferred_element_type=F32)
            y_pair = jnp.where(lane_lo, res[0:L], res[L:2 * L])
            ds = jnp.where(lane_lo_s, res[2 * L:2 * L + M_STATE], res[2 * L + M_STATE:2 * L + 2 * M_STATE])
            e0 = jnp.broadcast_to(etot[2 * i:2 * i + 1, :], (M_STATE, 2 * M_HEADDIM))
            e1 = jnp.broadcast_to(etot[2 * i + 1:2 * i + 2, :], (M_STATE, 2 * M_HEADDIM))
            st_scr[i] = jnp.where(lane_lo_s, e0, e1) * st_pair + ds
            outs.append(y_pair)
        return jnp.concatenate(outs, axis=1)

    @pl.when(j == 0)
    def _():
        load_state(0)

    @pl.when(j == nc)
    def _():
        load_state(1)

    @pl.when(fwd)
    def _():
        x = xs_ref[...].astype(F32)
        prow = jnp.where(c > 0, xsp_ref[...].astype(F32)[15:16, :], 0.0)
        nrow = jnp.where(c < nc - 1, xsn_ref[...].astype(F32)[0:1, :], 0.0)
        xc = conv_silu(x, prow, nrow, cwx_ref, cbx_ref).astype(BF16)
        xc_scr[pl.ds(row0, L), :] = xc
        m = mi_ref[...]
        bcx = m[:, 0:256]
        prow_m = jnp.where(c > 0, mip_ref[7:8, 0:256], 0.0)
        nrow_m = jnp.where(c < nc - 1, min_ref[0:1, 0:256], 0.0)
        bc = conv_silu(bcx, prow_m, nrow_m, cwm_ref, cbm_ref)
        bc_scr[pl.ds(row0, L), :] = bc
        yf_scr[pl.ds(row0, L), :] = chunk(0, xc, bc, m[:, 256:384])

    @pl.when(j == nc - 1)
    def _():
        sto_ref[0] = st_scr[...]

    @pl.when(jnp.logical_not(fwd))
    def _():
        xc = xc_scr[pl.ds(row0, L), :]
        bc = bc_scr[pl.ds(row0, L), :]
        yb = chunk(1, xc, bc, mi_ref[:, 256:384])
        y = yf_scr[pl.ds(row0, L), :] + yb + dsum_ref[...] * xc.astype(F32)
        y = y * _silu(z_ref[...].astype(F32))
        ms = jnp.mean(y * y, axis=-1, keepdims=True)
        y_ref[...] = (y * lax.rsqrt(ms + EPS) * nw_ref[...]).astype(y_ref.dtype)

    @pl.when(j == 2 * nc - 1)
    def _():
        sto_ref[1] = st_scr[...]


def _ssd(main, misc, nseq, T, cwx, cbx, cwm, cbm, dtb, alog, dsum, nw, init):
    R = nseq * T
    L = M_CHUNK
    nc = T // L
    has_init = init is not None

    def cidx(j):
        return jnp.where(j < nc, j, 2 * nc - 1 - j)

    def oidx(j):
        return jnp.where(j < nc, nc - 1, 2 * nc - 1 - j)

    in_specs = [
        pl.BlockSpec((L, D_MODEL), lambda b, j: (b * nc + cidx(j), SEG_XS)),
        pl.BlockSpec((16, D_MODEL), lambda b, j: (jnp.maximum((b * nc + cidx(j)) * (L // 16) - 1, 0), SEG_XS)),
        pl.BlockSpec((16, D_MODEL),
                     lambda b, j: (jnp.minimum((b * nc + cidx(j) + 1) * (L // 16), R // 16 - 1), SEG_XS)),
        pl.BlockSpec((L, MISC_W), lambda b, j: (b * nc + cidx(j), 0)),
        pl.BlockSpec((8, MISC_W), lambda b, j: (jnp.maximum((b * nc + cidx(j)) * (L // 8) - 1, 0), 0)),
        pl.BlockSpec((8, MISC_W), lambda b, j: (jnp.minimum((b * nc + cidx(j) + 1) * (L // 8), R // 8 - 1), 0)),
        pl.BlockSpec((L, D_MODEL), lambda b, j: (b * nc + oidx(j), SEG_Z)),
        pl.BlockSpec((3, D_MODEL), lambda b, j: (0, 0)),
        pl.BlockSpec((1, D_MODEL), lambda b, j: (0, 0)),
        pl.BlockSpec((3, 256), lambda b, j: (0, 0)),
        pl.BlockSpec((1, 256), lambda b, j: (0, 0)),
        pl.BlockSpec((1, 128), lambda b, j: (0, 0)),
        pl.BlockSpec((1, 128), lambda b, j: (0, 0)),
        pl.BlockSpec((1, D_MODEL), lambda b, j: (0, 0)),
        pl.BlockSpec((1, D_MODEL), lambda b, j: (0, 0)),
    ]
    args = [main, main, main, misc, misc, misc, main, cwx, cbx, cwm, cbm, dtb, alog, dsum, nw]
    if has_init:
        in_specs.append(pl.BlockSpec((None, 2, M_PAIRS, M_STATE, 128), lambda b, j: (b, 0, 0, 0, 0)))
        args.append(init)
    return pl.pallas_call(
        functools.partial(_ssd_kernel, nc=nc, has_init=has_init),
        out_shape=[jax.ShapeDtypeStruct((R, D_MODEL), BF16),
                   jax.ShapeDtypeStruct((nseq, 2, M_PAIRS, M_STATE, 128), F32)],
        grid=(nseq, 2 * nc),
        in_specs=in_specs,
        out_specs=[pl.BlockSpec((L, D_MODEL), lambda b, j: (b * nc + oidx(j), 0)),
                   pl.BlockSpec((None, 2, M_PAIRS, M_STATE, 128), lambda b, j: (b, 0, 0, 0, 0))],
        scratch_shapes=[pltpu.VMEM((T, D_MODEL), BF16), pltpu.VMEM((T, 256), F32),
                        pltpu.VMEM((T, D_MODEL), F32), pltpu.VMEM((M_PAIRS, M_STATE, 128), F32)],
        compiler_params=_cparams(("parallel", "arbitrary")),
        name="ssd",
    )(*args)


def _attn_kernel(*refs, hp, has_cache, lam_init):
    it = iter(refs)
    q_ref, k_ref, v_ref, gb_ref = next(it), next(it), next(it), next(it)
    if has_cache:
        kc_ref, vc_ref = next(it), next(it)
    lv_ref, hw_ref = next(it), next(it)
    o_ref = next(it)

    tq = q_ref.shape[0]
    lv = lv_ref[...]
    lam = (jnp.exp(jnp.sum(lv[0:1, :] * lv[1:2, :], axis=-1, keepdims=True))
           - jnp.exp(jnp.sum(lv[2:3, :] * lv[3:4, :], axis=-1, keepdims=True)) + lam_init)
    lane = lax.broadcasted_iota(jnp.int32, (tq, 2 * DA_QK), 1)
    scale = DA_QK ** -0.5
    dn_t = (((1,), (1,)), ((), ()))

    for hh in range(hp):
        cs = slice(hh * 128, (hh + 1) * 128)
        q = q_ref[:, cs].astype(F32) * scale
        q2 = jnp.concatenate([jnp.where(lane < DA_QK, q, 0.0), jnp.where(lane < DA_QK, 0.0, q)],
                             axis=0).astype(BF16)
        k = k_ref[:, cs]
        v = v_ref[:, cs]
        s = lax.dot_general(q2, k, dn_t, preferred_element_type=F32)
        mx = jnp.max(s, axis=-1, keepdims=True)
        if has_cache:
            kc = kc_ref[:, cs].astype(BF16)
            vc = vc_ref[:, cs].astype(BF16)
            sc = lax.dot_general(q2, kc, dn_t, preferred_element_type=F32)
            mx = jnp.maximum(mx, jnp.max(sc, axis=-1, keepdims=True))
        e = jnp.exp(s - mx)
        den = jnp.sum(e, axis=-1, keepdims=True)
        if has_cache:
            ec = jnp.exp(sc - mx)
            den = den + jnp.sum(ec, axis=-1, keepdims=True)
        r = 1.0 / den
        r1 = r[0:tq]
        r2 = r[tq:2 * tq] * lam
        w = (e[0:tq] * r1 - e[tq:2 * tq] * r2).astype(BF16)
        o = jnp.dot(w, v, preferred_element_type=F32)
        if has_cache:
            wc = (ec[0:tq] * r1 - ec[tq:2 * tq] * r2).astype(BF16)
            o = o + jnp.dot(wc, vc, preferred_element_type=F32)
        ms = jnp.mean(o * o, axis=-1, keepdims=True)
        o = o * lax.rsqrt(ms + EPS) * hw_ref[...] * (1.0 - lam_init)
        o_ref[:, cs] = (o * _silu(gb_ref[:, cs].astype(F32))).astype(o_ref.dtype)


def _attention(main, nseq, T, tq, hp, lam_vecs, head_w, lam_init, cache):
    R = nseq * T
    nq = T // tq
    ng = DA_HEADS // hp
    wb = hp * 128
    spb = D_MODEL // wb
    has_cache = cache is not None
    in_specs = [
        pl.BlockSpec((tq, wb), lambda b, g, qi: (b * nq + qi, SEG_Q * spb + g)),
        pl.BlockSpec((T, wb), lambda b, g, qi: (b, SEG_K * spb + g)),
        pl.BlockSpec((T, wb), lambda b, g, qi: (b, SEG_V * spb + g)),
        pl.BlockSpec((tq, wb), lambda b, g, qi: (b * nq + qi, SEG_GB * spb + g)),
    ]
    args = [main, main, main, main]
    if has_cache:
        ck, cv = cache
        lc = ck.shape[1]
        in_specs += [pl.BlockSpec((None, lc, wb), lambda b, g, qi: (b, 0, g)),
                     pl.BlockSpec((None, lc, wb), lambda b, g, qi: (b, 0, g))]
        args += [ck, cv]
    in_specs += [pl.BlockSpec((4, DA_QK), lambda b, g, qi: (0, 0)),
                 pl.BlockSpec((1, DA_VD), lambda b, g, qi: (0, 0))]
    args += [lam_vecs, head_w]
    return pl.pallas_call(
        functools.partial(_attn_kernel, hp=hp, has_cache=has_cache, lam_init=lam_init),
        out_shape=jax.ShapeDtypeStruct((R, D_MODEL), BF16),
        grid=(nseq, ng, nq),
        in_specs=in_specs,
        out_specs=pl.BlockSpec((tq, wb), lambda b, g, qi: (b * nq + qi, g)),
        compiler_params=_cparams(("parallel", "parallel", "arbitrary")),
        name="diff_attn",
    )(*args)


def _sgmlp_kernel(u_ref, sv_ref, gc_ref, vw_ref, ws_ref, bias_ref, o_ref):
    tm = u_ref.shape[0]
    u = _gelu_tanh(u_ref[...].astype(F32))
    v = _gelu_tanh(sv_ref[...].astype(F32))
    vc = v - jnp.mean(v, axis=-1, keepdims=True)
    v = vc * lax.rsqrt(jnp.mean(vc * vc, axis=-1, keepdims=True) + EPS) * vw_ref[...]
    vb = v.astype(BF16)
    gate = _silu(gc_ref[...].astype(F32))
    for ci in range(tm // SG_CHUNK):
        rs = slice(ci * SG_CHUNK, (ci + 1) * SG_CHUNK)
        cols = []
        for g in range(SG_GROUPS):
            cols.append(jnp.dot(ws_ref[g], vb[rs, g * 128:(g + 1) * 128], preferred_element_type=F32))
        mixed = jnp.concatenate(cols, axis=1) + bias_ref[...]
        o_ref[rs, :] = (u[rs] * mixed * gate[rs]).astype(o_ref.dtype)


def _sgmlp(main, vnorm_w, ws_bf, bias_exp, tm=256):
    R = main.shape[0]
    return pl.pallas_call(
        _sgmlp_kernel,
        out_shape=jax.ShapeDtypeStruct((R, D_MODEL), BF16),
        grid=(R // tm,),
        in_specs=[
            pl.BlockSpec((tm, D_MODEL), lambda i: (i, SEG_U)),
            pl.BlockSpec((tm, D_MODEL), lambda i: (i, SEG_SV)),
            pl.BlockSpec((tm, D_MODEL), lambda i: (i, SEG_GC)),
            pl.BlockSpec((1, D_MODEL), lambda i: (0, 0)),
            pl.BlockSpec((SG_GROUPS, SG_CHUNK, SG_CHUNK), lambda i: (0, 0, 0)),
            pl.BlockSpec((SG_CHUNK, D_MODEL), lambda i: (0, 0)),
        ],
        out_specs=pl.BlockSpec((tm, D_MODEL), lambda i: (i, 0)),
        compiler_params=_cparams(("parallel",)),
        name="sgmlp",
    )(main, main, main, vnorm_w, ws_bf, bias_exp)


def _merge_kernel(ya_ref, yb_ref, yc_ref, ga_ref, gb_ref, gc_ref, wb_ref, wo_ref, pw_ref, mod_ref, x_ref, o_ref):
    def branch(y_ref, g_ref, i):
        p = jnp.dot(y_ref[...], wb_ref[i], preferred_element_type=F32)
        return _sigmoid(g_ref[...].astype(F32)) * p

    merged = branch(ya_ref, ga_ref, 0) + branch(yb_ref, gb_ref, 1) + branch(yc_ref, gc_ref, 2)
    o = jnp.dot(merged.astype(BF16), wo_ref[...], preferred_element_type=F32)
    ms = jnp.mean(o * o, axis=-1, keepdims=True)
    o = o * lax.rsqrt(ms + EPS) * pw_ref[...]
    o_ref[...] = x_ref[...] + mod_ref[2:3, :] * o


def _merge(ya, yb, yc, main, wb_bf, wo_bf, post_w, mod, rows_per_mod, x, tm=512):
    R = x.shape[0]
    row = lambda i: (i, 0)
    return pl.pallas_call(
        _merge_kernel,
        out_shape=jax.ShapeDtypeStruct((R, D_MODEL), F32),
        grid=(R // tm,),
        in_specs=[
            pl.BlockSpec((tm, D_MODEL), row),
            pl.BlockSpec((tm, D_MODEL), row),
            pl.BlockSpec((tm, D_MODEL), row),
            pl.BlockSpec((tm, D_MODEL), lambda i: (i, SEG_MGA)),
            pl.BlockSpec((tm, D_MODEL), lambda i: (i, SEG_MGB)),
            pl.BlockSpec((tm, D_MODEL), lambda i: (i, SEG_MGC)),
            pl.BlockSpec((3, D_MODEL, D_MODEL), lambda i: (0, 0, 0)),
            pl.BlockSpec((D_MODEL, D_MODEL), lambda i: (0, 0)),
            pl.BlockSpec((1, D_MODEL), lambda i: (0, 0)),
            pl.BlockSpec((None, 3, D_MODEL), lambda i: ((i * tm) // rows_per_mod, 0, 0)),
            pl.BlockSpec((tm, D_MODEL), row),
        ],
        out_specs=pl.BlockSpec((tm, D_MODEL), row),
        compiler_params=_cparams(("parallel",)),
        name="merge_out",
    )(ya, yb, yc, main, main, main, wb_bf, wo_bf, post_w, mod, x)


def _rope_tables(n_tokens):
    n_rows = n_tokens // GRID_W
    rows = jnp.repeat(jnp.arange(n_rows, dtype=F32), GRID_W)
    cols = jnp.tile(jnp.arange(GRID_W, dtype=F32), n_rows)
    n_freq = DA_QK // 4
    inv = ROPE_BASE ** (-jnp.arange(n_freq, dtype=F32) / n_freq)
    ang = jnp.concatenate([rows[:, None] * inv, cols[:, None] * inv], -1)
    cos, sin = jnp.cos(ang), jnp.sin(ang)
    return jnp.tile(cos, (1, 4)), jnp.concatenate([-sin, sin, -sin, sin], axis=1)


def _state_to_pairs(s):
    lead = s.shape[:-3]
    s = s.reshape(lead + (M_PAIRS, 2, M_HEADDIM, M_STATE))
    s = jnp.moveaxis(s, -1, -3)
    return s.reshape(lead + (M_PAIRS, M_STATE, 2 * M_HEADDIM))


def _pairs_to_state(s):
    lead = s.shape[:-3]
    s = s.reshape(lead + (M_PAIRS, M_STATE, 2, M_HEADDIM))
    s = jnp.moveaxis(s, -3, -1)
    return s.reshape(lead + (M_HEADS, M_HEADDIM, M_STATE))


def kernel(x_prompt, x_sample, cache_k, cache_v, state_ssm, c, c_ctx, pre_norm_w, post_norm_w, w_mod, b_mod, w_in,
           m_conv_w, m_conv_b, m_A_log, m_dt_bias, m_D, m_norm_w, da_lambda, da_head_norm_w, sg_vnorm_w,
           sg_spatial_w, sg_spatial_b, w_branch, w_out):
    depth = w_in.shape[0]
    nb, seq, _ = x_prompt.shape
    db, dseq, _ = x_sample.shape
    past = cache_k.shape[2]

    w_main = jnp.concatenate([w_in[:, :, 0:2048], w_in[:, :, 2336:]], axis=2).astype(BF16)
    w_misc = jnp.concatenate([w_in[:, :, 2048:2336], jnp.zeros((depth, D_MODEL, MISC_W - 288), w_in.dtype)],
                             axis=2).astype(BF16)
    wb_bf = w_branch.astype(BF16)
    wo_bf = w_out.astype(BF16)
    ws_bf = sg_spatial_w.astype(BF16)
    bias_exp = jnp.repeat(jnp.swapaxes(sg_spatial_b, 1, 2), D_MODEL // SG_GROUPS, axis=2)
    cw = jnp.swapaxes(m_conv_w, 1, 2)
    dtb = jnp.pad(m_dt_bias.reshape(depth, 1, 2 * M_HEADS), ((0, 0), (0, 0), (0, 128 - 2 * M_HEADS)))
    alog = jnp.pad(m_A_log.reshape(depth, 1, 2 * M_HEADS), ((0, 0), (0, 0), (0, 128 - 2 * M_HEADS)))
    dsum = jnp.repeat(m_D[:, 0] + m_D[:, 1], M_HEADDIM, axis=1).reshape(depth, 1, D_MODEL)

    cvec = jnp.concatenate([c_ctx[None, :], c, jnp.zeros((8 - 1 - db, D_MODEL), F32)], axis=0)
    mods = _modulation(cvec, w_mod, b_mod).reshape(depth, 8, 3, D_MODEL)

    rope_tabs = _rope_tables(dseq)
    init_pairs = _state_to_pairs(state_ssm)
    ck = cache_k.reshape(db, depth, past, D_MODEL)
    cv = cache_v.reshape(db, depth, past, D_MODEL)

    xp = x_prompt.reshape(nb * seq, D_MODEL)
    xs = x_sample.reshape(db * dseq, D_MODEL)
    ks_out, vs_out, ss_out = [], [], []

    def layer(x, l, nseq, T, mod, rows_per_mod, latent):
        lam_init = 0.8 - 0.6 * math.exp(-0.3 * l)
        outs = _inproj(x, mod, pre_norm_w[l][None], w_main[l], w_misc[l], rows_per_mod,
                       rope_tabs if latent else None, emit_kv=not latent)
        main, misc = outs[0], outs[1]
        ya, st = _ssd(main, misc, nseq, T, cw[l, :, 0:D_MODEL], m_conv_b[l][None, 0:D_MODEL],
                      cw[l, :, D_MODEL:], m_conv_b[l][None, D_MODEL:], dtb[l], alog[l], dsum[l],
                      m_norm_w[l][None], init_pairs[:, l] if latent else None)
        if latent:
            yb = _attention(main, nseq, T, 128, 1, da_lambda[l], da_head_norm_w[l][None], lam_init,
                            (ck[:, l], cv[:, l]))
        else:
            yb = _attention(main, nseq, T, T, DA_HEADS, da_lambda[l], da_head_norm_w[l][None], lam_init, None)
        yc = _sgmlp(main, sg_vnorm_w[l][None], ws_bf[l], bias_exp[l])
        x_new = _merge(ya, yb, yc, main, wb_bf[l], wo_bf[l], post_norm_w[l][None], mod, rows_per_mod, x)
        return x_new, outs[2:], st

    for l in range(depth):
        xp, kv, st = layer(xp, l, nb, seq, mods[l, 0:1], nb * seq, False)
        ks_out.append(kv[0].reshape(nb, seq, DA_HEADS, 2 * DA_QK))
        vs_out.append(kv[1].reshape(nb, seq, DA_HEADS, DA_VD))
        ss_out.append(_pairs_to_state(st))
        xs, _, _ = layer(xs, l, db, dseq, mods[l, 1:1 + db], dseq, True)

    return (xp.reshape(nb, seq, D_MODEL), xs.reshape(db, dseq, D_MODEL),
            jnp.stack(ks_out, 1), jnp.stack(vs_out, 1), jnp.stack(ss_out, 1))
```

```python
import functools
import math

import jax
import jax.numpy as jnp
from jax import lax
from jax.experimental import pallas as pl
from jax.experimental.pallas import tpu as pltpu

F32 = jnp.float32
BF16 = jnp.bfloat16

D_MODEL = 1024
EPS = 1e-6
GRID_W = 64
ROPE_BASE = 10000.0
M_HEADS = 16
M_HEADDIM = 64
M_STATE = 64
M_GROUPS = 2
M_CHUNK = 128
M_PAIRS = M_HEADS // 2
DA_HEADS = 8
DA_QK = 64
DA_VD = 128
SG_GROUPS = 8
SG_CHUNK = 128

SEG_Z, SEG_XS, SEG_Q, SEG_K, SEG_V, SEG_GB, SEG_U, SEG_SV, SEG_GC, SEG_MGA, SEG_MGB, SEG_MGC = range(12)
N_SEG = 12
MISC_W = 384

VMEM_LIMIT = 56 * 1024 * 1024
NEG_BIG = -1e30
ATT_SUBQ = 128
INPROJ_TM = 1024
INPROJ_SUBN = 512


def _silu(x):
    return x * (1.0 / (1.0 + jnp.exp(-x)))


def _sigmoid(x):
    return 1.0 / (1.0 + jnp.exp(-x))


def _gelu_tanh(x):
    return 0.5 * x * (1.0 + jnp.tanh(math.sqrt(2.0 / math.pi) * (x + 0.044715 * (x * x * x))))


def _cparams(sem):
    return pltpu.CompilerParams(dimension_semantics=sem, vmem_limit_bytes=VMEM_LIMIT)


def _mod_kernel(c_ref, w_ref, b_ref, o_ref):
    c = c_ref[...]
    s = _silu(c).astype(BF16)
    o_ref[...] = jnp.dot(s, w_ref[...].astype(BF16), preferred_element_type=F32) + b_ref[...]


def _modulation(cvec, w_mod, b_mod):
    depth = w_mod.shape[0]
    nt = 3
    return pl.pallas_call(
        _mod_kernel,
        out_shape=jax.ShapeDtypeStruct((depth, 8, 3 * D_MODEL), F32),
        grid=(depth, nt),
        in_specs=[
            pl.BlockSpec((8, D_MODEL), lambda l, j: (0, 0)),
            pl.BlockSpec((None, D_MODEL, D_MODEL), lambda l, j: (l, 0, j)),
            pl.BlockSpec((None, 1, D_MODEL), lambda l, j: (l, 0, j)),
        ],
        out_specs=pl.BlockSpec((None, 8, D_MODEL), lambda l, j: (l, 0, j)),
        compiler_params=_cparams(("arbitrary", "arbitrary")),
        name="modulation",
    )(cvec, w_mod, b_mod.reshape(depth, 1, 3 * D_MODEL))


def _inproj_kernel(*refs, rope, emit_kv):
    it = iter(refs)
    x_ref, mod_ref, prew_ref, w_ref, wm_ref = next(it), next(it), next(it), next(it), next(it)
    if rope:
        cos_ref, sin_ref = next(it), next(it)
    main_ref, misc_ref = next(it), next(it)
    if emit_kv:
        kv_ref = next(it)
    h_scr = next(it)

    j = pl.program_id(1)
    tm = x_ref.shape[0]

    @pl.when(j == 0)
    def _():
        x = x_ref[...]
        ms = jnp.mean(x * x, axis=-1, keepdims=True)
        y = x * lax.rsqrt(ms + EPS) * prew_ref[...]
        h = y * (1.0 + mod_ref[1:2, :]) + mod_ref[0:1, :]
        hb = h.astype(BF16)
        h_scr[...] = hb
        misc_ref[...] = jnp.dot(hb, wm_ref[...], preferred_element_type=F32)

    h = h_scr[...]
    sub = INPROJ_SUBN
    for c0 in range(0, D_MODEL, sub):
        cs = slice(c0, c0 + sub)
        acc = jnp.dot(h, w_ref[:, cs], preferred_element_type=F32)

        if rope:
            is_rope = jnp.logical_or(j == SEG_Q, j == SEG_K)

            @pl.when(is_rope)
            def _():
                cos = jnp.concatenate([cos_ref[...]] * (sub // 128), axis=1)
                sin = jnp.concatenate([sin_ref[...]] * (sub // 128), axis=1)
                lane = lax.broadcasted_iota(jnp.int32, (tm, sub), 1)
                first_half = (lane & (DA_QK - 1)) < (DA_QK // 2)
                rot = jnp.where(first_half, pltpu.roll(acc, sub - DA_QK // 2, 1), pltpu.roll(acc, DA_QK // 2, 1))
                main_ref[:, cs] = (acc * cos + rot * sin).astype(main_ref.dtype)

            @pl.when(jnp.logical_not(is_rope))
            def _():
                main_ref[:, cs] = acc.astype(main_ref.dtype)
        else:
            main_ref[:, cs] = acc.astype(main_ref.dtype)

        if emit_kv:
            @pl.when(jnp.logical_or(j == SEG_K, j == SEG_V))
            def _():
                kv_ref[:, cs] = acc


def _inproj(x, mod, pre_w, w_main, w_misc, l, rows_per_mod, rope_tabs, emit_kv, tm=INPROJ_TM):
    R = x.shape[0]
    ni = R // tm
    rope = rope_tabs is not None
    in_specs = [
        pl.BlockSpec((tm, D_MODEL), lambda i, j: (i, 0)),
        pl.BlockSpec((None, 3, D_MODEL), lambda i, j: ((i * tm) // rows_per_mod, 0, 0)),
        pl.BlockSpec((1, D_MODEL), lambda i, j: (0, 0)),
        pl.BlockSpec((None, D_MODEL, D_MODEL), lambda i, j: (l, 0, j)),
        pl.BlockSpec((None, D_MODEL, MISC_W), lambda i, j: (l, 0, 0)),
    ]
    args = [x, mod, pre_w, w_main, w_misc]
    if rope:
        cos, sin = rope_tabs
        nt = cos.shape[0] // tm
        in_specs += [pl.BlockSpec((tm, 128), lambda i, j: (i % nt, 0)),
                     pl.BlockSpec((tm, 128), lambda i, j: (i % nt, 0))]
        args += [cos, sin]
    out_shape = [jax.ShapeDtypeStruct((R, N_SEG * D_MODEL), BF16), jax.ShapeDtypeStruct((R, MISC_W), F32)]
    out_specs = [pl.BlockSpec((tm, D_MODEL), lambda i, j: (i, j)), pl.BlockSpec((tm, MISC_W), lambda i, j: (i, 0))]
    if emit_kv:
        out_shape.append(jax.ShapeDtypeStruct((R, 2 * D_MODEL), F32))
        out_specs.append(pl.BlockSpec((tm, D_MODEL), lambda i, j: (i, jnp.where(j <= SEG_K, 0, 1))))
    return pl.pallas_call(
        functools.partial(_inproj_kernel, rope=rope, emit_kv=emit_kv),
        out_shape=out_shape,
        grid=(ni, N_SEG),
        in_specs=in_specs,
        out_specs=out_specs,
        scratch_shapes=[pltpu.VMEM((tm, D_MODEL), BF16)],
        compiler_params=_cparams(("parallel", "arbitrary")),
        name="inproj",
    )(*args)


def _split3(a):
    hi = a.astype(BF16)
    r1 = a - hi.astype(F32)
    mid = r1.astype(BF16)
    lo = (r1 - mid.astype(F32)).astype(BF16)
    return hi, mid, lo


def _ssd_kernel(*refs, nc, has_init):
    it = iter(refs)
    xs_ref, xsp_ref, xsn_ref = next(it), next(it), next(it)
    mi_ref, mip_ref, min_ref = next(it), next(it), next(it)
    z_ref = next(it)
    cwx_ref, cbx_ref, cwm_ref, cbm_ref = next(it), next(it), next(it), next(it)
    dtb_ref, alog_ref, dsum_ref, nw_ref = next(it), next(it), next(it), next(it)
    if has_init:
        init_ref = next(it)
    y_ref, sto_ref = next(it), next(it)
    xc_scr, bc_scr, yf_scr, st_scr = next(it), next(it), next(it), next(it)

    L = M_CHUNK
    j = pl.program_id(1)
    fwd = j < nc
    c = jnp.where(fwd, j, 2 * nc - 1 - j)
    row0 = pl.multiple_of(c * L, L)

    rid = lax.broadcasted_iota(jnp.int32, (L, L), 0)
    cid = lax.broadcasted_iota(jnp.int32, (L, L), 1)

    def conv_silu(x, prow, nrow, w_ref, b_ref):
        n = x.shape[1]
        r = lax.broadcasted_iota(jnp.int32, (L, n), 0)
        xm = jnp.where(r == 0, prow, pltpu.roll(x, 1, 0))
        xp = jnp.where(r == L - 1, nrow, pltpu.roll(x, L - 1, 0))
        y = w_ref[0:1, :] * xm + w_ref[1:2, :] * x + w_ref[2:3, :] * xp + b_ref[...]
        return _silu(y)

    def load_state(d):
        if has_init:
            for i in range(M_PAIRS):
                st_scr[i] = init_ref[d, i]
        else:
            st_scr[...] = jnp.zeros_like(st_scr)

    def store_state(d):
        pad = jnp.zeros((2 * M_HEADDIM - M_STATE, 2 * M_HEADDIM), F32)
        for i in range(M_PAIRS):
            t = jnp.concatenate([st_scr[i], pad], axis=0).T
            sto_ref[d, i * 2 * M_HEADDIM:(i + 1) * 2 * M_HEADDIM, :] = t[:, 0:M_STATE]

    def chunk(d, x_bf, bc, dt_raw):
        if d == 0:
            tri = (cid <= rid)
        else:
            tri = (cid >= rid)
        tri_bf = jnp.where(tri, 1.0, 0.0).astype(BF16)
        dt = dt_raw + dtb_ref[...]
        dt = jnp.maximum(dt, 0.0) + jnp.log1p(jnp.exp(-jnp.abs(dt)))
        a = dt * (-jnp.exp(alog_ref[...]))
        a_hi, a_mid, a_lo = _split3(a)
        p_col = (jnp.dot(tri_bf, a_hi, preferred_element_type=F32)
                 + jnp.dot(tri_bf, a_mid, preferred_element_type=F32)
                 + jnp.dot(tri_bf, a_lo, preferred_element_type=F32))
        p_row = p_col.T[d * M_HEADS:(d + 1) * M_HEADS, :]
        dt_row = dt.T[d * M_HEADS:(d + 1) * M_HEADS, :]
        tot = p_row[:, L - 1:L] if d == 0 else p_row[:, 0:1]
        w_row = dt_row * jnp.exp(tot - p_row)
        etot = jnp.exp(tot)
        ep_col = jnp.exp(p_col)

        b_all = bc[:, 0:M_GROUPS * M_STATE]
        c_all = bc[:, M_GROUPS * M_STATE:2 * M_GROUPS * M_STATE]
        bt_all = b_all.T
        b_bf = b_all.astype(BF16)

        lane_lo = cid < M_HEADDIM
        lane_lo_s = lax.broadcasted_iota(jnp.int32, (M_STATE, 2 * M_HEADDIM), 1) < M_HEADDIM
        outs = []
        heads_per_group = M_HEADS // M_GROUPS
        g_mats = []
        for g in range(M_GROUPS):
            cg = jnp.where(lane_lo if g == 0 else jnp.logical_not(lane_lo), c_all, 0.0).astype(BF16)
            g_mats.append(lax.dot_general(cg, b_bf, (((1,), (1,)), ((), ())), preferred_element_type=F32))

        zeros_s = jnp.zeros((M_STATE, 2 * M_HEADDIM), BF16)
        for i in range(M_PAIRS):
            g = (2 * i) // heads_per_group
            btg = bt_all[g * M_STATE:(g + 1) * M_STATE, :]
            lhs_rows = []
            bw_rows = []
            for hh in range(2):
                h = 2 * i + hh
                col = d * M_HEADS + h
                pc = jnp.broadcast_to(p_col[:, col:col + 1], (L, L))
                diff = pc - p_row[h:h + 1, :]
                dm = jnp.exp(jnp.where(tri, diff, NEG_BIG))
                m_h = g_mats[g] * dm * dt_row[h:h + 1, :]
                ce_h = c_all * jnp.broadcast_to(ep_col[:, col:col + 1], (L, L))
                lhs_rows.append(jnp.concatenate([m_h, ce_h], axis=1).astype(BF16))
                bw = (btg * w_row[h:h + 1, :]).astype(BF16)
                bw_rows.append(jnp.concatenate([bw, zeros_s], axis=1))
            lhs = jnp.concatenate(lhs_rows + bw_rows, axis=0)
            x_pair = x_bf[:, i * 128:(i + 1) * 128]
            st_pair = st_scr[i]
            st_bf = st_pair.astype(BF16)
            rhs = jnp.concatenate([x_pair] + ([st_bf, zeros_s] if g == 0 else [zeros_s, st_bf]), axis=0)
            res = jnp.dot(lhs, rhs, preferred_element_type=F32)
            y_pair = jnp.where(lane_lo, res[0:L], res[L:2 * L])
            ds = jnp.where(lane_lo_s, res[2 * L:2 * L + M_STATE], res[2 * L + M_STATE:2 * L + 2 * M_STATE])
            e0 = jnp.broadcast_to(etot[2 * i:2 * i + 1, :], (M_STATE, 2 * M_HEADDIM))
            e1 = jnp.broadcast_to(etot[2 * i + 1:2 * i + 2, :], (M_STATE, 2 * M_HEADDIM))
            st_scr[i] = jnp.where(lane_lo_s, e0, e1) * st_pair + ds
            outs.append(y_pair)
        return jnp.concatenate(outs, axis=1)

    @pl.when(j == 0)
    def _():
        load_state(0)

    @pl.when(j == nc)
    def _():
        load_state(1)

    @pl.when(fwd)
    def _():
        x = xs_ref[...].astype(F32)
        prow = jnp.where(c > 0, xsp_ref[...].astype(F32)[15:16, :], 0.0)
        nrow = jnp.where(c < nc - 1, xsn_ref[...].astype(F32)[0:1, :], 0.0)
        xc = conv_silu(x, prow, nrow, cwx_ref, cbx_ref).astype(BF16)
        xc_scr[pl.ds(row0, L), :] = xc
        m = mi_ref[...]
        bcx = m[:, 0:256]
        prow_m = jnp.where(c > 0, mip_ref[7:8, 0:256], 0.0)
        nrow_m = jnp.where(c < nc - 1, min_ref[0:1, 0:256], 0.0)
        bc = conv_silu(bcx, prow_m, nrow_m, cwm_ref, cbm_ref)
        bc_scr[pl.ds(row0, L), :] = bc
        yf_scr[pl.ds(row0, L), :] = chunk(0, xc, bc, m[:, 256:384])

    @pl.when(j == nc - 1)
    def _():
        store_state(0)

    @pl.when(jnp.logical_not(fwd))
    def _():
        xc = xc_scr[pl.ds(row0, L), :]
        bc = bc_scr[pl.ds(row0, L), :]
        yb = chunk(1, xc, bc, mi_ref[:, 256:384])
        y = yf_scr[pl.ds(row0, L), :] + yb + dsum_ref[...] * xc.astype(F32)
        y = y * _silu(z_ref[...].astype(F32))
        ms = jnp.mean(y * y, axis=-1, keepdims=True)
        y_ref[...] = (y * lax.rsqrt(ms + EPS) * nw_ref[...]).astype(y_ref.dtype)

    @pl.when(j == 2 * nc - 1)
    def _():
        store_state(1)


def _ssd(main, misc, nseq, T, cwx, cbx, cwm, cbm, dtb, alog, dsum, nw, init):
    R = nseq * T
    L = M_CHUNK
    nc = T // L
    has_init = init is not None

    def cidx(j):
        return jnp.where(j < nc, j, 2 * nc - 1 - j)

    def oidx(j):
        return jnp.where(j < nc, nc - 1, 2 * nc - 1 - j)

    in_specs = [
        pl.BlockSpec((L, D_MODEL), lambda b, j: (b * nc + cidx(j), SEG_XS)),
        pl.BlockSpec((16, D_MODEL), lambda b, j: (jnp.maximum((b * nc + cidx(j)) * (L // 16) - 1, 0), SEG_XS)),
        pl.BlockSpec((16, D_MODEL),
                     lambda b, j: (jnp.minimum((b * nc + cidx(j) + 1) * (L // 16), R // 16 - 1), SEG_XS)),
        pl.BlockSpec((L, MISC_W), lambda b, j: (b * nc + cidx(j), 0)),
        pl.BlockSpec((8, MISC_W), lambda b, j: (jnp.maximum((b * nc + cidx(j)) * (L // 8) - 1, 0), 0)),
        pl.BlockSpec((8, MISC_W), lambda b, j: (jnp.minimum((b * nc + cidx(j) + 1) * (L // 8), R // 8 - 1), 0)),
        pl.BlockSpec((L, D_MODEL), lambda b, j: (b * nc + oidx(j), SEG_Z)),
        pl.BlockSpec((3, D_MODEL), lambda b, j: (0, 0)),
        pl.BlockSpec((1, D_MODEL), lambda b, j: (0, 0)),
        pl.BlockSpec((3, 256), lambda b, j: (0, 0)),
        pl.BlockSpec((1, 256), lambda b, j: (0, 0)),
        pl.BlockSpec((1, 128), lambda b, j: (0, 0)),
        pl.BlockSpec((1, 128), lambda b, j: (0, 0)),
        pl.BlockSpec((1, D_MODEL), lambda b, j: (0, 0)),
        pl.BlockSpec((1, D_MODEL), lambda b, j: (0, 0)),
    ]
    args = [main, main, main, misc, misc, misc, main, cwx, cbx, cwm, cbm, dtb, alog, dsum, nw]
    if has_init:
        in_specs.append(pl.BlockSpec((None, 2, M_PAIRS, M_STATE, 128), lambda b, j: (b, 0, 0, 0, 0)))
        args.append(init)
    return pl.pallas_call(
        functools.partial(_ssd_kernel, nc=nc, has_init=has_init),
        out_shape=[jax.ShapeDtypeStruct((R, D_MODEL), BF16),
                   jax.ShapeDtypeStruct((nseq, 2, M_HEADS * M_HEADDIM, M_STATE), F32)],
        grid=(nseq, 2 * nc),
        in_specs=in_specs,
        out_specs=[pl.BlockSpec((L, D_MODEL), lambda b, j: (b * nc + oidx(j), 0)),
                   pl.BlockSpec((None, 2, M_HEADS * M_HEADDIM, M_STATE), lambda b, j: (b, 0, 0, 0))],
        scratch_shapes=[pltpu.VMEM((T, D_MODEL), BF16), pltpu.VMEM((T, 256), F32),
                        pltpu.VMEM((T, D_MODEL), F32), pltpu.VMEM((M_PAIRS, M_STATE, 128), F32)],
        compiler_params=_cparams(("parallel", "arbitrary")),
        name="ssd",
    )(*args)


def _attn_kernel(*refs, hp, has_cache, lam_init):
    it = iter(refs)
    q_ref, k_ref, v_ref, gb_ref = next(it), next(it), next(it), next(it)
    if has_cache:
        kc_ref, vc_ref = next(it), next(it)
    lv_ref, hw_ref = next(it), next(it)
    o_ref = next(it)

    tq = q_ref.shape[0]
    lv = lv_ref[...]
    lam = (jnp.exp(jnp.sum(lv[0:1, :] * lv[1:2, :], axis=-1, keepdims=True))
           - jnp.exp(jnp.sum(lv[2:3, :] * lv[3:4, :], axis=-1, keepdims=True)) + lam_init)
    lane = lax.broadcasted_iota(jnp.int32, (tq, 2 * DA_QK), 1)
    scale = DA_QK ** -0.5
    dn_t = (((1,), (1,)), ((), ()))

    for hh in range(hp):
        cs = slice(hh * 128, (hh + 1) * 128)
        q = q_ref[:, cs].astype(F32) * scale
        q2 = jnp.concatenate([jnp.where(lane < DA_QK, q, 0.0), jnp.where(lane < DA_QK, 0.0, q)],
                             axis=0).astype(BF16)
        k = k_ref[:, cs]
        v = v_ref[:, cs]
        s = lax.dot_general(q2, k, dn_t, preferred_element_type=F32)
        mx = jnp.max(s, axis=-1, keepdims=True)
        if has_cache:
            kc = kc_ref[:, cs].astype(BF16)
            vc = vc_ref[:, cs].astype(BF16)
            sc = lax.dot_general(q2, kc, dn_t, preferred_element_type=F32)
            mx = jnp.maximum(mx, jnp.max(sc, axis=-1, keepdims=True))
        e = jnp.exp(s - mx)
        den = jnp.sum(e, axis=-1, keepdims=True)
        if has_cache:
            ec = jnp.exp(sc - mx)
            den = den + jnp.sum(ec, axis=-1, keepdims=True)
        r = 1.0 / den
        r1 = r[0:tq]
        r2 = r[tq:2 * tq] * lam
        w = (e[0:tq] * r1 - e[tq:2 * tq] * r2).astype(BF16)
        o = jnp.dot(w, v, preferred_element_type=F32)
        if has_cache:
            wc = (ec[0:tq] * r1 - ec[tq:2 * tq] * r2).astype(BF16)
            o = o + jnp.dot(wc, vc, preferred_element_type=F32)
        ms = jnp.mean(o * o, axis=-1, keepdims=True)
        o = o * lax.rsqrt(ms + EPS) * hw_ref[...] * (1.0 - lam_init)
        o_ref[:, cs] = (o * _silu(gb_ref[:, cs].astype(F32))).astype(o_ref.dtype)


def _attention(main, nseq, T, tq, hp, lam_vecs, head_w, lam_init, cache):
    R = nseq * T
    nq = T // tq
    ng = DA_HEADS // hp
    wb = hp * 128
    spb = D_MODEL // wb
    has_cache = cache is not None
    in_specs = [
        pl.BlockSpec((tq, wb), lambda b, g, qi: (b * nq + qi, SEG_Q * spb + g)),
        pl.BlockSpec((T, wb), lambda b, g, qi: (b, SEG_K * spb + g)),
        pl.BlockSpec((T, wb), lambda b, g, qi: (b, SEG_V * spb + g)),
        pl.BlockSpec((tq, wb), lambda b, g, qi: (b * nq + qi, SEG_GB * spb + g)),
    ]
    args = [main, main, main, main]
    if has_cache:
        ck, cv = cache
        lc = ck.shape[1]
        in_specs += [pl.BlockSpec((None, lc, wb), lambda b, g, qi: (b, 0, g)),
                     pl.BlockSpec((None, lc, wb), lambda b, g, qi: (b, 0, g))]
        args += [ck, cv]
    in_specs += [pl.BlockSpec((4, DA_QK), lambda b, g, qi: (0, 0)),
                 pl.BlockSpec((1, DA_VD), lambda b, g, qi: (0, 0))]
    args += [lam_vecs, head_w]
    return pl.pallas_call(
        functools.partial(_attn_kernel, hp=hp, has_cache=has_cache, lam_init=lam_init),
        out_shape=jax.ShapeDtypeStruct((R, D_MODEL), BF16),
        grid=(nseq, ng, nq),
        in_specs=in_specs,
        out_specs=pl.BlockSpec((tq, wb), lambda b, g, qi: (b * nq + qi, g)),
        compiler_params=_cparams(("parallel", "parallel", "arbitrary")),
        name="diff_attn",
    )(*args)


def _attn_lat_kernel(q_ref, k_ref, v_ref, gb_ref, kc_ref, vc_ref, lv_ref, hw_ref, o_ref, vt_scr, vct_scr,
                     *, lam_init, kb):
    tq = q_ref.shape[0]
    T = k_ref.shape[0]
    lc = kc_ref.shape[0]
    qi = pl.program_id(2)

    @pl.when(qi == 0)
    def _():
        for c0 in range(0, T, 128):
            vt_scr[:, c0:c0 + 128] = v_ref[c0:c0 + 128, :].astype(F32).T.astype(BF16)
        for c0 in range(0, lc, 128):
            vct_scr[:, c0:c0 + 128] = vc_ref[c0:c0 + 128, :].T.astype(BF16)

    lv = lv_ref[...]
    lam = (jnp.exp(jnp.sum(lv[0:1, :] * lv[1:2, :], axis=-1, keepdims=True))
           - jnp.exp(jnp.sum(lv[2:3, :] * lv[3:4, :], axis=-1, keepdims=True)) + lam_init)
    sq = ATT_SUBQ
    lane = lax.broadcasted_iota(jnp.int32, (sq, 2 * DA_QK), 1)
    dn_t = (((1,), (1,)), ((), ()))

    q2s = []
    for c in range(tq // sq):
        q = q_ref[c * sq:(c + 1) * sq, :].astype(F32) * (DA_QK ** -0.5)
        q2s.append(jnp.concatenate([jnp.where(lane < DA_QK, q, 0.0), jnp.where(lane < DA_QK, 0.0, q)],
                                   axis=0).astype(BF16))

    def scores(k_blk, q2):
        s = lax.dot_general(k_blk, q2, dn_t, preferred_element_type=F32)
        return s, jnp.max(s, axis=0, keepdims=True)

    def accumulate(s, bm, vt_blk, state):
        m_new = bm if state is None else jnp.maximum(state[0], bm)
        p = jnp.exp(s - m_new)
        bs = jnp.sum(p, axis=0, keepdims=True)
        pv = jnp.dot(vt_blk, p.astype(BF16), preferred_element_type=F32)
        if state is None:
            return m_new, bs, pv
        alpha = jnp.exp(state[0] - m_new)
        return m_new, alpha * state[1] + bs, alpha * state[2] + pv

    kcb = kc_ref[...].astype(BF16)
    cur = [scores(kcb, q2) for q2 in q2s]
    cur_vt = vct_scr[...]
    states = [None] * len(q2s)
    for k0 in range(0, T, kb):
        k_blk = k_ref[k0:k0 + kb, :]
        nxt = [scores(k_blk, q2) for q2 in q2s]
        states = [accumulate(cu[0], cu[1], cur_vt, st) for cu, st in zip(cur, states)]
        cur, cur_vt = nxt, vt_scr[:, k0:k0 + kb]
    states = [accumulate(cu[0], cu[1], cur_vt, st) for cu, st in zip(cur, states)]
    for c, (_, den, acc) in enumerate(states):
        rs = slice(c * sq, (c + 1) * sq)
        r = 1.0 / den
        o_t = acc[:, 0:sq] * r[:, 0:sq] - acc[:, sq:2 * sq] * (r[:, sq:2 * sq] * lam)
        o = o_t.T
        ms = jnp.mean(o * o, axis=-1, keepdims=True)
        o = o * lax.rsqrt(ms + EPS) * hw_ref[...] * (1.0 - lam_init)
        o_ref[rs, :] = (o * _silu(gb_ref[rs, :].astype(F32))).astype(o_ref.dtype)


def _attention_latent(main, nseq, T, tq, kb, lam_vecs, head_w, lam_init, ck, cv):
    R = nseq * T
    nq = T // tq
    lc = ck.shape[1]
    spb = D_MODEL // 128
    return pl.pallas_call(
        functools.partial(_attn_lat_kernel, lam_init=lam_init, kb=kb),
        out_shape=jax.ShapeDtypeStruct((R, D_MODEL), BF16),
        grid=(nseq, DA_HEADS, nq),
        in_specs=[
            pl.BlockSpec((tq, 128), lambda b, g, qi: (b * nq + qi, SEG_Q * spb + g)),
            pl.BlockSpec((T, 128), lambda b, g, qi: (b, SEG_K * spb + g)),
            pl.BlockSpec((T, 128), lambda b, g, qi: (b, SEG_V * spb + g)),
            pl.BlockSpec((tq, 128), lambda b, g, qi: (b * nq + qi, SEG_GB * spb + g)),
            pl.BlockSpec((None, lc, 128), lambda b, g, qi: (b, 0, g)),
            pl.BlockSpec((None, lc, 128), lambda b, g, qi: (b, 0, g)),
            pl.BlockSpec((4, DA_QK), lambda b, g, qi: (0, 0)),
            pl.BlockSpec((1, DA_VD), lambda b, g, qi: (0, 0)),
        ],
        out_specs=pl.BlockSpec((tq, 128), lambda b, g, qi: (b * nq + qi, g)),
        scratch_shapes=[pltpu.VMEM((DA_VD, T), BF16), pltpu.VMEM((DA_VD, lc), BF16)],
        compiler_params=_cparams(("parallel", "parallel", "arbitrary")),
        name="diff_attn_lat",
    )(main, main, main, main, ck, cv, lam_vecs, head_w)


def _sgmlp_kernel(u_ref, sv_ref, gc_ref, vw_ref, ws_ref, bias_ref, o_ref):
    tm = u_ref.shape[0]
    u = _gelu_tanh(u_ref[...].astype(F32))
    v = _gelu_tanh(sv_ref[...].astype(F32))
    vc = v - jnp.mean(v, axis=-1, keepdims=True)
    v = vc * lax.rsqrt(jnp.mean(vc * vc, axis=-1, keepdims=True) + EPS) * vw_ref[...]
    vb = v.astype(BF16)
    gate = _silu(gc_ref[...].astype(F32))
    for ci in range(tm // SG_CHUNK):
        rs = slice(ci * SG_CHUNK, (ci + 1) * SG_CHUNK)
        cols = []
        for g in range(SG_GROUPS):
            cols.append(jnp.dot(ws_ref[g], vb[rs, g * 128:(g + 1) * 128], preferred_element_type=F32))
        mixed = jnp.concatenate(cols, axis=1) + bias_ref[...]
        o_ref[rs, :] = (u[rs] * mixed * gate[rs]).astype(o_ref.dtype)


def _sgmlp(main, vnorm_w, ws_bf, bias_exp, l, tm=256):
    R = main.shape[0]
    return pl.pallas_call(
        _sgmlp_kernel,
        out_shape=jax.ShapeDtypeStruct((R, D_MODEL), BF16),
        grid=(R // tm,),
        in_specs=[
            pl.BlockSpec((tm, D_MODEL), lambda i: (i, SEG_U)),
            pl.BlockSpec((tm, D_MODEL), lambda i: (i, SEG_SV)),
            pl.BlockSpec((tm, D_MODEL), lambda i: (i, SEG_GC)),
            pl.BlockSpec((1, D_MODEL), lambda i: (0, 0)),
            pl.BlockSpec((None, SG_GROUPS, SG_CHUNK, SG_CHUNK), lambda i: (l, 0, 0, 0)),
            pl.BlockSpec((None, SG_CHUNK, D_MODEL), lambda i: (l, 0, 0)),
        ],
        out_specs=pl.BlockSpec((tm, D_MODEL), lambda i: (i, 0)),
        compiler_params=_cparams(("parallel",)),
        name="sgmlp",
    )(main, main, main, vnorm_w, ws_bf, bias_exp)


def _merge_kernel(ya_ref, yb_ref, yc_ref, ga_ref, gb_ref, gc_ref, wb_ref, wo_ref, pw_ref, mod_ref, x_ref, o_ref):
    def branch(y_ref, g_ref, i):
        p = jnp.dot(y_ref[...], wb_ref[i], preferred_element_type=F32)
        return _sigmoid(g_ref[...].astype(F32)) * p

    merged = branch(ya_ref, ga_ref, 0) + branch(yb_ref, gb_ref, 1) + branch(yc_ref, gc_ref, 2)
    o = jnp.dot(merged.astype(BF16), wo_ref[...], preferred_element_type=F32)
    ms = jnp.mean(o * o, axis=-1, keepdims=True)
    o = o * lax.rsqrt(ms + EPS) * pw_ref[...]
    o_ref[...] = x_ref[...] + mod_ref[2:3, :] * o


def _merge(ya, yb, yc, main, wb_bf, wo_bf, l, post_w, mod, rows_per_mod, x, tm=512):
    R = x.shape[0]
    row = lambda i: (i, 0)
    return pl.pallas_call(
        _merge_kernel,
        out_shape=jax.ShapeDtypeStruct((R, D_MODEL), F32),
        grid=(R // tm,),
        in_specs=[
            pl.BlockSpec((tm, D_MODEL), row),
            pl.BlockSpec((tm, D_MODEL), row),
            pl.BlockSpec((tm, D_MODEL), row),
            pl.BlockSpec((tm, D_MODEL), lambda i: (i, SEG_MGA)),
            pl.BlockSpec((tm, D_MODEL), lambda i: (i, SEG_MGB)),
            pl.BlockSpec((tm, D_MODEL), lambda i: (i, SEG_MGC)),
            pl.BlockSpec((None, 3, D_MODEL, D_MODEL), lambda i: (l, 0, 0, 0)),
            pl.BlockSpec((None, D_MODEL, D_MODEL), lambda i: (l, 0, 0)),
            pl.BlockSpec((1, D_MODEL), lambda i: (0, 0)),
            pl.BlockSpec((None, 3, D_MODEL), lambda i: ((i * tm) // rows_per_mod, 0, 0)),
            pl.BlockSpec((tm, D_MODEL), row),
        ],
        out_specs=pl.BlockSpec((tm, D_MODEL), row),
        compiler_params=_cparams(("parallel",)),
        name="merge_out",
    )(ya, yb, yc, main, main, main, wb_bf, wo_bf, post_w, mod, x)


def _rope_tables(n_tokens):
    n_rows = n_tokens // GRID_W
    rows = jnp.repeat(jnp.arange(n_rows, dtype=F32), GRID_W)
    cols = jnp.tile(jnp.arange(GRID_W, dtype=F32), n_rows)
    n_freq = DA_QK // 4
    inv = ROPE_BASE ** (-jnp.arange(n_freq, dtype=F32) / n_freq)
    ang = jnp.concatenate([rows[:, None] * inv, cols[:, None] * inv], -1)
    cos, sin = jnp.cos(ang), jnp.sin(ang)
    return jnp.tile(cos, (1, 4)), jnp.concatenate([-sin, sin, -sin, sin], axis=1)


def _state_to_pairs(s):
    lead = s.shape[:-3]
    s = s.reshape(lead + (M_PAIRS, 2, M_HEADDIM, M_STATE))
    s = jnp.moveaxis(s, -1, -3)
    return s.reshape(lead + (M_PAIRS, M_STATE, 2 * M_HEADDIM))


def kernel(x_prompt, x_sample, cache_k, cache_v, state_ssm, c, c_ctx, pre_norm_w, post_norm_w, w_mod, b_mod, w_in,
           m_conv_w, m_conv_b, m_A_log, m_dt_bias, m_D, m_norm_w, da_lambda, da_head_norm_w, sg_vnorm_w,
           sg_spatial_w, sg_spatial_b, w_branch, w_out):
    depth = w_in.shape[0]
    nb, seq, _ = x_prompt.shape
    db, dseq, _ = x_sample.shape
    past = cache_k.shape[2]

    w_main = jnp.concatenate([w_in[:, :, 0:2048], w_in[:, :, 2336:]], axis=2).astype(BF16)
    w_misc = jnp.concatenate([w_in[:, :, 2048:2336], jnp.zeros((depth, D_MODEL, MISC_W - 288), w_in.dtype)],
                             axis=2).astype(BF16)
    wb_bf = w_branch.astype(BF16)
    wo_bf = w_out.astype(BF16)
    ws_bf = sg_spatial_w.astype(BF16)
    bias_exp = jnp.repeat(jnp.swapaxes(sg_spatial_b, 1, 2), D_MODEL // SG_GROUPS, axis=2)
    cw = jnp.swapaxes(m_conv_w, 1, 2)
    dtb = jnp.pad(m_dt_bias.reshape(depth, 1, 2 * M_HEADS), ((0, 0), (0, 0), (0, 128 - 2 * M_HEADS)))
    alog = jnp.pad(m_A_log.reshape(depth, 1, 2 * M_HEADS), ((0, 0), (0, 0), (0, 128 - 2 * M_HEADS)))
    dsum = jnp.repeat(m_D[:, 0] + m_D[:, 1], M_HEADDIM, axis=1).reshape(depth, 1, D_MODEL)

    cvec = jnp.concatenate([c_ctx[None, :], c, jnp.zeros((8 - 1 - db, D_MODEL), F32)], axis=0)
    mods = _modulation(cvec, w_mod, b_mod).reshape(depth, 8, 3, D_MODEL)

    rope_tabs = _rope_tables(dseq)
    init_pairs = _state_to_pairs(state_ssm)
    ck = cache_k.reshape(db, depth, past, D_MODEL)
    cv = cache_v.reshape(db, depth, past, D_MODEL)

    xp = x_prompt.reshape(nb * seq, D_MODEL)
    xs = x_sample.reshape(db * dseq, D_MODEL)
    ks_out, vs_out, ss_out = [], [], []

    def layer(x, l, nseq, T, mod, rows_per_mod, latent):
        lam_init = 0.8 - 0.6 * math.exp(-0.3 * l)
        outs = _inproj(x, mod, pre_norm_w[l][None], w_main, w_misc, l, rows_per_mod,
                       rope_tabs if latent else None, emit_kv=not latent)
        main, misc = outs[0], outs[1]
        ya, st = _ssd(main, misc, nseq, T, cw[l, :, 0:D_MODEL], m_conv_b[l][None, 0:D_MODEL],
                      cw[l, :, D_MODEL:], m_conv_b[l][None, D_MODEL:], dtb[l], alog[l], dsum[l],
                      m_norm_w[l][None], init_pairs[:, l] if latent else None)
        if latent:
            yb = _attention_latent(main, nseq, T, 512, 1024, da_lambda[l], da_head_norm_w[l][None], lam_init,
                                   ck[:, l], cv[:, l])
        else:
            yb = _attention(main, nseq, T, T, DA_HEADS, da_lambda[l], da_head_norm_w[l][None], lam_init, None)
        yc = _sgmlp(main, sg_vnorm_w[l][None], ws_bf, bias_exp, l)
        x_new = _merge(ya, yb, yc, main, wb_bf, wo_bf, l, post_norm_w[l][None], mod, rows_per_mod, x)
        return x_new, outs[2:], st

    for l in range(depth):
        xp, kv, st = layer(xp, l, nb, seq, mods[l, 0:1], nb * seq, False)
        ks_out.append(kv[0][:, 0:D_MODEL].reshape(nb, seq, DA_HEADS, 2 * DA_QK))
        vs_out.append(kv[0][:, D_MODEL:].reshape(nb, seq, DA_HEADS, DA_VD))
        ss_out.append(st.reshape(nb, 2, M_HEADS, M_HEADDIM, M_STATE))
        xs, _, _ = layer(xs, l, db, dseq, mods[l, 1:1 + db], dseq, True)

    return (xp.reshape(nb, seq, D_MODEL), xs.reshape(db, dseq, D_MODEL),
            jnp.stack(ks_out, 1), jnp.stack(vs_out, 1), jnp.stack(ss_out, 1))
```

```python
import functools
import math

import jax
import jax.numpy as jnp
from jax import lax
from jax.experimental import pallas as pl
from jax.experimental.pallas import tpu as pltpu

F32 = jnp.float32
BF16 = jnp.bfloat16

D_MODEL = 1024
EPS = 1e-6
GRID_W = 64
ROPE_BASE = 10000.0
M_HEADS = 16
M_HEADDIM = 64
M_STATE = 64
M_GROUPS = 2
M_CHUNK = 128
M_PAIRS = M_HEADS // 2
DA_HEADS = 8
DA_QK = 64
DA_VD = 128
SG_GROUPS = 8
SG_CHUNK = 128

SEG_Z, SEG_XS, SEG_Q, SEG_K, SEG_V, SEG_GB, SEG_U, SEG_SV, SEG_GC, SEG_MGA, SEG_MGB, SEG_MGC = range(12)
N_SEG = 12
MISC_W = 384

VMEM_LIMIT = 56 * 1024 * 1024
NEG_BIG = -1e30
ATT_SUBQ = 128
SSD_CHUNKS_PER_STEP = 4
INPROJ_TM = 1024
INPROJ_SUBN = 512


def _silu(x):
    return x * (1.0 / (1.0 + jnp.exp(-x)))


def _sigmoid(x):
    return 1.0 / (1.0 + jnp.exp(-x))


def _gelu_tanh(x):
    return 0.5 * x * (1.0 + jnp.tanh(math.sqrt(2.0 / math.pi) * (x + 0.044715 * (x * x * x))))


def _cparams(sem):
    return pltpu.CompilerParams(dimension_semantics=sem, vmem_limit_bytes=VMEM_LIMIT)


def _mod_kernel(c_ref, w_ref, b_ref, o_ref):
    c = c_ref[...]
    s = _silu(c).astype(BF16)
    o_ref[...] = jnp.dot(s, w_ref[...].astype(BF16), preferred_element_type=F32) + b_ref[...]


def _modulation(cvec, w_mod, b_mod):
    depth = w_mod.shape[0]
    nt = 3
    return pl.pallas_call(
        _mod_kernel,
        out_shape=jax.ShapeDtypeStruct((depth, 8, 3 * D_MODEL), F32),
        grid=(depth, nt),
        in_specs=[
            pl.BlockSpec((8, D_MODEL), lambda l, j: (0, 0)),
            pl.BlockSpec((None, D_MODEL, D_MODEL), lambda l, j: (l, 0, j)),
            pl.BlockSpec((None, 1, D_MODEL), lambda l, j: (l, 0, j)),
        ],
        out_specs=pl.BlockSpec((None, 8, D_MODEL), lambda l, j: (l, 0, j)),
        compiler_params=_cparams(("arbitrary", "arbitrary")),
        name="modulation",
    )(cvec, w_mod, b_mod.reshape(depth, 1, 3 * D_MODEL))


W_IN_COLS = 12576
W_XS_END = 2048
W_MISC_END = 2336


def _wprep_kernel(w_ref, main_ref, misc_ref):
    tr = w_ref.shape[0]
    main_ref[:, 0:W_XS_END] = w_ref[:, 0:W_XS_END].astype(BF16)
    main_ref[:, W_XS_END:] = w_ref[:, W_MISC_END:W_IN_COLS].astype(BF16)
    misc_ref[...] = jnp.concatenate(
        [w_ref[:, W_XS_END:W_MISC_END], jnp.zeros((tr, MISC_W - (W_MISC_END - W_XS_END)), F32)],
        axis=1).astype(BF16)


def _prep_in_weights(w_in, tr=128):
    depth = w_in.shape[0]
    return pl.pallas_call(
        _wprep_kernel,
        out_shape=[jax.ShapeDtypeStruct((depth, D_MODEL, N_SEG * D_MODEL), BF16),
                   jax.ShapeDtypeStruct((depth, D_MODEL, MISC_W), BF16)],
        grid=(depth, D_MODEL // tr),
        in_specs=[pl.BlockSpec((None, tr, W_IN_COLS), lambda l, i: (l, i, 0))],
        out_specs=[pl.BlockSpec((None, tr, N_SEG * D_MODEL), lambda l, i: (l, i, 0)),
                   pl.BlockSpec((None, tr, MISC_W), lambda l, i: (l, i, 0))],
        compiler_params=_cparams(("parallel", "parallel")),
        name="w_prep",
    )(w_in)


def _inproj_kernel(*refs, rope, emit_kv):
    it = iter(refs)
    x_ref, mod_ref, prew_ref, w_ref, wm_ref = next(it), next(it), next(it), next(it), next(it)
    if rope:
        cos_ref, sin_ref = next(it), next(it)
    main_ref, misc_ref = next(it), next(it)
    if emit_kv:
        kv_ref = next(it)
    h_scr = next(it)

    j = pl.program_id(1)
    tm = x_ref.shape[0]

    @pl.when(j == 0)
    def _():
        x = x_ref[...]
        ms = jnp.mean(x * x, axis=-1, keepdims=True)
        y = x * lax.rsqrt(ms + EPS) * prew_ref[...]
        h = y * (1.0 + mod_ref[1:2, :]) + mod_ref[0:1, :]
        hb = h.astype(BF16)
        h_scr[...] = hb
        misc_ref[...] = jnp.dot(hb, wm_ref[...], preferred_element_type=F32)

    h = h_scr[...]
    sub = INPROJ_SUBN
    for c0 in range(0, D_MODEL, sub):
        cs = slice(c0, c0 + sub)
        acc = jnp.dot(h, w_ref[:, cs], preferred_element_type=F32)

        if rope:
            is_rope = jnp.logical_or(j == SEG_Q, j == SEG_K)

            @pl.when(is_rope)
            def _():
                cos = jnp.concatenate([cos_ref[...]] * (sub // 128), axis=1)
                sin = jnp.concatenate([sin_ref[...]] * (sub // 128), axis=1)
                lane = lax.broadcasted_iota(jnp.int32, (tm, sub), 1)
                first_half = (lane & (DA_QK - 1)) < (DA_QK // 2)
                rot = jnp.where(first_half, pltpu.roll(acc, sub - DA_QK // 2, 1), pltpu.roll(acc, DA_QK // 2, 1))
                main_ref[:, cs] = (acc * cos + rot * sin).astype(main_ref.dtype)

            @pl.when(jnp.logical_not(is_rope))
            def _():
                main_ref[:, cs] = acc.astype(main_ref.dtype)
        else:
            main_ref[:, cs] = acc.astype(main_ref.dtype)

        if emit_kv:
            @pl.when(jnp.logical_or(j == SEG_K, j == SEG_V))
            def _():
                kv_ref[:, cs] = acc


def _inproj(x, mod, pre_w, w_main, w_misc, l, rows_per_mod, rope_tabs, emit_kv, tm=INPROJ_TM):
    R = x.shape[0]
    ni = R // tm
    rope = rope_tabs is not None
    in_specs = [
        pl.BlockSpec((tm, D_MODEL), lambda i, j: (i, 0)),
        pl.BlockSpec((None, 3, D_MODEL), lambda i, j: ((i * tm) // rows_per_mod, 0, 0)),
        pl.BlockSpec((1, D_MODEL), lambda i, j: (0, 0)),
        pl.BlockSpec((None, D_MODEL, D_MODEL), lambda i, j: (l, 0, j)),
        pl.BlockSpec((None, D_MODEL, MISC_W), lambda i, j: (l, 0, 0)),
    ]
    args = [x, mod, pre_w, w_main, w_misc]
    if rope:
        cos, sin = rope_tabs
        nt = cos.shape[0] // tm
        in_specs += [pl.BlockSpec((tm, 128), lambda i, j: (i % nt, 0)),
                     pl.BlockSpec((tm, 128), lambda i, j: (i % nt, 0))]
        args += [cos, sin]
    out_shape = [jax.ShapeDtypeStruct((R, N_SEG * D_MODEL), BF16), jax.ShapeDtypeStruct((R, MISC_W), F32)]
    out_specs = [pl.BlockSpec((tm, D_MODEL), lambda i, j: (i, j)), pl.BlockSpec((tm, MISC_W), lambda i, j: (i, 0))]
    if emit_kv:
        out_shape.append(jax.ShapeDtypeStruct((R, 2 * D_MODEL), F32))
        out_specs.append(pl.BlockSpec((tm, D_MODEL), lambda i, j: (i, jnp.where(j <= SEG_K, 0, 1))))
    return pl.pallas_call(
        functools.partial(_inproj_kernel, rope=rope, emit_kv=emit_kv),
        out_shape=out_shape,
        grid=(ni, N_SEG),
        in_specs=in_specs,
        out_specs=out_specs,
        scratch_shapes=[pltpu.VMEM((tm, D_MODEL), BF16)],
        compiler_params=_cparams(("parallel", "arbitrary")),
        name="inproj",
    )(*args)


def _split3(a):
    hi = a.astype(BF16)
    r1 = a - hi.astype(F32)
    mid = r1.astype(BF16)
    lo = (r1 - mid.astype(F32)).astype(BF16)
    return hi, mid, lo


def _ssd_kernel(*refs, nc, cps, has_init):
    it = iter(refs)
    xs_ref, xsp_ref, xsn_ref = next(it), next(it), next(it)
    mi_ref, mip_ref, min_ref = next(it), next(it), next(it)
    z_ref = next(it)
    cwx_ref, cbx_ref, cwm_ref, cbm_ref = next(it), next(it), next(it), next(it)
    dtb_ref, alog_ref, dsum_ref, nw_ref = next(it), next(it), next(it), next(it)
    if has_init:
        init_ref = next(it)
    y_ref, sto_ref = next(it), next(it)
    xc_scr, bc_scr, yf_scr, st_scr = next(it), next(it), next(it), next(it)

    L = M_CHUNK
    LB = cps * L
    ns = nc // cps
    j = pl.program_id(1)
    fwd = j < ns
    c = jnp.where(fwd, j, 2 * ns - 1 - j)
    row0 = pl.multiple_of(c * LB, LB)
    LOG2E = math.log2(math.e)

    rid = lax.broadcasted_iota(jnp.int32, (L, L), 0)
    cid = lax.broadcasted_iota(jnp.int32, (L, L), 1)

    def conv_silu(x, prow, nrow, w_ref, b_ref):
        n = x.shape[1]
        r = lax.broadcasted_iota(jnp.int32, (LB, n), 0)
        xm = jnp.where(r == 0, prow, pltpu.roll(x, 1, 0))
        xp = jnp.where(r == LB - 1, nrow, pltpu.roll(x, LB - 1, 0))
        y = w_ref[0:1, :] * xm + w_ref[1:2, :] * x + w_ref[2:3, :] * xp + b_ref[...]
        return _silu(y)

    def load_state(d):
        if has_init:
            for i in range(M_PAIRS):
                st_scr[i] = init_ref[d, i]
        else:
            st_scr[...] = jnp.zeros_like(st_scr)

    def store_state(d):
        pad = jnp.zeros((2 * M_HEADDIM - M_STATE, 2 * M_HEADDIM), F32)
        for i in range(M_PAIRS):
            t = jnp.concatenate([st_scr[i], pad], axis=0).T
            sto_ref[d, i * 2 * M_HEADDIM:(i + 1) * 2 * M_HEADDIM, :] = t[:, 0:M_STATE]

    lane_lo = cid < M_HEADDIM
    lane_lo_s = lax.broadcasted_iota(jnp.int32, (M_STATE, 2 * M_HEADDIM), 1) < M_HEADDIM
    heads_per_group = M_HEADS // M_GROUPS
    zeros_s = jnp.zeros((M_STATE, 2 * M_HEADDIM), BF16)

    def setup(d, bc, dt_raw):
        tri = (cid <= rid) if d == 0 else (cid >= rid)
        tri_bf = jnp.where(tri, 1.0, 0.0).astype(BF16)
        dt = dt_raw + dtb_ref[...]
        dt = jnp.maximum(dt, 0.0) + jnp.log1p(jnp.exp(-jnp.abs(dt)))
        a = dt * (-jnp.exp(alog_ref[...]))
        a_hi, a_mid, a_lo = _split3(a)
        p_col = (jnp.dot(tri_bf, a_hi, preferred_element_type=F32)
                 + jnp.dot(tri_bf, a_mid, preferred_element_type=F32)
                 + jnp.dot(tri_bf, a_lo, preferred_element_type=F32))
        p_row = p_col.T[d * M_HEADS:(d + 1) * M_HEADS, :]
        dt_row = dt.T[d * M_HEADS:(d + 1) * M_HEADS, :]
        tot = p_row[:, L - 1:L] if d == 0 else p_row[:, 0:1]
        b_all = bc[:, 0:M_GROUPS * M_STATE]
        c_all = bc[:, M_GROUPS * M_STATE:2 * M_GROUPS * M_STATE]
        b_bf = b_all.astype(BF16)
        g_mats = []
        for g in range(M_GROUPS):
            cg = jnp.where(lane_lo if g == 0 else jnp.logical_not(lane_lo), c_all, 0.0).astype(BF16)
            g_mats.append(lax.dot_general(cg, b_bf, (((1,), (1,)), ((), ())), preferred_element_type=F32))
        return dict(
            tri=tri, g_mats=g_mats, c_all=c_all, bt_all=b_all.T,
            p2_col=p_col * LOG2E,
            q2_row=(p_row - jnp.log(dt_row)) * LOG2E,
            w_row=dt_row * jnp.exp(tot - p_row),
            etot=jnp.exp(tot),
            ep_col=jnp.exp(p_col))

    def pairs(d, s, x_bf):
        outs = []
        for i in range(M_PAIRS):
            g = (2 * i) // heads_per_group
            btg = s["bt_all"][g * M_STATE:(g + 1) * M_STATE, :]
            lhs_rows = []
            bw_rows = []
            for hh in range(2):
                h = 2 * i + hh
                col = d * M_HEADS + h
                pc = jnp.broadcast_to(s["p2_col"][:, col:col + 1], (L, L))
                dm = jnp.exp2(jnp.where(s["tri"], pc - s["q2_row"][h:h + 1, :], NEG_BIG))
                m_h = s["g_mats"][g] * dm
                ce_h = s["c_all"] * jnp.broadcast_to(s["ep_col"][:, col:col + 1], (L, L))
                lhs_rows.append(jnp.concatenate([m_h, ce_h], axis=1).astype(BF16))
                bw = (btg * s["w_row"][h:h + 1, :]).astype(BF16)
                bw_rows.append(jnp.concatenate([bw, zeros_s], axis=1))
            lhs = jnp.concatenate(lhs_rows + bw_rows, axis=0)
            x_pair = x_bf[:, i * 128:(i + 1) * 128]
            st_pair = st_scr[i]
            st_bf = st_pair.astype(BF16)
            rhs = jnp.concatenate([x_pair] + ([st_bf, zeros_s] if g == 0 else [zeros_s, st_bf]), axis=0)
            res = jnp.dot(lhs, rhs, preferred_element_type=F32)
            y_pair = jnp.where(lane_lo, res[0:L], res[L:2 * L])
            ds = jnp.where(lane_lo_s, res[2 * L:2 * L + M_STATE], res[2 * L + M_STATE:2 * L + 2 * M_STATE])
            e0 = jnp.broadcast_to(s["etot"][2 * i:2 * i + 1, :], (M_STATE, 2 * M_HEADDIM))
            e1 = jnp.broadcast_to(s["etot"][2 * i + 1:2 * i + 2, :], (M_STATE, 2 * M_HEADDIM))
            st_scr[i] = jnp.where(lane_lo_s, e0, e1) * st_pair + ds
            outs.append(y_pair)
        return jnp.concatenate(outs, axis=1)

    def block(d, x_bf, bc, dt_raw):
        order = range(cps) if d == 0 else range(cps - 1, -1, -1)
        su = {ci: setup(d, bc[ci * L:(ci + 1) * L], dt_raw[ci * L:(ci + 1) * L]) for ci in order}
        ys = {ci: pairs(d, su[ci], x_bf[ci * L:(ci + 1) * L]) for ci in order}
        return jnp.concatenate([ys[ci] for ci in range(cps)], axis=0)

    @pl.when(j == 0)
    def _():
        load_state(0)

    @pl.when(j == ns)
    def _():
        load_state(1)

    @pl.when(fwd)
    def _():
        x = xs_ref[...].astype(F32)
        prow = jnp.where(c > 0, xsp_ref[...].astype(F32)[15:16, :], 0.0)
        nrow = jnp.where(c < ns - 1, xsn_ref[...].astype(F32)[0:1, :], 0.0)
        xc = conv_silu(x, prow, nrow, cwx_ref, cbx_ref).astype(BF16)
        xc_scr[pl.ds(row0, LB), :] = xc
        m = mi_ref[...]
        bcx = m[:, 0:256]
        prow_m = jnp.where(c > 0, mip_ref[7:8, 0:256], 0.0)
        nrow_m = jnp.where(c < ns - 1, min_ref[0:1, 0:256], 0.0)
        bc = conv_silu(bcx, prow_m, nrow_m, cwm_ref, cbm_ref)
        bc_scr[pl.ds(row0, LB), :] = bc
        yf_scr[pl.ds(row0, LB), :] = block(0, xc, bc, m[:, 256:384])

    @pl.when(j == ns - 1)
    def _():
        store_state(0)

    @pl.when(jnp.logical_not(fwd))
    def _():
        xc = xc_scr[pl.ds(row0, LB), :]
        bc = bc_scr[pl.ds(row0, LB), :]
        yb = block(1, xc, bc, mi_ref[:, 256:384])
        y = yf_scr[pl.ds(row0, LB), :] + yb + dsum_ref[...] * xc.astype(F32)
        y = y * _silu(z_ref[...].astype(F32))
        ms = jnp.mean(y * y, axis=-1, keepdims=True)
        y_ref[...] = (y * lax.rsqrt(ms + EPS) * nw_ref[...]).astype(y_ref.dtype)

    @pl.when(j == 2 * ns - 1)
    def _():
        store_state(1)


def _ssd(main, misc, nseq, T, cwx, cbx, cwm, cbm, dtb, alog, dsum, nw, init, cps=SSD_CHUNKS_PER_STEP):
    R = nseq * T
    nc = T // M_CHUNK
    L = cps * M_CHUNK
    ns = nc // cps
    nc, full_nc = ns, nc
    has_init = init is not None

    def cidx(j):
        return jnp.where(j < nc, j, 2 * nc - 1 - j)

    def oidx(j):
        return jnp.where(j < nc, nc - 1, 2 * nc - 1 - j)

    in_specs = [
        pl.BlockSpec((L, D_MODEL), lambda b, j: (b * nc + cidx(j), SEG_XS)),
        pl.BlockSpec((16, D_MODEL), lambda b, j: (jnp.maximum((b * nc + cidx(j)) * (L // 16) - 1, 0), SEG_XS)),
        pl.BlockSpec((16, D_MODEL),
                     lambda b, j: (jnp.minimum((b * nc + cidx(j) + 1) * (L // 16), R // 16 - 1), SEG_XS)),
        pl.BlockSpec((L, MISC_W), lambda b, j: (b * nc + cidx(j), 0)),
        pl.BlockSpec((8, MISC_W), lambda b, j: (jnp.maximum((b * nc + cidx(j)) * (L // 8) - 1, 0), 0)),
        pl.BlockSpec((8, MISC_W), lambda b, j: (jnp.minimum((b * nc + cidx(j) + 1) * (L // 8), R // 8 - 1), 0)),
        pl.BlockSpec((L, D_MODEL), lambda b, j: (b * nc + oidx(j), SEG_Z)),
        pl.BlockSpec((3, D_MODEL), lambda b, j: (0, 0)),
        pl.BlockSpec((1, D_MODEL), lambda b, j: (0, 0)),
        pl.BlockSpec((3, 256), lambda b, j: (0, 0)),
        pl.BlockSpec((1, 256), lambda b, j: (0, 0)),
        pl.BlockSpec((1, 128), lambda b, j: (0, 0)),
        pl.BlockSpec((1, 128), lambda b, j: (0, 0)),
        pl.BlockSpec((1, D_MODEL), lambda b, j: (0, 0)),
        pl.BlockSpec((1, D_MODEL), lambda b, j: (0, 0)),
    ]
    args = [main, main, main, misc, misc, misc, main, cwx, cbx, cwm, cbm, dtb, alog, dsum, nw]
    if has_init:
        in_specs.append(pl.BlockSpec((None, 2, M_PAIRS, M_STATE, 128), lambda b, j: (b, 0, 0, 0, 0)))
        args.append(init)
    return pl.pallas_call(
        functools.partial(_ssd_kernel, nc=full_nc, cps=cps, has_init=has_init),
        out_shape=[jax.ShapeDtypeStruct((R, D_MODEL), BF16),
                   jax.ShapeDtypeStruct((nseq, 2, M_HEADS * M_HEADDIM, M_STATE), F32)],
        grid=(nseq, 2 * nc),
        in_specs=in_specs,
        out_specs=[pl.BlockSpec((L, D_MODEL), lambda b, j: (b * nc + oidx(j), 0)),
                   pl.BlockSpec((None, 2, M_HEADS * M_HEADDIM, M_STATE), lambda b, j: (b, 0, 0, 0))],
        scratch_shapes=[pltpu.VMEM((T, D_MODEL), BF16), pltpu.VMEM((T, 256), F32),
                        pltpu.VMEM((T, D_MODEL), F32), pltpu.VMEM((M_PAIRS, M_STATE, 128), F32)],
        compiler_params=_cparams(("parallel", "arbitrary")),
        name="ssd",
    )(*args)


def _attn_lat_kernel(q_ref, k_ref, v_ref, gb_ref, kc_ref, vc_ref, lv_ref, hw_ref, o_ref, vt_scr, vct_scr,
                     *, lam_init, kb):
    tq = q_ref.shape[0]
    T = k_ref.shape[0]
    lc = kc_ref.shape[0]
    qi = pl.program_id(2)

    @pl.when(qi == 0)
    def _():
        for c0 in range(0, T, 128):
            vt_scr[0:DA_VD, c0:c0 + 128] = v_ref[c0:c0 + 128, :].astype(F32).T.astype(BF16)
        vt_scr[DA_VD:DA_VD + 16, :] = jnp.ones((16, T), BF16)
        for c0 in range(0, lc, 128):
            vct_scr[0:DA_VD, c0:c0 + 128] = vc_ref[c0:c0 + 128, :].T.astype(BF16)
        vct_scr[DA_VD:DA_VD + 16, :] = jnp.ones((16, lc), BF16)

    lv = lv_ref[...]
    lam = (jnp.exp(jnp.sum(lv[0:1, :] * lv[1:2, :], axis=-1, keepdims=True))
           - jnp.exp(jnp.sum(lv[2:3, :] * lv[3:4, :], axis=-1, keepdims=True)) + lam_init)
    sq = ATT_SUBQ
    lane = lax.broadcasted_iota(jnp.int32, (sq, 2 * DA_QK), 1)
    dn_t = (((1,), (1,)), ((), ()))

    q2s = []
    for c in range(tq // sq):
        q = q_ref[c * sq:(c + 1) * sq, :].astype(F32) * (DA_QK ** -0.5 * math.log2(math.e))
        q2s.append(jnp.concatenate([jnp.where(lane < DA_QK, q, 0.0), jnp.where(lane < DA_QK, 0.0, q)],
                                   axis=0).astype(BF16))

    def scores(k_blk, q2):
        s = lax.dot_general(k_blk, q2, dn_t, preferred_element_type=F32)
        return s, jnp.max(s, axis=0, keepdims=True)

    def accumulate(s, bm, vt_blk, state):
        m_new = bm if state is None else jnp.maximum(state[0], bm)
        p = jnp.exp2(s - m_new).astype(BF16)
        pv = jnp.dot(vt_blk, p, preferred_element_type=F32)
        if state is None:
            return m_new, pv
        return m_new, jnp.exp2(state[0] - m_new) * state[1] + pv

    kcb = kc_ref[...].astype(BF16)
    cur = [scores(kcb, q2) for q2 in q2s]
    cur_vt = vct_scr[...]
    states = [None] * len(q2s)
    for k0 in range(0, T, kb):
        k_blk = k_ref[k0:k0 + kb, :]
        nxt = [scores(k_blk, q2) for q2 in q2s]
        states = [accumulate(cu[0], cu[1], cur_vt, st) for cu, st in zip(cur, states)]
        cur, cur_vt = nxt, vt_scr[:, k0:k0 + kb]
    states = [accumulate(cu[0], cu[1], cur_vt, st) for cu, st in zip(cur, states)]
    for c, (_, acc) in enumerate(states):
        rs = slice(c * sq, (c + 1) * sq)
        r = 1.0 / acc[DA_VD:DA_VD + 1, :]
        o_t = acc[0:DA_VD, 0:sq] * r[:, 0:sq] - acc[0:DA_VD, sq:2 * sq] * (r[:, sq:2 * sq] * lam)
        o = o_t.T
        ms = jnp.mean(o * o, axis=-1, keepdims=True)
        o = o * lax.rsqrt(ms + EPS) * hw_ref[...] * (1.0 - lam_init)
        o_ref[rs, :] = (o * _silu(gb_ref[rs, :].astype(F32))).astype(o_ref.dtype)


def _attn_ctx_kernel(q_ref, k_ref, v_ref, gb_ref, lv_ref, hw_ref, o_ref, *, lam_init):
    T = q_ref.shape[0]
    lv = lv_ref[...]
    lam = (jnp.exp(jnp.sum(lv[0:1, :] * lv[1:2, :], axis=-1, keepdims=True))
           - jnp.exp(jnp.sum(lv[2:3, :] * lv[3:4, :], axis=-1, keepdims=True)) + lam_init)
    lane = lax.broadcasted_iota(jnp.int32, (T, 2 * DA_QK), 1)
    dn_t = (((1,), (1,)), ((), ()))
    ones = jnp.ones((16, T), BF16)
    for hh in range(DA_HEADS):
        cs = slice(hh * 128, (hh + 1) * 128)
        q = q_ref[:, cs].astype(F32) * (DA_QK ** -0.5 * math.log2(math.e))
        q2 = jnp.concatenate([jnp.where(lane < DA_QK, q, 0.0), jnp.where(lane < DA_QK, 0.0, q)],
                             axis=0).astype(BF16)
        v = v_ref[:, cs].astype(F32)
        vt = jnp.concatenate([v[c0:c0 + 128, :].T for c0 in range(0, T, 128)], axis=1).astype(BF16)
        vt1 = jnp.concatenate([vt, ones], axis=0)
        s = lax.dot_general(k_ref[:, cs], q2, dn_t, preferred_element_type=F32)
        p = jnp.exp2(s - jnp.max(s, axis=0, keepdims=True)).astype(BF16)
        acc = jnp.dot(vt1, p, preferred_element_type=F32)
        r = 1.0 / acc[DA_VD:DA_VD + 1, :]
        o_t = acc[0:DA_VD, 0:T] * r[:, 0:T] - acc[0:DA_VD, T:2 * T] * (r[:, T:2 * T] * lam)
        o = jnp.concatenate([o_t[:, c0:c0 + 128].T for c0 in range(0, T, 128)], axis=0)
        ms = jnp.mean(o * o, axis=-1, keepdims=True)
        o = o * lax.rsqrt(ms + EPS) * hw_ref[...] * (1.0 - lam_init)
        o_ref[:, cs] = (o * _silu(gb_ref[:, cs].astype(F32))).astype(o_ref.dtype)


def _attention_ctx(main, nseq, T, lam_vecs, head_w, lam_init):
    return pl.pallas_call(
        functools.partial(_attn_ctx_kernel, lam_init=lam_init),
        out_shape=jax.ShapeDtypeStruct((nseq * T, D_MODEL), BF16),
        grid=(nseq,),
        in_specs=[
            pl.BlockSpec((T, D_MODEL), lambda b: (b, SEG_Q)),
            pl.BlockSpec((T, D_MODEL), lambda b: (b, SEG_K)),
            pl.BlockSpec((T, D_MODEL), lambda b: (b, SEG_V)),
            pl.BlockSpec((T, D_MODEL), lambda b: (b, SEG_GB)),
            pl.BlockSpec((4, DA_QK), lambda b: (0, 0)),
            pl.BlockSpec((1, DA_VD), lambda b: (0, 0)),
        ],
        out_specs=pl.BlockSpec((T, D_MODEL), lambda b: (b, 0)),
        compiler_params=_cparams(("parallel",)),
        name="diff_attn_ctx",
    )(main, main, main, main, lam_vecs, head_w)


def _attention_latent(main, nseq, T, tq, kb, lam_vecs, head_w, lam_init, ck, cv):
    R = nseq * T
    nq = T // tq
    lc = ck.shape[1]
    spb = D_MODEL // 128
    return pl.pallas_call(
        functools.partial(_attn_lat_kernel, lam_init=lam_init, kb=kb),
        out_shape=jax.ShapeDtypeStruct((R, D_MODEL), BF16),
        grid=(nseq, DA_HEADS, nq),
        in_specs=[
            pl.BlockSpec((tq, 128), lambda b, g, qi: (b * nq + qi, SEG_Q * spb + g)),
            pl.BlockSpec((T, 128), lambda b, g, qi: (b, SEG_K * spb + g)),
            pl.BlockSpec((T, 128), lambda b, g, qi: (b, SEG_V * spb + g)),
            pl.BlockSpec((tq, 128), lambda b, g, qi: (b * nq + qi, SEG_GB * spb + g)),
            pl.BlockSpec((None, lc, 128), lambda b, g, qi: (b, 0, g)),
            pl.BlockSpec((None, lc, 128), lambda b, g, qi: (b, 0, g)),
            pl.BlockSpec((4, DA_QK), lambda b, g, qi: (0, 0)),
            pl.BlockSpec((1, DA_VD), lambda b, g, qi: (0, 0)),
        ],
        out_specs=pl.BlockSpec((tq, 128), lambda b, g, qi: (b * nq + qi, g)),
        scratch_shapes=[pltpu.VMEM((DA_VD + 16, T), BF16), pltpu.VMEM((DA_VD + 16, lc), BF16)],
        compiler_params=_cparams(("parallel", "parallel", "arbitrary")),
        name="diff_attn_lat",
    )(main, main, main, main, ck, cv, lam_vecs, head_w)


def _sgmlp_kernel(u_ref, sv_ref, gc_ref, vw_ref, ws_ref, bias_ref, o_ref):
    tm = u_ref.shape[0]
    u = _gelu_tanh(u_ref[...].astype(F32))
    v = _gelu_tanh(sv_ref[...].astype(F32))
    vc = v - jnp.mean(v, axis=-1, keepdims=True)
    v = vc * lax.rsqrt(jnp.mean(vc * vc, axis=-1, keepdims=True) + EPS) * vw_ref[...]
    vb = v.astype(BF16)
    gate = _silu(gc_ref[...].astype(F32))
    for ci in range(tm // SG_CHUNK):
        rs = slice(ci * SG_CHUNK, (ci + 1) * SG_CHUNK)
        cols = []
        for g in range(SG_GROUPS):
            cols.append(jnp.dot(ws_ref[g], vb[rs, g * 128:(g + 1) * 128], preferred_element_type=F32))
        mixed = jnp.concatenate(cols, axis=1) + bias_ref[...]
        o_ref[rs, :] = (u[rs] * mixed * gate[rs]).astype(o_ref.dtype)


def _sgmlp(main, vnorm_w, ws_bf, bias_exp, l, tm=256):
    R = main.shape[0]
    return pl.pallas_call(
        _sgmlp_kernel,
        out_shape=jax.ShapeDtypeStruct((R, D_MODEL), BF16),
        grid=(R // tm,),
        in_specs=[
            pl.BlockSpec((tm, D_MODEL), lambda i: (i, SEG_U)),
            pl.BlockSpec((tm, D_MODEL), lambda i: (i, SEG_SV)),
            pl.BlockSpec((tm, D_MODEL), lambda i: (i, SEG_GC)),
            pl.BlockSpec((1, D_MODEL), lambda i: (0, 0)),
            pl.BlockSpec((None, SG_GROUPS, SG_CHUNK, SG_CHUNK), lambda i: (l, 0, 0, 0)),
            pl.BlockSpec((None, SG_CHUNK, D_MODEL), lambda i: (l, 0, 0)),
        ],
        out_specs=pl.BlockSpec((tm, D_MODEL), lambda i: (i, 0)),
        compiler_params=_cparams(("parallel",)),
        name="sgmlp",
    )(main, main, main, vnorm_w, ws_bf, bias_exp)


def _merge_kernel(ya_ref, yb_ref, yc_ref, ga_ref, gb_ref, gc_ref, wb_ref, wo_ref, pw_ref, mod_ref, x_ref, o_ref):
    def branch(y_ref, g_ref, i):
        p = jnp.dot(y_ref[...], wb_ref[i], preferred_element_type=F32)
        return _sigmoid(g_ref[...].astype(F32)) * p

    merged = branch(ya_ref, ga_ref, 0) + branch(yb_ref, gb_ref, 1) + branch(yc_ref, gc_ref, 2)
    o = jnp.dot(merged.astype(BF16), wo_ref[...], preferred_element_type=F32)
    ms = jnp.mean(o * o, axis=-1, keepdims=True)
    o = o * lax.rsqrt(ms + EPS) * pw_ref[...]
    o_ref[...] = x_ref[...] + mod_ref[2:3, :] * o


def _merge(ya, yb, yc, main, wb_bf, wo_bf, l, post_w, mod, rows_per_mod, x, tm=512):
    R = x.shape[0]
    row = lambda i: (i, 0)
    return pl.pallas_call(
        _merge_kernel,
        out_shape=jax.ShapeDtypeStruct((R, D_MODEL), F32),
        grid=(R // tm,),
        in_specs=[
            pl.BlockSpec((tm, D_MODEL), row),
            pl.BlockSpec((tm, D_MODEL), row),
            pl.BlockSpec((tm, D_MODEL), row),
            pl.BlockSpec((tm, D_MODEL), lambda i: (i, SEG_MGA)),
            pl.BlockSpec((tm, D_MODEL), lambda i: (i, SEG_MGB)),
            pl.BlockSpec((tm, D_MODEL), lambda i: (i, SEG_MGC)),
            pl.BlockSpec((None, 3, D_MODEL, D_MODEL), lambda i: (l, 0, 0, 0)),
            pl.BlockSpec((None, D_MODEL, D_MODEL), lambda i: (l, 0, 0)),
            pl.BlockSpec((1, D_MODEL), lambda i: (0, 0)),
            pl.BlockSpec((None, 3, D_MODEL), lambda i: ((i * tm) // rows_per_mod, 0, 0)),
            pl.BlockSpec((tm, D_MODEL), row),
        ],
        out_specs=pl.BlockSpec((tm, D_MODEL), row),
        compiler_params=_cparams(("parallel",)),
        name="merge_out",
    )(ya, yb, yc, main, main, main, wb_bf, wo_bf, post_w, mod, x)


def _rope_tables(n_tokens):
    n_rows = n_tokens // GRID_W
    rows = jnp.repeat(jnp.arange(n_rows, dtype=F32), GRID_W)
    cols = jnp.tile(jnp.arange(GRID_W, dtype=F32), n_rows)
    n_freq = DA_QK // 4
    inv = ROPE_BASE ** (-jnp.arange(n_freq, dtype=F32) / n_freq)
    ang = jnp.concatenate([rows[:, None] * inv, cols[:, None] * inv], -1)
    cos, sin = jnp.cos(ang), jnp.sin(ang)
    return jnp.tile(cos, (1, 4)), jnp.concatenate([-sin, sin, -sin, sin], axis=1)


def _state_to_pairs(s):
    lead = s.shape[:-3]
    s = s.reshape(lead + (M_PAIRS, 2, M_HEADDIM, M_STATE))
    s = jnp.moveaxis(s, -1, -3)
    return s.reshape(lead + (M_PAIRS, M_STATE, 2 * M_HEADDIM))


def kernel(x_prompt, x_sample, cache_k, cache_v, state_ssm, c, c_ctx, pre_norm_w, post_norm_w, w_mod, b_mod, w_in,
           m_conv_w, m_conv_b, m_A_log, m_dt_bias, m_D, m_norm_w, da_lambda, da_head_norm_w, sg_vnorm_w,
           sg_spatial_w, sg_spatial_b, w_branch, w_out):
    depth = w_in.shape[0]
    nb, seq, _ = x_prompt.shape
    db, dseq, _ = x_sample.shape
    past = cache_k.shape[2]

    w_main, w_misc = _prep_in_weights(w_in)
    wb_bf = w_branch.astype(BF16)
    wo_bf = w_out.astype(BF16)
    ws_bf = sg_spatial_w.astype(BF16)
    bias_exp = jnp.repeat(jnp.swapaxes(sg_spatial_b, 1, 2), D_MODEL // SG_GROUPS, axis=2)
    cw = jnp.swapaxes(m_conv_w, 1, 2)
    dtb = jnp.pad(m_dt_bias.reshape(depth, 1, 2 * M_HEADS), ((0, 0), (0, 0), (0, 128 - 2 * M_HEADS)))
    alog = jnp.pad(m_A_log.reshape(depth, 1, 2 * M_HEADS), ((0, 0), (0, 0), (0, 128 - 2 * M_HEADS)))
    dsum = jnp.repeat(m_D[:, 0] + m_D[:, 1], M_HEADDIM, axis=1).reshape(depth, 1, D_MODEL)

    cvec = jnp.concatenate([c_ctx[None, :], c, jnp.zeros((8 - 1 - db, D_MODEL), F32)], axis=0)
    mods = _modulation(cvec, w_mod, b_mod).reshape(depth, 8, 3, D_MODEL)

    rope_tabs = _rope_tables(dseq)
    init_pairs = _state_to_pairs(state_ssm)
    ck = cache_k.reshape(db, depth, past, D_MODEL)
    cv = cache_v.reshape(db, depth, past, D_MODEL)

    xp = x_prompt.reshape(nb * seq, D_MODEL)
    xs = x_sample.reshape(db * dseq, D_MODEL)
    ks_out, vs_out, ss_out = [], [], []

    def layer(x, l, nseq, T, mod, rows_per_mod, latent):
        lam_init = 0.8 - 0.6 * math.exp(-0.3 * l)
        outs = _inproj(x, mod, pre_norm_w[l][None], w_main, w_misc, l, rows_per_mod,
                       rope_tabs if latent else None, emit_kv=not latent)
        main, misc = outs[0], outs[1]
        ya, st = _ssd(main, misc, nseq, T, cw[l, :, 0:D_MODEL], m_conv_b[l][None, 0:D_MODEL],
                      cw[l, :, D_MODEL:], m_conv_b[l][None, D_MODEL:], dtb[l], alog[l], dsum[l],
                      m_norm_w[l][None], init_pairs[:, l] if latent else None,
                      cps=min(SSD_CHUNKS_PER_STEP, T // M_CHUNK))
        if latent:
            yb = _attention_latent(main, nseq, T, 512, 256, da_lambda[l], da_head_norm_w[l][None], lam_init,
                                   ck[:, l], cv[:, l])
        else:
            yb = _attention_ctx(main, nseq, T, da_lambda[l], da_head_norm_w[l][None], lam_init)
        yc = _sgmlp(main, sg_vnorm_w[l][None], ws_bf, bias_exp, l)
        x_new = _merge(ya, yb, yc, main, wb_bf, wo_bf, l, post_norm_w[l][None], mod, rows_per_mod, x)
        return x_new, outs[2:], st

    for l in range(depth):
        xp, kv, st = layer(xp, l, nb, seq, mods[l, 0:1], nb * seq, False)
        ks_out.append(kv[0][:, 0:D_MODEL].reshape(nb, seq, DA_HEADS, 2 * DA_QK))
        vs_out.append(kv[0][:, D_MODEL:].reshape(nb, seq, DA_HEADS, DA_VD))
        ss_out.append(st.reshape(nb, 2, M_HEADS, M_HEADDIM, M_STATE))
        xs, _, _ = layer(xs, l, db, dseq, mods[l, 1:1 + db], dseq, True)

    return (xp.reshape(nb, seq, D_MODEL), xs.reshape(db, dseq, D_MODEL),
            jnp.stack(ks_out, 1), jnp.stack(vs_out, 1), jnp.stack(ss_out, 1))
```

```python
import functools
import math

import jax
import jax.numpy as jnp
from jax import lax
from jax.experimental import pallas as pl
from jax.experimental.pallas import tpu as pltpu

F32 = jnp.float32
BF16 = jnp.bfloat16

D_MODEL = 1024
EPS = 1e-6
GRID_W = 64
ROPE_BASE = 10000.0
M_HEADS = 16
M_HEADDIM = 64
M_STATE = 64
M_GROUPS = 2
M_CHUNK = 128
M_PAIRS = M_HEADS // 2
DA_HEADS = 8
DA_QK = 64
DA_VD = 128
SG_GROUPS = 8
SG_CHUNK = 128

SEG_Z, SEG_XS, SEG_Q, SEG_K, SEG_V, SEG_GB, SEG_U, SEG_SV, SEG_GC, SEG_MGA, SEG_MGB, SEG_MGC = range(12)
N_SEG = 12
LATENT_SEG_SHIFT = 4
MISC_W = 384

VMEM_LIMIT = 56 * 1024 * 1024
NEG_BIG = -1e30
ATT_SUBQ = 128
ATT_TQ = 512
ATT_KB = 512
SSD_CHUNKS_PER_STEP = 4
INPROJ_TM = 1024
INPROJ_SUBN = 512


def _silu(x):
    return x * (1.0 / (1.0 + jnp.exp(-x)))


def _sigmoid(x):
    return 1.0 / (1.0 + jnp.exp(-x))


def _gelu_tanh(x):
    return 0.5 * x * (1.0 + jnp.tanh(math.sqrt(2.0 / math.pi) * (x + 0.044715 * (x * x * x))))


def _cparams(sem):
    return pltpu.CompilerParams(dimension_semantics=sem, vmem_limit_bytes=VMEM_LIMIT)


def _mod_kernel(c_ref, w_ref, b_ref, o_ref):
    c = c_ref[...]
    s = _silu(c).astype(BF16)
    o_ref[...] = jnp.dot(s, w_ref[...].astype(BF16), preferred_element_type=F32) + b_ref[...]


def _modulation(cvec, w_mod, b_mod):
    depth = w_mod.shape[0]
    nt = 3
    return pl.pallas_call(
        _mod_kernel,
        out_shape=jax.ShapeDtypeStruct((depth, 8, 3 * D_MODEL), F32),
        grid=(depth, nt),
        in_specs=[
            pl.BlockSpec((8, D_MODEL), lambda l, j: (0, 0)),
            pl.BlockSpec((None, D_MODEL, D_MODEL), lambda l, j: (l, 0, j)),
            pl.BlockSpec((None, 1, D_MODEL), lambda l, j: (l, 0, j)),
        ],
        out_specs=pl.BlockSpec((None, 8, D_MODEL), lambda l, j: (l, 0, j)),
        compiler_params=_cparams(("arbitrary", "arbitrary")),
        name="modulation",
    )(cvec, w_mod, b_mod.reshape(depth, 1, 3 * D_MODEL))


W_IN_COLS = 12576
W_XS_END = 2048
W_MISC_END = 2336


def _wprep_head_kernel(w_ref, wm_ref, main_ref, misc_ref):
    main_ref[...] = w_ref[...].T.astype(BF16)

    @pl.when(pl.program_id(1) == 0)
    def _():
        n = W_MISC_END - W_XS_END
        wm = jnp.concatenate([wm_ref[...], jnp.zeros((MISC_W - n, D_MODEL), F32)], axis=0)
        misc_ref[...] = wm.T.astype(BF16)


def _wprep_tail_kernel(w_ref, main_in_ref, main_ref):
    del main_in_ref
    main_ref[...] = w_ref[...].T.astype(BF16)


def _prep_in_weights(w_in_t):
    depth = w_in_t.shape[0]
    n_head = W_XS_END // D_MODEL
    main_sds = jax.ShapeDtypeStruct((depth, D_MODEL, N_SEG * D_MODEL), BF16)
    w2d = w_in_t.reshape(depth * W_IN_COLS, D_MODEL)
    main, misc = pl.pallas_call(
        _wprep_head_kernel,
        out_shape=[main_sds, jax.ShapeDtypeStruct((depth, D_MODEL, MISC_W), BF16)],
        grid=(depth, n_head),
        in_specs=[pl.BlockSpec((None, D_MODEL, D_MODEL), lambda l, j: (l, j, 0)),
                  pl.BlockSpec((pl.Element(W_MISC_END - W_XS_END), pl.Element(D_MODEL)),
                               lambda l, j: (pl.multiple_of(l * W_IN_COLS + W_XS_END, 32), 0))],
        out_specs=[pl.BlockSpec((None, D_MODEL, D_MODEL), lambda l, j: (l, 0, j)),
                   pl.BlockSpec((None, D_MODEL, MISC_W), lambda l, j: (l, 0, 0))],
        compiler_params=_cparams(("parallel", "arbitrary")),
        name="w_prep_head",
    )(w_in_t, w2d)
    main = pl.pallas_call(
        _wprep_tail_kernel,
        out_shape=main_sds,
        grid=(depth, N_SEG - n_head),
        in_specs=[pl.BlockSpec((pl.Element(D_MODEL), pl.Element(D_MODEL)),
                               lambda l, j: (pl.multiple_of(l * W_IN_COLS + W_MISC_END + j * D_MODEL, 32), 0)),
                  pl.BlockSpec(memory_space=pl.ANY)],
        out_specs=pl.BlockSpec((None, D_MODEL, D_MODEL), lambda l, j: (l, 0, j + n_head)),
        input_output_aliases={1: 0},
        compiler_params=_cparams(("parallel", "parallel")),
        name="w_prep_tail",
    )(w2d, main)
    return main, misc


def _inproj_kernel(*refs, rope, emit_kv):
    it = iter(refs)
    x_ref, mod_ref, prew_ref, w_ref, wm_ref = next(it), next(it), next(it), next(it), next(it)
    if rope:
        cos_ref, sin_ref = next(it), next(it)
    main_ref, misc_ref = next(it), next(it)
    if emit_kv:
        kf_ref, vf_ref = next(it), next(it)
    if rope:
        hm_ref = next(it)
    h_scr = next(it)

    j = pl.program_id(1)
    tm = x_ref.shape[0]

    @pl.when(j == 0)
    def _():
        x = x_ref[...]
        ms = jnp.mean(x * x, axis=-1, keepdims=True)
        y = x * lax.rsqrt(ms + EPS) * prew_ref[...]
        h = y * (1.0 + mod_ref[1:2, :]) + mod_ref[0:1, :]
        hb = h.astype(BF16)
        h_scr[...] = hb
        misc_ref[...] = jnp.dot(hb, wm_ref[...], preferred_element_type=F32)

    h = h_scr[...]
    sub = INPROJ_SUBN
    for c0 in range(0, D_MODEL, sub):
        cs = slice(c0, c0 + sub)
        acc = jnp.dot(h, w_ref[:, cs], preferred_element_type=F32)

        if rope:
            is_rope = jnp.logical_or(j == SEG_Q, j == SEG_K)
            is_plain_hm = jnp.logical_or(j == SEG_V, j == SEG_GB)

            def store_heads(val):
                vb = val.astype(hm_ref.dtype)
                for hh in range(sub // 128):
                    hm_ref[c0 // 128 + hh] = vb[:, hh * 128:(hh + 1) * 128]

            @pl.when(is_rope)
            def _():
                cos = jnp.concatenate([cos_ref[...]] * (sub // 128), axis=1)
                sin = jnp.concatenate([sin_ref[...]] * (sub // 128), axis=1)
                lane = lax.broadcasted_iota(jnp.int32, (tm, sub), 1)
                first_half = (lane & (DA_QK - 1)) < (DA_QK // 2)
                rot = jnp.where(first_half, pltpu.roll(acc, sub - DA_QK // 2, 1), pltpu.roll(acc, DA_QK // 2, 1))
                store_heads(acc * cos + rot * sin)

            @pl.when(is_plain_hm)
            def _():
                store_heads(acc)

            @pl.when(jnp.logical_not(jnp.logical_or(is_rope, is_plain_hm)))
            def _():
                main_ref[:, cs] = acc.astype(main_ref.dtype)
        else:
            main_ref[:, cs] = acc.astype(main_ref.dtype)

        if emit_kv:
            @pl.when(j == SEG_K)
            def _():
                kf_ref[:, cs] = acc

            @pl.when(j == SEG_V)
            def _():
                vf_ref[:, cs] = acc


def _inproj(x, mod, pre_w, w_main, w_misc, l, rows_per_mod, rope_tabs, emit_kv, tm=INPROJ_TM):
    R = x.shape[0]
    ni = R // tm
    rope = rope_tabs is not None
    in_specs = [
        pl.BlockSpec((tm, D_MODEL), lambda i, j: (i, 0)),
        pl.BlockSpec((None, 3, D_MODEL), lambda i, j: ((i * tm) // rows_per_mod, 0, 0)),
        pl.BlockSpec((1, D_MODEL), lambda i, j: (0, 0)),
        pl.BlockSpec((None, D_MODEL, D_MODEL), lambda i, j: (l, 0, j)),
        pl.BlockSpec((None, D_MODEL, MISC_W), lambda i, j: (l, 0, 0)),
    ]
    args = [x, mod, pre_w, w_main, w_misc]
    if rope:
        cos, sin = rope_tabs
        nt = cos.shape[0] // tm
        in_specs += [pl.BlockSpec((tm, 128), lambda i, j: (i % nt, 0)),
                     pl.BlockSpec((tm, 128), lambda i, j: (i % nt, 0))]
        args += [cos, sin]
    if rope:
        n_main = N_SEG - LATENT_SEG_SHIFT
        main_idx = lambda i, j: (i, jnp.where(j < SEG_Q, j, jnp.where(j <= SEG_GB, SEG_XS, j - LATENT_SEG_SHIFT)))
    else:
        n_main = N_SEG
        main_idx = lambda i, j: (i, j)
    out_shape = [jax.ShapeDtypeStruct((R, n_main * D_MODEL), BF16), jax.ShapeDtypeStruct((R, MISC_W), F32)]
    out_specs = [pl.BlockSpec((tm, D_MODEL), main_idx), pl.BlockSpec((tm, MISC_W), lambda i, j: (i, 0))]
    if emit_kv:
        out_shape += [jax.ShapeDtypeStruct((R, D_MODEL), F32)] * 2
        out_specs += [pl.BlockSpec((tm, D_MODEL), lambda i, j: (i, 0))] * 2
    if rope:
        T = cos.shape[0]
        out_shape.append(jax.ShapeDtypeStruct((4, R // T, DA_HEADS, T, 128), BF16))
        out_specs.append(pl.BlockSpec(
            (None, None, DA_HEADS, tm, 128),
            lambda i, j: (jnp.clip(j - SEG_Q, 0, 3), (i * tm) // T, 0, ((i * tm) % T) // tm, 0)))
    return pl.pallas_call(
        functools.partial(_inproj_kernel, rope=rope, emit_kv=emit_kv),
        out_shape=out_shape,
        grid=(ni, N_SEG),
        in_specs=in_specs,
        out_specs=out_specs,
        scratch_shapes=[pltpu.VMEM((tm, D_MODEL), BF16)],
        compiler_params=_cparams(("parallel", "arbitrary")),
        name="inproj",
    )(*args)


def _split3(a):
    hi = a.astype(BF16)
    r1 = a - hi.astype(F32)
    mid = r1.astype(BF16)
    lo = (r1 - mid.astype(F32)).astype(BF16)
    return hi, mid, lo


def _ssd_kernel(*refs, nc, cps, has_init):
    it = iter(refs)
    xs_ref, xsp_ref, xsn_ref = next(it), next(it), next(it)
    mi_ref, mip_ref, min_ref = next(it), next(it), next(it)
    z_ref = next(it)
    cwx_ref, cbx_ref, cwm_ref, cbm_ref = next(it), next(it), next(it), next(it)
    dtb_ref, alog_ref, dsum_ref, nw_ref = next(it), next(it), next(it), next(it)
    if has_init:
        init_ref = next(it)
    y_ref, sto_ref = next(it), next(it)
    xc_scr, bc_scr, yf_scr, st_scr = next(it), next(it), next(it), next(it)

    L = M_CHUNK
    LB = cps * L
    ns = nc // cps
    j = pl.program_id(1)
    fwd = j < ns
    c = jnp.where(fwd, j, 2 * ns - 1 - j)
    row0 = pl.multiple_of(c * LB, LB)
    LOG2E = math.log2(math.e)

    rid = lax.broadcasted_iota(jnp.int32, (L, L), 0)
    cid = lax.broadcasted_iota(jnp.int32, (L, L), 1)

    def conv_silu(x, prow, nrow, w_ref, b_ref):
        n = x.shape[1]
        r = lax.broadcasted_iota(jnp.int32, (LB, n), 0)
        xm = jnp.where(r == 0, prow, pltpu.roll(x, 1, 0))
        xp = jnp.where(r == LB - 1, nrow, pltpu.roll(x, LB - 1, 0))
        y = w_ref[0:1, :] * xm + w_ref[1:2, :] * x + w_ref[2:3, :] * xp + b_ref[...]
        return _silu(y)

    def load_state(d):
        if has_init:
            for i in range(M_PAIRS):
                st_scr[i] = init_ref[d, i]
        else:
            st_scr[...] = jnp.zeros_like(st_scr)

    def store_state(d):
        pad = jnp.zeros((2 * M_HEADDIM - M_STATE, 2 * M_HEADDIM), F32)
        for i in range(M_PAIRS):
            t = jnp.concatenate([st_scr[i], pad], axis=0).T
            sto_ref[d, i * 2 * M_HEADDIM:(i + 1) * 2 * M_HEADDIM, :] = t[:, 0:M_STATE]

    lane_lo = cid < M_HEADDIM
    lane_lo_s = lax.broadcasted_iota(jnp.int32, (M_STATE, 2 * M_HEADDIM), 1) < M_HEADDIM
    heads_per_group = M_HEADS // M_GROUPS
    zeros_s = jnp.zeros((M_STATE, 2 * M_HEADDIM), BF16)

    def setup(d, bc, dt_raw):
        tri = (cid <= rid) if d == 0 else (cid >= rid)
        tri_bf = jnp.where(tri, 1.0, 0.0).astype(BF16)
        dt = dt_raw + dtb_ref[...]
        dt = jnp.maximum(dt, 0.0) + jnp.log1p(jnp.exp(-jnp.abs(dt)))
        a = dt * (-jnp.exp(alog_ref[...]))
        a_hi, a_mid, a_lo = _split3(a)
        p_col = (jnp.dot(tri_bf, a_hi, preferred_element_type=F32)
                 + jnp.dot(tri_bf, a_mid, preferred_element_type=F32)
                 + jnp.dot(tri_bf, a_lo, preferred_element_type=F32))
        p_row = p_col.T[d * M_HEADS:(d + 1) * M_HEADS, :]
        dt_row = dt.T[d * M_HEADS:(d + 1) * M_HEADS, :]
        tot = p_row[:, L - 1:L] if d == 0 else p_row[:, 0:1]
        b_all = bc[:, 0:M_GROUPS * M_STATE]
        c_all = bc[:, M_GROUPS * M_STATE:2 * M_GROUPS * M_STATE]
        b_bf = b_all.astype(BF16)
        g_mats = []
        for g in range(M_GROUPS):
            cg = jnp.where(lane_lo if g == 0 else jnp.logical_not(lane_lo), c_all, 0.0).astype(BF16)
            g_mats.append(lax.dot_general(cg, b_bf, (((1,), (1,)), ((), ())), preferred_element_type=F32))
        return dict(
            tri=tri, g_mats=g_mats, c_all=c_all, bt_all=b_all.T,
            p2_col=p_col * LOG2E,
            q2_row=(p_row - jnp.log(dt_row)) * LOG2E,
            w_row=dt_row * jnp.exp(tot - p_row),
            etot=jnp.exp(tot),
            ep_col=jnp.exp(p_col))

    def pairs(d, s, x_bf):
        outs = []
        for i in range(M_PAIRS):
            g = (2 * i) // heads_per_group
            btg = s["bt_all"][g * M_STATE:(g + 1) * M_STATE, :]
            lhs_rows = []
            bw_rows = []
            for hh in range(2):
                h = 2 * i + hh
                col = d * M_HEADS + h
                pc = jnp.broadcast_to(s["p2_col"][:, col:col + 1], (L, L))
                dm = jnp.exp2(jnp.where(s["tri"], pc - s["q2_row"][h:h + 1, :], NEG_BIG))
                m_h = s["g_mats"][g] * dm
                ce_h = s["c_all"] * jnp.broadcast_to(s["ep_col"][:, col:col + 1], (L, L))
                lhs_rows.append(jnp.concatenate([m_h, ce_h], axis=1).astype(BF16))
                bw = (btg * s["w_row"][h:h + 1, :]).astype(BF16)
                bw_rows.append(jnp.concatenate([bw, zeros_s], axis=1))
            lhs = jnp.concatenate(lhs_rows + bw_rows, axis=0)
            x_pair = x_bf[:, i * 128:(i + 1) * 128]
            st_pair = st_scr[i]
            st_bf = st_pair.astype(BF16)
            rhs = jnp.concatenate([x_pair] + ([st_bf, zeros_s] if g == 0 else [zeros_s, st_bf]), axis=0)
            res = jnp.dot(lhs, rhs, preferred_element_type=F32)
            y_pair = jnp.where(lane_lo, res[0:L], res[L:2 * L])
            ds = jnp.where(lane_lo_s, res[2 * L:2 * L + M_STATE], res[2 * L + M_STATE:2 * L + 2 * M_STATE])
            e0 = jnp.broadcast_to(s["etot"][2 * i:2 * i + 1, :], (M_STATE, 2 * M_HEADDIM))
            e1 = jnp.broadcast_to(s["etot"][2 * i + 1:2 * i + 2, :], (M_STATE, 2 * M_HEADDIM))
            st_scr[i] = jnp.where(lane_lo_s, e0, e1) * st_pair + ds
            outs.append(y_pair)
        return jnp.concatenate(outs, axis=1)

    def block(d, x_bf, bc, dt_raw):
        order = range(cps) if d == 0 else range(cps - 1, -1, -1)
        su = {ci: setup(d, bc[ci * L:(ci + 1) * L], dt_raw[ci * L:(ci + 1) * L]) for ci in order}
        ys = {ci: pairs(d, su[ci], x_bf[ci * L:(ci + 1) * L]) for ci in order}
        return jnp.concatenate([ys[ci] for ci in range(cps)], axis=0)

    @pl.when(j == 0)
    def _():
        load_state(0)

    @pl.when(j == ns)
    def _():
        load_state(1)

    @pl.when(fwd)
    def _():
        x = xs_ref[...].astype(F32)
        prow = jnp.where(c > 0, xsp_ref[...].astype(F32)[15:16, :], 0.0)
        nrow = jnp.where(c < ns - 1, xsn_ref[...].astype(F32)[0:1, :], 0.0)
        xc = conv_silu(x, prow, nrow, cwx_ref, cbx_ref).astype(BF16)
        xc_scr[pl.ds(row0, LB), :] = xc
        m = mi_ref[...]
        bcx = m[:, 0:256]
        prow_m = jnp.where(c > 0, mip_ref[7:8, 0:256], 0.0)
        nrow_m = jnp.where(c < ns - 1, min_ref[0:1, 0:256], 0.0)
        bc = conv_silu(bcx, prow_m, nrow_m, cwm_ref, cbm_ref)
        bc_scr[pl.ds(row0, LB), :] = bc
        yf_scr[pl.ds(row0, LB), :] = block(0, xc, bc, m[:, 256:384])

    @pl.when(j == ns - 1)
    def _():
        store_state(0)

    @pl.when(jnp.logical_not(fwd))
    def _():
        xc = xc_scr[pl.ds(row0, LB), :]
        bc = bc_scr[pl.ds(row0, LB), :]
        yb = block(1, xc, bc, mi_ref[:, 256:384])
        y = yf_scr[pl.ds(row0, LB), :] + yb + dsum_ref[...] * xc.astype(F32)
        y = y * _silu(z_ref[...].astype(F32))
        ms = jnp.mean(y * y, axis=-1, keepdims=True)
        y_ref[...] = (y * lax.rsqrt(ms + EPS) * nw_ref[...]).astype(y_ref.dtype)

    @pl.when(j == 2 * ns - 1)
    def _():
        store_state(1)


def _ssd(main, misc, nseq, T, cwx, cbx, cwm, cbm, dtb, alog, dsum, nw, init, cps=SSD_CHUNKS_PER_STEP):
    R = nseq * T
    nc = T // M_CHUNK
    L = cps * M_CHUNK
    ns = nc // cps
    nc, full_nc = ns, nc
    has_init = init is not None

    def cidx(j):
        return jnp.where(j < nc, j, 2 * nc - 1 - j)

    def oidx(j):
        return jnp.where(j < nc, nc - 1, 2 * nc - 1 - j)

    in_specs = [
        pl.BlockSpec((L, D_MODEL), lambda b, j: (b * nc + cidx(j), SEG_XS)),
        pl.BlockSpec((16, D_MODEL), lambda b, j: (jnp.maximum((b * nc + cidx(j)) * (L // 16) - 1, 0), SEG_XS)),
        pl.BlockSpec((16, D_MODEL),
                     lambda b, j: (jnp.minimum((b * nc + cidx(j) + 1) * (L // 16), R // 16 - 1), SEG_XS)),
        pl.BlockSpec((L, MISC_W), lambda b, j: (b * nc + cidx(j), 0)),
        pl.BlockSpec((8, MISC_W), lambda b, j: (jnp.maximum((b * nc + cidx(j)) * (L // 8) - 1, 0), 0)),
        pl.BlockSpec((8, MISC_W), lambda b, j: (jnp.minimum((b * nc + cidx(j) + 1) * (L // 8), R // 8 - 1), 0)),
        pl.BlockSpec((L, D_MODEL), lambda b, j: (b * nc + oidx(j), SEG_Z)),
        pl.BlockSpec((3, D_MODEL), lambda b, j: (0, 0)),
        pl.BlockSpec((1, D_MODEL), lambda b, j: (0, 0)),
        pl.BlockSpec((3, 256), lambda b, j: (0, 0)),
        pl.BlockSpec((1, 256), lambda b, j: (0, 0)),
        pl.BlockSpec((1, 128), lambda b, j: (0, 0)),
        pl.BlockSpec((1, 128), lambda b, j: (0, 0)),
        pl.BlockSpec((1, D_MODEL), lambda b, j: (0, 0)),
        pl.BlockSpec((1, D_MODEL), lambda b, j: (0, 0)),
    ]
    args = [main, main, main, misc, misc, misc, main, cwx, cbx, cwm, cbm, dtb, alog, dsum, nw]
    if has_init:
        in_specs.append(pl.BlockSpec((None, 2, M_PAIRS, M_STATE, 128), lambda b, j: (b, 0, 0, 0, 0)))
        args.append(init)
    return pl.pallas_call(
        functools.partial(_ssd_kernel, nc=full_nc, cps=cps, has_init=has_init),
        out_shape=[jax.ShapeDtypeStruct((R, D_MODEL), BF16),
                   jax.ShapeDtypeStruct((nseq, 2, M_HEADS * M_HEADDIM, M_STATE), F32)],
        grid=(nseq, 2 * nc),
        in_specs=in_specs,
        out_specs=[pl.BlockSpec((L, D_MODEL), lambda b, j: (b * nc + oidx(j), 0)),
                   pl.BlockSpec((None, 2, M_HEADS * M_HEADDIM, M_STATE), lambda b, j: (b, 0, 0, 0))],
        scratch_shapes=[pltpu.VMEM((T, D_MODEL), BF16), pltpu.VMEM((T, 256), F32),
                        pltpu.VMEM((T, D_MODEL), F32), pltpu.VMEM((M_PAIRS, M_STATE, 128), F32)],
        compiler_params=_cparams(("parallel", "arbitrary")),
        name="ssd",
    )(*args)


def _attn_lat_kernel(q_ref, k_ref, v_ref, gb_ref, kc_ref, vc_ref, lv_ref, hw_ref, o_ref, vt_scr, vct_scr,
                     *, lam_init, kb):
    tq = q_ref.shape[0]
    T = k_ref.shape[0]
    lc = kc_ref.shape[0]
    qi = pl.program_id(2)

    @pl.when(qi == 0)
    def _():
        for c0 in range(0, T, 128):
            vt_scr[0:DA_VD, c0:c0 + 128] = v_ref[c0:c0 + 128, :].astype(F32).T.astype(BF16)
        vt_scr[DA_VD:DA_VD + 16, :] = jnp.ones((16, T), BF16)
        for c0 in range(0, lc, 128):
            vct_scr[0:DA_VD, c0:c0 + 128] = vc_ref[c0:c0 + 128, :].T.astype(BF16)
        vct_scr[DA_VD:DA_VD + 16, :] = jnp.ones((16, lc), BF16)

    lv = lv_ref[...]
    lam = (jnp.exp(jnp.sum(lv[0:1, :] * lv[1:2, :], axis=-1, keepdims=True))
           - jnp.exp(jnp.sum(lv[2:3, :] * lv[3:4, :], axis=-1, keepdims=True)) + lam_init)
    sq = ATT_SUBQ
    lane = lax.broadcasted_iota(jnp.int32, (sq, 2 * DA_QK), 1)
    dn_t = (((1,), (1,)), ((), ()))

    q2s = []
    for c in range(tq // sq):
        q = q_ref[c * sq:(c + 1) * sq, :].astype(F32) * (DA_QK ** -0.5 * math.log2(math.e))
        q2s.append(jnp.concatenate([jnp.where(lane < DA_QK, q, 0.0), jnp.where(lane < DA_QK, 0.0, q)],
                                   axis=0).astype(BF16))

    def scores(k_blk, q2):
        s = lax.dot_general(k_blk, q2, dn_t, preferred_element_type=F32)
        return s, jnp.max(s, axis=0, keepdims=True)

    def accumulate(s, bm, vt_blk, state):
        m_new = bm if state is None else jnp.maximum(state[0], bm)
        p = jnp.exp2(s - m_new).astype(BF16)
        pv = jnp.dot(vt_blk, p, preferred_element_type=F32)
        if state is None:
            return m_new, pv
        return m_new, jnp.exp2(state[0] - m_new) * state[1] + pv

    kcb = kc_ref[...].astype(BF16)
    cur = [scores(kcb, q2) for q2 in q2s]
    cur_vt = vct_scr[...]
    states = [None] * len(q2s)
    for k0 in range(0, T, kb):
        k_blk = k_ref[k0:k0 + kb, :]
        nxt = [scores(k_blk, q2) for q2 in q2s]
        states = [accumulate(cu[0], cu[1], cur_vt, st) for cu, st in zip(cur, states)]
        cur, cur_vt = nxt, vt_scr[:, k0:k0 + kb]
    states = [accumulate(cu[0], cu[1], cur_vt, st) for cu, st in zip(cur, states)]
    for c, (_, acc) in enumerate(states):
        rs = slice(c * sq, (c + 1) * sq)
        r = 1.0 / acc[DA_VD:DA_VD + 1, :]
        o_t = acc[0:DA_VD, 0:sq] * r[:, 0:sq] - acc[0:DA_VD, sq:2 * sq] * (r[:, sq:2 * sq] * lam)
        o = o_t.T
        ms = jnp.mean(o * o, axis=-1, keepdims=True)
        o = o * lax.rsqrt(ms + EPS) * hw_ref[...] * (1.0 - lam_init)
        o_ref[rs, :] = (o * _silu(gb_ref[rs, :].astype(F32))).astype(o_ref.dtype)


def _attn_ctx_kernel(q_ref, k_ref, v_ref, gb_ref, lv_ref, hw_ref, o_ref, *, lam_init):
    T = q_ref.shape[0]
    lv = lv_ref[...]
    lam = (jnp.exp(jnp.sum(lv[0:1, :] * lv[1:2, :], axis=-1, keepdims=True))
           - jnp.exp(jnp.sum(lv[2:3, :] * lv[3:4, :], axis=-1, keepdims=True)) + lam_init)
    lane = lax.broadcasted_iota(jnp.int32, (T, 2 * DA_QK), 1)
    dn_t = (((1,), (1,)), ((), ()))
    ones = jnp.ones((16, T), BF16)
    for hh in range(DA_HEADS):
        cs = slice(hh * 128, (hh + 1) * 128)
        q = q_ref[:, cs].astype(F32) * (DA_QK ** -0.5 * math.log2(math.e))
        q2 = jnp.concatenate([jnp.where(lane < DA_QK, q, 0.0), jnp.where(lane < DA_QK, 0.0, q)],
                             axis=0).astype(BF16)
        v = v_ref[:, cs].astype(F32)
        vt = jnp.concatenate([v[c0:c0 + 128, :].T for c0 in range(0, T, 128)], axis=1).astype(BF16)
        vt1 = jnp.concatenate([vt, ones], axis=0)
        s = lax.dot_general(k_ref[:, cs], q2, dn_t, preferred_element_type=F32)
        p = jnp.exp2(s - jnp.max(s, axis=0, keepdims=True)).astype(BF16)
        acc = jnp.dot(vt1, p, preferred_element_type=F32)
        r = 1.0 / acc[DA_VD:DA_VD + 1, :]
        o_t = acc[0:DA_VD, 0:T] * r[:, 0:T] - acc[0:DA_VD, T:2 * T] * (r[:, T:2 * T] * lam)
        o = jnp.concatenate([o_t[:, c0:c0 + 128].T for c0 in range(0, T, 128)], axis=0)
        ms = jnp.mean(o * o, axis=-1, keepdims=True)
        o = o * lax.rsqrt(ms + EPS) * hw_ref[...] * (1.0 - lam_init)
        o_ref[:, cs] = (o * _silu(gb_ref[:, cs].astype(F32))).astype(o_ref.dtype)


def _attention_ctx(main, nseq, T, lam_vecs, head_w, lam_init):
    return pl.pallas_call(
        functools.partial(_attn_ctx_kernel, lam_init=lam_init),
        out_shape=jax.ShapeDtypeStruct((nseq * T, D_MODEL), BF16),
        grid=(nseq,),
        in_specs=[
            pl.BlockSpec((T, D_MODEL), lambda b: (b, SEG_Q)),
            pl.BlockSpec((T, D_MODEL), lambda b: (b, SEG_K)),
            pl.BlockSpec((T, D_MODEL), lambda b: (b, SEG_V)),
            pl.BlockSpec((T, D_MODEL), lambda b: (b, SEG_GB)),
            pl.BlockSpec((4, DA_QK), lambda b: (0, 0)),
            pl.BlockSpec((1, DA_VD), lambda b: (0, 0)),
        ],
        out_specs=pl.BlockSpec((T, D_MODEL), lambda b: (b, 0)),
        compiler_params=_cparams(("parallel",)),
        name="diff_attn_ctx",
    )(main, main, main, main, lam_vecs, head_w)


def _attention_latent(hm, nseq, T, tq, kb, lam_vecs, head_w, lam_init, ck, cv):
    R = nseq * T
    nq = T // tq
    lc = ck.shape[1]
    return pl.pallas_call(
        functools.partial(_attn_lat_kernel, lam_init=lam_init, kb=kb),
        out_shape=jax.ShapeDtypeStruct((R, D_MODEL), BF16),
        grid=(nseq, DA_HEADS, nq),
        in_specs=[
            pl.BlockSpec((None, None, None, tq, 128), lambda b, g, qi: (0, b, g, qi, 0)),
            pl.BlockSpec((None, None, None, T, 128), lambda b, g, qi: (1, b, g, 0, 0)),
            pl.BlockSpec((None, None, None, T, 128), lambda b, g, qi: (2, b, g, 0, 0)),
            pl.BlockSpec((None, None, None, tq, 128), lambda b, g, qi: (3, b, g, qi, 0)),
            pl.BlockSpec((None, lc, 128), lambda b, g, qi: (b, 0, g)),
            pl.BlockSpec((None, lc, 128), lambda b, g, qi: (b, 0, g)),
            pl.BlockSpec((4, DA_QK), lambda b, g, qi: (0, 0)),
            pl.BlockSpec((1, DA_VD), lambda b, g, qi: (0, 0)),
        ],
        out_specs=pl.BlockSpec((tq, 128), lambda b, g, qi: (b * nq + qi, g)),
        scratch_shapes=[pltpu.VMEM((DA_VD + 16, T), BF16), pltpu.VMEM((DA_VD + 16, lc), BF16)],
        compiler_params=_cparams(("parallel", "parallel", "arbitrary")),
        name="diff_attn_lat",
    )(hm, hm, hm, hm, ck, cv, lam_vecs, head_w)


def _sgmlp_kernel(u_ref, sv_ref, gc_ref, vw_ref, ws_ref, bias_ref, o_ref):
    tm = u_ref.shape[0]
    u = _gelu_tanh(u_ref[...].astype(F32))
    v = _gelu_tanh(sv_ref[...].astype(F32))
    vc = v - jnp.mean(v, axis=-1, keepdims=True)
    v = vc * lax.rsqrt(jnp.mean(vc * vc, axis=-1, keepdims=True) + EPS) * vw_ref[...]
    vb = v.astype(BF16)
    gate = _silu(gc_ref[...].astype(F32))
    for ci in range(tm // SG_CHUNK):
        rs = slice(ci * SG_CHUNK, (ci + 1) * SG_CHUNK)
        cols = []
        for g in range(SG_GROUPS):
            cols.append(jnp.dot(ws_ref[g], vb[rs, g * 128:(g + 1) * 128], preferred_element_type=F32))
        mixed = jnp.concatenate(cols, axis=1) + bias_ref[...]
        o_ref[rs, :] = (u[rs] * mixed * gate[rs]).astype(o_ref.dtype)


def _sgmlp(main, vnorm_w, ws_bf, bias_exp, l, seg_shift, tm=256):
    R = main.shape[0]
    return pl.pallas_call(
        _sgmlp_kernel,
        out_shape=jax.ShapeDtypeStruct((R, D_MODEL), BF16),
        grid=(R // tm,),
        in_specs=[
            pl.BlockSpec((tm, D_MODEL), lambda i: (i, SEG_U - seg_shift)),
            pl.BlockSpec((tm, D_MODEL), lambda i: (i, SEG_SV - seg_shift)),
            pl.BlockSpec((tm, D_MODEL), lambda i: (i, SEG_GC - seg_shift)),
            pl.BlockSpec((1, D_MODEL), lambda i: (0, 0)),
            pl.BlockSpec((None, SG_GROUPS, SG_CHUNK, SG_CHUNK), lambda i: (l, 0, 0, 0)),
            pl.BlockSpec((None, SG_CHUNK, D_MODEL), lambda i: (l, 0, 0)),
        ],
        out_specs=pl.BlockSpec((tm, D_MODEL), lambda i: (i, 0)),
        compiler_params=_cparams(("parallel",)),
        name="sgmlp",
    )(main, main, main, vnorm_w, ws_bf, bias_exp)


def _merge_kernel(ya_ref, yb_ref, yc_ref, ga_ref, gb_ref, gc_ref, wb_ref, wo_ref, pw_ref, mod_ref, x_ref, o_ref):
    def branch(y_ref, g_ref, i):
        p = jnp.dot(y_ref[...], wb_ref[i], preferred_element_type=F32)
        return _sigmoid(g_ref[...].astype(F32)) * p

    merged = branch(ya_ref, ga_ref, 0) + branch(yb_ref, gb_ref, 1) + branch(yc_ref, gc_ref, 2)
    o = jnp.dot(merged.astype(BF16), wo_ref[...], preferred_element_type=F32)
    ms = jnp.mean(o * o, axis=-1, keepdims=True)
    o = o * lax.rsqrt(ms + EPS) * pw_ref[...]
    o_ref[...] = x_ref[...] + mod_ref[2:3, :] * o


def _merge(ya, yb, yc, main, seg_shift, wb_bf, wo_bf, l, post_w, mod, rows_per_mod, x, tm=512):
    R = x.shape[0]
    row = lambda i: (i, 0)
    return pl.pallas_call(
        _merge_kernel,
        out_shape=jax.ShapeDtypeStruct((R, D_MODEL), F32),
        grid=(R // tm,),
        in_specs=[
            pl.BlockSpec((tm, D_MODEL), row),
            pl.BlockSpec((tm, D_MODEL), row),
            pl.BlockSpec((tm, D_MODEL), row),
            pl.BlockSpec((tm, D_MODEL), lambda i: (i, SEG_MGA - seg_shift)),
            pl.BlockSpec((tm, D_MODEL), lambda i: (i, SEG_MGB - seg_shift)),
            pl.BlockSpec((tm, D_MODEL), lambda i: (i, SEG_MGC - seg_shift)),
            pl.BlockSpec((None, 3, D_MODEL, D_MODEL), lambda i: (l, 0, 0, 0)),
            pl.BlockSpec((None, D_MODEL, D_MODEL), lambda i: (l, 0, 0)),
            pl.BlockSpec((1, D_MODEL), lambda i: (0, 0)),
            pl.BlockSpec((None, 3, D_MODEL), lambda i: ((i * tm) // rows_per_mod, 0, 0)),
            pl.BlockSpec((tm, D_MODEL), row),
        ],
        out_specs=pl.BlockSpec((tm, D_MODEL), row),
        compiler_params=_cparams(("parallel",)),
        name="merge_out",
    )(ya, yb, yc, main, main, main, wb_bf, wo_bf, post_w, mod, x)


def _rope_tables(n_tokens):
    n_rows = n_tokens // GRID_W
    rows = jnp.repeat(jnp.arange(n_rows, dtype=F32), GRID_W)
    cols = jnp.tile(jnp.arange(GRID_W, dtype=F32), n_rows)
    n_freq = DA_QK // 4
    inv = ROPE_BASE ** (-jnp.arange(n_freq, dtype=F32) / n_freq)
    ang = jnp.concatenate([rows[:, None] * inv, cols[:, None] * inv], -1)
    cos, sin = jnp.cos(ang), jnp.sin(ang)
    return jnp.tile(cos, (1, 4)), jnp.concatenate([-sin, sin, -sin, sin], axis=1)


def _state_to_pairs(s):
    lead = s.shape[:-3]
    s = s.reshape(lead + (M_PAIRS, 2, M_HEADDIM, M_STATE))
    s = jnp.moveaxis(s, -1, -3)
    return s.reshape(lead + (M_PAIRS, M_STATE, 2 * M_HEADDIM))


def kernel(x_prompt, x_sample, cache_k, cache_v, state_ssm, c, c_ctx, pre_norm_w, post_norm_w, w_mod, b_mod, w_in,
           m_conv_w, m_conv_b, m_A_log, m_dt_bias, m_D, m_norm_w, da_lambda, da_head_norm_w, sg_vnorm_w,
           sg_spatial_w, sg_spatial_b, w_branch, w_out):
    depth = w_in.shape[0]
    nb, seq, _ = x_prompt.shape
    db, dseq, _ = x_sample.shape
    past = cache_k.shape[2]

    w_main, w_misc = _prep_in_weights(jnp.swapaxes(w_in, 1, 2))
    wb_bf = w_branch.astype(BF16)
    wo_bf = w_out.astype(BF16)
    ws_bf = sg_spatial_w.astype(BF16)
    bias_exp = jnp.repeat(jnp.swapaxes(sg_spatial_b, 1, 2), D_MODEL // SG_GROUPS, axis=2)
    cw = jnp.swapaxes(m_conv_w, 1, 2)
    dtb = jnp.pad(m_dt_bias.reshape(depth, 1, 2 * M_HEADS), ((0, 0), (0, 0), (0, 128 - 2 * M_HEADS)))
    alog = jnp.pad(m_A_log.reshape(depth, 1, 2 * M_HEADS), ((0, 0), (0, 0), (0, 128 - 2 * M_HEADS)))
    dsum = jnp.repeat(m_D[:, 0] + m_D[:, 1], M_HEADDIM, axis=1).reshape(depth, 1, D_MODEL)

    cvec = jnp.concatenate([c_ctx[None, :], c, jnp.zeros((8 - 1 - db, D_MODEL), F32)], axis=0)
    mods = _modulation(cvec, w_mod, b_mod).reshape(depth, 8, 3, D_MODEL)

    rope_tabs = _rope_tables(dseq)
    init_pairs = _state_to_pairs(state_ssm)
    ck = cache_k.reshape(db, depth, past, D_MODEL)
    cv = cache_v.reshape(db, depth, past, D_MODEL)

    xp = x_prompt.reshape(nb * seq, D_MODEL)
    xs = x_sample.reshape(db * dseq, D_MODEL)
    ks_out, vs_out, ss_out = [], [], []

    def layer(x, l, nseq, T, mod, rows_per_mod, latent):
        lam_init = 0.8 - 0.6 * math.exp(-0.3 * l)
        outs = _inproj(x, mod, pre_norm_w[l][None], w_main, w_misc, l, rows_per_mod,
                       rope_tabs if latent else None, emit_kv=not latent)
        main, misc = outs[0], outs[1]
        ya, st = _ssd(main, misc, nseq, T, cw[l, :, 0:D_MODEL], m_conv_b[l][None, 0:D_MODEL],
                      cw[l, :, D_MODEL:], m_conv_b[l][None, D_MODEL:], dtb[l], alog[l], dsum[l],
                      m_norm_w[l][None], init_pairs[:, l] if latent else None,
                      cps=min(SSD_CHUNKS_PER_STEP, T // M_CHUNK))
        if latent:
            yb = _attention_latent(outs[2], nseq, T, ATT_TQ, ATT_KB, da_lambda[l], da_head_norm_w[l][None],
                                   lam_init, ck[:, l], cv[:, l])
        else:
            yb = _attention_ctx(main, nseq, T, da_lambda[l], da_head_norm_w[l][None], lam_init)
        shift = LATENT_SEG_SHIFT if latent else 0
        yc = _sgmlp(main, sg_vnorm_w[l][None], ws_bf, bias_exp, l, shift)
        x_new = _merge(ya, yb, yc, main, shift, wb_bf, wo_bf, l, post_norm_w[l][None], mod, rows_per_mod, x)
        return x_new, outs[2:], st

    for l in range(depth):
        xp, kv, st = layer(xp, l, nb, seq, mods[l, 0:1], nb * seq, False)
        ks_out.append(kv[0].reshape(nb, seq, DA_HEADS, 2 * DA_QK))
        vs_out.append(kv[1].reshape(nb, seq, DA_HEADS, DA_VD))
        ss_out.append(st.reshape(nb, 2, M_HEADS, M_HEADDIM, M_STATE))
        xs, _, _ = layer(xs, l, db, dseq, mods[l, 1:1 + db], dseq, True)

    return (xp.reshape(nb, seq, D_MODEL), xs.reshape(db, dseq, D_MODEL),
            jnp.stack(ks_out, 1), jnp.stack(vs_out, 1), jnp.stack(ss_out, 1))
```

```python
import functools
import math

import jax
import jax.numpy as jnp
from jax import lax
from jax.experimental import pallas as pl
from jax.experimental.pallas import tpu as pltpu

F32 = jnp.float32
BF16 = jnp.bfloat16

D_MODEL = 1024
EPS = 1e-6
GRID_W = 64
ROPE_BASE = 10000.0
M_HEADS = 16
M_HEADDIM = 64
M_STATE = 64
M_GROUPS = 2
M_CHUNK = 128
M_PAIRS = M_HEADS // 2
DA_HEADS = 8
DA_QK = 64
DA_VD = 128
SG_GROUPS = 8
SG_CHUNK = 128

SEG_Z, SEG_XS, SEG_Q, SEG_K, SEG_V, SEG_GB, SEG_U, SEG_SV, SEG_GC, SEG_MGA, SEG_MGB, SEG_MGC = range(12)
N_SEG = 12
LATENT_SEG_SHIFT = 4
MISC_W = 384

VMEM_LIMIT = 56 * 1024 * 1024
NEG_BIG = -1e30
ATT_SUBQ = 128
ATT_TQ = 512
ATT_KB = 512
SSD_CHUNKS_PER_STEP = 4
INPROJ_TM = 1024
INPROJ_SUBN = 512


def _silu(x):
    return x * (1.0 / (1.0 + jnp.exp(-x)))


def _sigmoid(x):
    return 1.0 / (1.0 + jnp.exp(-x))


def _gelu_tanh(x):
    return 0.5 * x * (1.0 + jnp.tanh(math.sqrt(2.0 / math.pi) * (x + 0.044715 * (x * x * x))))


def _cparams(sem):
    return pltpu.CompilerParams(dimension_semantics=sem, vmem_limit_bytes=VMEM_LIMIT)


def _mod_kernel(c_ref, w_ref, b_ref, o_ref):
    c = c_ref[...]
    s = _silu(c).astype(BF16)
    o_ref[...] = jnp.dot(s, w_ref[...].astype(BF16), preferred_element_type=F32) + b_ref[...]


def _modulation(cvec, w_mod, b_mod):
    depth = w_mod.shape[0]
    nt = 3
    return pl.pallas_call(
        _mod_kernel,
        out_shape=jax.ShapeDtypeStruct((depth, 8, 3 * D_MODEL), F32),
        grid=(depth, nt),
        in_specs=[
            pl.BlockSpec((8, D_MODEL), lambda l, j: (0, 0)),
            pl.BlockSpec((None, D_MODEL, D_MODEL), lambda l, j: (l, 0, j)),
            pl.BlockSpec((None, 1, D_MODEL), lambda l, j: (l, 0, j)),
        ],
        out_specs=pl.BlockSpec((None, 8, D_MODEL), lambda l, j: (l, 0, j)),
        compiler_params=_cparams(("arbitrary", "arbitrary")),
        name="modulation",
    )(cvec, w_mod, b_mod.reshape(depth, 1, 3 * D_MODEL))


W_IN_COLS = 12576
W_XS_END = 2048
W_MISC_END = 2336


def _wprep_head_kernel(w_ref, wm_ref, main_ref, misc_ref):
    main_ref[...] = w_ref[...].T.astype(BF16)

    @pl.when(pl.program_id(1) == 0)
    def _():
        n = W_MISC_END - W_XS_END
        wm = jnp.concatenate([wm_ref[...], jnp.zeros((MISC_W - n, D_MODEL), F32)], axis=0)
        misc_ref[...] = wm.T.astype(BF16)


def _wprep_tail_kernel(w_ref, main_in_ref, main_ref):
    del main_in_ref
    main_ref[...] = w_ref[...].T.astype(BF16)


def _prep_in_weights(w_in_t):
    depth = w_in_t.shape[0]
    n_head = W_XS_END // D_MODEL
    main_sds = jax.ShapeDtypeStruct((depth, D_MODEL, N_SEG * D_MODEL), BF16)
    w2d = w_in_t.reshape(depth * W_IN_COLS, D_MODEL)
    main, misc = pl.pallas_call(
        _wprep_head_kernel,
        out_shape=[main_sds, jax.ShapeDtypeStruct((depth, D_MODEL, MISC_W), BF16)],
        grid=(depth, n_head),
        in_specs=[pl.BlockSpec((None, D_MODEL, D_MODEL), lambda l, j: (l, j, 0)),
                  pl.BlockSpec((pl.Element(W_MISC_END - W_XS_END), pl.Element(D_MODEL)),
                               lambda l, j: (pl.multiple_of(l * W_IN_COLS + W_XS_END, 32), 0))],
        out_specs=[pl.BlockSpec((None, D_MODEL, D_MODEL), lambda l, j: (l, 0, j)),
                   pl.BlockSpec((None, D_MODEL, MISC_W), lambda l, j: (l, 0, 0))],
        compiler_params=_cparams(("parallel", "arbitrary")),
        name="w_prep_head",
    )(w_in_t, w2d)
    main = pl.pallas_call(
        _wprep_tail_kernel,
        out_shape=main_sds,
        grid=(depth, N_SEG - n_head),
        in_specs=[pl.BlockSpec((pl.Element(D_MODEL), pl.Element(D_MODEL)),
                               lambda l, j: (pl.multiple_of(l * W_IN_COLS + W_MISC_END + j * D_MODEL, 32), 0)),
                  pl.BlockSpec(memory_space=pl.ANY)],
        out_specs=pl.BlockSpec((None, D_MODEL, D_MODEL), lambda l, j: (l, 0, j + n_head)),
        input_output_aliases={1: 0},
        compiler_params=_cparams(("parallel", "parallel")),
        name="w_prep_tail",
    )(w2d, main)
    return main, misc


def _inproj_kernel(*refs, rope, emit_kv):
    it = iter(refs)
    x_ref, mod_ref, prew_ref, w_ref, wm_ref = next(it), next(it), next(it), next(it), next(it)
    if rope:
        cos_ref, sin_ref = next(it), next(it)
    main_ref, misc_ref = next(it), next(it)
    if emit_kv:
        kf_ref, vf_ref = next(it), next(it)
    if rope:
        hm_ref = next(it)
    h_scr = next(it)

    j = pl.program_id(1)
    tm = x_ref.shape[0]

    @pl.when(j == 0)
    def _():
        x = x_ref[...]
        ms = jnp.mean(x * x, axis=-1, keepdims=True)
        y = x * lax.rsqrt(ms + EPS) * prew_ref[...]
        h = y * (1.0 + mod_ref[1:2, :]) + mod_ref[0:1, :]
        hb = h.astype(BF16)
        h_scr[...] = hb
        misc_ref[...] = jnp.dot(hb, wm_ref[...], preferred_element_type=F32)

    h = h_scr[...]
    sub = INPROJ_SUBN
    for c0 in range(0, D_MODEL, sub):
        cs = slice(c0, c0 + sub)
        acc = jnp.dot(h, w_ref[:, cs], preferred_element_type=F32)

        if rope:
            is_rope = jnp.logical_or(j == SEG_Q, j == SEG_K)
            is_plain_hm = jnp.logical_or(j == SEG_V, j == SEG_GB)

            def store_heads(val):
                vb = val.astype(hm_ref.dtype)
                for hh in range(sub // 128):
                    hm_ref[c0 // 128 + hh] = vb[:, hh * 128:(hh + 1) * 128]

            @pl.when(is_rope)
            def _():
                cos = jnp.concatenate([cos_ref[...]] * (sub // 128), axis=1)
                sin = jnp.concatenate([sin_ref[...]] * (sub // 128), axis=1)
                lane = lax.broadcasted_iota(jnp.int32, (tm, sub), 1)
                first_half = (lane & (DA_QK - 1)) < (DA_QK // 2)
                rot = jnp.where(first_half, pltpu.roll(acc, sub - DA_QK // 2, 1), pltpu.roll(acc, DA_QK // 2, 1))
                store_heads(acc * cos + rot * sin)

            @pl.when(is_plain_hm)
            def _():
                store_heads(acc)

            @pl.when(jnp.logical_not(jnp.logical_or(is_rope, is_plain_hm)))
            def _():
                main_ref[:, cs] = acc.astype(main_ref.dtype)
        else:
            main_ref[:, cs] = acc.astype(main_ref.dtype)

        if emit_kv:
            @pl.when(j == SEG_K)
            def _():
                kf_ref[:, cs] = acc

            @pl.when(j == SEG_V)
            def _():
                vf_ref[:, cs] = acc


def _inproj(x, mod, pre_w, w_main, w_misc, l, rows_per_mod, rope_tabs, emit_kv, tm=INPROJ_TM):
    R = x.shape[0]
    ni = R // tm
    rope = rope_tabs is not None
    in_specs = [
        pl.BlockSpec((tm, D_MODEL), lambda i, j: (i, 0)),
        pl.BlockSpec((None, 3, D_MODEL), lambda i, j: ((i * tm) // rows_per_mod, 0, 0)),
        pl.BlockSpec((1, D_MODEL), lambda i, j: (0, 0)),
        pl.BlockSpec((None, D_MODEL, D_MODEL), lambda i, j: (l, 0, j)),
        pl.BlockSpec((None, D_MODEL, MISC_W), lambda i, j: (l, 0, 0)),
    ]
    args = [x, mod, pre_w, w_main, w_misc]
    if rope:
        cos, sin = rope_tabs
        nt = cos.shape[0] // tm
        in_specs += [pl.BlockSpec((tm, 128), lambda i, j: (i % nt, 0)),
                     pl.BlockSpec((tm, 128), lambda i, j: (i % nt, 0))]
        args += [cos, sin]
    if rope:
        n_main = N_SEG - LATENT_SEG_SHIFT
        main_idx = lambda i, j: (i, jnp.where(j < SEG_Q, j, jnp.where(j <= SEG_GB, SEG_XS, j - LATENT_SEG_SHIFT)))
    else:
        n_main = N_SEG
        main_idx = lambda i, j: (i, j)
    out_shape = [jax.ShapeDtypeStruct((R, n_main * D_MODEL), BF16), jax.ShapeDtypeStruct((R, MISC_W), F32)]
    out_specs = [pl.BlockSpec((tm, D_MODEL), main_idx), pl.BlockSpec((tm, MISC_W), lambda i, j: (i, 0))]
    if emit_kv:
        out_shape += [jax.ShapeDtypeStruct((R, D_MODEL), F32)] * 2
        out_specs += [pl.BlockSpec((tm, D_MODEL), lambda i, j: (i, 0))] * 2
    if rope:
        T = cos.shape[0]
        out_shape.append(jax.ShapeDtypeStruct((4, R // T, DA_HEADS, T, 128), BF16))
        out_specs.append(pl.BlockSpec(
            (None, None, DA_HEADS, tm, 128),
            lambda i, j: (jnp.clip(j - SEG_Q, 0, 3), (i * tm) // T, 0, ((i * tm) % T) // tm, 0)))
    return pl.pallas_call(
        functools.partial(_inproj_kernel, rope=rope, emit_kv=emit_kv),
        out_shape=out_shape,
        grid=(ni, N_SEG),
        in_specs=in_specs,
        out_specs=out_specs,
        scratch_shapes=[pltpu.VMEM((tm, D_MODEL), BF16)],
        compiler_params=_cparams(("parallel", "arbitrary")),
        name="inproj",
    )(*args)


def _split3(a):
    hi = a.astype(BF16)
    r1 = a - hi.astype(F32)
    mid = r1.astype(BF16)
    lo = (r1 - mid.astype(F32)).astype(BF16)
    return hi, mid, lo


def _ssd_kernel(*refs, nc, cps, has_init, has_prev):
    it = iter(refs)
    xs_ref, xsp_ref, xsn_ref = next(it), next(it), next(it)
    mi_ref, mip_ref, min_ref = next(it), next(it), next(it)
    z_ref = next(it)
    cwx_ref, cbx_ref, cwm_ref, cbm_ref = next(it), next(it), next(it), next(it)
    dtb_ref, alog_ref, dsum_ref, nw_ref = next(it), next(it), next(it), next(it)
    if has_init:
        init_ref = next(it)
    if has_prev:
        next(it)
    y_ref, sto_ref = next(it), next(it)
    xc_scr, bc_scr, yf_scr, st_scr = next(it), next(it), next(it), next(it)

    L = M_CHUNK
    LB = cps * L
    ns = nc // cps
    j = pl.program_id(1)
    fwd = j < ns
    c = jnp.where(fwd, j, 2 * ns - 1 - j)
    row0 = pl.multiple_of(c * LB, LB)
    LOG2E = math.log2(math.e)

    rid = lax.broadcasted_iota(jnp.int32, (L, L), 0)
    cid = lax.broadcasted_iota(jnp.int32, (L, L), 1)

    def conv_silu(x, prow, nrow, w_ref, b_ref):
        n = x.shape[1]
        r = lax.broadcasted_iota(jnp.int32, (LB, n), 0)
        xm = jnp.where(r == 0, prow, pltpu.roll(x, 1, 0))
        xp = jnp.where(r == LB - 1, nrow, pltpu.roll(x, LB - 1, 0))
        y = w_ref[0:1, :] * xm + w_ref[1:2, :] * x + w_ref[2:3, :] * xp + b_ref[...]
        return _silu(y)

    def load_state(d):
        if has_init:
            for i in range(M_PAIRS):
                st_scr[i] = init_ref[d, i]
        else:
            st_scr[...] = jnp.zeros_like(st_scr)

    def store_state(d):
        pad = jnp.zeros((2 * M_HEADDIM - M_STATE, 2 * M_HEADDIM), F32)
        for i in range(M_PAIRS):
            t = jnp.concatenate([st_scr[i], pad], axis=0).T
            sto_ref[d, i * 2 * M_HEADDIM:(i + 1) * 2 * M_HEADDIM, :] = t[:, 0:M_STATE]

    lane_lo = cid < M_HEADDIM
    lane_lo_s = lax.broadcasted_iota(jnp.int32, (M_STATE, 2 * M_HEADDIM), 1) < M_HEADDIM
    heads_per_group = M_HEADS // M_GROUPS
    zeros_s = jnp.zeros((M_STATE, 2 * M_HEADDIM), BF16)

    def setup(d, bc, dt_raw):
        tri = (cid <= rid) if d == 0 else (cid >= rid)
        tri_bf = jnp.where(tri, 1.0, 0.0).astype(BF16)
        dt = dt_raw + dtb_ref[...]
        dt = jnp.maximum(dt, 0.0) + jnp.log1p(jnp.exp(-jnp.abs(dt)))
        a = dt * (-jnp.exp(alog_ref[...]))
        a_hi, a_mid, a_lo = _split3(a)
        p_col = (jnp.dot(tri_bf, a_hi, preferred_element_type=F32)
                 + jnp.dot(tri_bf, a_mid, preferred_element_type=F32)
                 + jnp.dot(tri_bf, a_lo, preferred_element_type=F32))
        p_row = p_col.T[d * M_HEADS:(d + 1) * M_HEADS, :]
        dt_row = dt.T[d * M_HEADS:(d + 1) * M_HEADS, :]
        tot = p_row[:, L - 1:L] if d == 0 else p_row[:, 0:1]
        b_all = bc[:, 0:M_GROUPS * M_STATE]
        c_all = bc[:, M_GROUPS * M_STATE:2 * M_GROUPS * M_STATE]
        b_bf = b_all.astype(BF16)
        g_mats = []
        for g in range(M_GROUPS):
            cg = jnp.where(lane_lo if g == 0 else jnp.logical_not(lane_lo), c_all, 0.0).astype(BF16)
            g_mats.append(lax.dot_general(cg, b_bf, (((1,), (1,)), ((), ())), preferred_element_type=F32))
        return dict(
            tri=tri, g_mats=g_mats, c_all=c_all, bt_all=b_all.T,
            p2_col=p_col * LOG2E,
            q2_row=(p_row - jnp.log(dt_row)) * LOG2E,
            w_row=dt_row * jnp.exp(tot - p_row),
            etot=jnp.exp(tot),
            ep_col=jnp.exp(p_col))

    def pairs(d, s, x_bf):
        outs = []
        for i in range(M_PAIRS):
            g = (2 * i) // heads_per_group
            btg = s["bt_all"][g * M_STATE:(g + 1) * M_STATE, :]
            lhs_rows = []
            bw_rows = []
            for hh in range(2):
                h = 2 * i + hh
                col = d * M_HEADS + h
                pc = jnp.broadcast_to(s["p2_col"][:, col:col + 1], (L, L))
                dm = jnp.exp2(jnp.where(s["tri"], pc - s["q2_row"][h:h + 1, :], NEG_BIG))
                m_h = s["g_mats"][g] * dm
                ce_h = s["c_all"] * jnp.broadcast_to(s["ep_col"][:, col:col + 1], (L, L))
                lhs_rows.append(jnp.concatenate([m_h, ce_h], axis=1).astype(BF16))
                bw = (btg * s["w_row"][h:h + 1, :]).astype(BF16)
                bw_rows.append(jnp.concatenate([bw, zeros_s], axis=1))
            lhs = jnp.concatenate(lhs_rows + bw_rows, axis=0)
            x_pair = x_bf[:, i * 128:(i + 1) * 128]
            st_pair = st_scr[i]
            st_bf = st_pair.astype(BF16)
            rhs = jnp.concatenate([x_pair] + ([st_bf, zeros_s] if g == 0 else [zeros_s, st_bf]), axis=0)
            res = jnp.dot(lhs, rhs, preferred_element_type=F32)
            y_pair = jnp.where(lane_lo, res[0:L], res[L:2 * L])
            ds = jnp.where(lane_lo_s, res[2 * L:2 * L + M_STATE], res[2 * L + M_STATE:2 * L + 2 * M_STATE])
            e0 = jnp.broadcast_to(s["etot"][2 * i:2 * i + 1, :], (M_STATE, 2 * M_HEADDIM))
            e1 = jnp.broadcast_to(s["etot"][2 * i + 1:2 * i + 2, :], (M_STATE, 2 * M_HEADDIM))
            st_scr[i] = jnp.where(lane_lo_s, e0, e1) * st_pair + ds
            outs.append(y_pair)
        return jnp.concatenate(outs, axis=1)

    def block(d, x_bf, bc, dt_raw):
        order = range(cps) if d == 0 else range(cps - 1, -1, -1)
        su = {ci: setup(d, bc[ci * L:(ci + 1) * L], dt_raw[ci * L:(ci + 1) * L]) for ci in order}
        ys = {ci: pairs(d, su[ci], x_bf[ci * L:(ci + 1) * L]) for ci in order}
        return jnp.concatenate([ys[ci] for ci in range(cps)], axis=0)

    @pl.when(j == 0)
    def _():
        load_state(0)

    @pl.when(j == ns)
    def _():
        load_state(1)

    @pl.when(fwd)
    def _():
        x = xs_ref[...].astype(F32)
        prow = jnp.where(c > 0, xsp_ref[...].astype(F32)[15:16, :], 0.0)
        nrow = jnp.where(c < ns - 1, xsn_ref[...].astype(F32)[0:1, :], 0.0)
        xc = conv_silu(x, prow, nrow, cwx_ref, cbx_ref).astype(BF16)
        xc_scr[pl.ds(row0, LB), :] = xc
        m = mi_ref[...]
        bcx = m[:, 0:256]
        prow_m = jnp.where(c > 0, mip_ref[7:8, 0:256], 0.0)
        nrow_m = jnp.where(c < ns - 1, min_ref[0:1, 0:256], 0.0)
        bc = conv_silu(bcx, prow_m, nrow_m, cwm_ref, cbm_ref)
        bc_scr[pl.ds(row0, LB), :] = bc
        yf_scr[pl.ds(row0, LB), :] = block(0, xc, bc, m[:, 256:384])

    @pl.when(j == ns - 1)
    def _():
        store_state(0)

    @pl.when(jnp.logical_not(fwd))
    def _():
        xc = xc_scr[pl.ds(row0, LB), :]
        bc = bc_scr[pl.ds(row0, LB), :]
        yb = block(1, xc, bc, mi_ref[:, 256:384])
        y = yf_scr[pl.ds(row0, LB), :] + yb + dsum_ref[...] * xc.astype(F32)
        y = y * _silu(z_ref[...].astype(F32))
        ms = jnp.mean(y * y, axis=-1, keepdims=True)
        y_ref[...] = (y * lax.rsqrt(ms + EPS) * nw_ref[...]).astype(y_ref.dtype)

    @pl.when(j == 2 * ns - 1)
    def _():
        store_state(1)


def _ssd(main, misc, nseq, T, cwx, cbx, cwm, cbm, dtb, alog, dsum, nw, init, st_stack=None, layer=0, n_layers=1,
         cps=SSD_CHUNKS_PER_STEP):
    R = nseq * T
    nc = T // M_CHUNK
    L = cps * M_CHUNK
    ns = nc // cps
    nc, full_nc = ns, nc
    has_init = init is not None

    def cidx(j):
        return jnp.where(j < nc, j, 2 * nc - 1 - j)

    def oidx(j):
        return jnp.where(j < nc, nc - 1, 2 * nc - 1 - j)

    in_specs = [
        pl.BlockSpec((L, D_MODEL), lambda b, j: (b * nc + cidx(j), SEG_XS)),
        pl.BlockSpec((16, D_MODEL), lambda b, j: (jnp.maximum((b * nc + cidx(j)) * (L // 16) - 1, 0), SEG_XS)),
        pl.BlockSpec((16, D_MODEL),
                     lambda b, j: (jnp.minimum((b * nc + cidx(j) + 1) * (L // 16), R // 16 - 1), SEG_XS)),
        pl.BlockSpec((L, MISC_W), lambda b, j: (b * nc + cidx(j), 0)),
        pl.BlockSpec((8, MISC_W), lambda b, j: (jnp.maximum((b * nc + cidx(j)) * (L // 8) - 1, 0), 0)),
        pl.BlockSpec((8, MISC_W), lambda b, j: (jnp.minimum((b * nc + cidx(j) + 1) * (L // 8), R // 8 - 1), 0)),
        pl.BlockSpec((L, D_MODEL), lambda b, j: (b * nc + oidx(j), SEG_Z)),
        pl.BlockSpec((3, D_MODEL), lambda b, j: (0, 0)),
        pl.BlockSpec((1, D_MODEL), lambda b, j: (0, 0)),
        pl.BlockSpec((3, 256), lambda b, j: (0, 0)),
        pl.BlockSpec((1, 256), lambda b, j: (0, 0)),
        pl.BlockSpec((1, 128), lambda b, j: (0, 0)),
        pl.BlockSpec((1, 128), lambda b, j: (0, 0)),
        pl.BlockSpec((1, D_MODEL), lambda b, j: (0, 0)),
        pl.BlockSpec((1, D_MODEL), lambda b, j: (0, 0)),
    ]
    args = [main, main, main, misc, misc, misc, main, cwx, cbx, cwm, cbm, dtb, alog, dsum, nw]
    if has_init:
        in_specs.append(pl.BlockSpec((None, 2, M_PAIRS, M_STATE, 128), lambda b, j: (b, 0, 0, 0, 0)))
        args.append(init)
    aliases = {}
    if st_stack is not None:
        in_specs.append(pl.BlockSpec(memory_space=pl.ANY))
        args.append(st_stack)
        aliases = {len(args) - 1: 1}
    return pl.pallas_call(
        functools.partial(_ssd_kernel, nc=full_nc, cps=cps, has_init=has_init, has_prev=st_stack is not None),
        out_shape=[jax.ShapeDtypeStruct((R, D_MODEL), BF16),
                   jax.ShapeDtypeStruct((nseq, n_layers, 2, M_HEADS * M_HEADDIM, M_STATE), F32)],
        grid=(nseq, 2 * nc),
        in_specs=in_specs,
        out_specs=[pl.BlockSpec((L, D_MODEL), lambda b, j: (b * nc + oidx(j), 0)),
                   pl.BlockSpec((None, None, 2, M_HEADS * M_HEADDIM, M_STATE), lambda b, j: (b, layer, 0, 0, 0))],
        scratch_shapes=[pltpu.VMEM((T, D_MODEL), BF16), pltpu.VMEM((T, 256), F32),
                        pltpu.VMEM((T, D_MODEL), F32), pltpu.VMEM((M_PAIRS, M_STATE, 128), F32)],
        input_output_aliases=aliases,
        compiler_params=_cparams(("parallel", "arbitrary")),
        name="ssd",
    )(*args)


def _attn_lat_kernel(q_ref, k_ref, v_ref, gb_ref, kc_ref, vc_ref, lv_ref, hw_ref, o_ref, vt_scr, vct_scr,
                     *, lam_init, kb):
    tq = q_ref.shape[0]
    T = k_ref.shape[0]
    lc = kc_ref.shape[0]
    qi = pl.program_id(2)

    @pl.when(qi == 0)
    def _():
        for c0 in range(0, T, 128):
            vt_scr[0:DA_VD, c0:c0 + 128] = v_ref[c0:c0 + 128, :].astype(F32).T.astype(BF16)
        vt_scr[DA_VD:DA_VD + 16, :] = jnp.ones((16, T), BF16)
        for c0 in range(0, lc, 128):
            vct_scr[0:DA_VD, c0:c0 + 128] = vc_ref[c0:c0 + 128, :].T.astype(BF16)
        vct_scr[DA_VD:DA_VD + 16, :] = jnp.ones((16, lc), BF16)

    lv = lv_ref[...]
    lam = (jnp.exp(jnp.sum(lv[0:1, :] * lv[1:2, :], axis=-1, keepdims=True))
           - jnp.exp(jnp.sum(lv[2:3, :] * lv[3:4, :], axis=-1, keepdims=True)) + lam_init)
    sq = ATT_SUBQ
    lane = lax.broadcasted_iota(jnp.int32, (sq, 2 * DA_QK), 1)
    dn_t = (((1,), (1,)), ((), ()))

    q2s = []
    for c in range(tq // sq):
        q = q_ref[c * sq:(c + 1) * sq, :].astype(F32) * (DA_QK ** -0.5 * math.log2(math.e))
        q2s.append(jnp.concatenate([jnp.where(lane < DA_QK, q, 0.0), jnp.where(lane < DA_QK, 0.0, q)],
                                   axis=0).astype(BF16))

    def scores(k_blk, q2):
        s = lax.dot_general(k_blk, q2, dn_t, preferred_element_type=F32)
        return s, jnp.max(s, axis=0, keepdims=True)

    def accumulate(s, bm, vt_blk, state):
        m_new = bm if state is None else jnp.maximum(state[0], bm)
        p = jnp.exp2(s - m_new).astype(BF16)
        pv = jnp.dot(vt_blk, p, preferred_element_type=F32)
        if state is None:
            return m_new, pv
        return m_new, jnp.exp2(state[0] - m_new) * state[1] + pv

    kcb = kc_ref[...].astype(BF16)
    cur = [scores(kcb, q2) for q2 in q2s]
    cur_vt = vct_scr[...]
    states = [None] * len(q2s)
    for k0 in range(0, T, kb):
        k_blk = k_ref[k0:k0 + kb, :]
        nxt = [scores(k_blk, q2) for q2 in q2s]
        states = [accumulate(cu[0], cu[1], cur_vt, st) for cu, st in zip(cur, states)]
        cur, cur_vt = nxt, vt_scr[:, k0:k0 + kb]
    states = [accumulate(cu[0], cu[1], cur_vt, st) for cu, st in zip(cur, states)]
    for c, (_, acc) in enumerate(states):
        rs = slice(c * sq, (c + 1) * sq)
        r = 1.0 / acc[DA_VD:DA_VD + 1, :]
        o_t = acc[0:DA_VD, 0:sq] * r[:, 0:sq] - acc[0:DA_VD, sq:2 * sq] * (r[:, sq:2 * sq] * lam)
        o = o_t.T
        ms = jnp.mean(o * o, axis=-1, keepdims=True)
        o = o * lax.rsqrt(ms + EPS) * hw_ref[...] * (1.0 - lam_init)
        o_ref[rs, :] = (o * _silu(gb_ref[rs, :].astype(F32))).astype(o_ref.dtype)


def _attn_ctx_kernel(q_ref, k_ref, v_ref, gb_ref, lv_ref, hw_ref, o_ref, *, lam_init):
    T = q_ref.shape[0]
    lv = lv_ref[...]
    lam = (jnp.exp(jnp.sum(lv[0:1, :] * lv[1:2, :], axis=-1, keepdims=True))
           - jnp.exp(jnp.sum(lv[2:3, :] * lv[3:4, :], axis=-1, keepdims=True)) + lam_init)
    lane = lax.broadcasted_iota(jnp.int32, (T, 2 * DA_QK), 1)
    dn_t = (((1,), (1,)), ((), ()))
    ones = jnp.ones((16, T), BF16)
    for hh in range(DA_HEADS):
        cs = slice(hh * 128, (hh + 1) * 128)
        q = q_ref[:, cs].astype(F32) * (DA_QK ** -0.5 * math.log2(math.e))
        q2 = jnp.concatenate([jnp.where(lane < DA_QK, q, 0.0), jnp.where(lane < DA_QK, 0.0, q)],
                             axis=0).astype(BF16)
        v = v_ref[:, cs].astype(F32)
        vt = jnp.concatenate([v[c0:c0 + 128, :].T for c0 in range(0, T, 128)], axis=1).astype(BF16)
        vt1 = jnp.concatenate([vt, ones], axis=0)
        s = lax.dot_general(k_ref[:, cs], q2, dn_t, preferred_element_type=F32)
        p = jnp.exp2(s - jnp.max(s, axis=0, keepdims=True)).astype(BF16)
        acc = jnp.dot(vt1, p, preferred_element_type=F32)
        r = 1.0 / acc[DA_VD:DA_VD + 1, :]
        o_t = acc[0:DA_VD, 0:T] * r[:, 0:T] - acc[0:DA_VD, T:2 * T] * (r[:, T:2 * T] * lam)
        o = jnp.concatenate([o_t[:, c0:c0 + 128].T for c0 in range(0, T, 128)], axis=0)
        ms = jnp.mean(o * o, axis=-1, keepdims=True)
        o = o * lax.rsqrt(ms + EPS) * hw_ref[...] * (1.0 - lam_init)
        o_ref[:, cs] = (o * _silu(gb_ref[:, cs].astype(F32))).astype(o_ref.dtype)


def _attention_ctx(main, nseq, T, lam_vecs, head_w, lam_init):
    return pl.pallas_call(
        functools.partial(_attn_ctx_kernel, lam_init=lam_init),
        out_shape=jax.ShapeDtypeStruct((nseq * T, D_MODEL), BF16),
        grid=(nseq,),
        in_specs=[
            pl.BlockSpec((T, D_MODEL), lambda b: (b, SEG_Q)),
            pl.BlockSpec((T, D_MODEL), lambda b: (b, SEG_K)),
            pl.BlockSpec((T, D_MODEL), lambda b: (b, SEG_V)),
            pl.BlockSpec((T, D_MODEL), lambda b: (b, SEG_GB)),
            pl.BlockSpec((4, DA_QK), lambda b: (0, 0)),
            pl.BlockSpec((1, DA_VD), lambda b: (0, 0)),
        ],
        out_specs=pl.BlockSpec((T, D_MODEL), lambda b: (b, 0)),
        compiler_params=_cparams(("parallel",)),
        name="diff_attn_ctx",
    )(main, main, main, main, lam_vecs, head_w)


def _attention_latent(hm, nseq, T, tq, kb, lam_vecs, head_w, lam_init, ck, cv):
    R = nseq * T
    nq = T // tq
    lc = ck.shape[1]
    return pl.pallas_call(
        functools.partial(_attn_lat_kernel, lam_init=lam_init, kb=kb),
        out_shape=jax.ShapeDtypeStruct((R, D_MODEL), BF16),
        grid=(nseq, DA_HEADS, nq),
        in_specs=[
            pl.BlockSpec((None, None, None, tq, 128), lambda b, g, qi: (0, b, g, qi, 0)),
            pl.BlockSpec((None, None, None, T, 128), lambda b, g, qi: (1, b, g, 0, 0)),
            pl.BlockSpec((None, None, None, T, 128), lambda b, g, qi: (2, b, g, 0, 0)),
            pl.BlockSpec((None, None, None, tq, 128), lambda b, g, qi: (3, b, g, qi, 0)),
            pl.BlockSpec((None, lc, 128), lambda b, g, qi: (b, 0, g)),
            pl.BlockSpec((None, lc, 128), lambda b, g, qi: (b, 0, g)),
            pl.BlockSpec((4, DA_QK), lambda b, g, qi: (0, 0)),
            pl.BlockSpec((1, DA_VD), lambda b, g, qi: (0, 0)),
        ],
        out_specs=pl.BlockSpec((tq, 128), lambda b, g, qi: (b * nq + qi, g)),
        scratch_shapes=[pltpu.VMEM((DA_VD + 16, T), BF16), pltpu.VMEM((DA_VD + 16, lc), BF16)],
        compiler_params=_cparams(("parallel", "parallel", "arbitrary")),
        name="diff_attn_lat",
    )(hm, hm, hm, hm, ck, cv, lam_vecs, head_w)


def _sgmlp_merge_kernel(ya_ref, yb_ref, u_ref, sv_ref, gc_ref, ga_ref, gb_ref, gm_ref, vw_ref, ws_ref, bias_ref,
                        wb_ref, wo_ref, pw_ref, mod_ref, x_ref, o_ref):
    tm = u_ref.shape[0]

    def gated(y, g_ref, i):
        p = jnp.dot(y, wb_ref[i], preferred_element_type=F32)
        return _sigmoid(g_ref[...].astype(F32)) * p

    merged = gated(ya_ref[...], ga_ref, 0) + gated(yb_ref[...], gb_ref, 1)

    u = _gelu_tanh(u_ref[...].astype(F32))
    v = _gelu_tanh(sv_ref[...].astype(F32))
    vc = v - jnp.mean(v, axis=-1, keepdims=True)
    vb = (vc * lax.rsqrt(jnp.mean(vc * vc, axis=-1, keepdims=True) + EPS) * vw_ref[...]).astype(BF16)
    ug = u * _silu(gc_ref[...].astype(F32))
    rows = []
    for ci in range(tm // SG_CHUNK):
        rs = slice(ci * SG_CHUNK, (ci + 1) * SG_CHUNK)
        cols = [jnp.dot(ws_ref[g], vb[rs, g * 128:(g + 1) * 128], preferred_element_type=F32)
                for g in range(SG_GROUPS)]
        rows.append(((jnp.concatenate(cols, axis=1) + bias_ref[...]) * ug[rs]).astype(BF16))
    yc = jnp.concatenate(rows, axis=0)

    merged = merged + gated(yc, gm_ref, 2)
    o = jnp.dot(merged.astype(BF16), wo_ref[...], preferred_element_type=F32)
    ms = jnp.mean(o * o, axis=-1, keepdims=True)
    o = o * lax.rsqrt(ms + EPS) * pw_ref[...]
    o_ref[...] = x_ref[...] + mod_ref[2:3, :] * o


def _sgmlp_merge(ya, yb, main, seg_shift, vnorm_w, ws_bf, bias_exp, wb_bf, wo_bf, l, post_w, mod, rows_per_mod, x,
                 tm=512):
    R = x.shape[0]
    row = lambda i: (i, 0)
    seg = lambda s: pl.BlockSpec((tm, D_MODEL), lambda i: (i, s - seg_shift))
    const = dict(pipeline_mode=pl.Buffered(1))
    return pl.pallas_call(
        _sgmlp_merge_kernel,
        out_shape=jax.ShapeDtypeStruct((R, D_MODEL), F32),
        grid=(R // tm,),
        in_specs=[
            pl.BlockSpec((tm, D_MODEL), row),
            pl.BlockSpec((tm, D_MODEL), row),
            seg(SEG_U), seg(SEG_SV), seg(SEG_GC), seg(SEG_MGA), seg(SEG_MGB), seg(SEG_MGC),
            pl.BlockSpec((1, D_MODEL), lambda i: (0, 0)),
            pl.BlockSpec((None, SG_GROUPS, SG_CHUNK, SG_CHUNK), lambda i: (l, 0, 0, 0), **const),
            pl.BlockSpec((None, SG_CHUNK, D_MODEL), lambda i: (l, 0, 0), **const),
            pl.BlockSpec((None, 3, D_MODEL, D_MODEL), lambda i: (l, 0, 0, 0), **const),
            pl.BlockSpec((None, D_MODEL, D_MODEL), lambda i: (l, 0, 0), **const),
            pl.BlockSpec((1, D_MODEL), lambda i: (0, 0)),
            pl.BlockSpec((None, 3, D_MODEL), lambda i: ((i * tm) // rows_per_mod, 0, 0)),
            pl.BlockSpec((tm, D_MODEL), row),
        ],
        out_specs=pl.BlockSpec((tm, D_MODEL), row),
        compiler_params=_cparams(("parallel",)),
        name="sgmlp_merge",
    )(ya, yb, main, main, main, main, main, main, vnorm_w, ws_bf, bias_exp, wb_bf, wo_bf, post_w, mod, x)


def _rope_tables(n_tokens):
    n_rows = n_tokens // GRID_W
    rows = jnp.repeat(jnp.arange(n_rows, dtype=F32), GRID_W)
    cols = jnp.tile(jnp.arange(GRID_W, dtype=F32), n_rows)
    n_freq = DA_QK // 4
    inv = ROPE_BASE ** (-jnp.arange(n_freq, dtype=F32) / n_freq)
    ang = jnp.concatenate([rows[:, None] * inv, cols[:, None] * inv], -1)
    cos, sin = jnp.cos(ang), jnp.sin(ang)
    return jnp.tile(cos, (1, 4)), jnp.concatenate([-sin, sin, -sin, sin], axis=1)


def _state_to_pairs(s):
    lead = s.shape[:-3]
    s = s.reshape(lead + (M_PAIRS, 2, M_HEADDIM, M_STATE))
    s = jnp.moveaxis(s, -1, -3)
    return s.reshape(lead + (M_PAIRS, M_STATE, 2 * M_HEADDIM))


def kernel(x_prompt, x_sample, cache_k, cache_v, state_ssm, c, c_ctx, pre_norm_w, post_norm_w, w_mod, b_mod, w_in,
           m_conv_w, m_conv_b, m_A_log, m_dt_bias, m_D, m_norm_w, da_lambda, da_head_norm_w, sg_vnorm_w,
           sg_spatial_w, sg_spatial_b, w_branch, w_out):
    depth = w_in.shape[0]
    nb, seq, _ = x_prompt.shape
    db, dseq, _ = x_sample.shape
    past = cache_k.shape[2]

    w_main, w_misc = _prep_in_weights(jnp.swapaxes(w_in, 1, 2))
    wb_bf = w_branch.astype(BF16)
    wo_bf = w_out.astype(BF16)
    ws_bf = sg_spatial_w.astype(BF16)
    bias_exp = jnp.repeat(jnp.swapaxes(sg_spatial_b, 1, 2), D_MODEL // SG_GROUPS, axis=2)
    cw = jnp.swapaxes(m_conv_w, 1, 2)
    dtb = jnp.pad(m_dt_bias.reshape(depth, 1, 2 * M_HEADS), ((0, 0), (0, 0), (0, 128 - 2 * M_HEADS)))
    alog = jnp.pad(m_A_log.reshape(depth, 1, 2 * M_HEADS), ((0, 0), (0, 0), (0, 128 - 2 * M_HEADS)))
    dsum = jnp.repeat(m_D[:, 0] + m_D[:, 1], M_HEADDIM, axis=1).reshape(depth, 1, D_MODEL)

    cvec = jnp.concatenate([c_ctx[None, :], c, jnp.zeros((8 - 1 - db, D_MODEL), F32)], axis=0)
    mods = _modulation(cvec, w_mod, b_mod).reshape(depth, 8, 3, D_MODEL)

    rope_tabs = _rope_tables(dseq)
    init_pairs = _state_to_pairs(state_ssm)
    ck = cache_k.reshape(db, depth, past, D_MODEL)
    cv = cache_v.reshape(db, depth, past, D_MODEL)

    xp = x_prompt.reshape(nb * seq, D_MODEL)
    xs = x_sample.reshape(db * dseq, D_MODEL)
    ks_out, vs_out = [], []

    def layer(x, l, nseq, T, mod, rows_per_mod, latent, st_stack=None):
        lam_init = 0.8 - 0.6 * math.exp(-0.3 * l)
        outs = _inproj(x, mod, pre_norm_w[l][None], w_main, w_misc, l, rows_per_mod,
                       rope_tabs if latent else None, emit_kv=not latent)
        main, misc = outs[0], outs[1]
        ya, st = _ssd(main, misc, nseq, T, cw[l, :, 0:D_MODEL], m_conv_b[l][None, 0:D_MODEL],
                      cw[l, :, D_MODEL:], m_conv_b[l][None, D_MODEL:], dtb[l], alog[l], dsum[l],
                      m_norm_w[l][None], init_pairs[:, l] if latent else None,
                      st_stack=None if latent else st_stack, layer=0 if latent else l,
                      n_layers=1 if latent else depth, cps=min(SSD_CHUNKS_PER_STEP, T // M_CHUNK))
        if latent:
            yb = _attention_latent(outs[2], nseq, T, ATT_TQ, ATT_KB, da_lambda[l], da_head_norm_w[l][None],
                                   lam_init, ck[:, l], cv[:, l])
        else:
            yb = _attention_ctx(main, nseq, T, da_lambda[l], da_head_norm_w[l][None], lam_init)
        shift = LATENT_SEG_SHIFT if latent else 0
        x_new = _sgmlp_merge(ya, yb, main, shift, sg_vnorm_w[l][None], ws_bf, bias_exp, wb_bf, wo_bf, l,
                             post_norm_w[l][None], mod, rows_per_mod, x)
        return x_new, outs[2:], st

    st_all = None
    for l in range(depth):
        xp, kv, st_all = layer(xp, l, nb, seq, mods[l, 0:1], nb * seq, False, st_all)
        ks_out.append(kv[0].reshape(nb, seq, DA_HEADS, 2 * DA_QK))
        vs_out.append(kv[1].reshape(nb, seq, DA_HEADS, DA_VD))
        xs, _, _ = layer(xs, l, db, dseq, mods[l, 1:1 + db], dseq, True)

    return (xp.reshape(nb, seq, D_MODEL), xs.reshape(db, dseq, D_MODEL),
            jnp.stack(ks_out, 1), jnp.stack(vs_out, 1),
            st_all.reshape(nb, depth, 2, M_HEADS, M_HEADDIM, M_STATE))
```

```python
import functools
import math

import jax
import jax.numpy as jnp
from jax import lax
from jax.experimental import pallas as pl
from jax.experimental.pallas import tpu as pltpu

F32 = jnp.float32
BF16 = jnp.bfloat16

D_MODEL = 1024
EPS = 1e-6
GRID_W = 64
ROPE_BASE = 10000.0
M_HEADS = 16
M_HEADDIM = 64
M_STATE = 64
M_GROUPS = 2
M_CHUNK = 128
M_PAIRS = M_HEADS // 2
DA_HEADS = 8
DA_QK = 64
DA_VD = 128
SG_GROUPS = 8
SG_CHUNK = 128

SEG_Z, SEG_XS, SEG_Q, SEG_K, SEG_V, SEG_GB, SEG_U, SEG_SV, SEG_GC, SEG_MGA, SEG_MGB, SEG_MGC = range(12)
N_SEG = 12
LATENT_SEG_SHIFT = 4
MISC_W = 384

VMEM_LIMIT = 56 * 1024 * 1024
NEG_BIG = -1e30
ATT_SUBQ = 128
ATT_TQ = 512
ATT_KB = 512
SSD_CHUNKS_PER_STEP = 4
INPROJ_TM = 1024
INPROJ_TN = 2048
INPROJ_SUBN = 1024


def _silu(x):
    return x * (1.0 / (1.0 + jnp.exp(-x)))


def _sigmoid(x):
    return 1.0 / (1.0 + jnp.exp(-x))


def _gelu_tanh(x):
    return 0.5 * x * (1.0 + jnp.tanh(math.sqrt(2.0 / math.pi) * (x + 0.044715 * (x * x * x))))


def _cparams(sem):
    return pltpu.CompilerParams(dimension_semantics=sem, vmem_limit_bytes=VMEM_LIMIT)


def _mod_kernel(c_ref, w_ref, b_ref, o_ref):
    c = c_ref[...]
    s = _silu(c).astype(BF16)
    o_ref[...] = jnp.dot(s, w_ref[...].astype(BF16), preferred_element_type=F32) + b_ref[...]


def _modulation(cvec, w_mod, b_mod):
    depth = w_mod.shape[0]
    nt = 3
    return pl.pallas_call(
        _mod_kernel,
        out_shape=jax.ShapeDtypeStruct((depth, 8, 3 * D_MODEL), F32),
        grid=(depth, nt),
        in_specs=[
            pl.BlockSpec((8, D_MODEL), lambda l, j: (0, 0)),
            pl.BlockSpec((None, D_MODEL, D_MODEL), lambda l, j: (l, 0, j)),
            pl.BlockSpec((None, 1, D_MODEL), lambda l, j: (l, 0, j)),
        ],
        out_specs=pl.BlockSpec((None, 8, D_MODEL), lambda l, j: (l, 0, j)),
        compiler_params=_cparams(("arbitrary", "arbitrary")),
        name="modulation",
    )(cvec, w_mod, b_mod.reshape(depth, 1, 3 * D_MODEL))


W_IN_COLS = 12576
W_XS_END = 2048
W_MISC_END = 2336


def _wprep_kernel(wa_ref, wb_ref, wm_ref, main_ref, misc_ref):
    j = pl.program_id(1)

    @pl.when(j < W_XS_END // D_MODEL)
    def _():
        main_ref[...] = wa_ref[...].T.astype(BF16)

    @pl.when(j >= W_XS_END // D_MODEL)
    def _():
        main_ref[...] = wb_ref[...].T.astype(BF16)

    @pl.when(j == 0)
    def _():
        n = W_MISC_END - W_XS_END
        wm = jnp.concatenate([wm_ref[...], jnp.zeros((MISC_W - n, D_MODEL), F32)], axis=0)
        misc_ref[...] = wm.T.astype(BF16)


def _prep_in_weights(w_in_t):
    depth = w_in_t.shape[0]
    n_head = W_XS_END // D_MODEL
    w2d = w_in_t.reshape(depth * W_IN_COLS, D_MODEL)
    skew = W_MISC_END - W_XS_END
    return pl.pallas_call(
        _wprep_kernel,
        out_shape=[jax.ShapeDtypeStruct((depth, D_MODEL, N_SEG * D_MODEL), BF16),
                   jax.ShapeDtypeStruct((depth, D_MODEL, MISC_W), BF16)],
        grid=(depth, N_SEG),
        in_specs=[pl.BlockSpec((None, D_MODEL, D_MODEL), lambda l, j: (l, jnp.minimum(j, n_head - 1), 0)),
                  pl.BlockSpec((pl.Element(D_MODEL), pl.Element(D_MODEL)),
                               lambda l, j: (pl.multiple_of(l * W_IN_COLS + skew + j * D_MODEL, 32), 0)),
                  pl.BlockSpec((pl.Element(skew), pl.Element(D_MODEL)),
                               lambda l, j: (pl.multiple_of(l * W_IN_COLS + W_XS_END, 32), 0))],
        out_specs=[pl.BlockSpec((None, D_MODEL, D_MODEL), lambda l, j: (l, 0, j)),
                   pl.BlockSpec((None, D_MODEL, MISC_W), lambda l, j: (l, 0, 0))],
        compiler_params=_cparams(("parallel", "arbitrary")),
        name="w_prep",
    )(w_in_t, w2d, w2d)


def _inproj_kernel(*refs, rope, emit_kv):
    it = iter(refs)
    x_ref, mod_ref, prew_ref, w_ref, wm_ref = next(it), next(it), next(it), next(it), next(it)
    if rope:
        cos_ref, sin_ref = next(it), next(it)
    main_ref, misc_ref = next(it), next(it)
    if emit_kv:
        kf_ref, vf_ref = next(it), next(it)
    if rope:
        hm_ref = next(it)
    h_scr = next(it)

    j = pl.program_id(1)
    tm = x_ref.shape[0]
    tn = w_ref.shape[1]
    spt = tn // D_MODEL

    @pl.when(j == 0)
    def _():
        x = x_ref[...]
        ms = jnp.mean(x * x, axis=-1, keepdims=True)
        y = x * lax.rsqrt(ms + EPS) * prew_ref[...]
        h = y * (1.0 + mod_ref[1:2, :]) + mod_ref[0:1, :]
        hb = h.astype(BF16)
        h_scr[...] = hb
        misc_ref[...] = jnp.dot(hb, wm_ref[...], preferred_element_type=F32)

    h = h_scr[...]
    sub = INPROJ_SUBN
    for c0 in range(0, tn, sub):
        cs = slice(c0, c0 + sub)
        seg = j * spt + c0 // D_MODEL
        lc = c0 % D_MODEL
        ls = slice(lc, lc + sub)
        acc = jnp.dot(h, w_ref[:, cs], preferred_element_type=F32)

        if rope:
            is_rope = jnp.logical_or(seg == SEG_Q, seg == SEG_K)
            is_plain_hm = jnp.logical_or(seg == SEG_V, seg == SEG_GB)

            def store_heads(val):
                vb = val.astype(hm_ref.dtype)
                for hh in range(sub // 128):
                    hm_ref[c0 // D_MODEL, lc // 128 + hh] = vb[:, hh * 128:(hh + 1) * 128]

            @pl.when(is_rope)
            def _():
                cos = jnp.concatenate([cos_ref[...]] * (sub // 128), axis=1)
                sin = jnp.concatenate([sin_ref[...]] * (sub // 128), axis=1)
                lane = lax.broadcasted_iota(jnp.int32, (tm, sub), 1)
                first_half = (lane & (DA_QK - 1)) < (DA_QK // 2)
                rot = jnp.where(first_half, pltpu.roll(acc, sub - DA_QK // 2, 1), pltpu.roll(acc, DA_QK // 2, 1))
                store_heads(acc * cos + rot * sin)

            @pl.when(is_plain_hm)
            def _():
                store_heads(acc)

            @pl.when(jnp.logical_not(jnp.logical_or(is_rope, is_plain_hm)))
            def _():
                main_ref[:, cs] = acc.astype(main_ref.dtype)
        else:
            main_ref[:, cs] = acc.astype(main_ref.dtype)

        if emit_kv:
            @pl.when(seg == SEG_K)
            def _():
                kf_ref[:, ls] = acc

            @pl.when(seg == SEG_V)
            def _():
                vf_ref[:, ls] = acc


def _inproj(x, mod, pre_w, w_main, w_misc, l, rows_per_mod, rope_tabs, emit_kv, tm=INPROJ_TM, tn=INPROJ_TN):
    R = x.shape[0]
    ni = R // tm
    spt = tn // D_MODEL
    rope = rope_tabs is not None
    const = dict(pipeline_mode=pl.Buffered(1))
    in_specs = [
        pl.BlockSpec((tm, D_MODEL), lambda i, j: (i, 0), **const),
        pl.BlockSpec((None, 3, D_MODEL), lambda i, j: ((i * tm) // rows_per_mod, 0, 0)),
        pl.BlockSpec((1, D_MODEL), lambda i, j: (0, 0)),
        pl.BlockSpec((None, D_MODEL, tn), lambda i, j: (l, 0, j)),
        pl.BlockSpec((None, D_MODEL, MISC_W), lambda i, j: (l, 0, 0), **const),
    ]
    args = [x, mod, pre_w, w_main, w_misc]
    if rope:
        cos, sin = rope_tabs
        nt = cos.shape[0] // tm
        in_specs += [pl.BlockSpec((tm, 128), lambda i, j: (i % nt, 0)),
                     pl.BlockSpec((tm, 128), lambda i, j: (i % nt, 0))]
        args += [cos, sin]
    if rope:
        n_main = N_SEG - LATENT_SEG_SHIFT
        main_idx = lambda i, j: (i, jnp.where(j < SEG_Q // spt, j,
                                              jnp.where(j <= SEG_GB // spt, SEG_XS // spt,
                                                        j - LATENT_SEG_SHIFT // spt)))
    else:
        n_main = N_SEG
        main_idx = lambda i, j: (i, j)
    out_shape = [jax.ShapeDtypeStruct((R, n_main * D_MODEL), BF16), jax.ShapeDtypeStruct((R, MISC_W), F32)]
    out_specs = [pl.BlockSpec((tm, tn), main_idx), pl.BlockSpec((tm, MISC_W), lambda i, j: (i, 0))]
    if emit_kv:
        out_shape += [jax.ShapeDtypeStruct((R, D_MODEL), F32)] * 2
        out_specs += [pl.BlockSpec((tm, D_MODEL), lambda i, j: (i, 0))] * 2
    if rope:
        T = cos.shape[0]
        out_shape.append(jax.ShapeDtypeStruct((4, R // T, DA_HEADS, T, 128), BF16))
        out_specs.append(pl.BlockSpec(
            (spt, None, DA_HEADS, tm, 128),
            lambda i, j: (jnp.clip(j - SEG_Q // spt, 0, 4 // spt - 1), (i * tm) // T, 0, ((i * tm) % T) // tm, 0)))
    return pl.pallas_call(
        functools.partial(_inproj_kernel, rope=rope, emit_kv=emit_kv),
        out_shape=out_shape,
        grid=(ni, N_SEG // spt),
        in_specs=in_specs,
        out_specs=out_specs,
        scratch_shapes=[pltpu.VMEM((tm, D_MODEL), BF16)],
        compiler_params=_cparams(("parallel", "arbitrary")),
        name="inproj",
    )(*args)


def _split3(a):
    hi = a.astype(BF16)
    r1 = a - hi.astype(F32)
    mid = r1.astype(BF16)
    lo = (r1 - mid.astype(F32)).astype(BF16)
    return hi, mid, lo


def _ssd_kernel(*refs, nc, cps, has_init, has_prev):
    it = iter(refs)
    xs_ref, xsp_ref, xsn_ref = next(it), next(it), next(it)
    mi_ref, mip_ref, min_ref = next(it), next(it), next(it)
    z_ref = next(it)
    cwx_ref, cbx_ref, cwm_ref, cbm_ref = next(it), next(it), next(it), next(it)
    dtb_ref, alog_ref, dsum_ref, nw_ref = next(it), next(it), next(it), next(it)
    if has_init:
        init_ref = next(it)
    if has_prev:
        next(it)
    y_ref, sto_ref = next(it), next(it)
    xc_scr, bc_scr, yf_scr, st_scr = next(it), next(it), next(it), next(it)

    L = M_CHUNK
    LB = cps * L
    ns = nc // cps
    j = pl.program_id(1)
    fwd = j < ns
    c = jnp.where(fwd, j, 2 * ns - 1 - j)
    row0 = pl.multiple_of(c * LB, LB)
    LOG2E = math.log2(math.e)

    rid = lax.broadcasted_iota(jnp.int32, (L, L), 0)
    cid = lax.broadcasted_iota(jnp.int32, (L, L), 1)

    def conv_silu(x, prow, nrow, w_ref, b_ref):
        n = x.shape[1]
        r = lax.broadcasted_iota(jnp.int32, (LB, n), 0)
        xm = jnp.where(r == 0, prow, pltpu.roll(x, 1, 0))
        xp = jnp.where(r == LB - 1, nrow, pltpu.roll(x, LB - 1, 0))
        y = w_ref[0:1, :] * xm + w_ref[1:2, :] * x + w_ref[2:3, :] * xp + b_ref[...]
        return _silu(y)

    def load_state(d):
        if has_init:
            for i in range(M_PAIRS):
                st_scr[i] = init_ref[d, i]
        else:
            st_scr[...] = jnp.zeros_like(st_scr)

    def store_state(d):
        pad = jnp.zeros((2 * M_HEADDIM - M_STATE, 2 * M_HEADDIM), F32)
        for i in range(M_PAIRS):
            t = jnp.concatenate([st_scr[i], pad], axis=0).T
            sto_ref[d, i * 2 * M_HEADDIM:(i + 1) * 2 * M_HEADDIM, :] = t[:, 0:M_STATE]

    lane_lo = cid < M_HEADDIM
    lane_lo_s = lax.broadcasted_iota(jnp.int32, (M_STATE, 2 * M_HEADDIM), 1) < M_HEADDIM
    heads_per_group = M_HEADS // M_GROUPS
    zeros_s = jnp.zeros((M_STATE, 2 * M_HEADDIM), BF16)

    def setup(d, bc, dt_raw):
        tri = (cid <= rid) if d == 0 else (cid >= rid)
        tri_bf = jnp.where(tri, 1.0, 0.0).astype(BF16)
        dt = dt_raw + dtb_ref[...]
        dt = jnp.maximum(dt, 0.0) + jnp.log1p(jnp.exp(-jnp.abs(dt)))
        a = dt * (-jnp.exp(alog_ref[...]))
        a_hi, a_mid, a_lo = _split3(a)
        p_col = (jnp.dot(tri_bf, a_hi, preferred_element_type=F32)
                 + jnp.dot(tri_bf, a_mid, preferred_element_type=F32)
                 + jnp.dot(tri_bf, a_lo, preferred_element_type=F32))
        p_row = p_col.T[d * M_HEADS:(d + 1) * M_HEADS, :]
        dt_row = dt.T[d * M_HEADS:(d + 1) * M_HEADS, :]
        tot = p_row[:, L - 1:L] if d == 0 else p_row[:, 0:1]
        b_all = bc[:, 0:M_GROUPS * M_STATE]
        c_all = bc[:, M_GROUPS * M_STATE:2 * M_GROUPS * M_STATE]
        b_bf = b_all.astype(BF16)
        g_mats = []
        for g in range(M_GROUPS):
            cg = jnp.where(lane_lo if g == 0 else jnp.logical_not(lane_lo), c_all, 0.0).astype(BF16)
            g_mats.append(lax.dot_general(cg, b_bf, (((1,), (1,)), ((), ())), preferred_element_type=F32))
        return dict(
            tri=tri, g_mats=g_mats, c_all=c_all, bt_all=b_all.T,
            p2_col=p_col * LOG2E,
            q2_row=(p_row - jnp.log(dt_row)) * LOG2E,
            w_row=dt_row * jnp.exp(tot - p_row),
            etot=jnp.exp(tot),
            ep_col=jnp.exp(p_col))

    def pairs(d, s, x_bf):
        outs = []
        for i in range(M_PAIRS):
            g = (2 * i) // heads_per_group
            btg = s["bt_all"][g * M_STATE:(g + 1) * M_STATE, :]
            lhs_rows = []
            bw_rows = []
            for hh in range(2):
                h = 2 * i + hh
                col = d * M_HEADS + h
                pc = jnp.broadcast_to(s["p2_col"][:, col:col + 1], (L, L))
                dm = jnp.exp2(jnp.where(s["tri"], pc - s["q2_row"][h:h + 1, :], NEG_BIG))
                m_h = s["g_mats"][g] * dm
                ce_h = s["c_all"] * jnp.broadcast_to(s["ep_col"][:, col:col + 1], (L, L))
                lhs_rows.append(jnp.concatenate([m_h, ce_h], axis=1).astype(BF16))
                bw = (btg * s["w_row"][h:h + 1, :]).astype(BF16)
                bw_rows.append(jnp.concatenate([bw, zeros_s], axis=1))
            lhs = jnp.concatenate(lhs_rows + bw_rows, axis=0)
            x_pair = x_bf[:, i * 128:(i + 1) * 128]
            st_pair = st_scr[i]
            st_bf = st_pair.astype(BF16)
            rhs = jnp.concatenate([x_pair] + ([st_bf, zeros_s] if g == 0 else [zeros_s, st_bf]), axis=0)
            res = jnp.dot(lhs, rhs, preferred_element_type=F32)
            y_pair = jnp.where(lane_lo, res[0:L], res[L:2 * L])
            ds = jnp.where(lane_lo_s, res[2 * L:2 * L + M_STATE], res[2 * L + M_STATE:2 * L + 2 * M_STATE])
            e0 = jnp.broadcast_to(s["etot"][2 * i:2 * i + 1, :], (M_STATE, 2 * M_HEADDIM))
            e1 = jnp.broadcast_to(s["etot"][2 * i + 1:2 * i + 2, :], (M_STATE, 2 * M_HEADDIM))
            st_scr[i] = jnp.where(lane_lo_s, e0, e1) * st_pair + ds
            outs.append(y_pair)
        return jnp.concatenate(outs, axis=1)

    def block(d, x_bf, bc, dt_raw):
        order = range(cps) if d == 0 else range(cps - 1, -1, -1)
        su = {ci: setup(d, bc[ci * L:(ci + 1) * L], dt_raw[ci * L:(ci + 1) * L]) for ci in order}
        ys = {ci: pairs(d, su[ci], x_bf[ci * L:(ci + 1) * L]) for ci in order}
        return jnp.concatenate([ys[ci] for ci in range(cps)], axis=0)

    @pl.when(j == 0)
    def _():
        load_state(0)

    @pl.when(j == ns)
    def _():
        load_state(1)

    @pl.when(fwd)
    def _():
        x = xs_ref[...].astype(F32)
        prow = jnp.where(c > 0, xsp_ref[...].astype(F32)[15:16, :], 0.0)
        nrow = jnp.where(c < ns - 1, xsn_ref[...].astype(F32)[0:1, :], 0.0)
        xc = conv_silu(x, prow, nrow, cwx_ref, cbx_ref).astype(BF16)
        xc_scr[pl.ds(row0, LB), :] = xc
        m = mi_ref[...]
        bcx = m[:, 0:256]
        prow_m = jnp.where(c > 0, mip_ref[7:8, 0:256], 0.0)
        nrow_m = jnp.where(c < ns - 1, min_ref[0:1, 0:256], 0.0)
        bc = conv_silu(bcx, prow_m, nrow_m, cwm_ref, cbm_ref)
        bc_scr[pl.ds(row0, LB), :] = bc
        yf_scr[pl.ds(row0, LB), :] = block(0, xc, bc, m[:, 256:384])

    @pl.when(j == ns - 1)
    def _():
        store_state(0)

    @pl.when(jnp.logical_not(fwd))
    def _():
        xc = xc_scr[pl.ds(row0, LB), :]
        bc = bc_scr[pl.ds(row0, LB), :]
        yb = block(1, xc, bc, mi_ref[:, 256:384])
        y = yf_scr[pl.ds(row0, LB), :] + yb + dsum_ref[...] * xc.astype(F32)
        y = y * _silu(z_ref[...].astype(F32))
        ms = jnp.mean(y * y, axis=-1, keepdims=True)
        y_ref[...] = (y * lax.rsqrt(ms + EPS) * nw_ref[...]).astype(y_ref.dtype)

    @pl.when(j == 2 * ns - 1)
    def _():
        store_state(1)


def _ssd(main, misc, nseq, T, cwx, cbx, cwm, cbm, dtb, alog, dsum, nw, init, st_stack=None, layer=0, n_layers=1,
         cps=SSD_CHUNKS_PER_STEP):
    R = nseq * T
    nc = T // M_CHUNK
    L = cps * M_CHUNK
    ns = nc // cps
    nc, full_nc = ns, nc
    has_init = init is not None

    def cidx(j):
        return jnp.where(j < nc, j, 2 * nc - 1 - j)

    def oidx(j):
        return jnp.where(j < nc, nc - 1, 2 * nc - 1 - j)

    in_specs = [
        pl.BlockSpec((L, D_MODEL), lambda b, j: (b * nc + cidx(j), SEG_XS)),
        pl.BlockSpec((16, D_MODEL), lambda b, j: (jnp.maximum((b * nc + cidx(j)) * (L // 16) - 1, 0), SEG_XS)),
        pl.BlockSpec((16, D_MODEL),
                     lambda b, j: (jnp.minimum((b * nc + cidx(j) + 1) * (L // 16), R // 16 - 1), SEG_XS)),
        pl.BlockSpec((L, MISC_W), lambda b, j: (b * nc + cidx(j), 0)),
        pl.BlockSpec((8, MISC_W), lambda b, j: (jnp.maximum((b * nc + cidx(j)) * (L // 8) - 1, 0), 0)),
        pl.BlockSpec((8, MISC_W), lambda b, j: (jnp.minimum((b * nc + cidx(j) + 1) * (L // 8), R // 8 - 1), 0)),
        pl.BlockSpec((L, D_MODEL), lambda b, j: (b * nc + oidx(j), SEG_Z)),
        pl.BlockSpec((3, D_MODEL), lambda b, j: (0, 0)),
        pl.BlockSpec((1, D_MODEL), lambda b, j: (0, 0)),
        pl.BlockSpec((3, 256), lambda b, j: (0, 0)),
        pl.BlockSpec((1, 256), lambda b, j: (0, 0)),
        pl.BlockSpec((1, 128), lambda b, j: (0, 0)),
        pl.BlockSpec((1, 128), lambda b, j: (0, 0)),
        pl.BlockSpec((1, D_MODEL), lambda b, j: (0, 0)),
        pl.BlockSpec((1, D_MODEL), lambda b, j: (0, 0)),
    ]
    args = [main, main, main, misc, misc, misc, main, cwx, cbx, cwm, cbm, dtb, alog, dsum, nw]
    if has_init:
        in_specs.append(pl.BlockSpec((None, 2, M_PAIRS, M_STATE, 128), lambda b, j: (b, 0, 0, 0, 0)))
        args.append(init)
    aliases = {}
    if st_stack is not None:
        in_specs.append(pl.BlockSpec(memory_space=pl.ANY))
        args.append(st_stack)
        aliases = {len(args) - 1: 1}
    return pl.pallas_call(
        functools.partial(_ssd_kernel, nc=full_nc, cps=cps, has_init=has_init, has_prev=st_stack is not None),
        out_shape=[jax.ShapeDtypeStruct((R, D_MODEL), BF16),
                   jax.ShapeDtypeStruct((nseq, n_layers, 2, M_HEADS * M_HEADDIM, M_STATE), F32)],
        grid=(nseq, 2 * nc),
        in_specs=in_specs,
        out_specs=[pl.BlockSpec((L, D_MODEL), lambda b, j: (b * nc + oidx(j), 0)),
                   pl.BlockSpec((None, None, 2, M_HEADS * M_HEADDIM, M_STATE), lambda b, j: (b, layer, 0, 0, 0))],
        scratch_shapes=[pltpu.VMEM((T, D_MODEL), BF16), pltpu.VMEM((T, 256), F32),
                        pltpu.VMEM((T, D_MODEL), F32), pltpu.VMEM((M_PAIRS, M_STATE, 128), F32)],
        input_output_aliases=aliases,
        compiler_params=_cparams(("parallel", "arbitrary")),
        name="ssd",
    )(*args)


def _attn_lat_kernel(q_ref, k_ref, v_ref, gb_ref, kc_ref, vc_ref, lv_ref, hw_ref, o_ref, vt_scr, vct_scr,
                     *, lam_init, kb):
    tq = q_ref.shape[0]
    T = k_ref.shape[0]
    lc = kc_ref.shape[0]
    qi = pl.program_id(2)

    @pl.when(qi == 0)
    def _():
        for c0 in range(0, T, 128):
            vt_scr[0:DA_VD, c0:c0 + 128] = v_ref[c0:c0 + 128, :].astype(F32).T.astype(BF16)
        vt_scr[DA_VD:DA_VD + 16, :] = jnp.ones((16, T), BF16)
        for c0 in range(0, lc, 128):
            vct_scr[0:DA_VD, c0:c0 + 128] = vc_ref[c0:c0 + 128, :].T.astype(BF16)
        vct_scr[DA_VD:DA_VD + 16, :] = jnp.ones((16, lc), BF16)

    lv = lv_ref[...]
    lam = (jnp.exp(jnp.sum(lv[0:1, :] * lv[1:2, :], axis=-1, keepdims=True))
           - jnp.exp(jnp.sum(lv[2:3, :] * lv[3:4, :], axis=-1, keepdims=True)) + lam_init)
    sq = ATT_SUBQ
    lane = lax.broadcasted_iota(jnp.int32, (sq, 2 * DA_QK), 1)
    dn_t = (((1,), (1,)), ((), ()))

    q2s = []
    for c in range(tq // sq):
        q = q_ref[c * sq:(c + 1) * sq, :].astype(F32) * (DA_QK ** -0.5 * math.log2(math.e))
        q2s.append(jnp.concatenate([jnp.where(lane < DA_QK, q, 0.0), jnp.where(lane < DA_QK, 0.0, q)],
                                   axis=0).astype(BF16))

    def scores(k_blk, q2):
        s = lax.dot_general(k_blk, q2, dn_t, preferred_element_type=F32)
        return s, jnp.max(s, axis=0, keepdims=True)

    def accumulate(s, bm, vt_blk, state):
        m_new = bm if state is None else jnp.maximum(state[0], bm)
        p = jnp.exp2(s - m_new).astype(BF16)
        pv = jnp.dot(vt_blk, p, preferred_element_type=F32)
        if state is None:
            return m_new, pv
        return m_new, jnp.exp2(state[0] - m_new) * state[1] + pv

    kcb = kc_ref[...].astype(BF16)
    cur = [scores(kcb, q2) for q2 in q2s]
    cur_vt = vct_scr[...]
    states = [None] * len(q2s)
    for k0 in range(0, T, kb):
        k_blk = k_ref[k0:k0 + kb, :]
        nxt = [scores(k_blk, q2) for q2 in q2s]
        states = [accumulate(cu[0], cu[1], cur_vt, st) for cu, st in zip(cur, states)]
        cur, cur_vt = nxt, vt_scr[:, k0:k0 + kb]
    states = [accumulate(cu[0], cu[1], cur_vt, st) for cu, st in zip(cur, states)]
    for c, (_, acc) in enumerate(states):
        rs = slice(c * sq, (c + 1) * sq)
        r = 1.0 / acc[DA_VD:DA_VD + 1, :]
        o_t = acc[0:DA_VD, 0:sq] * r[:, 0:sq] - acc[0:DA_VD, sq:2 * sq] * (r[:, sq:2 * sq] * lam)
        o = o_t.T
        ms = jnp.mean(o * o, axis=-1, keepdims=True)
        o = o * lax.rsqrt(ms + EPS) * hw_ref[...] * (1.0 - lam_init)
        o_ref[rs, :] = (o * _silu(gb_ref[rs, :].astype(F32))).astype(o_ref.dtype)


def _attn_ctx_kernel(q_ref, k_ref, v_ref, gb_ref, lv_ref, hw_ref, o_ref, *, lam_init):
    T = q_ref.shape[0]
    lv = lv_ref[...]
    lam = (jnp.exp(jnp.sum(lv[0:1, :] * lv[1:2, :], axis=-1, keepdims=True))
           - jnp.exp(jnp.sum(lv[2:3, :] * lv[3:4, :], axis=-1, keepdims=True)) + lam_init)
    lane = lax.broadcasted_iota(jnp.int32, (T, 2 * DA_QK), 1)
    dn_t = (((1,), (1,)), ((), ()))
    ones = jnp.ones((16, T), BF16)
    for hh in range(DA_HEADS):
        cs = slice(hh * 128, (hh + 1) * 128)
        q = q_ref[:, cs].astype(F32) * (DA_QK ** -0.5 * math.log2(math.e))
        q2 = jnp.concatenate([jnp.where(lane < DA_QK, q, 0.0), jnp.where(lane < DA_QK, 0.0, q)],
                             axis=0).astype(BF16)
        v = v_ref[:, cs].astype(F32)
        vt = jnp.concatenate([v[c0:c0 + 128, :].T for c0 in range(0, T, 128)], axis=1).astype(BF16)
        vt1 = jnp.concatenate([vt, ones], axis=0)
        s = lax.dot_general(k_ref[:, cs], q2, dn_t, preferred_element_type=F32)
        p = jnp.exp2(s - jnp.max(s, axis=0, keepdims=True)).astype(BF16)
        acc = jnp.dot(vt1, p, preferred_element_type=F32)
        r = 1.0 / acc[DA_VD:DA_VD + 1, :]
        o_t = acc[0:DA_VD, 0:T] * r[:, 0:T] - acc[0:DA_VD, T:2 * T] * (r[:, T:2 * T] * lam)
        o = jnp.concatenate([o_t[:, c0:c0 + 128].T for c0 in range(0, T, 128)], axis=0)
        ms = jnp.mean(o * o, axis=-1, keepdims=True)
        o = o * lax.rsqrt(ms + EPS) * hw_ref[...] * (1.0 - lam_init)
        o_ref[:, cs] = (o * _silu(gb_ref[:, cs].astype(F32))).astype(o_ref.dtype)


def _attention_ctx(main, nseq, T, lam_vecs, head_w, lam_init):
    return pl.pallas_call(
        functools.partial(_attn_ctx_kernel, lam_init=lam_init),
        out_shape=jax.ShapeDtypeStruct((nseq * T, D_MODEL), BF16),
        grid=(nseq,),
        in_specs=[
            pl.BlockSpec((T, D_MODEL), lambda b: (b, SEG_Q)),
            pl.BlockSpec((T, D_MODEL), lambda b: (b, SEG_K)),
            pl.BlockSpec((T, D_MODEL), lambda b: (b, SEG_V)),
            pl.BlockSpec((T, D_MODEL), lambda b: (b, SEG_GB)),
            pl.BlockSpec((4, DA_QK), lambda b: (0, 0)),
            pl.BlockSpec((1, DA_VD), lambda b: (0, 0)),
        ],
        out_specs=pl.BlockSpec((T, D_MODEL), lambda b: (b, 0)),
        compiler_params=_cparams(("parallel",)),
        name="diff_attn_ctx",
    )(main, main, main, main, lam_vecs, head_w)


def _attention_latent(hm, nseq, T, tq, kb, lam_vecs, head_w, lam_init, ck, cv):
    R = nseq * T
    nq = T // tq
    lc = ck.shape[1]
    return pl.pallas_call(
        functools.partial(_attn_lat_kernel, lam_init=lam_init, kb=kb),
        out_shape=jax.ShapeDtypeStruct((R, D_MODEL), BF16),
        grid=(nseq, DA_HEADS, nq),
        in_specs=[
            pl.BlockSpec((None, None, None, tq, 128), lambda b, g, qi: (0, b, g, qi, 0)),
            pl.BlockSpec((None, None, None, T, 128), lambda b, g, qi: (1, b, g, 0, 0)),
            pl.BlockSpec((None, None, None, T, 128), lambda b, g, qi: (2, b, g, 0, 0)),
            pl.BlockSpec((None, None, None, tq, 128), lambda b, g, qi: (3, b, g, qi, 0)),
            pl.BlockSpec((None, lc, 128), lambda b, g, qi: (b, 0, g)),
            pl.BlockSpec((None, lc, 128), lambda b, g, qi: (b, 0, g)),
            pl.BlockSpec((4, DA_QK), lambda b, g, qi: (0, 0)),
            pl.BlockSpec((1, DA_VD), lambda b, g, qi: (0, 0)),
        ],
        out_specs=pl.BlockSpec((tq, 128), lambda b, g, qi: (b * nq + qi, g)),
        scratch_shapes=[pltpu.VMEM((DA_VD + 16, T), BF16), pltpu.VMEM((DA_VD + 16, lc), BF16)],
        compiler_params=_cparams(("parallel", "parallel", "arbitrary")),
        name="diff_attn_lat",
    )(hm, hm, hm, hm, ck, cv, lam_vecs, head_w)


def _sgmlp_merge_kernel(ya_ref, yb_ref, u_ref, sv_ref, gc_ref, ga_ref, gb_ref, gm_ref, vw_ref, ws_ref, bias_ref,
                        wb_ref, wo_ref, pw_ref, mod_ref, x_ref, o_ref):
    tm = u_ref.shape[0]

    def gated(y, g_ref, i):
        p = jnp.dot(y, wb_ref[i], preferred_element_type=F32)
        return _sigmoid(g_ref[...].astype(F32)) * p

    merged = gated(ya_ref[...], ga_ref, 0) + gated(yb_ref[...], gb_ref, 1)

    u = _gelu_tanh(u_ref[...].astype(F32))
    v = _gelu_tanh(sv_ref[...].astype(F32))
    vc = v - jnp.mean(v, axis=-1, keepdims=True)
    vb = (vc * lax.rsqrt(jnp.mean(vc * vc, axis=-1, keepdims=True) + EPS) * vw_ref[...]).astype(BF16)
    ug = u * _silu(gc_ref[...].astype(F32))
    rows = []
    for ci in range(tm // SG_CHUNK):
        rs = slice(ci * SG_CHUNK, (ci + 1) * SG_CHUNK)
        cols = [jnp.dot(ws_ref[g], vb[rs, g * 128:(g + 1) * 128], preferred_element_type=F32)
                for g in range(SG_GROUPS)]
        rows.append(((jnp.concatenate(cols, axis=1) + bias_ref[...]) * ug[rs]).astype(BF16))
    yc = jnp.concatenate(rows, axis=0)

    merged = merged + gated(yc, gm_ref, 2)
    o = jnp.dot(merged.astype(BF16), wo_ref[...], preferred_element_type=F32)
    ms = jnp.mean(o * o, axis=-1, keepdims=True)
    o = o * lax.rsqrt(ms + EPS) * pw_ref[...]
    o_ref[...] = x_ref[...] + mod_ref[2:3, :] * o


def _sgmlp_merge(ya, yb, main, seg_shift, vnorm_w, ws_bf, bias_exp, wb_bf, wo_bf, l, post_w, mod, rows_per_mod, x,
                 tm=512):
    R = x.shape[0]
    row = lambda i: (i, 0)
    seg = lambda s: pl.BlockSpec((tm, D_MODEL), lambda i: (i, s - seg_shift))
    const = dict(pipeline_mode=pl.Buffered(1))
    return pl.pallas_call(
        _sgmlp_merge_kernel,
        out_shape=jax.ShapeDtypeStruct((R, D_MODEL), F32),
        grid=(R // tm,),
        in_specs=[
            pl.BlockSpec((tm, D_MODEL), row),
            pl.BlockSpec((tm, D_MODEL), row),
            seg(SEG_U), seg(SEG_SV), seg(SEG_GC), seg(SEG_MGA), seg(SEG_MGB), seg(SEG_MGC),
            pl.BlockSpec((1, D_MODEL), lambda i: (0, 0)),
            pl.BlockSpec((None, SG_GROUPS, SG_CHUNK, SG_CHUNK), lambda i: (l, 0, 0, 0), **const),
            pl.BlockSpec((None, SG_CHUNK, D_MODEL), lambda i: (l, 0, 0), **const),
            pl.BlockSpec((None, 3, D_MODEL, D_MODEL), lambda i: (l, 0, 0, 0), **const),
            pl.BlockSpec((None, D_MODEL, D_MODEL), lambda i: (l, 0, 0), **const),
            pl.BlockSpec((1, D_MODEL), lambda i: (0, 0)),
            pl.BlockSpec((None, 3, D_MODEL), lambda i: ((i * tm) // rows_per_mod, 0, 0)),
            pl.BlockSpec((tm, D_MODEL), row),
        ],
        out_specs=pl.BlockSpec((tm, D_MODEL), row),
        compiler_params=_cparams(("parallel",)),
        name="sgmlp_merge",
    )(ya, yb, main, main, main, main, main, main, vnorm_w, ws_bf, bias_exp, wb_bf, wo_bf, post_w, mod, x)


def _rope_tables(n_tokens):
    n_rows = n_tokens // GRID_W
    rows = jnp.repeat(jnp.arange(n_rows, dtype=F32), GRID_W)
    cols = jnp.tile(jnp.arange(GRID_W, dtype=F32), n_rows)
    n_freq = DA_QK // 4
    inv = ROPE_BASE ** (-jnp.arange(n_freq, dtype=F32) / n_freq)
    ang = jnp.concatenate([rows[:, None] * inv, cols[:, None] * inv], -1)
    cos, sin = jnp.cos(ang), jnp.sin(ang)
    return jnp.tile(cos, (1, 4)), jnp.concatenate([-sin, sin, -sin, sin], axis=1)


def _state_to_pairs(s):
    lead = s.shape[:-3]
    s = s.reshape(lead + (M_PAIRS, 2, M_HEADDIM, M_STATE))
    s = jnp.moveaxis(s, -1, -3)
    return s.reshape(lead + (M_PAIRS, M_STATE, 2 * M_HEADDIM))


def kernel(x_prompt, x_sample, cache_k, cache_v, state_ssm, c, c_ctx, pre_norm_w, post_norm_w, w_mod, b_mod, w_in,
           m_conv_w, m_conv_b, m_A_log, m_dt_bias, m_D, m_norm_w, da_lambda, da_head_norm_w, sg_vnorm_w,
           sg_spatial_w, sg_spatial_b, w_branch, w_out):
    depth = w_in.shape[0]
    nb, seq, _ = x_prompt.shape
    db, dseq, _ = x_sample.shape
    past = cache_k.shape[2]

    w_main, w_misc = _prep_in_weights(jnp.swapaxes(w_in, 1, 2))
    wb_bf = w_branch.astype(BF16)
    wo_bf = w_out.astype(BF16)
    ws_bf = sg_spatial_w.astype(BF16)
    bias_exp = jnp.repeat(jnp.swapaxes(sg_spatial_b, 1, 2), D_MODEL // SG_GROUPS, axis=2)
    cw = jnp.swapaxes(m_conv_w, 1, 2)
    dtb = jnp.pad(m_dt_bias.reshape(depth, 1, 2 * M_HEADS), ((0, 0), (0, 0), (0, 128 - 2 * M_HEADS)))
    alog = jnp.pad(m_A_log.reshape(depth, 1, 2 * M_HEADS), ((0, 0), (0, 0), (0, 128 - 2 * M_HEADS)))
    dsum = jnp.repeat(m_D[:, 0] + m_D[:, 1], M_HEADDIM, axis=1).reshape(depth, 1, D_MODEL)

    cvec = jnp.concatenate([c_ctx[None, :], c, jnp.zeros((8 - 1 - db, D_MODEL), F32)], axis=0)
    mods = _modulation(cvec, w_mod, b_mod).reshape(depth, 8, 3, D_MODEL)

    rope_tabs = _rope_tables(dseq)
    init_pairs = _state_to_pairs(state_ssm)
    ck = cache_k.reshape(db, depth, past, D_MODEL)
    cv = cache_v.reshape(db, depth, past, D_MODEL)

    xp = x_prompt.reshape(nb * seq, D_MODEL)
    xs = x_sample.reshape(db * dseq, D_MODEL)
    ks_out, vs_out = [], []

    def layer(x, l, nseq, T, mod, rows_per_mod, latent, st_stack=None):
        lam_init = 0.8 - 0.6 * math.exp(-0.3 * l)
        outs = _inproj(x, mod, pre_norm_w[l][None], w_main, w_misc, l, rows_per_mod,
                       rope_tabs if latent else None, emit_kv=not latent)
        main, misc = outs[0], outs[1]
        ya, st = _ssd(main, misc, nseq, T, cw[l, :, 0:D_MODEL], m_conv_b[l][None, 0:D_MODEL],
                      cw[l, :, D_MODEL:], m_conv_b[l][None, D_MODEL:], dtb[l], alog[l], dsum[l],
                      m_norm_w[l][None], init_pairs[:, l] if latent else None,
                      st_stack=None if latent else st_stack, layer=0 if latent else l,
                      n_layers=1 if latent else depth, cps=min(SSD_CHUNKS_PER_STEP, T // M_CHUNK))
        if latent:
            yb = _attention_latent(outs[2], nseq, T, ATT_TQ, ATT_KB, da_lambda[l], da_head_norm_w[l][None],
                                   lam_init, ck[:, l], cv[:, l])
        else:
            yb = _attention_ctx(main, nseq, T, da_lambda[l], da_head_norm_w[l][None], lam_init)
        shift = LATENT_SEG_SHIFT if latent else 0
        x_new = _sgmlp_merge(ya, yb, main, shift, sg_vnorm_w[l][None], ws_bf, bias_exp, wb_bf, wo_bf, l,
                             post_norm_w[l][None], mod, rows_per_mod, x)
        return x_new, outs[2:], st

    st_all = None
    for l in range(depth):
        xp, kv, st_all = layer(xp, l, nb, seq, mods[l, 0:1], nb * seq, False, st_all)
        ks_out.append(kv[0].reshape(nb, seq, DA_HEADS, 2 * DA_QK))
        vs_out.append(kv[1].reshape(nb, seq, DA_HEADS, DA_VD))
        xs, _, _ = layer(xs, l, db, dseq, mods[l, 1:1 + db], dseq, True)

    return (xp.reshape(nb, seq, D_MODEL), xs.reshape(db, dseq, D_MODEL),
            jnp.stack(ks_out, 1), jnp.stack(vs_out, 1),
            st_all.reshape(nb, depth, 2, M_HEADS, M_HEADDIM, M_STATE))
```

```python
import functools
import math

import jax
import jax.numpy as jnp
from jax import lax
from jax.experimental import pallas as pl
from jax.experimental.pallas import tpu as pltpu

F32 = jnp.float32
BF16 = jnp.bfloat16

D_MODEL = 1024
EPS = 1e-6
GRID_W = 64
ROPE_BASE = 10000.0
M_HEADS = 16
M_HEADDIM = 64
M_STATE = 64
M_GROUPS = 2
M_CHUNK = 128
M_PAIRS = M_HEADS // 2
DA_HEADS = 8
DA_QK = 64
DA_VD = 128
SG_GROUPS = 8
SG_CHUNK = 128

SEG_Z, SEG_XS, SEG_Q, SEG_K, SEG_V, SEG_GB, SEG_U, SEG_SV, SEG_GC, SEG_MGA, SEG_MGB, SEG_MGC = range(12)
N_SEG = 12
LATENT_SEG_SHIFT = 4
MISC_W = 384

VMEM_LIMIT = 56 * 1024 * 1024
NEG_BIG = -1e30
ATT_SUBQ = 128
ATT_TQ = 1024
ATT_KB = 512
SSD_CHUNKS_PER_STEP = 4
INPROJ_TM = 1024
INPROJ_TN = 2048
INPROJ_SUBN = 1024


def _silu(x):
    return x * (1.0 / (1.0 + jnp.exp(-x)))


def _sigmoid(x):
    return 1.0 / (1.0 + jnp.exp(-x))


def _gelu_tanh(x):
    return 0.5 * x * (1.0 + jnp.tanh(math.sqrt(2.0 / math.pi) * (x + 0.044715 * (x * x * x))))


def _cparams(sem):
    return pltpu.CompilerParams(dimension_semantics=sem, vmem_limit_bytes=VMEM_LIMIT)


def _mod_kernel(c_ref, w_ref, b_ref, o_ref):
    c = c_ref[...]
    s = _silu(c).astype(BF16)
    o_ref[...] = jnp.dot(s, w_ref[...].astype(BF16), preferred_element_type=F32) + b_ref[...]


def _modulation(cvec, w_mod, b_mod):
    depth = w_mod.shape[0]
    nt = 3
    return pl.pallas_call(
        _mod_kernel,
        out_shape=jax.ShapeDtypeStruct((depth, 8, 3 * D_MODEL), F32),
        grid=(depth, nt),
        in_specs=[
            pl.BlockSpec((8, D_MODEL), lambda l, j: (0, 0)),
            pl.BlockSpec((None, D_MODEL, D_MODEL), lambda l, j: (l, 0, j)),
            pl.BlockSpec((None, 1, D_MODEL), lambda l, j: (l, 0, j)),
        ],
        out_specs=pl.BlockSpec((None, 8, D_MODEL), lambda l, j: (l, 0, j)),
        compiler_params=_cparams(("arbitrary", "arbitrary")),
        name="modulation",
    )(cvec, w_mod, b_mod.reshape(depth, 1, 3 * D_MODEL))


W_IN_COLS = 12576
W_XS_END = 2048
W_MISC_END = 2336


def _wprep_kernel(wa_ref, wb_ref, wm_ref, main_ref, misc_ref):
    j = pl.program_id(1)

    @pl.when(j < W_XS_END // D_MODEL)
    def _():
        main_ref[...] = wa_ref[...].T.astype(BF16)

    @pl.when(j >= W_XS_END // D_MODEL)
    def _():
        main_ref[...] = wb_ref[...].T.astype(BF16)

    @pl.when(j == 0)
    def _():
        n = W_MISC_END - W_XS_END
        wm = jnp.concatenate([wm_ref[...], jnp.zeros((MISC_W - n, D_MODEL), F32)], axis=0)
        misc_ref[...] = wm.T.astype(BF16)


def _prep_in_weights(w_in_t):
    depth = w_in_t.shape[0]
    n_head = W_XS_END // D_MODEL
    w2d = w_in_t.reshape(depth * W_IN_COLS, D_MODEL)
    skew = W_MISC_END - W_XS_END
    return pl.pallas_call(
        _wprep_kernel,
        out_shape=[jax.ShapeDtypeStruct((depth, D_MODEL, N_SEG * D_MODEL), BF16),
                   jax.ShapeDtypeStruct((depth, D_MODEL, MISC_W), BF16)],
        grid=(depth, N_SEG),
        in_specs=[pl.BlockSpec((None, D_MODEL, D_MODEL), lambda l, j: (l, jnp.minimum(j, n_head - 1), 0)),
                  pl.BlockSpec((pl.Element(D_MODEL), pl.Element(D_MODEL)),
                               lambda l, j: (pl.multiple_of(l * W_IN_COLS + skew + j * D_MODEL, 32), 0)),
                  pl.BlockSpec((pl.Element(skew), pl.Element(D_MODEL)),
                               lambda l, j: (pl.multiple_of(l * W_IN_COLS + W_XS_END, 32), 0))],
        out_specs=[pl.BlockSpec((None, D_MODEL, D_MODEL), lambda l, j: (l, 0, j)),
                   pl.BlockSpec((None, D_MODEL, MISC_W), lambda l, j: (l, 0, 0))],
        compiler_params=_cparams(("parallel", "arbitrary")),
        name="w_prep",
    )(w_in_t, w2d, w2d)


def _inproj_kernel(*refs, rope, emit_kv, has_prev):
    it = iter(refs)
    x_ref, mod_ref, prew_ref, w_ref, wm_ref = next(it), next(it), next(it), next(it), next(it)
    if rope:
        cos_ref, sin_ref = next(it), next(it)
    if has_prev:
        next(it), next(it)
    main_ref, misc_ref = next(it), next(it)
    if emit_kv:
        kf_ref, vf_ref = next(it), next(it)
    if rope:
        hm_ref = next(it)
    h_scr = next(it)

    j = pl.program_id(1)
    tm = x_ref.shape[0]
    tn = w_ref.shape[1]
    spt = tn // D_MODEL

    @pl.when(j == 0)
    def _():
        x = x_ref[...]
        ms = jnp.mean(x * x, axis=-1, keepdims=True)
        y = x * lax.rsqrt(ms + EPS) * prew_ref[...]
        h = y * (1.0 + mod_ref[1:2, :]) + mod_ref[0:1, :]
        hb = h.astype(BF16)
        h_scr[...] = hb
        misc_ref[...] = jnp.dot(hb, wm_ref[...], preferred_element_type=F32)

    h = h_scr[...]
    sub = INPROJ_SUBN
    for c0 in range(0, tn, sub):
        cs = slice(c0, c0 + sub)
        seg = j * spt + c0 // D_MODEL
        lc = c0 % D_MODEL
        ls = slice(lc, lc + sub)
        acc = jnp.dot(h, w_ref[:, cs], preferred_element_type=F32)

        if rope:
            is_rope = jnp.logical_or(seg == SEG_Q, seg == SEG_K)
            is_plain_hm = jnp.logical_or(seg == SEG_V, seg == SEG_GB)

            def store_heads(val):
                vb = val.astype(hm_ref.dtype)
                for hh in range(sub // 128):
                    hm_ref[c0 // D_MODEL, lc // 128 + hh] = vb[:, hh * 128:(hh + 1) * 128]

            @pl.when(is_rope)
            def _():
                cos = jnp.concatenate([cos_ref[...]] * (sub // 128), axis=1)
                sin = jnp.concatenate([sin_ref[...]] * (sub // 128), axis=1)
                lane = lax.broadcasted_iota(jnp.int32, (tm, sub), 1)
                first_half = (lane & (DA_QK - 1)) < (DA_QK // 2)
                rot = jnp.where(first_half, pltpu.roll(acc, sub - DA_QK // 2, 1), pltpu.roll(acc, DA_QK // 2, 1))
                store_heads(acc * cos + rot * sin)

            @pl.when(is_plain_hm)
            def _():
                store_heads(acc)

            @pl.when(jnp.logical_not(jnp.logical_or(is_rope, is_plain_hm)))
            def _():
                main_ref[:, cs] = acc.astype(main_ref.dtype)
        else:
            main_ref[:, cs] = acc.astype(main_ref.dtype)

        if emit_kv:
            nsq, sq_len = kf_ref.shape[0], kf_ref.shape[1]

            @pl.when(seg == SEG_K)
            def _():
                kf_ref[:, :, ls] = acc.reshape(nsq, sq_len, sub)

            @pl.when(seg == SEG_V)
            def _():
                vf_ref[:, :, ls] = acc.reshape(nsq, sq_len, sub)


def _inproj(x, mod, pre_w, w_main, w_misc, l, rows_per_mod, rope_tabs, kv_cache=None, tm=INPROJ_TM, tn=INPROJ_TN):
    R = x.shape[0]
    emit_kv = kv_cache is not None
    ni = R // tm
    spt = tn // D_MODEL
    rope = rope_tabs is not None
    const = dict(pipeline_mode=pl.Buffered(1))
    in_specs = [
        pl.BlockSpec((tm, D_MODEL), lambda i, j: (i, 0), **const),
        pl.BlockSpec((None, 3, D_MODEL), lambda i, j: ((i * tm) // rows_per_mod, 0, 0)),
        pl.BlockSpec((1, D_MODEL), lambda i, j: (0, 0)),
        pl.BlockSpec((None, D_MODEL, tn), lambda i, j: (l, 0, j)),
        pl.BlockSpec((None, D_MODEL, MISC_W), lambda i, j: (l, 0, 0), **const),
    ]
    args = [x, mod, pre_w, w_main, w_misc]
    if rope:
        cos, sin = rope_tabs
        nt = cos.shape[0] // tm
        in_specs += [pl.BlockSpec((tm, 128), lambda i, j: (i % nt, 0)),
                     pl.BlockSpec((tm, 128), lambda i, j: (i % nt, 0))]
        args += [cos, sin]
    if rope:
        n_main = N_SEG - LATENT_SEG_SHIFT
        main_idx = lambda i, j: (i, jnp.where(j < SEG_Q // spt, j,
                                              jnp.where(j <= SEG_GB // spt, SEG_XS // spt,
                                                        j - LATENT_SEG_SHIFT // spt)))
    else:
        n_main = N_SEG
        main_idx = lambda i, j: (i, j)
    out_shape = [jax.ShapeDtypeStruct((R, n_main * D_MODEL), BF16), jax.ShapeDtypeStruct((R, MISC_W), F32)]
    out_specs = [pl.BlockSpec((tm, tn), main_idx), pl.BlockSpec((tm, MISC_W), lambda i, j: (i, 0))]
    aliases = {}
    has_prev = False
    if emit_kv:
        seq_len, n_layers, k_prev, v_prev = kv_cache
        out_shape += [jax.ShapeDtypeStruct((R // seq_len, n_layers, seq_len, D_MODEL), F32)] * 2
        out_specs += [pl.BlockSpec((tm // seq_len, None, seq_len, D_MODEL), lambda i, j: (i, l, 0, 0))] * 2
        if k_prev is not None:
            has_prev = True
            in_specs += [pl.BlockSpec(memory_space=pl.ANY)] * 2
            args += [k_prev, v_prev]
            aliases = {len(args) - 2: 2, len(args) - 1: 3}
    if rope:
        T = cos.shape[0]
        out_shape.append(jax.ShapeDtypeStruct((4, R // T, DA_HEADS, T, 128), BF16))
        out_specs.append(pl.BlockSpec(
            (spt, None, DA_HEADS, tm, 128),
            lambda i, j: (jnp.clip(j - SEG_Q // spt, 0, 4 // spt - 1), (i * tm) // T, 0, ((i * tm) % T) // tm, 0)))
    return pl.pallas_call(
        functools.partial(_inproj_kernel, rope=rope, emit_kv=emit_kv, has_prev=has_prev),
        out_shape=out_shape,
        grid=(ni, N_SEG // spt),
        in_specs=in_specs,
        out_specs=out_specs,
        input_output_aliases=aliases,
        scratch_shapes=[pltpu.VMEM((tm, D_MODEL), BF16)],
        compiler_params=_cparams(("parallel", "arbitrary")),
        name="inproj",
    )(*args)


def _split3(a):
    hi = a.astype(BF16)
    r1 = a - hi.astype(F32)
    mid = r1.astype(BF16)
    lo = (r1 - mid.astype(F32)).astype(BF16)
    return hi, mid, lo


def _ssd_kernel(*refs, nc, cps, has_init, has_prev):
    it = iter(refs)
    xs_ref, xsp_ref, xsn_ref = next(it), next(it), next(it)
    mi_ref, mip_ref, min_ref = next(it), next(it), next(it)
    z_ref = next(it)
    cwx_ref, cbx_ref, cwm_ref, cbm_ref = next(it), next(it), next(it), next(it)
    dtb_ref, alog_ref, dsum_ref, nw_ref = next(it), next(it), next(it), next(it)
    if has_init:
        init_ref = next(it)
    if has_prev:
        next(it)
    y_ref, sto_ref = next(it), next(it)
    xc_scr, bc_scr, yf_scr, st_scr = next(it), next(it), next(it), next(it)

    L = M_CHUNK
    LB = cps * L
    ns = nc // cps
    j = pl.program_id(1)
    fwd = j < ns
    c = jnp.where(fwd, j, 2 * ns - 1 - j)
    row0 = pl.multiple_of(c * LB, LB)
    LOG2E = math.log2(math.e)

    rid = lax.broadcasted_iota(jnp.int32, (L, L), 0)
    cid = lax.broadcasted_iota(jnp.int32, (L, L), 1)

    def conv_silu(x, prow, nrow, w_ref, b_ref):
        n = x.shape[1]
        r = lax.broadcasted_iota(jnp.int32, (LB, n), 0)
        xm = jnp.where(r == 0, prow, pltpu.roll(x, 1, 0))
        xp = jnp.where(r == LB - 1, nrow, pltpu.roll(x, LB - 1, 0))
        y = w_ref[0:1, :] * xm + w_ref[1:2, :] * x + w_ref[2:3, :] * xp + b_ref[...]
        return _silu(y)

    def load_state(d):
        if has_init:
            for i in range(M_PAIRS):
                st_scr[i] = init_ref[d, i]
        else:
            st_scr[...] = jnp.zeros_like(st_scr)

    def store_state(d):
        pad = jnp.zeros((2 * M_HEADDIM - M_STATE, 2 * M_HEADDIM), F32)
        for i in range(M_PAIRS):
            t = jnp.concatenate([st_scr[i], pad], axis=0).T
            sto_ref[d, i * 2 * M_HEADDIM:(i + 1) * 2 * M_HEADDIM, :] = t[:, 0:M_STATE]

    lane_lo = cid < M_HEADDIM
    lane_lo_s = lax.broadcasted_iota(jnp.int32, (M_STATE, 2 * M_HEADDIM), 1) < M_HEADDIM
    heads_per_group = M_HEADS // M_GROUPS
    zeros_s = jnp.zeros((M_STATE, 2 * M_HEADDIM), BF16)

    def setup(d, bc, dt_raw):
        tri = (cid <= rid) if d == 0 else (cid >= rid)
        tri_bf = jnp.where(tri, 1.0, 0.0).astype(BF16)
        dt = dt_raw + dtb_ref[...]
        dt = jnp.maximum(dt, 0.0) + jnp.log1p(jnp.exp(-jnp.abs(dt)))
        a = dt * (-jnp.exp(alog_ref[...]))
        a_hi, a_mid, a_lo = _split3(a)
        p_col = (jnp.dot(tri_bf, a_hi, preferred_element_type=F32)
                 + jnp.dot(tri_bf, a_mid, preferred_element_type=F32)
                 + jnp.dot(tri_bf, a_lo, preferred_element_type=F32))
        p_row = p_col.T[d * M_HEADS:(d + 1) * M_HEADS, :]
        dt_row = dt.T[d * M_HEADS:(d + 1) * M_HEADS, :]
        tot = p_row[:, L - 1:L] if d == 0 else p_row[:, 0:1]
        b_all = bc[:, 0:M_GROUPS * M_STATE]
        c_all = bc[:, M_GROUPS * M_STATE:2 * M_GROUPS * M_STATE]
        b_bf = b_all.astype(BF16)
        g_mats = []
        for g in range(M_GROUPS):
            cg = jnp.where(lane_lo if g == 0 else jnp.logical_not(lane_lo), c_all, 0.0).astype(BF16)
            g_mats.append(lax.dot_general(cg, b_bf, (((1,), (1,)), ((), ())), preferred_element_type=F32))
        return dict(
            tri=tri, g_mats=g_mats, c_all=c_all, bt_all=b_all.T,
            p2_col=p_col * LOG2E,
            q2_row=(p_row - jnp.log(dt_row)) * LOG2E,
            w_row=dt_row * jnp.exp(tot - p_row),
            etot=jnp.exp(tot),
            ep_col=jnp.exp(p_col))

    def pairs(d, s, x_bf):
        outs = []
        for i in range(M_PAIRS):
            g = (2 * i) // heads_per_group
            btg = s["bt_all"][g * M_STATE:(g + 1) * M_STATE, :]
            lhs_rows = []
            bw_rows = []
            for hh in range(2):
                h = 2 * i + hh
                col = d * M_HEADS + h
                pc = jnp.broadcast_to(s["p2_col"][:, col:col + 1], (L, L))
                dm = jnp.exp2(jnp.where(s["tri"], pc - s["q2_row"][h:h + 1, :], NEG_BIG))
                m_h = s["g_mats"][g] * dm
                ce_h = s["c_all"] * jnp.broadcast_to(s["ep_col"][:, col:col + 1], (L, L))
                lhs_rows.append(jnp.concatenate([m_h, ce_h], axis=1).astype(BF16))
                bw = (btg * s["w_row"][h:h + 1, :]).astype(BF16)
                bw_rows.append(jnp.concatenate([bw, zeros_s], axis=1))
            lhs = jnp.concatenate(lhs_rows + bw_rows, axis=0)
            x_pair = x_bf[:, i * 128:(i + 1) * 128]
            st_pair = st_scr[i]
            st_bf = st_pair.astype(BF16)
            rhs = jnp.concatenate([x_pair] + ([st_bf, zeros_s] if g == 0 else [zeros_s, st_bf]), axis=0)
            res = jnp.dot(lhs, rhs, preferred_element_type=F32)
            y_pair = jnp.where(lane_lo, res[0:L], res[L:2 * L])
            ds = jnp.where(lane_lo_s, res[2 * L:2 * L + M_STATE], res[2 * L + M_STATE:2 * L + 2 * M_STATE])
            e0 = jnp.broadcast_to(s["etot"][2 * i:2 * i + 1, :], (M_STATE, 2 * M_HEADDIM))
            e1 = jnp.broadcast_to(s["etot"][2 * i + 1:2 * i + 2, :], (M_STATE, 2 * M_HEADDIM))
            st_scr[i] = jnp.where(lane_lo_s, e0, e1) * st_pair + ds
            outs.append(y_pair)
        return jnp.concatenate(outs, axis=1)

    def block(d, x_bf, bc, dt_raw):
        order = range(cps) if d == 0 else range(cps - 1, -1, -1)
        su = {ci: setup(d, bc[ci * L:(ci + 1) * L], dt_raw[ci * L:(ci + 1) * L]) for ci in order}
        ys = {ci: pairs(d, su[ci], x_bf[ci * L:(ci + 1) * L]) for ci in order}
        return jnp.concatenate([ys[ci] for ci in range(cps)], axis=0)

    @pl.when(j == 0)
    def _():
        load_state(0)

    @pl.when(j == ns)
    def _():
        load_state(1)

    @pl.when(fwd)
    def _():
        x = xs_ref[...].astype(F32)
        prow = jnp.where(c > 0, xsp_ref[...].astype(F32)[15:16, :], 0.0)
        nrow = jnp.where(c < ns - 1, xsn_ref[...].astype(F32)[0:1, :], 0.0)
        xc = conv_silu(x, prow, nrow, cwx_ref, cbx_ref).astype(BF16)
        xc_scr[pl.ds(row0, LB), :] = xc
        m = mi_ref[...]
        bcx = m[:, 0:256]
        prow_m = jnp.where(c > 0, mip_ref[7:8, 0:256], 0.0)
        nrow_m = jnp.where(c < ns - 1, min_ref[0:1, 0:256], 0.0)
        bc = conv_silu(bcx, prow_m, nrow_m, cwm_ref, cbm_ref)
        bc_scr[pl.ds(row0, LB), :] = bc
        yf_scr[pl.ds(row0, LB), :] = block(0, xc, bc, m[:, 256:384])

    @pl.when(j == ns - 1)
    def _():
        store_state(0)

    @pl.when(jnp.logical_not(fwd))
    def _():
        xc = xc_scr[pl.ds(row0, LB), :]
        bc = bc_scr[pl.ds(row0, LB), :]
        yb = block(1, xc, bc, mi_ref[:, 256:384])
        y = yf_scr[pl.ds(row0, LB), :] + yb + dsum_ref[...] * xc.astype(F32)
        y = y * _silu(z_ref[...].astype(F32))
        ms = jnp.mean(y * y, axis=-1, keepdims=True)
        y_ref[...] = (y * lax.rsqrt(ms + EPS) * nw_ref[...]).astype(y_ref.dtype)

    @pl.when(j == 2 * ns - 1)
    def _():
        store_state(1)


def _ssd(main, misc, nseq, T, cwx, cbx, cwm, cbm, dtb, alog, dsum, nw, init, st_stack=None, layer=0, n_layers=1,
         cps=SSD_CHUNKS_PER_STEP):
    R = nseq * T
    nc = T // M_CHUNK
    L = cps * M_CHUNK
    ns = nc // cps
    nc, full_nc = ns, nc
    has_init = init is not None

    def cidx(j):
        return jnp.where(j < nc, j, 2 * nc - 1 - j)

    def oidx(j):
        return jnp.where(j < nc, nc - 1, 2 * nc - 1 - j)

    in_specs = [
        pl.BlockSpec((L, D_MODEL), lambda b, j: (b * nc + cidx(j), SEG_XS)),
        pl.BlockSpec((16, D_MODEL), lambda b, j: (jnp.maximum((b * nc + cidx(j)) * (L // 16) - 1, 0), SEG_XS)),
        pl.BlockSpec((16, D_MODEL),
                     lambda b, j: (jnp.minimum((b * nc + cidx(j) + 1) * (L // 16), R // 16 - 1), SEG_XS)),
        pl.BlockSpec((L, MISC_W), lambda b, j: (b * nc + cidx(j), 0)),
        pl.BlockSpec((8, MISC_W), lambda b, j: (jnp.maximum((b * nc + cidx(j)) * (L // 8) - 1, 0), 0)),
        pl.BlockSpec((8, MISC_W), lambda b, j: (jnp.minimum((b * nc + cidx(j) + 1) * (L // 8), R // 8 - 1), 0)),
        pl.BlockSpec((L, D_MODEL), lambda b, j: (b * nc + oidx(j), SEG_Z)),
        pl.BlockSpec((3, D_MODEL), lambda b, j: (0, 0)),
        pl.BlockSpec((1, D_MODEL), lambda b, j: (0, 0)),
        pl.BlockSpec((3, 256), lambda b, j: (0, 0)),
        pl.BlockSpec((1, 256), lambda b, j: (0, 0)),
        pl.BlockSpec((1, 128), lambda b, j: (0, 0)),
        pl.BlockSpec((1, 128), lambda b, j: (0, 0)),
        pl.BlockSpec((1, D_MODEL), lambda b, j: (0, 0)),
        pl.BlockSpec((1, D_MODEL), lambda b, j: (0, 0)),
    ]
    args = [main, main, main, misc, misc, misc, main, cwx, cbx, cwm, cbm, dtb, alog, dsum, nw]
    if has_init:
        in_specs.append(pl.BlockSpec((None, 2, M_PAIRS, M_STATE, 128), lambda b, j: (b, 0, 0, 0, 0)))
        args.append(init)
    aliases = {}
    if st_stack is not None:
        in_specs.append(pl.BlockSpec(memory_space=pl.ANY))
        args.append(st_stack)
        aliases = {len(args) - 1: 1}
    return pl.pallas_call(
        functools.partial(_ssd_kernel, nc=full_nc, cps=cps, has_init=has_init, has_prev=st_stack is not None),
        out_shape=[jax.ShapeDtypeStruct((R, D_MODEL), BF16),
                   jax.ShapeDtypeStruct((nseq, n_layers, 2, M_HEADS * M_HEADDIM, M_STATE), F32)],
        grid=(nseq, 2 * nc),
        in_specs=in_specs,
        out_specs=[pl.BlockSpec((L, D_MODEL), lambda b, j: (b * nc + oidx(j), 0)),
                   pl.BlockSpec((None, None, 2, M_HEADS * M_HEADDIM, M_STATE), lambda b, j: (b, layer, 0, 0, 0))],
        scratch_shapes=[pltpu.VMEM((T, D_MODEL), BF16), pltpu.VMEM((T, 256), F32),
                        pltpu.VMEM((T, D_MODEL), F32), pltpu.VMEM((M_PAIRS, M_STATE, 128), F32)],
        input_output_aliases=aliases,
        compiler_params=_cparams(("parallel", "arbitrary")),
        name="ssd",
    )(*args)


def _attn_lat_kernel(q_ref, k_ref, v_ref, gb_ref, kc_ref, vc_ref, lv_ref, hw_ref, o_ref, vt_scr, vct_scr,
                     *, lam_init, kb):
    tq = q_ref.shape[0]
    T = k_ref.shape[0]
    lc = kc_ref.shape[0]
    qi = pl.program_id(2)

    @pl.when(qi == 0)
    def _():
        for c0 in range(0, T, 128):
            vt_scr[0:DA_VD, c0:c0 + 128] = v_ref[c0:c0 + 128, :].astype(F32).T.astype(BF16)
        vt_scr[DA_VD:DA_VD + 16, :] = jnp.ones((16, T), BF16)
        for c0 in range(0, lc, 128):
            vct_scr[0:DA_VD, c0:c0 + 128] = vc_ref[c0:c0 + 128, :].T.astype(BF16)
        vct_scr[DA_VD:DA_VD + 16, :] = jnp.ones((16, lc), BF16)

    lv = lv_ref[...]
    lam = (jnp.exp(jnp.sum(lv[0:1, :] * lv[1:2, :], axis=-1, keepdims=True))
           - jnp.exp(jnp.sum(lv[2:3, :] * lv[3:4, :], axis=-1, keepdims=True)) + lam_init)
    sq = ATT_SUBQ
    lane = lax.broadcasted_iota(jnp.int32, (sq, 2 * DA_QK), 1)
    dn_t = (((1,), (1,)), ((), ()))

    q2s = []
    for c in range(tq // sq):
        q = q_ref[c * sq:(c + 1) * sq, :].astype(F32) * (DA_QK ** -0.5 * math.log2(math.e))
        q2s.append(jnp.concatenate([jnp.where(lane < DA_QK, q, 0.0), jnp.where(lane < DA_QK, 0.0, q)],
                                   axis=0).astype(BF16))

    def scores(k_blk, q2):
        s = lax.dot_general(k_blk, q2, dn_t, preferred_element_type=F32)
        return s, jnp.max(s, axis=0, keepdims=True)

    def accumulate(s, bm, vt_blk, state):
        m_new = bm if state is None else jnp.maximum(state[0], bm)
        p = jnp.exp2(s - m_new).astype(BF16)
        pv = jnp.dot(vt_blk, p, preferred_element_type=F32)
        if state is None:
            return m_new, pv
        return m_new, jnp.exp2(state[0] - m_new) * state[1] + pv

    kcb = kc_ref[...].astype(BF16)
    cur = [scores(kcb, q2) for q2 in q2s]
    cur_vt = vct_scr[...]
    states = [None] * len(q2s)
    for k0 in range(0, T, kb):
        k_blk = k_ref[k0:k0 + kb, :]
        nxt = [scores(k_blk, q2) for q2 in q2s]
        states = [accumulate(cu[0], cu[1], cur_vt, st) for cu, st in zip(cur, states)]
        cur, cur_vt = nxt, vt_scr[:, k0:k0 + kb]
    states = [accumulate(cu[0], cu[1], cur_vt, st) for cu, st in zip(cur, states)]
    for c, (_, acc) in enumerate(states):
        rs = slice(c * sq, (c + 1) * sq)
        r = 1.0 / acc[DA_VD:DA_VD + 1, :]
        o_t = acc[0:DA_VD, 0:sq] * r[:, 0:sq] - acc[0:DA_VD, sq:2 * sq] * (r[:, sq:2 * sq] * lam)
        o = o_t.T
        ms = jnp.mean(o * o, axis=-1, keepdims=True)
        o = o * lax.rsqrt(ms + EPS) * hw_ref[...] * (1.0 - lam_init)
        o_ref[rs, :] = (o * _silu(gb_ref[rs, :].astype(F32))).astype(o_ref.dtype)


def _attn_ctx_kernel(q_ref, k_ref, v_ref, gb_ref, lv_ref, hw_ref, o_ref, *, lam_init):
    T = q_ref.shape[0]
    lv = lv_ref[...]
    lam = (jnp.exp(jnp.sum(lv[0:1, :] * lv[1:2, :], axis=-1, keepdims=True))
           - jnp.exp(jnp.sum(lv[2:3, :] * lv[3:4, :], axis=-1, keepdims=True)) + lam_init)
    lane = lax.broadcasted_iota(jnp.int32, (T, 2 * DA_QK), 1)
    dn_t = (((1,), (1,)), ((), ()))
    ones = jnp.ones((16, T), BF16)
    for hh in range(DA_HEADS):
        cs = slice(hh * 128, (hh + 1) * 128)
        q = q_ref[:, cs].astype(F32) * (DA_QK ** -0.5 * math.log2(math.e))
        q2 = jnp.concatenate([jnp.where(lane < DA_QK, q, 0.0), jnp.where(lane < DA_QK, 0.0, q)],
                             axis=0).astype(BF16)
        v = v_ref[:, cs].astype(F32)
        vt = jnp.concatenate([v[c0:c0 + 128, :].T for c0 in range(0, T, 128)], axis=1).astype(BF16)
        vt1 = jnp.concatenate([vt, ones], axis=0)
        s = lax.dot_general(k_ref[:, cs], q2, dn_t, preferred_element_type=F32)
        p = jnp.exp2(s - jnp.max(s, axis=0, keepdims=True)).astype(BF16)
        acc = jnp.dot(vt1, p, preferred_element_type=F32)
        r = 1.0 / acc[DA_VD:DA_VD + 1, :]
        o_t = acc[0:DA_VD, 0:T] * r[:, 0:T] - acc[0:DA_VD, T:2 * T] * (r[:, T:2 * T] * lam)
        o = jnp.concatenate([o_t[:, c0:c0 + 128].T for c0 in range(0, T, 128)], axis=0)
        ms = jnp.mean(o * o, axis=-1, keepdims=True)
        o = o * lax.rsqrt(ms + EPS) * hw_ref[...] * (1.0 - lam_init)
        o_ref[:, cs] = (o * _silu(gb_ref[:, cs].astype(F32))).astype(o_ref.dtype)


def _attention_ctx(main, nseq, T, lam_vecs, head_w, lam_init):
    return pl.pallas_call(
        functools.partial(_attn_ctx_kernel, lam_init=lam_init),
        out_shape=jax.ShapeDtypeStruct((nseq * T, D_MODEL), BF16),
        grid=(nseq,),
        in_specs=[
            pl.BlockSpec((T, D_MODEL), lambda b: (b, SEG_Q)),
            pl.BlockSpec((T, D_MODEL), lambda b: (b, SEG_K)),
            pl.BlockSpec((T, D_MODEL), lambda b: (b, SEG_V)),
            pl.BlockSpec((T, D_MODEL), lambda b: (b, SEG_GB)),
            pl.BlockSpec((4, DA_QK), lambda b: (0, 0)),
            pl.BlockSpec((1, DA_VD), lambda b: (0, 0)),
        ],
        out_specs=pl.BlockSpec((T, D_MODEL), lambda b: (b, 0)),
        compiler_params=_cparams(("parallel",)),
        name="diff_attn_ctx",
    )(main, main, main, main, lam_vecs, head_w)


def _attention_latent(hm, nseq, T, tq, kb, lam_vecs, head_w, lam_init, ck, cv):
    R = nseq * T
    nq = T // tq
    lc = ck.shape[1]
    return pl.pallas_call(
        functools.partial(_attn_lat_kernel, lam_init=lam_init, kb=kb),
        out_shape=jax.ShapeDtypeStruct((R, D_MODEL), BF16),
        grid=(nseq, DA_HEADS, nq),
        in_specs=[
            pl.BlockSpec((None, None, None, tq, 128), lambda b, g, qi: (0, b, g, qi, 0)),
            pl.BlockSpec((None, None, None, T, 128), lambda b, g, qi: (1, b, g, 0, 0)),
            pl.BlockSpec((None, None, None, T, 128), lambda b, g, qi: (2, b, g, 0, 0)),
            pl.BlockSpec((None, None, None, tq, 128), lambda b, g, qi: (3, b, g, qi, 0)),
            pl.BlockSpec((None, lc, 128), lambda b, g, qi: (b, 0, g)),
            pl.BlockSpec((None, lc, 128), lambda b, g, qi: (b, 0, g)),
            pl.BlockSpec((4, DA_QK), lambda b, g, qi: (0, 0)),
            pl.BlockSpec((1, DA_VD), lambda b, g, qi: (0, 0)),
        ],
        out_specs=pl.BlockSpec((tq, 128), lambda b, g, qi: (b * nq + qi, g)),
        scratch_shapes=[pltpu.VMEM((DA_VD + 16, T), BF16), pltpu.VMEM((DA_VD + 16, lc), BF16)],
        compiler_params=_cparams(("parallel", "parallel", "arbitrary")),
        name="diff_attn_lat",
    )(hm, hm, hm, hm, ck, cv, lam_vecs, head_w)


def _sgmlp_merge_kernel(ya_ref, yb_ref, u_ref, sv_ref, gc_ref, ga_ref, gb_ref, gm_ref, vw_ref, ws_ref, bias_ref,
                        wb_ref, wo_ref, pw_ref, mod_ref, x_ref, o_ref):
    tm = u_ref.shape[0]

    def gated(y, g_ref, i):
        p = jnp.dot(y, wb_ref[i], preferred_element_type=F32)
        return _sigmoid(g_ref[...].astype(F32)) * p

    u = _gelu_tanh(u_ref[...].astype(F32))
    v = _gelu_tanh(sv_ref[...].astype(F32))
    vc = v - jnp.mean(v, axis=-1, keepdims=True)
    vb = (vc * lax.rsqrt(jnp.mean(vc * vc, axis=-1, keepdims=True) + EPS) * vw_ref[...]).astype(BF16)
    ug = u * _silu(gc_ref[...].astype(F32))
    rows = []
    for ci in range(tm // SG_CHUNK):
        rs = slice(ci * SG_CHUNK, (ci + 1) * SG_CHUNK)
        cols = [jnp.dot(ws_ref[g], vb[rs, g * 128:(g + 1) * 128], preferred_element_type=F32)
                for g in range(SG_GROUPS)]
        rows.append(((jnp.concatenate(cols, axis=1) + bias_ref[...]) * ug[rs]).astype(BF16))
    yc = jnp.concatenate(rows, axis=0)

    merged = gated(ya_ref[...], ga_ref, 0) + gated(yb_ref[...], gb_ref, 1) + gated(yc, gm_ref, 2)
    o = jnp.dot(merged.astype(BF16), wo_ref[...], preferred_element_type=F32)
    ms = jnp.mean(o * o, axis=-1, keepdims=True)
    o = o * lax.rsqrt(ms + EPS) * pw_ref[...]
    o_ref[...] = x_ref[...] + mod_ref[2:3, :] * o


def _sgmlp_merge(ya, yb, main, seg_shift, vnorm_w, ws_bf, bias_exp, wb_bf, wo_bf, l, post_w, mod, rows_per_mod, x,
                 tm=512):
    R = x.shape[0]
    row = lambda i: (i, 0)
    seg = lambda s: pl.BlockSpec((tm, D_MODEL), lambda i: (i, s - seg_shift))
    const = dict(pipeline_mode=pl.Buffered(1))
    return pl.pallas_call(
        _sgmlp_merge_kernel,
        out_shape=jax.ShapeDtypeStruct((R, D_MODEL), F32),
        grid=(R // tm,),
        in_specs=[
            pl.BlockSpec((tm, D_MODEL), row),
            pl.BlockSpec((tm, D_MODEL), row),
            seg(SEG_U), seg(SEG_SV), seg(SEG_GC), seg(SEG_MGA), seg(SEG_MGB), seg(SEG_MGC),
            pl.BlockSpec((1, D_MODEL), lambda i: (0, 0)),
            pl.BlockSpec((None, SG_GROUPS, SG_CHUNK, SG_CHUNK), lambda i: (l, 0, 0, 0), **const),
            pl.BlockSpec((None, SG_CHUNK, D_MODEL), lambda i: (l, 0, 0), **const),
            pl.BlockSpec((None, 3, D_MODEL, D_MODEL), lambda i: (l, 0, 0, 0), **const),
            pl.BlockSpec((None, D_MODEL, D_MODEL), lambda i: (l, 0, 0), **const),
            pl.BlockSpec((1, D_MODEL), lambda i: (0, 0)),
            pl.BlockSpec((None, 3, D_MODEL), lambda i: ((i * tm) // rows_per_mod, 0, 0)),
            pl.BlockSpec((tm, D_MODEL), row),
        ],
        out_specs=pl.BlockSpec((tm, D_MODEL), row),
        compiler_params=_cparams(("parallel",)),
        name="sgmlp_merge",
    )(ya, yb, main, main, main, main, main, main, vnorm_w, ws_bf, bias_exp, wb_bf, wo_bf, post_w, mod, x)


def _rope_tables(n_tokens):
    n_rows = n_tokens // GRID_W
    rows = jnp.repeat(jnp.arange(n_rows, dtype=F32), GRID_W)
    cols = jnp.tile(jnp.arange(GRID_W, dtype=F32), n_rows)
    n_freq = DA_QK // 4
    inv = ROPE_BASE ** (-jnp.arange(n_freq, dtype=F32) / n_freq)
    ang = jnp.concatenate([rows[:, None] * inv, cols[:, None] * inv], -1)
    cos, sin = jnp.cos(ang), jnp.sin(ang)
    return jnp.tile(cos, (1, 4)), jnp.concatenate([-sin, sin, -sin, sin], axis=1)


def _state_to_pairs(s):
    lead = s.shape[:-3]
    s = s.reshape(lead + (M_PAIRS, 2, M_HEADDIM, M_STATE))
    s = jnp.moveaxis(s, -1, -3)
    return s.reshape(lead + (M_PAIRS, M_STATE, 2 * M_HEADDIM))


def kernel(x_prompt, x_sample, cache_k, cache_v, state_ssm, c, c_ctx, pre_norm_w, post_norm_w, w_mod, b_mod, w_in,
           m_conv_w, m_conv_b, m_A_log, m_dt_bias, m_D, m_norm_w, da_lambda, da_head_norm_w, sg_vnorm_w,
           sg_spatial_w, sg_spatial_b, w_branch, w_out):
    depth = w_in.shape[0]
    nb, seq, _ = x_prompt.shape
    db, dseq, _ = x_sample.shape
    past = cache_k.shape[2]

    w_main, w_misc = _prep_in_weights(jnp.swapaxes(w_in, 1, 2))
    wb_bf = w_branch.astype(BF16)
    wo_bf = w_out.astype(BF16)
    ws_bf = sg_spatial_w.astype(BF16)
    bias_exp = jnp.repeat(jnp.swapaxes(sg_spatial_b, 1, 2), D_MODEL // SG_GROUPS, axis=2)
    cw = jnp.swapaxes(m_conv_w, 1, 2)
    dtb = jnp.pad(m_dt_bias.reshape(depth, 1, 2 * M_HEADS), ((0, 0), (0, 0), (0, 128 - 2 * M_HEADS)))
    alog = jnp.pad(m_A_log.reshape(depth, 1, 2 * M_HEADS), ((0, 0), (0, 0), (0, 128 - 2 * M_HEADS)))
    dsum = jnp.repeat(m_D[:, 0] + m_D[:, 1], M_HEADDIM, axis=1).reshape(depth, 1, D_MODEL)

    cvec = jnp.concatenate([c_ctx[None, :], c, jnp.zeros((8 - 1 - db, D_MODEL), F32)], axis=0)
    mods = _modulation(cvec, w_mod, b_mod).reshape(depth, 8, 3, D_MODEL)

    rope_tabs = _rope_tables(dseq)
    init_pairs = _state_to_pairs(state_ssm)
    ck = cache_k.reshape(db, depth, past, D_MODEL)
    cv = cache_v.reshape(db, depth, past, D_MODEL)

    xp = x_prompt.reshape(nb * seq, D_MODEL)
    xs = x_sample.reshape(db * dseq, D_MODEL)

    def layer(x, l, nseq, T, mod, rows_per_mod, latent, stacks=(None, None, None)):
        lam_init = 0.8 - 0.6 * math.exp(-0.3 * l)
        k_stack, v_stack, st_stack = stacks
        outs = _inproj(x, mod, pre_norm_w[l][None], w_main, w_misc, l, rows_per_mod,
                       rope_tabs if latent else None, None if latent else (T, depth, k_stack, v_stack))
        main, misc = outs[0], outs[1]
        ya, st = _ssd(main, misc, nseq, T, cw[l, :, 0:D_MODEL], m_conv_b[l][None, 0:D_MODEL],
                      cw[l, :, D_MODEL:], m_conv_b[l][None, D_MODEL:], dtb[l], alog[l], dsum[l],
                      m_norm_w[l][None], init_pairs[:, l] if latent else None,
                      st_stack=None if latent else st_stack, layer=0 if latent else l,
                      n_layers=1 if latent else depth, cps=min(SSD_CHUNKS_PER_STEP, T // M_CHUNK))
        if latent:
            yb = _attention_latent(outs[2], nseq, T, ATT_TQ, ATT_KB, da_lambda[l], da_head_norm_w[l][None],
                                   lam_init, ck[:, l], cv[:, l])
        else:
            yb = _attention_ctx(main, nseq, T, da_lambda[l], da_head_norm_w[l][None], lam_init)
        shift = LATENT_SEG_SHIFT if latent else 0
        x_new = _sgmlp_merge(ya, yb, main, shift, sg_vnorm_w[l][None], ws_bf, bias_exp, wb_bf, wo_bf, l,
                             post_norm_w[l][None], mod, rows_per_mod, x)
        return x_new, (None if latent else (outs[2], outs[3], st))

    stacks = (None, None, None)
    for l in range(depth):
        xp, stacks = layer(xp, l, nb, seq, mods[l, 0:1], nb * seq, False, stacks)
        xs, _ = layer(xs, l, db, dseq, mods[l, 1:1 + db], dseq, True)

    k_all, v_all, st_all = stacks
    return (xp.reshape(nb, seq, D_MODEL), xs.reshape(db, dseq, D_MODEL),
            k_all.reshape(nb, depth, seq, DA_HEADS, 2 * DA_QK), v_all.reshape(nb, depth, seq, DA_HEADS, DA_VD),
            st_all.reshape(nb, depth, 2, M_HEADS, M_HEADDIM, M_STATE))
```

```python
import functools
import math

import jax
import jax.numpy as jnp
from jax import lax
from jax.experimental import pallas as pl
from jax.experimental.pallas import tpu as pltpu

F32 = jnp.float32
BF16 = jnp.bfloat16

D_MODEL = 1024
EPS = 1e-6
GRID_W = 64
ROPE_BASE = 10000.0
M_HEADS = 16
M_HEADDIM = 64
M_STATE = 64
M_GROUPS = 2
M_CHUNK = 128
M_PAIRS = M_HEADS // 2
DA_HEADS = 8
DA_QK = 64
DA_VD = 128
SG_GROUPS = 8
SG_CHUNK = 128

SEG_Z, SEG_XS, SEG_Q, SEG_K, SEG_V, SEG_GB, SEG_U, SEG_SV, SEG_GC, SEG_MGA, SEG_MGB, SEG_MGC = range(12)
N_SEG = 12
LATENT_SEG_SHIFT = 4
MISC_W = 384

VMEM_LIMIT = 56 * 1024 * 1024
NEG_BIG = -1e30
ATT_SUBQ = 128
ATT_TQ = 1024
ATT_KB = 512
SSD_CHUNKS_PER_STEP = 4
INPROJ_TM = 1024
INPROJ_TN = 2048
INPROJ_SUBN = 1024


def _silu(x):
    return x * (1.0 / (1.0 + jnp.exp(-x)))


def _sigmoid(x):
    return 1.0 / (1.0 + jnp.exp(-x))


def _gelu_tanh(x):
    return 0.5 * x * (1.0 + jnp.tanh(math.sqrt(2.0 / math.pi) * (x + 0.044715 * (x * x * x))))


def _cparams(sem):
    return pltpu.CompilerParams(dimension_semantics=sem, vmem_limit_bytes=VMEM_LIMIT)


def _mod_kernel(c_ref, w_ref, b_ref, o_ref):
    c = c_ref[...]
    s = _silu(c).astype(BF16)
    o_ref[...] = jnp.dot(s, w_ref[...].astype(BF16), preferred_element_type=F32) + b_ref[...]


def _modulation(cvec, w_mod, b_mod):
    depth = w_mod.shape[0]
    nt = 3
    return pl.pallas_call(
        _mod_kernel,
        out_shape=jax.ShapeDtypeStruct((depth, 8, 3 * D_MODEL), F32),
        grid=(depth, nt),
        in_specs=[
            pl.BlockSpec((8, D_MODEL), lambda l, j: (0, 0)),
            pl.BlockSpec((None, D_MODEL, D_MODEL), lambda l, j: (l, 0, j)),
            pl.BlockSpec((None, 1, D_MODEL), lambda l, j: (l, 0, j)),
        ],
        out_specs=pl.BlockSpec((None, 8, D_MODEL), lambda l, j: (l, 0, j)),
        compiler_params=_cparams(("arbitrary", "arbitrary")),
        name="modulation",
    )(cvec, w_mod, b_mod.reshape(depth, 1, 3 * D_MODEL))


W_IN_COLS = 12576
W_XS_END = 2048
W_MISC_END = 2336


def _wprep_kernel(wa_ref, wb_ref, wm_ref, main_ref, misc_ref):
    j = pl.program_id(1)

    @pl.when(j < W_XS_END // D_MODEL)
    def _():
        main_ref[...] = wa_ref[...].T.astype(BF16)

    @pl.when(j >= W_XS_END // D_MODEL)
    def _():
        main_ref[...] = wb_ref[...].T.astype(BF16)

    @pl.when(j == 0)
    def _():
        n = W_MISC_END - W_XS_END
        wm = jnp.concatenate([wm_ref[...], jnp.zeros((MISC_W - n, D_MODEL), F32)], axis=0)
        misc_ref[...] = wm.T.astype(BF16)


def _prep_in_weights(w_in_t):
    depth = w_in_t.shape[0]
    n_head = W_XS_END // D_MODEL
    w2d = w_in_t.reshape(depth * W_IN_COLS, D_MODEL)
    skew = W_MISC_END - W_XS_END
    return pl.pallas_call(
        _wprep_kernel,
        out_shape=[jax.ShapeDtypeStruct((depth, D_MODEL, N_SEG * D_MODEL), BF16),
                   jax.ShapeDtypeStruct((depth, D_MODEL, MISC_W), BF16)],
        grid=(depth, N_SEG),
        in_specs=[pl.BlockSpec((None, D_MODEL, D_MODEL), lambda l, j: (l, jnp.minimum(j, n_head - 1), 0)),
                  pl.BlockSpec((pl.Element(D_MODEL), pl.Element(D_MODEL)),
                               lambda l, j: (pl.multiple_of(l * W_IN_COLS + skew + j * D_MODEL, 32), 0)),
                  pl.BlockSpec((pl.Element(skew), pl.Element(D_MODEL)),
                               lambda l, j: (pl.multiple_of(l * W_IN_COLS + W_XS_END, 32), 0))],
        out_specs=[pl.BlockSpec((None, D_MODEL, D_MODEL), lambda l, j: (l, 0, j)),
                   pl.BlockSpec((None, D_MODEL, MISC_W), lambda l, j: (l, 0, 0))],
        compiler_params=_cparams(("parallel", "arbitrary")),
        name="w_prep",
    )(w_in_t, w2d, w2d)


def _inproj_kernel(*refs, rope, emit_kv, has_prev):
    it = iter(refs)
    x_ref, mod_ref, prew_ref, w_ref, wm_ref = next(it), next(it), next(it), next(it), next(it)
    if rope:
        cos_ref, sin_ref = next(it), next(it)
    if has_prev:
        next(it), next(it)
    main_ref, misc_ref = next(it), next(it)
    if emit_kv:
        kf_ref, vf_ref = next(it), next(it)
    if rope:
        hm_ref = next(it)
    h_scr = next(it)

    j = pl.program_id(1)
    tm = x_ref.shape[0]
    tn = w_ref.shape[1]
    spt = tn // D_MODEL

    @pl.when(j == 0)
    def _():
        x = x_ref[...]
        ms = jnp.mean(x * x, axis=-1, keepdims=True)
        y = x * lax.rsqrt(ms + EPS) * prew_ref[...]
        h = y * (1.0 + mod_ref[1:2, :]) + mod_ref[0:1, :]
        hb = h.astype(BF16)
        h_scr[...] = hb
        misc_ref[...] = jnp.dot(hb, wm_ref[...], preferred_element_type=F32)

    h = h_scr[...]
    sub = INPROJ_SUBN
    for c0 in range(0, tn, sub):
        cs = slice(c0, c0 + sub)
        seg = j * spt + c0 // D_MODEL
        lc = c0 % D_MODEL
        ls = slice(lc, lc + sub)
        acc = jnp.dot(h, w_ref[:, cs], preferred_element_type=F32)

        if rope:
            is_rope = jnp.logical_or(seg == SEG_Q, seg == SEG_K)
            is_plain_hm = jnp.logical_or(seg == SEG_V, seg == SEG_GB)

            def store_heads(val):
                vb = val.astype(hm_ref.dtype)
                for hh in range(sub // 128):
                    hm_ref[c0 // D_MODEL, lc // 128 + hh] = vb[:, hh * 128:(hh + 1) * 128]

            @pl.when(is_rope)
            def _():
                cos = jnp.concatenate([cos_ref[...]] * (sub // 128), axis=1)
                sin = jnp.concatenate([sin_ref[...]] * (sub // 128), axis=1)
                lane = lax.broadcasted_iota(jnp.int32, (tm, sub), 1)
                first_half = (lane & (DA_QK - 1)) < (DA_QK // 2)
                rot = jnp.where(first_half, pltpu.roll(acc, sub - DA_QK // 2, 1), pltpu.roll(acc, DA_QK // 2, 1))
                store_heads(acc * cos + rot * sin)

            @pl.when(is_plain_hm)
            def _():
                store_heads(acc)

            @pl.when(jnp.logical_not(jnp.logical_or(is_rope, is_plain_hm)))
            def _():
                main_ref[:, cs] = acc.astype(main_ref.dtype)
        else:
            main_ref[:, cs] = acc.astype(main_ref.dtype)

        if emit_kv:
            nsq, sq_len = kf_ref.shape[0], kf_ref.shape[1]

            @pl.when(seg == SEG_K)
            def _():
                kf_ref[:, :, ls] = acc.reshape(nsq, sq_len, sub)

            @pl.when(seg == SEG_V)
            def _():
                vf_ref[:, :, ls] = acc.reshape(nsq, sq_len, sub)


def _inproj(x, mod, pre_w, w_main, w_misc, l, rows_per_mod, rope_tabs, kv_cache=None, tm=INPROJ_TM, tn=INPROJ_TN):
    R = x.shape[0]
    emit_kv = kv_cache is not None
    ni = R // tm
    spt = tn // D_MODEL
    rope = rope_tabs is not None
    const = dict(pipeline_mode=pl.Buffered(1))
    in_specs = [
        pl.BlockSpec((tm, D_MODEL), lambda i, j: (i, 0), **const),
        pl.BlockSpec((None, 3, D_MODEL), lambda i, j: ((i * tm) // rows_per_mod, 0, 0)),
        pl.BlockSpec((1, D_MODEL), lambda i, j: (0, 0)),
        pl.BlockSpec((None, D_MODEL, tn), lambda i, j: (l, 0, j)),
        pl.BlockSpec((None, D_MODEL, MISC_W), lambda i, j: (l, 0, 0), **const),
    ]
    args = [x, mod, pre_w, w_main, w_misc]
    if rope:
        cos, sin = rope_tabs
        nt = cos.shape[0] // tm
        in_specs += [pl.BlockSpec((tm, 128), lambda i, j: (i % nt, 0)),
                     pl.BlockSpec((tm, 128), lambda i, j: (i % nt, 0))]
        args += [cos, sin]
    if rope:
        n_main = N_SEG - LATENT_SEG_SHIFT
        main_idx = lambda i, j: (i, jnp.where(j < SEG_Q // spt, j,
                                              jnp.where(j <= SEG_GB // spt, SEG_XS // spt,
                                                        j - LATENT_SEG_SHIFT // spt)))
    else:
        n_main = N_SEG
        main_idx = lambda i, j: (i, j)
    out_shape = [jax.ShapeDtypeStruct((R, n_main * D_MODEL), BF16), jax.ShapeDtypeStruct((R, MISC_W), F32)]
    out_specs = [pl.BlockSpec((tm, tn), main_idx), pl.BlockSpec((tm, MISC_W), lambda i, j: (i, 0))]
    aliases = {}
    has_prev = False
    if emit_kv:
        seq_len, n_layers, k_prev, v_prev = kv_cache
        out_shape += [jax.ShapeDtypeStruct((R // seq_len, n_layers, seq_len, D_MODEL), F32)] * 2
        out_specs += [pl.BlockSpec((tm // seq_len, None, seq_len, D_MODEL), lambda i, j: (i, l, 0, 0))] * 2
        if k_prev is not None:
            has_prev = True
            in_specs += [pl.BlockSpec(memory_space=pl.ANY)] * 2
            args += [k_prev, v_prev]
            aliases = {len(args) - 2: 2, len(args) - 1: 3}
    if rope:
        T = cos.shape[0]
        out_shape.append(jax.ShapeDtypeStruct((4, R // T, DA_HEADS, T, 128), BF16))
        out_specs.append(pl.BlockSpec(
            (spt, None, DA_HEADS, tm, 128),
            lambda i, j: (jnp.clip(j - SEG_Q // spt, 0, 4 // spt - 1), (i * tm) // T, 0, ((i * tm) % T) // tm, 0)))
    return pl.pallas_call(
        functools.partial(_inproj_kernel, rope=rope, emit_kv=emit_kv, has_prev=has_prev),
        out_shape=out_shape,
        grid=(ni, N_SEG // spt),
        in_specs=in_specs,
        out_specs=out_specs,
        input_output_aliases=aliases,
        scratch_shapes=[pltpu.VMEM((tm, D_MODEL), BF16)],
        compiler_params=_cparams(("parallel", "arbitrary")),
        name="inproj",
    )(*args)


def _split3(a):
    hi = a.astype(BF16)
    r1 = a - hi.astype(F32)
    mid = r1.astype(BF16)
    lo = (r1 - mid.astype(F32)).astype(BF16)
    return hi, mid, lo


def _ssd_kernel(*refs, nc, cps, has_init, has_prev):
    it = iter(refs)
    xs_ref, xsp_ref, xsn_ref = next(it), next(it), next(it)
    mi_ref, mip_ref, min_ref = next(it), next(it), next(it)
    z_ref = next(it)
    cwx_ref, cbx_ref, cwm_ref, cbm_ref = next(it), next(it), next(it), next(it)
    dtb_ref, alog_ref, dsum_ref, nw_ref = next(it), next(it), next(it), next(it)
    if has_init:
        init_ref = next(it)
    if has_prev:
        next(it)
    y_ref, sto_ref = next(it), next(it)
    xc_scr, bc_scr, yf_scr, st_scr = next(it), next(it), next(it), next(it)

    L = M_CHUNK
    LB = cps * L
    ns = nc // cps
    j = pl.program_id(1)
    fwd = j < ns
    c = jnp.where(fwd, j, 2 * ns - 1 - j)
    row0 = pl.multiple_of(c * LB, LB)
    LOG2E = math.log2(math.e)

    rid = lax.broadcasted_iota(jnp.int32, (L, L), 0)
    cid = lax.broadcasted_iota(jnp.int32, (L, L), 1)

    def conv_silu(x, prow, nrow, w_ref, b_ref):
        n = x.shape[1]
        r = lax.broadcasted_iota(jnp.int32, (LB, n), 0)
        xm = jnp.where(r == 0, prow, pltpu.roll(x, 1, 0))
        xp = jnp.where(r == LB - 1, nrow, pltpu.roll(x, LB - 1, 0))
        y = w_ref[0:1, :] * xm + w_ref[1:2, :] * x + w_ref[2:3, :] * xp + b_ref[...]
        return _silu(y)

    def load_state(d):
        if has_init:
            for i in range(M_PAIRS):
                st_scr[i] = init_ref[d, i]
        else:
            st_scr[...] = jnp.zeros_like(st_scr)

    def store_state(d):
        pad = jnp.zeros((2 * M_HEADDIM - M_STATE, 2 * M_HEADDIM), F32)
        for i in range(M_PAIRS):
            t = jnp.concatenate([st_scr[i], pad], axis=0).T
            sto_ref[d, i * 2 * M_HEADDIM:(i + 1) * 2 * M_HEADDIM, :] = t[:, 0:M_STATE]

    lane_lo = cid < M_HEADDIM
    lane_lo_s = lax.broadcasted_iota(jnp.int32, (M_STATE, 2 * M_HEADDIM), 1) < M_HEADDIM
    heads_per_group = M_HEADS // M_GROUPS
    zeros_s = jnp.zeros((M_STATE, 2 * M_HEADDIM), BF16)

    def setup(d, bc, dt_raw):
        tri = (cid <= rid) if d == 0 else (cid >= rid)
        tri_bf = jnp.where(tri, 1.0, 0.0).astype(BF16)
        dt = dt_raw + dtb_ref[...]
        dt = jnp.maximum(dt, 0.0) + jnp.log1p(jnp.exp(-jnp.abs(dt)))
        a = dt * (-jnp.exp(alog_ref[...]))
        a_hi, a_mid, a_lo = _split3(a)
        p_col = (jnp.dot(tri_bf, a_hi, preferred_element_type=F32)
                 + jnp.dot(tri_bf, a_mid, preferred_element_type=F32)
                 + jnp.dot(tri_bf, a_lo, preferred_element_type=F32))
        p_row = p_col.T[d * M_HEADS:(d + 1) * M_HEADS, :]
        dt_row = dt.T[d * M_HEADS:(d + 1) * M_HEADS, :]
        tot = p_row[:, L - 1:L] if d == 0 else p_row[:, 0:1]
        b_all = bc[:, 0:M_GROUPS * M_STATE]
        c_all = bc[:, M_GROUPS * M_STATE:2 * M_GROUPS * M_STATE]
        b_bf = b_all.astype(BF16)
        g_mats = []
        for g in range(M_GROUPS):
            cg = jnp.where(lane_lo if g == 0 else jnp.logical_not(lane_lo), c_all, 0.0).astype(BF16)
            g_mats.append(lax.dot_general(cg, b_bf, (((1,), (1,)), ((), ())),
                                          preferred_element_type=F32).astype(BF16))
        return dict(
            tri=tri, g_mats=g_mats, c_bf=c_all.astype(BF16), bt_all=b_all.T,
            p2_col=p_col * LOG2E,
            q2_row=(p_row - jnp.log(dt_row)) * LOG2E,
            w_row=dt_row * jnp.exp(tot - p_row),
            etot=jnp.exp(tot),
            ep_col=jnp.exp(p_col))

    def pairs(d, s, x_bf):
        outs = []
        for i in range(M_PAIRS):
            g = (2 * i) // heads_per_group
            btg = s["bt_all"][g * M_STATE:(g + 1) * M_STATE, :]
            lhs_rows = []
            bw_rows = []
            for hh in range(2):
                h = 2 * i + hh
                col = d * M_HEADS + h
                pc = jnp.broadcast_to(s["p2_col"][:, col:col + 1], (L, L))
                dm = jnp.exp2(jnp.where(s["tri"], pc - s["q2_row"][h:h + 1, :], NEG_BIG))
                m_h = s["g_mats"][g] * dm.astype(BF16)
                ce_h = s["c_bf"] * jnp.broadcast_to(s["ep_col"][:, col:col + 1], (L, L)).astype(BF16)
                lhs_rows.append(jnp.concatenate([m_h, ce_h], axis=1))
                bw = (btg * s["w_row"][h:h + 1, :]).astype(BF16)
                bw_rows.append(jnp.concatenate([bw, zeros_s], axis=1))
            lhs = jnp.concatenate(lhs_rows + bw_rows, axis=0)
            x_pair = x_bf[:, i * 128:(i + 1) * 128]
            st_pair = st_scr[i]
            st_bf = st_pair.astype(BF16)
            rhs = jnp.concatenate([x_pair] + ([st_bf, zeros_s] if g == 0 else [zeros_s, st_bf]), axis=0)
            res = jnp.dot(lhs, rhs, preferred_element_type=F32)
            y_pair = jnp.where(lane_lo, res[0:L], res[L:2 * L])
            ds = jnp.where(lane_lo_s, res[2 * L:2 * L + M_STATE], res[2 * L + M_STATE:2 * L + 2 * M_STATE])
            e0 = jnp.broadcast_to(s["etot"][2 * i:2 * i + 1, :], (M_STATE, 2 * M_HEADDIM))
            e1 = jnp.broadcast_to(s["etot"][2 * i + 1:2 * i + 2, :], (M_STATE, 2 * M_HEADDIM))
            st_scr[i] = jnp.where(lane_lo_s, e0, e1) * st_pair + ds
            outs.append(y_pair)
        return jnp.concatenate(outs, axis=1)

    def block(d, x_bf, bc, dt_raw):
        order = range(cps) if d == 0 else range(cps - 1, -1, -1)
        su = {ci: setup(d, bc[ci * L:(ci + 1) * L], dt_raw[ci * L:(ci + 1) * L]) for ci in order}
        ys = {ci: pairs(d, su[ci], x_bf[ci * L:(ci + 1) * L]) for ci in order}
        return jnp.concatenate([ys[ci] for ci in range(cps)], axis=0)

    @pl.when(j == 0)
    def _():
        load_state(0)

    @pl.when(j == ns)
    def _():
        load_state(1)

    @pl.when(fwd)
    def _():
        x = xs_ref[...].astype(F32)
        prow = jnp.where(c > 0, xsp_ref[...].astype(F32)[15:16, :], 0.0)
        nrow = jnp.where(c < ns - 1, xsn_ref[...].astype(F32)[0:1, :], 0.0)
        xc = conv_silu(x, prow, nrow, cwx_ref, cbx_ref).astype(BF16)
        xc_scr[pl.ds(row0, LB), :] = xc
        m = mi_ref[...]
        bcx = m[:, 0:256]
        prow_m = jnp.where(c > 0, mip_ref[7:8, 0:256], 0.0)
        nrow_m = jnp.where(c < ns - 1, min_ref[0:1, 0:256], 0.0)
        bc = conv_silu(bcx, prow_m, nrow_m, cwm_ref, cbm_ref)
        bc_scr[pl.ds(row0, LB), :] = bc
        yf_scr[pl.ds(row0, LB), :] = block(0, xc, bc, m[:, 256:384])

    @pl.when(j == ns - 1)
    def _():
        store_state(0)

    @pl.when(jnp.logical_not(fwd))
    def _():
        xc = xc_scr[pl.ds(row0, LB), :]
        bc = bc_scr[pl.ds(row0, LB), :]
        yb = block(1, xc, bc, mi_ref[:, 256:384])
        y = yf_scr[pl.ds(row0, LB), :] + yb + dsum_ref[...] * xc.astype(F32)
        y = y * _silu(z_ref[...].astype(F32))
        ms = jnp.mean(y * y, axis=-1, keepdims=True)
        y_ref[...] = (y * lax.rsqrt(ms + EPS) * nw_ref[...]).astype(y_ref.dtype)

    @pl.when(j == 2 * ns - 1)
    def _():
        store_state(1)


def _ssd(main, misc, nseq, T, cwx, cbx, cwm, cbm, dtb, alog, dsum, nw, init, st_stack=None, layer=0, n_layers=1,
         cps=SSD_CHUNKS_PER_STEP):
    R = nseq * T
    nc = T // M_CHUNK
    L = cps * M_CHUNK
    ns = nc // cps
    nc, full_nc = ns, nc
    has_init = init is not None

    def cidx(j):
        return jnp.where(j < nc, j, 2 * nc - 1 - j)

    def oidx(j):
        return jnp.where(j < nc, nc - 1, 2 * nc - 1 - j)

    in_specs = [
        pl.BlockSpec((L, D_MODEL), lambda b, j: (b * nc + cidx(j), SEG_XS)),
        pl.BlockSpec((16, D_MODEL), lambda b, j: (jnp.maximum((b * nc + cidx(j)) * (L // 16) - 1, 0), SEG_XS)),
        pl.BlockSpec((16, D_MODEL),
                     lambda b, j: (jnp.minimum((b * nc + cidx(j) + 1) * (L // 16), R // 16 - 1), SEG_XS)),
        pl.BlockSpec((L, MISC_W), lambda b, j: (b * nc + cidx(j), 0)),
        pl.BlockSpec((8, MISC_W), lambda b, j: (jnp.maximum((b * nc + cidx(j)) * (L // 8) - 1, 0), 0)),
        pl.BlockSpec((8, MISC_W), lambda b, j: (jnp.minimum((b * nc + cidx(j) + 1) * (L // 8), R // 8 - 1), 0)),
        pl.BlockSpec((L, D_MODEL), lambda b, j: (b * nc + oidx(j), SEG_Z)),
        pl.BlockSpec((3, D_MODEL), lambda b, j: (0, 0)),
        pl.BlockSpec((1, D_MODEL), lambda b, j: (0, 0)),
        pl.BlockSpec((3, 256), lambda b, j: (0, 0)),
        pl.BlockSpec((1, 256), lambda b, j: (0, 0)),
        pl.BlockSpec((1, 128), lambda b, j: (0, 0)),
        pl.BlockSpec((1, 128), lambda b, j: (0, 0)),
        pl.BlockSpec((1, D_MODEL), lambda b, j: (0, 0)),
        pl.BlockSpec((1, D_MODEL), lambda b, j: (0, 0)),
    ]
    args = [main, main, main, misc, misc, misc, main, cwx, cbx, cwm, cbm, dtb, alog, dsum, nw]
    if has_init:
        in_specs.append(pl.BlockSpec((None, 2, M_PAIRS, M_STATE, 128), lambda b, j: (b, 0, 0, 0, 0)))
        args.append(init)
    aliases = {}
    if st_stack is not None:
        in_specs.append(pl.BlockSpec(memory_space=pl.ANY))
        args.append(st_stack)
        aliases = {len(args) - 1: 1}
    return pl.pallas_call(
        functools.partial(_ssd_kernel, nc=full_nc, cps=cps, has_init=has_init, has_prev=st_stack is not None),
        out_shape=[jax.ShapeDtypeStruct((R, D_MODEL), BF16),
                   jax.ShapeDtypeStruct((nseq, n_layers, 2, M_HEADS * M_HEADDIM, M_STATE), F32)],
        grid=(nseq, 2 * nc),
        in_specs=in_specs,
        out_specs=[pl.BlockSpec((L, D_MODEL), lambda b, j: (b * nc + oidx(j), 0)),
                   pl.BlockSpec((None, None, 2, M_HEADS * M_HEADDIM, M_STATE), lambda b, j: (b, layer, 0, 0, 0))],
        scratch_shapes=[pltpu.VMEM((T, D_MODEL), BF16), pltpu.VMEM((T, 256), F32),
                        pltpu.VMEM((T, D_MODEL), F32), pltpu.VMEM((M_PAIRS, M_STATE, 128), F32)],
        input_output_aliases=aliases,
        compiler_params=_cparams(("parallel", "arbitrary")),
        name="ssd",
    )(*args)


def _attn_lat_kernel(q_ref, k_ref, v_ref, gb_ref, kc_ref, vc_ref, lv_ref, hw_ref, o_ref, vt_scr, vct_scr,
                     *, lam_init, kb):
    tq = q_ref.shape[0]
    T = k_ref.shape[0]
    lc = kc_ref.shape[0]
    qi = pl.program_id(2)

    @pl.when(qi == 0)
    def _():
        for c0 in range(0, T, 128):
            vt_scr[0:DA_VD, c0:c0 + 128] = v_ref[c0:c0 + 128, :].astype(F32).T.astype(BF16)
        vt_scr[DA_VD:DA_VD + 16, :] = jnp.ones((16, T), BF16)
        for c0 in range(0, lc, 128):
            vct_scr[0:DA_VD, c0:c0 + 128] = vc_ref[c0:c0 + 128, :].T.astype(BF16)
        vct_scr[DA_VD:DA_VD + 16, :] = jnp.ones((16, lc), BF16)

    lv = lv_ref[...]
    lam = (jnp.exp(jnp.sum(lv[0:1, :] * lv[1:2, :], axis=-1, keepdims=True))
           - jnp.exp(jnp.sum(lv[2:3, :] * lv[3:4, :], axis=-1, keepdims=True)) + lam_init)
    sq = ATT_SUBQ
    lane = lax.broadcasted_iota(jnp.int32, (sq, 2 * DA_QK), 1)
    dn_t = (((1,), (1,)), ((), ()))

    q2s = []
    for c in range(tq // sq):
        q = q_ref[c * sq:(c + 1) * sq, :].astype(F32) * (DA_QK ** -0.5 * math.log2(math.e))
        q2s.append(jnp.concatenate([jnp.where(lane < DA_QK, q, 0.0), jnp.where(lane < DA_QK, 0.0, q)],
                                   axis=0).astype(BF16))

    def scores(k_blk, q2):
        s = lax.dot_general(k_blk, q2, dn_t, preferred_element_type=F32)
        return s, jnp.max(s, axis=0, keepdims=True)

    def accumulate(s, bm, vt_blk, state):
        m_new = bm if state is None else jnp.maximum(state[0], bm)
        p = jnp.exp2(s - m_new).astype(BF16)
        pv = jnp.dot(vt_blk, p, preferred_element_type=F32)
        if state is None:
            return m_new, pv
        return m_new, jnp.exp2(state[0] - m_new) * state[1] + pv

    kcb = kc_ref[...].astype(BF16)
    cur = [scores(kcb, q2) for q2 in q2s]
    cur_vt = vct_scr[...]
    states = [None] * len(q2s)
    for k0 in range(0, T, kb):
        k_blk = k_ref[k0:k0 + kb, :]
        nxt = [scores(k_blk, q2) for q2 in q2s]
        states = [accumulate(cu[0], cu[1], cur_vt, st) for cu, st in zip(cur, states)]
        cur, cur_vt = nxt, vt_scr[:, k0:k0 + kb]
    states = [accumulate(cu[0], cu[1], cur_vt, st) for cu, st in zip(cur, states)]
    for c, (_, acc) in enumerate(states):
        rs = slice(c * sq, (c + 1) * sq)
        r = 1.0 / acc[DA_VD:DA_VD + 1, :]
        o_t = acc[0:DA_VD, 0:sq] * r[:, 0:sq] - acc[0:DA_VD, sq:2 * sq] * (r[:, sq:2 * sq] * lam)
        o = o_t.T
        ms = jnp.mean(o * o, axis=-1, keepdims=True)
        o = o * lax.rsqrt(ms + EPS) * hw_ref[...] * (1.0 - lam_init)
        o_ref[rs, :] = (o * _silu(gb_ref[rs, :].astype(F32))).astype(o_ref.dtype)


def _attn_ctx_kernel(q_ref, k_ref, v_ref, gb_ref, lv_ref, hw_ref, o_ref, *, lam_init):
    T = q_ref.shape[0]
    lv = lv_ref[...]
    lam = (jnp.exp(jnp.sum(lv[0:1, :] * lv[1:2, :], axis=-1, keepdims=True))
           - jnp.exp(jnp.sum(lv[2:3, :] * lv[3:4, :], axis=-1, keepdims=True)) + lam_init)
    lane = lax.broadcasted_iota(jnp.int32, (T, 2 * DA_QK), 1)
    dn_t = (((1,), (1,)), ((), ()))
    ones = jnp.ones((16, T), BF16)

    def scores(hh):
        cs = slice(hh * 128, (hh + 1) * 128)
        q = q_ref[:, cs].astype(F32) * (DA_QK ** -0.5 * math.log2(math.e))
        q2 = jnp.concatenate([jnp.where(lane < DA_QK, q, 0.0), jnp.where(lane < DA_QK, 0.0, q)],
                             axis=0).astype(BF16)
        v = v_ref[:, cs].astype(F32)
        vt = jnp.concatenate([v[c0:c0 + 128, :].T for c0 in range(0, T, 128)], axis=1).astype(BF16)
        vt1 = jnp.concatenate([vt, ones], axis=0)
        s = lax.dot_general(k_ref[:, cs], q2, dn_t, preferred_element_type=F32)
        return s, vt1

    nxt = scores(0)
    for hh in range(DA_HEADS):
        cs = slice(hh * 128, (hh + 1) * 128)
        s, vt1 = nxt
        if hh + 1 < DA_HEADS:
            nxt = scores(hh + 1)
        p = jnp.exp2(s - jnp.max(s, axis=0, keepdims=True)).astype(BF16)
        acc = jnp.dot(vt1, p, preferred_element_type=F32)
        r = 1.0 / acc[DA_VD:DA_VD + 1, :]
        o_t = acc[0:DA_VD, 0:T] * r[:, 0:T] - acc[0:DA_VD, T:2 * T] * (r[:, T:2 * T] * lam)
        o = jnp.concatenate([o_t[:, c0:c0 + 128].T for c0 in range(0, T, 128)], axis=0)
        ms = jnp.mean(o * o, axis=-1, keepdims=True)
        o = o * lax.rsqrt(ms + EPS) * hw_ref[...] * (1.0 - lam_init)
        o_ref[:, cs] = (o * _silu(gb_ref[:, cs].astype(F32))).astype(o_ref.dtype)


def _attention_ctx(main, nseq, T, lam_vecs, head_w, lam_init):
    return pl.pallas_call(
        functools.partial(_attn_ctx_kernel, lam_init=lam_init),
        out_shape=jax.ShapeDtypeStruct((nseq * T, D_MODEL), BF16),
        grid=(nseq,),
        in_specs=[
            pl.BlockSpec((T, D_MODEL), lambda b: (b, SEG_Q)),
            pl.BlockSpec((T, D_MODEL), lambda b: (b, SEG_K)),
            pl.BlockSpec((T, D_MODEL), lambda b: (b, SEG_V)),
            pl.BlockSpec((T, D_MODEL), lambda b: (b, SEG_GB)),
            pl.BlockSpec((4, DA_QK), lambda b: (0, 0)),
            pl.BlockSpec((1, DA_VD), lambda b: (0, 0)),
        ],
        out_specs=pl.BlockSpec((T, D_MODEL), lambda b: (b, 0)),
        compiler_params=_cparams(("parallel",)),
        name="diff_attn_ctx",
    )(main, main, main, main, lam_vecs, head_w)


def _attention_latent(hm, nseq, T, tq, kb, lam_vecs, head_w, lam_init, ck, cv):
    R = nseq * T
    nq = T // tq
    lc = ck.shape[1]
    return pl.pallas_call(
        functools.partial(_attn_lat_kernel, lam_init=lam_init, kb=kb),
        out_shape=jax.ShapeDtypeStruct((R, D_MODEL), BF16),
        grid=(nseq, DA_HEADS, nq),
        in_specs=[
            pl.BlockSpec((None, None, None, tq, 128), lambda b, g, qi: (0, b, g, qi, 0)),
            pl.BlockSpec((None, None, None, T, 128), lambda b, g, qi: (1, b, g, 0, 0)),
            pl.BlockSpec((None, None, None, T, 128), lambda b, g, qi: (2, b, g, 0, 0)),
            pl.BlockSpec((None, None, None, tq, 128), lambda b, g, qi: (3, b, g, qi, 0)),
            pl.BlockSpec((None, lc, 128), lambda b, g, qi: (b, 0, g)),
            pl.BlockSpec((None, lc, 128), lambda b, g, qi: (b, 0, g)),
            pl.BlockSpec((4, DA_QK), lambda b, g, qi: (0, 0)),
            pl.BlockSpec((1, DA_VD), lambda b, g, qi: (0, 0)),
        ],
        out_specs=pl.BlockSpec((tq, 128), lambda b, g, qi: (b * nq + qi, g)),
        scratch_shapes=[pltpu.VMEM((DA_VD + 16, T), BF16), pltpu.VMEM((DA_VD + 16, lc), BF16)],
        compiler_params=_cparams(("parallel", "parallel", "arbitrary")),
        name="diff_attn_lat",
    )(hm, hm, hm, hm, ck, cv, lam_vecs, head_w)


def _sgmlp_merge_kernel(ya_ref, yb_ref, u_ref, sv_ref, gc_ref, ga_ref, gb_ref, gm_ref, vw_ref, ws_ref, bias_ref,
                        wb_ref, wo_ref, pw_ref, mod_ref, x_ref, o_ref):
    tm = u_ref.shape[0]

    def gated(y, g_ref, i):
        p = jnp.dot(y, wb_ref[i], preferred_element_type=F32)
        return _sigmoid(g_ref[...].astype(F32)) * p

    u = _gelu_tanh(u_ref[...].astype(F32))
    v = _gelu_tanh(sv_ref[...].astype(F32))
    vc = v - jnp.mean(v, axis=-1, keepdims=True)
    vb = (vc * lax.rsqrt(jnp.mean(vc * vc, axis=-1, keepdims=True) + EPS) * vw_ref[...]).astype(BF16)
    ug = u * _silu(gc_ref[...].astype(F32))
    rows = []
    for ci in range(tm // SG_CHUNK):
        rs = slice(ci * SG_CHUNK, (ci + 1) * SG_CHUNK)
        cols = [jnp.dot(ws_ref[g], vb[rs, g * 128:(g + 1) * 128], preferred_element_type=F32)
                for g in range(SG_GROUPS)]
        rows.append(((jnp.concatenate(cols, axis=1) + bias_ref[...]) * ug[rs]).astype(BF16))
    yc = jnp.concatenate(rows, axis=0)

    merged = gated(ya_ref[...], ga_ref, 0) + gated(yb_ref[...], gb_ref, 1) + gated(yc, gm_ref, 2)
    o = jnp.dot(merged.astype(BF16), wo_ref[...], preferred_element_type=F32)
    ms = jnp.mean(o * o, axis=-1, keepdims=True)
    o = o * lax.rsqrt(ms + EPS) * pw_ref[...]
    o_ref[...] = x_ref[...] + mod_ref[2:3, :] * o


def _sgmlp_merge(ya, yb, main, seg_shift, vnorm_w, ws_bf, bias_exp, wb_bf, wo_bf, l, post_w, mod, rows_per_mod, x,
                 tm=512):
    R = x.shape[0]
    row = lambda i: (i, 0)
    seg = lambda s: pl.BlockSpec((tm, D_MODEL), lambda i: (i, s - seg_shift))
    const = dict(pipeline_mode=pl.Buffered(1))
    return pl.pallas_call(
        _sgmlp_merge_kernel,
        out_shape=jax.ShapeDtypeStruct((R, D_MODEL), F32),
        grid=(R // tm,),
        in_specs=[
            pl.BlockSpec((tm, D_MODEL), row),
            pl.BlockSpec((tm, D_MODEL), row),
            seg(SEG_U), seg(SEG_SV), seg(SEG_GC), seg(SEG_MGA), seg(SEG_MGB), seg(SEG_MGC),
            pl.BlockSpec((1, D_MODEL), lambda i: (0, 0)),
            pl.BlockSpec((None, SG_GROUPS, SG_CHUNK, SG_CHUNK), lambda i: (l, 0, 0, 0), **const),
            pl.BlockSpec((None, SG_CHUNK, D_MODEL), lambda i: (l, 0, 0), **const),
            pl.BlockSpec((None, 3, D_MODEL, D_MODEL), lambda i: (l, 0, 0, 0), **const),
            pl.BlockSpec((None, D_MODEL, D_MODEL), lambda i: (l, 0, 0), **const),
            pl.BlockSpec((1, D_MODEL), lambda i: (0, 0)),
            pl.BlockSpec((None, 3, D_MODEL), lambda i: ((i * tm) // rows_per_mod, 0, 0)),
            pl.BlockSpec((tm, D_MODEL), row),
        ],
        out_specs=pl.BlockSpec((tm, D_MODEL), row),
        compiler_params=_cparams(("parallel",)),
        name="sgmlp_merge",
    )(ya, yb, main, main, main, main, main, main, vnorm_w, ws_bf, bias_exp, wb_bf, wo_bf, post_w, mod, x)


def _rope_tables(n_tokens):
    n_rows = n_tokens // GRID_W
    rows = jnp.repeat(jnp.arange(n_rows, dtype=F32), GRID_W)
    cols = jnp.tile(jnp.arange(GRID_W, dtype=F32), n_rows)
    n_freq = DA_QK // 4
    inv = ROPE_BASE ** (-jnp.arange(n_freq, dtype=F32) / n_freq)
    ang = jnp.concatenate([rows[:, None] * inv, cols[:, None] * inv], -1)
    cos, sin = jnp.cos(ang), jnp.sin(ang)
    return jnp.tile(cos, (1, 4)), jnp.concatenate([-sin, sin, -sin, sin], axis=1)


def _state_to_pairs(s):
    lead = s.shape[:-3]
    s = s.reshape(lead + (M_PAIRS, 2, M_HEADDIM, M_STATE))
    s = jnp.moveaxis(s, -1, -3)
    return s.reshape(lead + (M_PAIRS, M_STATE, 2 * M_HEADDIM))


def kernel(x_prompt, x_sample, cache_k, cache_v, state_ssm, c, c_ctx, pre_norm_w, post_norm_w, w_mod, b_mod, w_in,
           m_conv_w, m_conv_b, m_A_log, m_dt_bias, m_D, m_norm_w, da_lambda, da_head_norm_w, sg_vnorm_w,
           sg_spatial_w, sg_spatial_b, w_branch, w_out):
    depth = w_in.shape[0]
    nb, seq, _ = x_prompt.shape
    db, dseq, _ = x_sample.shape
    past = cache_k.shape[2]

    w_main, w_misc = _prep_in_weights(jnp.swapaxes(w_in, 1, 2))
    wb_bf = w_branch.astype(BF16)
    wo_bf = w_out.astype(BF16)
    ws_bf = sg_spatial_w.astype(BF16)
    bias_exp = jnp.repeat(jnp.swapaxes(sg_spatial_b, 1, 2), D_MODEL // SG_GROUPS, axis=2)
    cw = jnp.swapaxes(m_conv_w, 1, 2)
    dtb = jnp.pad(m_dt_bias.reshape(depth, 1, 2 * M_HEADS), ((0, 0), (0, 0), (0, 128 - 2 * M_HEADS)))
    alog = jnp.pad(m_A_log.reshape(depth, 1, 2 * M_HEADS), ((0, 0), (0, 0), (0, 128 - 2 * M_HEADS)))
    dsum = jnp.repeat(m_D[:, 0] + m_D[:, 1], M_HEADDIM, axis=1).reshape(depth, 1, D_MODEL)

    cvec = jnp.concatenate([c_ctx[None, :], c, jnp.zeros((8 - 1 - db, D_MODEL), F32)], axis=0)
    mods = _modulation(cvec, w_mod, b_mod).reshape(depth, 8, 3, D_MODEL)

    rope_tabs = _rope_tables(dseq)
    init_pairs = _state_to_pairs(state_ssm)
    ck = cache_k.reshape(db, depth, past, D_MODEL)
    cv = cache_v.reshape(db, depth, past, D_MODEL)

    xp = x_prompt.reshape(nb * seq, D_MODEL)
    xs = x_sample.reshape(db * dseq, D_MODEL)

    def layer(x, l, nseq, T, mod, rows_per_mod, latent, stacks=(None, None, None)):
        lam_init = 0.8 - 0.6 * math.exp(-0.3 * l)
        k_stack, v_stack, st_stack = stacks
        outs = _inproj(x, mod, pre_norm_w[l][None], w_main, w_misc, l, rows_per_mod,
                       rope_tabs if latent else None, None if latent else (T, depth, k_stack, v_stack))
        main, misc = outs[0], outs[1]
        ya, st = _ssd(main, misc, nseq, T, cw[l, :, 0:D_MODEL], m_conv_b[l][None, 0:D_MODEL],
                      cw[l, :, D_MODEL:], m_conv_b[l][None, D_MODEL:], dtb[l], alog[l], dsum[l],
                      m_norm_w[l][None], init_pairs[:, l] if latent else None,
                      st_stack=None if latent else st_stack, layer=0 if latent else l,
                      n_layers=1 if latent else depth, cps=min(SSD_CHUNKS_PER_STEP, T // M_CHUNK))
        if latent:
            yb = _attention_latent(outs[2], nseq, T, ATT_TQ, ATT_KB, da_lambda[l], da_head_norm_w[l][None],
                                   lam_init, ck[:, l], cv[:, l])
        else:
            yb = _attention_ctx(main, nseq, T, da_lambda[l], da_head_norm_w[l][None], lam_init)
        shift = LATENT_SEG_SHIFT if latent else 0
        x_new = _sgmlp_merge(ya, yb, main, shift, sg_vnorm_w[l][None], ws_bf, bias_exp, wb_bf, wo_bf, l,
                             post_norm_w[l][None], mod, rows_per_mod, x)
        return x_new, (None if latent else (outs[2], outs[3], st))

    stacks = (None, None, None)
    for l in range(depth):
        xp, stacks = layer(xp, l, nb, seq, mods[l, 0:1], nb * seq, False, stacks)
        xs, _ = layer(xs, l, db, dseq, mods[l, 1:1 + db], dseq, True)

    k_all, v_all, st_all = stacks
    return (xp.reshape(nb, seq, D_MODEL), xs.reshape(db, dseq, D_MODEL),
            k_all.reshape(nb, depth, seq, DA_HEADS, 2 * DA_QK), v_all.reshape(nb, depth, seq, DA_HEADS, DA_VD),
            st_all.reshape(nb, depth, 2, M_HEADS, M_HEADDIM, M_STATE))
```

```python
import functools
import math

import jax
import jax.numpy as jnp
from jax import lax
from jax.experimental import pallas as pl
from jax.experimental.pallas import tpu as pltpu

F32 = jnp.float32
BF16 = jnp.bfloat16

D_MODEL = 1024
EPS = 1e-6
GRID_W = 64
ROPE_BASE = 10000.0
M_HEADS = 16
M_HEADDIM = 64
M_STATE = 64
M_GROUPS = 2
M_CHUNK = 128
M_PAIRS = M_HEADS // 2
DA_HEADS = 8
DA_QK = 64
DA_VD = 128
SG_GROUPS = 8
SG_CHUNK = 128

SEG_Z, SEG_XS, SEG_Q, SEG_K, SEG_V, SEG_GB, SEG_U, SEG_SV, SEG_GC, SEG_MGA, SEG_MGB, SEG_MGC = range(12)
N_SEG = 12
LATENT_SEG_SHIFT = 4
MISC_W = 384

VMEM_LIMIT = 56 * 1024 * 1024
NEG_BIG = -1e30
ATT_SUBQ = 128
ATT_TQ = 1024
ATT_KB = 512
SSD_CHUNKS_PER_STEP = 4
INPROJ_TM = 1024
INPROJ_TN = 2048
INPROJ_SUBN = 1024


_NEG_LOG2E = -math.log2(math.e)


def _silu(x):
    return x * _sigmoid(x)


def _sigmoid(x):
    return 1.0 / (1.0 + jnp.exp2(x * _NEG_LOG2E))


def _gelu_tanh(x):
    c = math.sqrt(2.0 / math.pi)
    hx = 0.5 * x
    return hx + hx * jnp.tanh(x * (c + (c * 0.044715) * (x * x)))


def _cparams(sem):
    return pltpu.CompilerParams(dimension_semantics=sem, vmem_limit_bytes=VMEM_LIMIT)


def _mod_kernel(c_ref, w_ref, b_ref, o_ref):
    c = c_ref[...]
    s = _silu(c).astype(BF16)
    o_ref[...] = jnp.dot(s, w_ref[...].astype(BF16), preferred_element_type=F32) + b_ref[...]


def _modulation(cvec, w_mod, b_mod):
    depth = w_mod.shape[0]
    nt = 3
    return pl.pallas_call(
        _mod_kernel,
        out_shape=jax.ShapeDtypeStruct((depth, 8, 3 * D_MODEL), F32),
        grid=(depth, nt),
        in_specs=[
            pl.BlockSpec((8, D_MODEL), lambda l, j: (0, 0)),
            pl.BlockSpec((None, D_MODEL, D_MODEL), lambda l, j: (l, 0, j)),
            pl.BlockSpec((None, 1, D_MODEL), lambda l, j: (l, 0, j)),
        ],
        out_specs=pl.BlockSpec((None, 8, D_MODEL), lambda l, j: (l, 0, j)),
        compiler_params=_cparams(("arbitrary", "arbitrary")),
        name="modulation",
    )(cvec, w_mod, b_mod.reshape(depth, 1, 3 * D_MODEL))


W_IN_COLS = 12576
W_XS_END = 2048
W_MISC_END = 2336


def _wprep_kernel(wa_ref, wb_ref, wm_ref, main_ref, misc_ref):
    j = pl.program_id(1)

    @pl.when(j < W_XS_END // D_MODEL)
    def _():
        main_ref[...] = wa_ref[...].T.astype(BF16)

    @pl.when(j >= W_XS_END // D_MODEL)
    def _():
        main_ref[...] = wb_ref[...].T.astype(BF16)

    @pl.when(j == 0)
    def _():
        n = W_MISC_END - W_XS_END
        wm = jnp.concatenate([wm_ref[...], jnp.zeros((MISC_W - n, D_MODEL), F32)], axis=0)
        misc_ref[...] = wm.T.astype(BF16)


def _prep_in_weights(w_in_t):
    depth = w_in_t.shape[0]
    n_head = W_XS_END // D_MODEL
    w2d = w_in_t.reshape(depth * W_IN_COLS, D_MODEL)
    skew = W_MISC_END - W_XS_END
    return pl.pallas_call(
        _wprep_kernel,
        out_shape=[jax.ShapeDtypeStruct((depth, D_MODEL, N_SEG * D_MODEL), BF16),
                   jax.ShapeDtypeStruct((depth, D_MODEL, MISC_W), BF16)],
        grid=(depth, N_SEG),
        in_specs=[pl.BlockSpec((None, D_MODEL, D_MODEL), lambda l, j: (l, jnp.minimum(j, n_head - 1), 0)),
                  pl.BlockSpec((pl.Element(D_MODEL), pl.Element(D_MODEL)),
                               lambda l, j: (pl.multiple_of(l * W_IN_COLS + skew + j * D_MODEL, 32), 0)),
                  pl.BlockSpec((pl.Element(skew), pl.Element(D_MODEL)),
                               lambda l, j: (pl.multiple_of(l * W_IN_COLS + W_XS_END, 32), 0))],
        out_specs=[pl.BlockSpec((None, D_MODEL, D_MODEL), lambda l, j: (l, 0, j)),
                   pl.BlockSpec((None, D_MODEL, MISC_W), lambda l, j: (l, 0, 0))],
        compiler_params=_cparams(("parallel", "arbitrary")),
        name="w_prep",
    )(w_in_t, w2d, w2d)


def _inproj_kernel(*refs, rope, emit_kv, has_prev):
    it = iter(refs)
    x_ref, mod_ref, prew_ref, w_ref, wm_ref = next(it), next(it), next(it), next(it), next(it)
    if rope:
        cos_ref, sin_ref = next(it), next(it)
    if has_prev:
        next(it), next(it)
    main_ref, misc_ref = next(it), next(it)
    if emit_kv:
        kf_ref, vf_ref = next(it), next(it)
    if rope:
        hm_ref = next(it)
    h_scr = next(it)

    j = pl.program_id(1)
    tm = x_ref.shape[0]
    tn = w_ref.shape[1]
    spt = tn // D_MODEL

    @pl.when(j == 0)
    def _():
        x = x_ref[...]
        ms = jnp.mean(x * x, axis=-1, keepdims=True)
        y = x * lax.rsqrt(ms + EPS) * prew_ref[...]
        h = y * (1.0 + mod_ref[1:2, :]) + mod_ref[0:1, :]
        hb = h.astype(BF16)
        h_scr[...] = hb
        misc_ref[...] = jnp.dot(hb, wm_ref[...], preferred_element_type=F32)

    h = h_scr[...]
    sub = INPROJ_SUBN
    for c0 in range(0, tn, sub):
        cs = slice(c0, c0 + sub)
        seg = j * spt + c0 // D_MODEL
        lc = c0 % D_MODEL
        ls = slice(lc, lc + sub)
        acc = jnp.dot(h, w_ref[:, cs], preferred_element_type=F32)

        if rope:
            is_rope = jnp.logical_or(seg == SEG_Q, seg == SEG_K)
            is_plain_hm = jnp.logical_or(seg == SEG_V, seg == SEG_GB)

            def store_heads(val):
                vb = val.astype(hm_ref.dtype)
                for hh in range(sub // 128):
                    hm_ref[c0 // D_MODEL, lc // 128 + hh] = vb[:, hh * 128:(hh + 1) * 128]

            @pl.when(is_rope)
            def _():
                cos = jnp.concatenate([cos_ref[...]] * (sub // 128), axis=1)
                sin = jnp.concatenate([sin_ref[...]] * (sub // 128), axis=1)
                lane = lax.broadcasted_iota(jnp.int32, (tm, sub), 1)
                first_half = (lane & (DA_QK - 1)) < (DA_QK // 2)
                rot = jnp.where(first_half, pltpu.roll(acc, sub - DA_QK // 2, 1), pltpu.roll(acc, DA_QK // 2, 1))
                store_heads(acc * cos + rot * sin)

            @pl.when(is_plain_hm)
            def _():
                store_heads(acc)

            @pl.when(jnp.logical_not(jnp.logical_or(is_rope, is_plain_hm)))
            def _():
                main_ref[:, cs] = acc.astype(main_ref.dtype)
        else:
            main_ref[:, cs] = acc.astype(main_ref.dtype)

        if emit_kv:
            nsq, sq_len = kf_ref.shape[0], kf_ref.shape[1]

            @pl.when(seg == SEG_K)
            def _():
                kf_ref[:, :, ls] = acc.reshape(nsq, sq_len, sub)

            @pl.when(seg == SEG_V)
            def _():
                vf_ref[:, :, ls] = acc.reshape(nsq, sq_len, sub)


def _inproj(x, mod, pre_w, w_main, w_misc, l, rows_per_mod, rope_tabs, kv_cache=None, tm=INPROJ_TM, tn=INPROJ_TN):
    R = x.shape[0]
    emit_kv = kv_cache is not None
    ni = R // tm
    spt = tn // D_MODEL
    rope = rope_tabs is not None
    const = dict(pipeline_mode=pl.Buffered(1))
    in_specs = [
        pl.BlockSpec((tm, D_MODEL), lambda i, j: (i, 0), **const),
        pl.BlockSpec((None, 3, D_MODEL), lambda i, j: ((i * tm) // rows_per_mod, 0, 0)),
        pl.BlockSpec((1, D_MODEL), lambda i, j: (0, 0)),
        pl.BlockSpec((None, D_MODEL, tn), lambda i, j: (l, 0, j)),
        pl.BlockSpec((None, D_MODEL, MISC_W), lambda i, j: (l, 0, 0), **const),
    ]
    args = [x, mod, pre_w, w_main, w_misc]
    if rope:
        cos, sin = rope_tabs
        nt = cos.shape[0] // tm
        in_specs += [pl.BlockSpec((tm, 128), lambda i, j: (i % nt, 0)),
                     pl.BlockSpec((tm, 128), lambda i, j: (i % nt, 0))]
        args += [cos, sin]
    if rope:
        n_main = N_SEG - LATENT_SEG_SHIFT
        main_idx = lambda i, j: (i, jnp.where(j < SEG_Q // spt, j,
                                              jnp.where(j <= SEG_GB // spt, SEG_XS // spt,
                                                        j - LATENT_SEG_SHIFT // spt)))
    else:
        n_main = N_SEG
        main_idx = lambda i, j: (i, j)
    out_shape = [jax.ShapeDtypeStruct((R, n_main * D_MODEL), BF16), jax.ShapeDtypeStruct((R, MISC_W), F32)]
    out_specs = [pl.BlockSpec((tm, tn), main_idx), pl.BlockSpec((tm, MISC_W), lambda i, j: (i, 0))]
    aliases = {}
    has_prev = False
    if emit_kv:
        seq_len, n_layers, k_prev, v_prev = kv_cache
        out_shape += [jax.ShapeDtypeStruct((R // seq_len, n_layers, seq_len, D_MODEL), F32)] * 2
        out_specs += [pl.BlockSpec((tm // seq_len, None, seq_len, D_MODEL), lambda i, j: (i, l, 0, 0))] * 2
        if k_prev is not None:
            has_prev = True
            in_specs += [pl.BlockSpec(memory_space=pl.ANY)] * 2
            args += [k_prev, v_prev]
            aliases = {len(args) - 2: 2, len(args) - 1: 3}
    if rope:
        T = cos.shape[0]
        out_shape.append(jax.ShapeDtypeStruct((4, R // T, DA_HEADS, T, 128), BF16))
        out_specs.append(pl.BlockSpec(
            (spt, None, DA_HEADS, tm, 128),
            lambda i, j: (jnp.clip(j - SEG_Q // spt, 0, 4 // spt - 1), (i * tm) // T, 0, ((i * tm) % T) // tm, 0)))
    return pl.pallas_call(
        functools.partial(_inproj_kernel, rope=rope, emit_kv=emit_kv, has_prev=has_prev),
        out_shape=out_shape,
        grid=(ni, N_SEG // spt),
        in_specs=in_specs,
        out_specs=out_specs,
        input_output_aliases=aliases,
        scratch_shapes=[pltpu.VMEM((tm, D_MODEL), BF16)],
        compiler_params=_cparams(("parallel", "arbitrary")),
        name="inproj",
    )(*args)


def _split3(a):
    hi = a.astype(BF16)
    r1 = a - hi.astype(F32)
    mid = r1.astype(BF16)
    lo = (r1 - mid.astype(F32)).astype(BF16)
    return hi, mid, lo


def _ssd_kernel(*refs, nc, cps, has_init, has_prev):
    it = iter(refs)
    xs_ref, xsp_ref, xsn_ref = next(it), next(it), next(it)
    mi_ref, mip_ref, min_ref = next(it), next(it), next(it)
    z_ref = next(it)
    cwx_ref, cbx_ref, cwm_ref, cbm_ref = next(it), next(it), next(it), next(it)
    dtb_ref, alog_ref, dsum_ref, nw_ref = next(it), next(it), next(it), next(it)
    if has_init:
        init_ref = next(it)
    if has_prev:
        next(it)
    y_ref, sto_ref = next(it), next(it)
    xc_scr, bc_scr, yf_scr, st_scr = next(it), next(it), next(it), next(it)

    L = M_CHUNK
    LB = cps * L
    ns = nc // cps
    j = pl.program_id(1)
    fwd = j < ns
    c = jnp.where(fwd, j, 2 * ns - 1 - j)
    row0 = pl.multiple_of(c * LB, LB)
    LOG2E = math.log2(math.e)

    rid = lax.broadcasted_iota(jnp.int32, (L, L), 0)
    cid = lax.broadcasted_iota(jnp.int32, (L, L), 1)

    def conv_silu(x, prow, nrow, w_ref, b_ref):
        n = x.shape[1]
        r = lax.broadcasted_iota(jnp.int32, (LB, n), 0)
        xm = jnp.where(r == 0, prow, pltpu.roll(x, 1, 0))
        xp = jnp.where(r == LB - 1, nrow, pltpu.roll(x, LB - 1, 0))
        y = w_ref[0:1, :] * xm + w_ref[1:2, :] * x + w_ref[2:3, :] * xp + b_ref[...]
        return _silu(y)

    def load_state(d):
        if has_init:
            for i in range(M_PAIRS):
                st_scr[i] = init_ref[d, i]
        else:
            st_scr[...] = jnp.zeros_like(st_scr)

    def store_state(d):
        pad = jnp.zeros((2 * M_HEADDIM - M_STATE, 2 * M_HEADDIM), F32)
        for i in range(M_PAIRS):
            t = jnp.concatenate([st_scr[i], pad], axis=0).T
            sto_ref[d, i * 2 * M_HEADDIM:(i + 1) * 2 * M_HEADDIM, :] = t[:, 0:M_STATE]

    lane_lo = cid < M_HEADDIM
    lane_lo_s = lax.broadcasted_iota(jnp.int32, (M_STATE, 2 * M_HEADDIM), 1) < M_HEADDIM
    heads_per_group = M_HEADS // M_GROUPS
    zeros_s = jnp.zeros((M_STATE, 2 * M_HEADDIM), BF16)

    def setup(d, bc, dt_raw):
        tri = (cid <= rid) if d == 0 else (cid >= rid)
        tri_bf = jnp.where(tri, 1.0, 0.0).astype(BF16)
        dt = dt_raw + dtb_ref[...]
        dt = jnp.maximum(dt, 0.0) + jnp.log1p(jnp.exp(-jnp.abs(dt)))
        a = dt * (-jnp.exp(alog_ref[...]))
        a_hi, a_mid, a_lo = _split3(a)
        p_col = (jnp.dot(tri_bf, a_hi, preferred_element_type=F32)
                 + jnp.dot(tri_bf, a_mid, preferred_element_type=F32)
                 + jnp.dot(tri_bf, a_lo, preferred_element_type=F32))
        p_row = p_col.T[d * M_HEADS:(d + 1) * M_HEADS, :]
        dt_row = dt.T[d * M_HEADS:(d + 1) * M_HEADS, :]
        tot = p_row[:, L - 1:L] if d == 0 else p_row[:, 0:1]
        b_all = bc[:, 0:M_GROUPS * M_STATE]
        c_all = bc[:, M_GROUPS * M_STATE:2 * M_GROUPS * M_STATE]
        b_bf = b_all.astype(BF16)
        g_mats = []
        for g in range(M_GROUPS):
            cg = jnp.where(lane_lo if g == 0 else jnp.logical_not(lane_lo), c_all, 0.0).astype(BF16)
            g_mats.append(lax.dot_general(cg, b_bf, (((1,), (1,)), ((), ())),
                                          preferred_element_type=F32).astype(BF16))
        return dict(
            tri=tri, g_mats=g_mats, c_bf=c_all.astype(BF16), bt_all=b_all.T,
            p2_col=p_col * LOG2E,
            q2_row=(p_row - jnp.log(dt_row)) * LOG2E,
            w_row=dt_row * jnp.exp(tot - p_row),
            etot=jnp.exp(tot),
            ep_col=jnp.exp(p_col))

    def pairs(d, s, x_bf):
        outs = []
        for i in range(M_PAIRS):
            g = (2 * i) // heads_per_group
            btg = s["bt_all"][g * M_STATE:(g + 1) * M_STATE, :]
            lhs_rows = []
            bw_rows = []
            for hh in range(2):
                h = 2 * i + hh
                col = d * M_HEADS + h
                pc = jnp.broadcast_to(s["p2_col"][:, col:col + 1], (L, L))
                dm = jnp.exp2(jnp.where(s["tri"], pc - s["q2_row"][h:h + 1, :], NEG_BIG))
                m_h = s["g_mats"][g] * dm.astype(BF16)
                ce_h = s["c_bf"] * jnp.broadcast_to(s["ep_col"][:, col:col + 1], (L, L)).astype(BF16)
                lhs_rows.append(jnp.concatenate([m_h, ce_h], axis=1))
                bw = (btg * s["w_row"][h:h + 1, :]).astype(BF16)
                bw_rows.append(jnp.concatenate([bw, zeros_s], axis=1))
            lhs = jnp.concatenate(lhs_rows + bw_rows, axis=0)
            x_pair = x_bf[:, i * 128:(i + 1) * 128]
            st_pair = st_scr[i]
            st_bf = st_pair.astype(BF16)
            rhs = jnp.concatenate([x_pair] + ([st_bf, zeros_s] if g == 0 else [zeros_s, st_bf]), axis=0)
            res = jnp.dot(lhs, rhs, preferred_element_type=F32)
            y_pair = jnp.where(lane_lo, res[0:L], res[L:2 * L])
            ds = jnp.where(lane_lo_s, res[2 * L:2 * L + M_STATE], res[2 * L + M_STATE:2 * L + 2 * M_STATE])
            e0 = jnp.broadcast_to(s["etot"][2 * i:2 * i + 1, :], (M_STATE, 2 * M_HEADDIM))
            e1 = jnp.broadcast_to(s["etot"][2 * i + 1:2 * i + 2, :], (M_STATE, 2 * M_HEADDIM))
            st_scr[i] = jnp.where(lane_lo_s, e0, e1) * st_pair + ds
            outs.append(y_pair)
        return jnp.concatenate(outs, axis=1)

    def block(d, x_bf, bc, dt_raw):
        order = range(cps) if d == 0 else range(cps - 1, -1, -1)
        su = {ci: setup(d, bc[ci * L:(ci + 1) * L], dt_raw[ci * L:(ci + 1) * L]) for ci in order}
        ys = {ci: pairs(d, su[ci], x_bf[ci * L:(ci + 1) * L]) for ci in order}
        return jnp.concatenate([ys[ci] for ci in range(cps)], axis=0)

    @pl.when(j == 0)
    def _():
        load_state(0)

    @pl.when(j == ns)
    def _():
        load_state(1)

    @pl.when(fwd)
    def _():
        x = xs_ref[...].astype(F32)
        prow = jnp.where(c > 0, xsp_ref[...].astype(F32)[15:16, :], 0.0)
        nrow = jnp.where(c < ns - 1, xsn_ref[...].astype(F32)[0:1, :], 0.0)
        xc = conv_silu(x, prow, nrow, cwx_ref, cbx_ref).astype(BF16)
        xc_scr[pl.ds(row0, LB), :] = xc
        m = mi_ref[...]
        bcx = m[:, 0:256]
        prow_m = jnp.where(c > 0, mip_ref[7:8, 0:256], 0.0)
        nrow_m = jnp.where(c < ns - 1, min_ref[0:1, 0:256], 0.0)
        bc = conv_silu(bcx, prow_m, nrow_m, cwm_ref, cbm_ref)
        bc_scr[pl.ds(row0, LB), :] = bc
        yf_scr[pl.ds(row0, LB), :] = block(0, xc, bc, m[:, 256:384])

    @pl.when(j == ns - 1)
    def _():
        store_state(0)

    @pl.when(jnp.logical_not(fwd))
    def _():
        xc = xc_scr[pl.ds(row0, LB), :]
        bc = bc_scr[pl.ds(row0, LB), :]
        yb = block(1, xc, bc, mi_ref[:, 256:384])
        y = yf_scr[pl.ds(row0, LB), :] + yb + dsum_ref[...] * xc.astype(F32)
        y = y * _silu(z_ref[...].astype(F32))
        ms = jnp.mean(y * y, axis=-1, keepdims=True)
        y_ref[...] = (y * lax.rsqrt(ms + EPS) * nw_ref[...]).astype(y_ref.dtype)

    @pl.when(j == 2 * ns - 1)
    def _():
        store_state(1)


def _ssd(main, misc, nseq, T, cwx, cbx, cwm, cbm, dtb, alog, dsum, nw, init, st_stack=None, layer=0, n_layers=1,
         cps=SSD_CHUNKS_PER_STEP):
    R = nseq * T
    nc = T // M_CHUNK
    L = cps * M_CHUNK
    ns = nc // cps
    nc, full_nc = ns, nc
    has_init = init is not None

    def cidx(j):
        return jnp.where(j < nc, j, 2 * nc - 1 - j)

    def oidx(j):
        return jnp.where(j < nc, nc - 1, 2 * nc - 1 - j)

    in_specs = [
        pl.BlockSpec((L, D_MODEL), lambda b, j: (b * nc + cidx(j), SEG_XS)),
        pl.BlockSpec((16, D_MODEL), lambda b, j: (jnp.maximum((b * nc + cidx(j)) * (L // 16) - 1, 0), SEG_XS)),
        pl.BlockSpec((16, D_MODEL),
                     lambda b, j: (jnp.minimum((b * nc + cidx(j) + 1) * (L // 16), R // 16 - 1), SEG_XS)),
        pl.BlockSpec((L, MISC_W), lambda b, j: (b * nc + cidx(j), 0)),
        pl.BlockSpec((8, MISC_W), lambda b, j: (jnp.maximum((b * nc + cidx(j)) * (L // 8) - 1, 0), 0)),
        pl.BlockSpec((8, MISC_W), lambda b, j: (jnp.minimum((b * nc + cidx(j) + 1) * (L // 8), R // 8 - 1), 0)),
        pl.BlockSpec((L, D_MODEL), lambda b, j: (b * nc + oidx(j), SEG_Z)),
        pl.BlockSpec((3, D_MODEL), lambda b, j: (0, 0)),
        pl.BlockSpec((1, D_MODEL), lambda b, j: (0, 0)),
        pl.BlockSpec((3, 256), lambda b, j: (0, 0)),
        pl.BlockSpec((1, 256), lambda b, j: (0, 0)),
        pl.BlockSpec((1, 128), lambda b, j: (0, 0)),
        pl.BlockSpec((1, 128), lambda b, j: (0, 0)),
        pl.BlockSpec((1, D_MODEL), lambda b, j: (0, 0)),
        pl.BlockSpec((1, D_MODEL), lambda b, j: (0, 0)),
    ]
    args = [main, main, main, misc, misc, misc, main, cwx, cbx, cwm, cbm, dtb, alog, dsum, nw]
    if has_init:
        in_specs.append(pl.BlockSpec((None, 2, M_PAIRS, M_STATE, 128), lambda b, j: (b, 0, 0, 0, 0)))
        args.append(init)
    aliases = {}
    if st_stack is not None:
        in_specs.append(pl.BlockSpec(memory_space=pl.ANY))
        args.append(st_stack)
        aliases = {len(args) - 1: 1}
    return pl.pallas_call(
        functools.partial(_ssd_kernel, nc=full_nc, cps=cps, has_init=has_init, has_prev=st_stack is not None),
        out_shape=[jax.ShapeDtypeStruct((R, D_MODEL), BF16),
                   jax.ShapeDtypeStruct((nseq, n_layers, 2, M_HEADS * M_HEADDIM, M_STATE), F32)],
        grid=(nseq, 2 * nc),
        in_specs=in_specs,
        out_specs=[pl.BlockSpec((L, D_MODEL), lambda b, j: (b * nc + oidx(j), 0)),
                   pl.BlockSpec((None, None, 2, M_HEADS * M_HEADDIM, M_STATE), lambda b, j: (b, layer, 0, 0, 0))],
        scratch_shapes=[pltpu.VMEM((T, D_MODEL), BF16), pltpu.VMEM((T, 256), F32),
                        pltpu.VMEM((T, D_MODEL), F32), pltpu.VMEM((M_PAIRS, M_STATE, 128), F32)],
        input_output_aliases=aliases,
        compiler_params=_cparams(("parallel", "arbitrary")),
        name="ssd",
    )(*args)


def _attn_lat_kernel(q_ref, k_ref, v_ref, gb_ref, kc_ref, vc_ref, lv_ref, hw_ref, o_ref, vt_scr, vct_scr,
                     *, lam_init, kb):
    tq = q_ref.shape[0]
    T = k_ref.shape[0]
    lc = kc_ref.shape[0]
    qi = pl.program_id(2)

    @pl.when(qi == 0)
    def _():
        for c0 in range(0, T, 128):
            vt_scr[0:DA_VD, c0:c0 + 128] = v_ref[c0:c0 + 128, :].astype(F32).T.astype(BF16)
        vt_scr[DA_VD:DA_VD + 16, :] = jnp.ones((16, T), BF16)
        for c0 in range(0, lc, 128):
            vct_scr[0:DA_VD, c0:c0 + 128] = vc_ref[c0:c0 + 128, :].T.astype(BF16)
        vct_scr[DA_VD:DA_VD + 16, :] = jnp.ones((16, lc), BF16)

    lv = lv_ref[...]
    lam = (jnp.exp(jnp.sum(lv[0:1, :] * lv[1:2, :], axis=-1, keepdims=True))
           - jnp.exp(jnp.sum(lv[2:3, :] * lv[3:4, :], axis=-1, keepdims=True)) + lam_init)
    sq = ATT_SUBQ
    lane = lax.broadcasted_iota(jnp.int32, (sq, 2 * DA_QK), 1)
    dn_t = (((1,), (1,)), ((), ()))

    q2s = []
    for c in range(tq // sq):
        q = q_ref[c * sq:(c + 1) * sq, :].astype(F32) * (DA_QK ** -0.5 * math.log2(math.e))
        q2s.append(jnp.concatenate([jnp.where(lane < DA_QK, q, 0.0), jnp.where(lane < DA_QK, 0.0, q)],
                                   axis=0).astype(BF16))

    def scores(k_blk, q2):
        s = lax.dot_general(k_blk, q2, dn_t, preferred_element_type=F32)
        return s, jnp.max(s, axis=0, keepdims=True)

    def accumulate(s, bm, vt_blk, state):
        m_new = bm if state is None else jnp.maximum(state[0], bm)
        p = jnp.exp2(s - m_new).astype(BF16)
        pv = jnp.dot(vt_blk, p, preferred_element_type=F32)
        if state is None:
            return m_new, pv
        return m_new, jnp.exp2(state[0] - m_new) * state[1] + pv

    kcb = kc_ref[...].astype(BF16)
    cur = [scores(kcb, q2) for q2 in q2s]
    cur_vt = vct_scr[...]
    states = [None] * len(q2s)
    for k0 in range(0, T, kb):
        k_blk = k_ref[k0:k0 + kb, :]
        nxt = [scores(k_blk, q2) for q2 in q2s]
        states = [accumulate(cu[0], cu[1], cur_vt, st) for cu, st in zip(cur, states)]
        cur, cur_vt = nxt, vt_scr[:, k0:k0 + kb]
    states = [accumulate(cu[0], cu[1], cur_vt, st) for cu, st in zip(cur, states)]
    for c, (_, acc) in enumerate(states):
        rs = slice(c * sq, (c + 1) * sq)
        r = 1.0 / acc[DA_VD:DA_VD + 1, :]
        o_t = acc[0:DA_VD, 0:sq] * r[:, 0:sq] - acc[0:DA_VD, sq:2 * sq] * (r[:, sq:2 * sq] * lam)
        o = o_t.T
        ms = jnp.mean(o * o, axis=-1, keepdims=True)
        o = o * lax.rsqrt(ms + EPS) * hw_ref[...] * (1.0 - lam_init)
        o_ref[rs, :] = (o * _silu(gb_ref[rs, :].astype(F32))).astype(o_ref.dtype)


def _attn_ctx_kernel(q_ref, k_ref, v_ref, gb_ref, lv_ref, hw_ref, o_ref, *, lam_init):
    T = q_ref.shape[0]
    lv = lv_ref[...]
    lam = (jnp.exp(jnp.sum(lv[0:1, :] * lv[1:2, :], axis=-1, keepdims=True))
           - jnp.exp(jnp.sum(lv[2:3, :] * lv[3:4, :], axis=-1, keepdims=True)) + lam_init)
    lane = lax.broadcasted_iota(jnp.int32, (T, 2 * DA_QK), 1)
    dn_t = (((1,), (1,)), ((), ()))
    ones = jnp.ones((16, T), BF16)

    def scores(hh):
        cs = slice(hh * 128, (hh + 1) * 128)
        q = q_ref[:, cs].astype(F32) * (DA_QK ** -0.5 * math.log2(math.e))
        q2 = jnp.concatenate([jnp.where(lane < DA_QK, q, 0.0), jnp.where(lane < DA_QK, 0.0, q)],
                             axis=0).astype(BF16)
        v = v_ref[:, cs].astype(F32)
        vt = jnp.concatenate([v[c0:c0 + 128, :].T for c0 in range(0, T, 128)], axis=1).astype(BF16)
        vt1 = jnp.concatenate([vt, ones], axis=0)
        s = lax.dot_general(k_ref[:, cs], q2, dn_t, preferred_element_type=F32)
        return s, vt1

    nxt = scores(0)
    for hh in range(DA_HEADS):
        cs = slice(hh * 128, (hh + 1) * 128)
        s, vt1 = nxt
        if hh + 1 < DA_HEADS:
            nxt = scores(hh + 1)
        p = jnp.exp2(s - jnp.max(s, axis=0, keepdims=True)).astype(BF16)
        acc = jnp.dot(vt1, p, preferred_element_type=F32)
        r = 1.0 / acc[DA_VD:DA_VD + 1, :]
        o_t = acc[0:DA_VD, 0:T] * r[:, 0:T] - acc[0:DA_VD, T:2 * T] * (r[:, T:2 * T] * lam)
        o = jnp.concatenate([o_t[:, c0:c0 + 128].T for c0 in range(0, T, 128)], axis=0)
        ms = jnp.mean(o * o, axis=-1, keepdims=True)
        o = o * lax.rsqrt(ms + EPS) * hw_ref[...] * (1.0 - lam_init)
        o_ref[:, cs] = (o * _silu(gb_ref[:, cs].astype(F32))).astype(o_ref.dtype)


def _attention_ctx(main, nseq, T, lam_vecs, head_w, lam_init):
    return pl.pallas_call(
        functools.partial(_attn_ctx_kernel, lam_init=lam_init),
        out_shape=jax.ShapeDtypeStruct((nseq * T, D_MODEL), BF16),
        grid=(nseq,),
        in_specs=[
            pl.BlockSpec((T, D_MODEL), lambda b: (b, SEG_Q)),
            pl.BlockSpec((T, D_MODEL), lambda b: (b, SEG_K)),
            pl.BlockSpec((T, D_MODEL), lambda b: (b, SEG_V)),
            pl.BlockSpec((T, D_MODEL), lambda b: (b, SEG_GB)),
            pl.BlockSpec((4, DA_QK), lambda b: (0, 0)),
            pl.BlockSpec((1, DA_VD), lambda b: (0, 0)),
        ],
        out_specs=pl.BlockSpec((T, D_MODEL), lambda b: (b, 0)),
        compiler_params=_cparams(("parallel",)),
        name="diff_attn_ctx",
    )(main, main, main, main, lam_vecs, head_w)


def _attention_latent(hm, nseq, T, tq, kb, lam_vecs, head_w, lam_init, ck, cv):
    R = nseq * T
    nq = T // tq
    lc = ck.shape[1]
    return pl.pallas_call(
        functools.partial(_attn_lat_kernel, lam_init=lam_init, kb=kb),
        out_shape=jax.ShapeDtypeStruct((R, D_MODEL), BF16),
        grid=(nseq, DA_HEADS, nq),
        in_specs=[
            pl.BlockSpec((None, None, None, tq, 128), lambda b, g, qi: (0, b, g, qi, 0)),
            pl.BlockSpec((None, None, None, T, 128), lambda b, g, qi: (1, b, g, 0, 0)),
            pl.BlockSpec((None, None, None, T, 128), lambda b, g, qi: (2, b, g, 0, 0)),
            pl.BlockSpec((None, None, None, tq, 128), lambda b, g, qi: (3, b, g, qi, 0)),
            pl.BlockSpec((None, lc, 128), lambda b, g, qi: (b, 0, g)),
            pl.BlockSpec((None, lc, 128), lambda b, g, qi: (b, 0, g)),
            pl.BlockSpec((4, DA_QK), lambda b, g, qi: (0, 0)),
            pl.BlockSpec((1, DA_VD), lambda b, g, qi: (0, 0)),
        ],
        out_specs=pl.BlockSpec((tq, 128), lambda b, g, qi: (b * nq + qi, g)),
        scratch_shapes=[pltpu.VMEM((DA_VD + 16, T), BF16), pltpu.VMEM((DA_VD + 16, lc), BF16)],
        compiler_params=_cparams(("parallel", "parallel", "arbitrary")),
        name="diff_attn_lat",
    )(hm, hm, hm, hm, ck, cv, lam_vecs, head_w)


def _sgmlp_merge_kernel(ya_ref, yb_ref, u_ref, sv_ref, gc_ref, ga_ref, gb_ref, gm_ref, vw_ref, ws_ref, bias_ref,
                        wb_ref, wo_ref, pw_ref, mod_ref, x_ref, o_ref):
    tm = u_ref.shape[0]

    def gated(y, g_ref, i):
        p = jnp.dot(y, wb_ref[i], preferred_element_type=F32)
        return _sigmoid(g_ref[...].astype(F32)) * p

    u = _gelu_tanh(u_ref[...].astype(F32))
    v = _gelu_tanh(sv_ref[...].astype(F32))
    vc = v - jnp.mean(v, axis=-1, keepdims=True)
    vb = (vc * lax.rsqrt(jnp.mean(vc * vc, axis=-1, keepdims=True) + EPS) * vw_ref[...]).astype(BF16)
    ug = u * _silu(gc_ref[...].astype(F32))
    rows = []
    for ci in range(tm // SG_CHUNK):
        rs = slice(ci * SG_CHUNK, (ci + 1) * SG_CHUNK)
        cols = [jnp.dot(ws_ref[g], vb[rs, g * 128:(g + 1) * 128], preferred_element_type=F32)
                for g in range(SG_GROUPS)]
        rows.append(((jnp.concatenate(cols, axis=1) + bias_ref[...]) * ug[rs]).astype(BF16))
    yc = jnp.concatenate(rows, axis=0)

    merged = gated(ya_ref[...], ga_ref, 0) + gated(yb_ref[...], gb_ref, 1) + gated(yc, gm_ref, 2)
    o = jnp.dot(merged.astype(BF16), wo_ref[...], preferred_element_type=F32)
    ms = jnp.mean(o * o, axis=-1, keepdims=True)
    o = o * lax.rsqrt(ms + EPS) * pw_ref[...]
    o_ref[...] = x_ref[...] + mod_ref[2:3, :] * o


def _sgmlp_merge(ya, yb, main, seg_shift, vnorm_w, ws_bf, bias_exp, wb_bf, wo_bf, l, post_w, mod, rows_per_mod, x,
                 tm=512):
    R = x.shape[0]
    row = lambda i: (i, 0)
    seg = lambda s: pl.BlockSpec((tm, D_MODEL), lambda i: (i, s - seg_shift))
    const = dict(pipeline_mode=pl.Buffered(1))
    return pl.pallas_call(
        _sgmlp_merge_kernel,
        out_shape=jax.ShapeDtypeStruct((R, D_MODEL), F32),
        grid=(R // tm,),
        in_specs=[
            pl.BlockSpec((tm, D_MODEL), row),
            pl.BlockSpec((tm, D_MODEL), row),
            seg(SEG_U), seg(SEG_SV), seg(SEG_GC), seg(SEG_MGA), seg(SEG_MGB), seg(SEG_MGC),
            pl.BlockSpec((1, D_MODEL), lambda i: (0, 0)),
            pl.BlockSpec((None, SG_GROUPS, SG_CHUNK, SG_CHUNK), lambda i: (l, 0, 0, 0), **const),
            pl.BlockSpec((None, SG_CHUNK, D_MODEL), lambda i: (l, 0, 0), **const),
            pl.BlockSpec((None, 3, D_MODEL, D_MODEL), lambda i: (l, 0, 0, 0), **const),
            pl.BlockSpec((None, D_MODEL, D_MODEL), lambda i: (l, 0, 0), **const),
            pl.BlockSpec((1, D_MODEL), lambda i: (0, 0)),
            pl.BlockSpec((None, 3, D_MODEL), lambda i: ((i * tm) // rows_per_mod, 0, 0)),
            pl.BlockSpec((tm, D_MODEL), row),
        ],
        out_specs=pl.BlockSpec((tm, D_MODEL), row),
        compiler_params=_cparams(("parallel",)),
        name="sgmlp_merge",
    )(ya, yb, main, main, main, main, main, main, vnorm_w, ws_bf, bias_exp, wb_bf, wo_bf, post_w, mod, x)


def _rope_tables(n_tokens):
    n_rows = n_tokens // GRID_W
    rows = jnp.repeat(jnp.arange(n_rows, dtype=F32), GRID_W)
    cols = jnp.tile(jnp.arange(GRID_W, dtype=F32), n_rows)
    n_freq = DA_QK // 4
    inv = ROPE_BASE ** (-jnp.arange(n_freq, dtype=F32) / n_freq)
    ang = jnp.concatenate([rows[:, None] * inv, cols[:, None] * inv], -1)
    cos, sin = jnp.cos(ang), jnp.sin(ang)
    return jnp.tile(cos, (1, 4)), jnp.concatenate([-sin, sin, -sin, sin], axis=1)


def _state_to_pairs(s):
    lead = s.shape[:-3]
    s = s.reshape(lead + (M_PAIRS, 2, M_HEADDIM, M_STATE))
    s = jnp.moveaxis(s, -1, -3)
    return s.reshape(lead + (M_PAIRS, M_STATE, 2 * M_HEADDIM))


def kernel(x_prompt, x_sample, cache_k, cache_v, state_ssm, c, c_ctx, pre_norm_w, post_norm_w, w_mod, b_mod, w_in,
           m_conv_w, m_conv_b, m_A_log, m_dt_bias, m_D, m_norm_w, da_lambda, da_head_norm_w, sg_vnorm_w,
           sg_spatial_w, sg_spatial_b, w_branch, w_out):
    depth = w_in.shape[0]
    nb, seq, _ = x_prompt.shape
    db, dseq, _ = x_sample.shape
    past = cache_k.shape[2]

    w_main, w_misc = _prep_in_weights(jnp.swapaxes(w_in, 1, 2))
    wb_bf = w_branch.astype(BF16)
    wo_bf = w_out.astype(BF16)
    ws_bf = sg_spatial_w.astype(BF16)
    bias_exp = jnp.repeat(jnp.swapaxes(sg_spatial_b, 1, 2), D_MODEL // SG_GROUPS, axis=2)
    cw = jnp.swapaxes(m_conv_w, 1, 2)
    dtb = jnp.pad(m_dt_bias.reshape(depth, 1, 2 * M_HEADS), ((0, 0), (0, 0), (0, 128 - 2 * M_HEADS)))
    alog = jnp.pad(m_A_log.reshape(depth, 1, 2 * M_HEADS), ((0, 0), (0, 0), (0, 128 - 2 * M_HEADS)))
    dsum = jnp.repeat(m_D[:, 0] + m_D[:, 1], M_HEADDIM, axis=1).reshape(depth, 1, D_MODEL)

    cvec = jnp.concatenate([c_ctx[None, :], c, jnp.zeros((8 - 1 - db, D_MODEL), F32)], axis=0)
    mods = _modulation(cvec, w_mod, b_mod).reshape(depth, 8, 3, D_MODEL)

    rope_tabs = _rope_tables(dseq)
    init_pairs = _state_to_pairs(state_ssm)
    ck = cache_k.reshape(db, depth, past, D_MODEL)
    cv = cache_v.reshape(db, depth, past, D_MODEL)

    xp = x_prompt.reshape(nb * seq, D_MODEL)
    xs = x_sample.reshape(db * dseq, D_MODEL)

    def layer(x, l, nseq, T, mod, rows_per_mod, latent, stacks=(None, None, None)):
        lam_init = 0.8 - 0.6 * math.exp(-0.3 * l)
        k_stack, v_stack, st_stack = stacks
        outs = _inproj(x, mod, pre_norm_w[l][None], w_main, w_misc, l, rows_per_mod,
                       rope_tabs if latent else None, None if latent else (T, depth, k_stack, v_stack))
        main, misc = outs[0], outs[1]
        ya, st = _ssd(main, misc, nseq, T, cw[l, :, 0:D_MODEL], m_conv_b[l][None, 0:D_MODEL],
                      cw[l, :, D_MODEL:], m_conv_b[l][None, D_MODEL:], dtb[l], alog[l], dsum[l],
                      m_norm_w[l][None], init_pairs[:, l] if latent else None,
                      st_stack=None if latent else st_stack, layer=0 if latent else l,
                      n_layers=1 if latent else depth, cps=min(SSD_CHUNKS_PER_STEP, T // M_CHUNK))
        if latent:
            yb = _attention_latent(outs[2], nseq, T, ATT_TQ, ATT_KB, da_lambda[l], da_head_norm_w[l][None],
                                   lam_init, ck[:, l], cv[:, l])
        else:
            yb = _attention_ctx(main, nseq, T, da_lambda[l], da_head_norm_w[l][None], lam_init)
        shift = LATENT_SEG_SHIFT if latent else 0
        x_new = _sgmlp_merge(ya, yb, main, shift, sg_vnorm_w[l][None], ws_bf, bias_exp, wb_bf, wo_bf, l,
                             post_norm_w[l][None], mod, rows_per_mod, x)
        return x_new, (None if latent else (outs[2], outs[3], st))

    stacks = (jnp.zeros((nb, depth, seq, D_MODEL), F32), jnp.zeros((nb, depth, seq, D_MODEL), F32),
              jnp.zeros((nb, depth, 2, M_HEADS * M_HEADDIM, M_STATE), F32))
    for l in range(depth):
        xp, stacks = layer(xp, l, nb, seq, mods[l, 0:1], nb * seq, False, stacks)
        xs, _ = layer(xs, l, db, dseq, mods[l, 1:1 + db], dseq, True)

    k_all, v_all, st_all = stacks
    return (xp.reshape(nb, seq, D_MODEL), xs.reshape(db, dseq, D_MODEL),
            k_all.reshape(nb, depth, seq, DA_HEADS, 2 * DA_QK), v_all.reshape(nb, depth, seq, DA_HEADS, DA_VD),
            st_all.reshape(nb, depth, 2, M_HEADS, M_HEADDIM, M_STATE))
```

```python
import functools
import math

import jax
import jax.numpy as jnp
from jax import lax
from jax.experimental import pallas as pl
from jax.experimental.pallas import tpu as pltpu

F32 = jnp.float32
BF16 = jnp.bfloat16

D_MODEL = 1024
EPS = 1e-6
GRID_W = 64
ROPE_BASE = 10000.0
M_HEADS = 16
M_HEADDIM = 64
M_STATE = 64
M_GROUPS = 2
M_CHUNK = 128
M_PAIRS = M_HEADS // 2
DA_HEADS = 8
DA_QK = 64
DA_VD = 128
SG_GROUPS = 8
SG_CHUNK = 128

SEG_Z, SEG_XS, SEG_Q, SEG_K, SEG_V, SEG_GB, SEG_U, SEG_SV, SEG_GC, SEG_MGA, SEG_MGB, SEG_MGC = range(12)
N_SEG = 12
LATENT_SEG_SHIFT = 4
MISC_W = 384

VMEM_LIMIT = 56 * 1024 * 1024
NEG_BIG = -1e30
ATT_SUBQ = 128
ATT_TQ = 1024
ATT_KB = 512
SSD_CHUNKS_PER_STEP = 4
INPROJ_TM = 1024
INPROJ_TN = 2048
INPROJ_SUBN = 1024


_NEG_LOG2E = -math.log2(math.e)


def _silu(x):
    return x * _sigmoid(x)


def _sigmoid(x):
    return 1.0 / (1.0 + jnp.exp2(x * _NEG_LOG2E))


def _gelu_tanh(x):
    c = math.sqrt(2.0 / math.pi)
    hx = 0.5 * x
    return hx + hx * jnp.tanh(x * (c + (c * 0.044715) * (x * x)))


def _cparams(sem):
    return pltpu.CompilerParams(dimension_semantics=sem, vmem_limit_bytes=VMEM_LIMIT)


def _mod_kernel(c_ref, w_ref, b_ref, o_ref):
    c = c_ref[...]
    s = _silu(c).astype(BF16)
    o_ref[...] = jnp.dot(s, w_ref[...].astype(BF16), preferred_element_type=F32) + b_ref[...]


def _modulation(cvec, w_mod, b_mod):
    depth = w_mod.shape[0]
    nt = 3
    return pl.pallas_call(
        _mod_kernel,
        out_shape=jax.ShapeDtypeStruct((depth, 8, 3 * D_MODEL), F32),
        grid=(depth, nt),
        in_specs=[
            pl.BlockSpec((8, D_MODEL), lambda l, j: (0, 0)),
            pl.BlockSpec((None, D_MODEL, D_MODEL), lambda l, j: (l, 0, j)),
            pl.BlockSpec((None, 1, D_MODEL), lambda l, j: (l, 0, j)),
        ],
        out_specs=pl.BlockSpec((None, 8, D_MODEL), lambda l, j: (l, 0, j)),
        compiler_params=_cparams(("arbitrary", "arbitrary")),
        name="modulation",
    )(cvec, w_mod, b_mod.reshape(depth, 1, 3 * D_MODEL))


W_IN_COLS = 12576
W_XS_END = 2048
W_MISC_END = 2336


def _wprep_kernel(wa_ref, wb_ref, wm_ref, main_ref, misc_ref):
    j = pl.program_id(1)

    @pl.when(j < W_XS_END // D_MODEL)
    def _():
        main_ref[...] = wa_ref[...].T.astype(BF16)

    @pl.when(j >= W_XS_END // D_MODEL)
    def _():
        main_ref[...] = wb_ref[...].T.astype(BF16)

    @pl.when(j == 0)
    def _():
        n = W_MISC_END - W_XS_END
        wm = jnp.concatenate([wm_ref[...], jnp.zeros((MISC_W - n, D_MODEL), F32)], axis=0)
        misc_ref[...] = wm.T.astype(BF16)


def _prep_in_weights(w_in_t):
    depth = w_in_t.shape[0]
    n_head = W_XS_END // D_MODEL
    w2d = w_in_t.reshape(depth * W_IN_COLS, D_MODEL)
    skew = W_MISC_END - W_XS_END
    return pl.pallas_call(
        _wprep_kernel,
        out_shape=[jax.ShapeDtypeStruct((depth, D_MODEL, N_SEG * D_MODEL), BF16),
                   jax.ShapeDtypeStruct((depth, D_MODEL, MISC_W), BF16)],
        grid=(depth, N_SEG),
        in_specs=[pl.BlockSpec((None, D_MODEL, D_MODEL), lambda l, j: (l, jnp.minimum(j, n_head - 1), 0)),
                  pl.BlockSpec((pl.Element(D_MODEL), pl.Element(D_MODEL)),
                               lambda l, j: (pl.multiple_of(l * W_IN_COLS + skew + j * D_MODEL, 32), 0)),
                  pl.BlockSpec((pl.Element(skew), pl.Element(D_MODEL)),
                               lambda l, j: (pl.multiple_of(l * W_IN_COLS + W_XS_END, 32), 0))],
        out_specs=[pl.BlockSpec((None, D_MODEL, D_MODEL), lambda l, j: (l, 0, j)),
                   pl.BlockSpec((None, D_MODEL, MISC_W), lambda l, j: (l, 0, 0))],
        compiler_params=_cparams(("parallel", "arbitrary")),
        name="w_prep",
    )(w_in_t, w2d, w2d)


def _inproj_kernel(*refs, rope, emit_kv, has_prev):
    it = iter(refs)
    x_ref, mod_ref, prew_ref, w_ref, wm_ref = next(it), next(it), next(it), next(it), next(it)
    if rope:
        cos_ref, sin_ref = next(it), next(it)
    if has_prev:
        next(it), next(it)
    main_ref, misc_ref = next(it), next(it)
    if emit_kv:
        kf_ref, vf_ref = next(it), next(it)
    if rope:
        hm_ref = next(it)
    h_scr = next(it)

    j = pl.program_id(1)
    tm = x_ref.shape[0]
    tn = w_ref.shape[1]
    spt = tn // D_MODEL

    @pl.when(j == 0)
    def _():
        x = x_ref[...]
        ms = jnp.mean(x * x, axis=-1, keepdims=True)
        y = x * lax.rsqrt(ms + EPS) * prew_ref[...]
        h = y * (1.0 + mod_ref[1:2, :]) + mod_ref[0:1, :]
        hb = h.astype(BF16)
        h_scr[...] = hb
        misc_ref[...] = jnp.dot(hb, wm_ref[...], preferred_element_type=F32)

    h = h_scr[...]
    sub = INPROJ_SUBN
    for c0 in range(0, tn, sub):
        cs = slice(c0, c0 + sub)
        seg = j * spt + c0 // D_MODEL
        lc = c0 % D_MODEL
        ls = slice(lc, lc + sub)
        acc = jnp.dot(h, w_ref[:, cs], preferred_element_type=F32)

        if rope:
            is_rope = jnp.logical_or(seg == SEG_Q, seg == SEG_K)
            is_plain_hm = jnp.logical_or(seg == SEG_V, seg == SEG_GB)

            def store_heads(val):
                vb = val.astype(hm_ref.dtype)
                for hh in range(sub // 128):
                    hm_ref[c0 // D_MODEL, lc // 128 + hh] = vb[:, hh * 128:(hh + 1) * 128]

            @pl.when(is_rope)
            def _():
                cos = jnp.concatenate([cos_ref[...]] * (sub // 128), axis=1)
                sin = jnp.concatenate([sin_ref[...]] * (sub // 128), axis=1)
                lane = lax.broadcasted_iota(jnp.int32, (tm, sub), 1)
                first_half = (lane & (DA_QK - 1)) < (DA_QK // 2)
                rot = jnp.where(first_half, pltpu.roll(acc, sub - DA_QK // 2, 1), pltpu.roll(acc, DA_QK // 2, 1))
                store_heads(acc * cos + rot * sin)

            @pl.when(is_plain_hm)
            def _():
                store_heads(acc)

            @pl.when(jnp.logical_not(jnp.logical_or(is_rope, is_plain_hm)))
            def _():
                main_ref[:, cs] = acc.astype(main_ref.dtype)
        else:
            main_ref[:, cs] = acc.astype(main_ref.dtype)

        if emit_kv:
            nsq, sq_len = kf_ref.shape[0], kf_ref.shape[1]

            @pl.when(seg == SEG_K)
            def _():
                kf_ref[:, :, ls] = acc.reshape(nsq, sq_len, sub)

            @pl.when(seg == SEG_V)
            def _():
                vf_ref[:, :, ls] = acc.reshape(nsq, sq_len, sub)


def _inproj(x, mod, pre_w, w_main, w_misc, l, rows_per_mod, rope_tabs, kv_cache=None, tm=INPROJ_TM, tn=INPROJ_TN):
    R = x.shape[0]
    emit_kv = kv_cache is not None
    ni = R // tm
    spt = tn // D_MODEL
    rope = rope_tabs is not None
    const = dict(pipeline_mode=pl.Buffered(1))
    in_specs = [
        pl.BlockSpec((tm, D_MODEL), lambda i, j: (i, 0), **const),
        pl.BlockSpec((None, 3, D_MODEL), lambda i, j: ((i * tm) // rows_per_mod, 0, 0)),
        pl.BlockSpec((1, D_MODEL), lambda i, j: (0, 0)),
        pl.BlockSpec((None, D_MODEL, tn), lambda i, j: (l, 0, j)),
        pl.BlockSpec((None, D_MODEL, MISC_W), lambda i, j: (l, 0, 0), **const),
    ]
    args = [x, mod, pre_w, w_main, w_misc]
    if rope:
        cos, sin = rope_tabs
        nt = cos.shape[0] // tm
        in_specs += [pl.BlockSpec((tm, 128), lambda i, j: (i % nt, 0)),
                     pl.BlockSpec((tm, 128), lambda i, j: (i % nt, 0))]
        args += [cos, sin]
    if rope:
        n_main = N_SEG - LATENT_SEG_SHIFT
        main_idx = lambda i, j: (i, jnp.where(j < SEG_Q // spt, j,
                                              jnp.where(j <= SEG_GB // spt, SEG_XS // spt,
                                                        j - LATENT_SEG_SHIFT // spt)))
    else:
        n_main = N_SEG
        main_idx = lambda i, j: (i, j)
    out_shape = [jax.ShapeDtypeStruct((R, n_main * D_MODEL), BF16), jax.ShapeDtypeStruct((R, MISC_W), F32)]
    out_specs = [pl.BlockSpec((tm, tn), main_idx), pl.BlockSpec((tm, MISC_W), lambda i, j: (i, 0))]
    aliases = {}
    has_prev = False
    if emit_kv:
        seq_len, n_layers, k_prev, v_prev = kv_cache
        out_shape += [jax.ShapeDtypeStruct((R // seq_len, n_layers, seq_len, D_MODEL), F32)] * 2
        out_specs += [pl.BlockSpec((tm // seq_len, None, seq_len, D_MODEL), lambda i, j: (i, l, 0, 0))] * 2
        if k_prev is not None:
            has_prev = True
            in_specs += [pl.BlockSpec(memory_space=pl.ANY)] * 2
            args += [k_prev, v_prev]
            aliases = {len(args) - 2: 2, len(args) - 1: 3}
    if rope:
        T = cos.shape[0]
        out_shape.append(jax.ShapeDtypeStruct((4, R // T, DA_HEADS, T, 128), BF16))
        out_specs.append(pl.BlockSpec(
            (spt, None, DA_HEADS, tm, 128),
            lambda i, j: (jnp.clip(j - SEG_Q // spt, 0, 4 // spt - 1), (i * tm) // T, 0, ((i * tm) % T) // tm, 0)))
    return pl.pallas_call(
        functools.partial(_inproj_kernel, rope=rope, emit_kv=emit_kv, has_prev=has_prev),
        out_shape=out_shape,
        grid=(ni, N_SEG // spt),
        in_specs=in_specs,
        out_specs=out_specs,
        input_output_aliases=aliases,
        scratch_shapes=[pltpu.VMEM((tm, D_MODEL), BF16)],
        compiler_params=_cparams(("parallel", "arbitrary")),
        name="inproj",
    )(*args)


def _split3(a):
    hi = a.astype(BF16)
    r1 = a - hi.astype(F32)
    mid = r1.astype(BF16)
    lo = (r1 - mid.astype(F32)).astype(BF16)
    return hi, mid, lo


def _ssd_kernel(*refs, nc, cps, has_init, has_prev):
    it = iter(refs)
    xs_ref, xsp_ref, xsn_ref = next(it), next(it), next(it)
    mi_ref, mip_ref, min_ref = next(it), next(it), next(it)
    z_ref = next(it)
    cwx_ref, cbx_ref, cwm_ref, cbm_ref = next(it), next(it), next(it), next(it)
    dtb_ref, alog_ref, dsum_ref, nw_ref = next(it), next(it), next(it), next(it)
    if has_init:
        init_ref = next(it)
    if has_prev:
        next(it)
    y_ref, sto_ref = next(it), next(it)
    xc_scr, bc_scr, yf_scr, st_scr = next(it), next(it), next(it), next(it)

    L = M_CHUNK
    LB = cps * L
    ns = nc // cps
    j = pl.program_id(1)
    fwd = j < ns
    c = jnp.where(fwd, j, 2 * ns - 1 - j)
    row0 = pl.multiple_of(c * LB, LB)
    LOG2E = math.log2(math.e)

    rid = lax.broadcasted_iota(jnp.int32, (L, L), 0)
    cid = lax.broadcasted_iota(jnp.int32, (L, L), 1)

    def conv_silu(x, prow, nrow, w_ref, b_ref):
        n = x.shape[1]
        r = lax.broadcasted_iota(jnp.int32, (LB, n), 0)
        xm = jnp.where(r == 0, prow, pltpu.roll(x, 1, 0))
        xp = jnp.where(r == LB - 1, nrow, pltpu.roll(x, LB - 1, 0))
        y = w_ref[0:1, :] * xm + w_ref[1:2, :] * x + w_ref[2:3, :] * xp + b_ref[...]
        return _silu(y)

    def load_state(d):
        if has_init:
            pad = jnp.zeros((2 * M_HEADDIM, 2 * M_HEADDIM - M_STATE), F32)
            for i in range(M_PAIRS):
                blk = init_ref[d, i * 2 * M_HEADDIM:(i + 1) * 2 * M_HEADDIM, :]
                st_scr[i] = jnp.concatenate([blk, pad], axis=1).T[0:M_STATE, :]
        else:
            st_scr[...] = jnp.zeros_like(st_scr)

    def store_state(d):
        pad = jnp.zeros((2 * M_HEADDIM - M_STATE, 2 * M_HEADDIM), F32)
        for i in range(M_PAIRS):
            t = jnp.concatenate([st_scr[i], pad], axis=0).T
            sto_ref[d, i * 2 * M_HEADDIM:(i + 1) * 2 * M_HEADDIM, :] = t[:, 0:M_STATE]

    lane_lo = cid < M_HEADDIM
    lane_lo_s = lax.broadcasted_iota(jnp.int32, (M_STATE, 2 * M_HEADDIM), 1) < M_HEADDIM
    heads_per_group = M_HEADS // M_GROUPS
    zeros_s = jnp.zeros((M_STATE, 2 * M_HEADDIM), BF16)

    def setup(d, bc, dt_raw):
        tri = (cid <= rid) if d == 0 else (cid >= rid)
        tri_bf = jnp.where(tri, 1.0, 0.0).astype(BF16)
        dt = dt_raw + dtb_ref[...]
        dt = jnp.maximum(dt, 0.0) + jnp.log1p(jnp.exp(-jnp.abs(dt)))
        a = dt * (-jnp.exp(alog_ref[...]))
        a_hi, a_mid, a_lo = _split3(a)
        p_col = (jnp.dot(tri_bf, a_hi, preferred_element_type=F32)
                 + jnp.dot(tri_bf, a_mid, preferred_element_type=F32)
                 + jnp.dot(tri_bf, a_lo, preferred_element_type=F32))
        p_row = p_col.T[d * M_HEADS:(d + 1) * M_HEADS, :]
        dt_row = dt.T[d * M_HEADS:(d + 1) * M_HEADS, :]
        tot = p_row[:, L - 1:L] if d == 0 else p_row[:, 0:1]
        b_all = bc[:, 0:M_GROUPS * M_STATE]
        c_all = bc[:, M_GROUPS * M_STATE:2 * M_GROUPS * M_STATE]
        b_bf = b_all.astype(BF16)
        g_mats = []
        for g in range(M_GROUPS):
            cg = jnp.where(lane_lo if g == 0 else jnp.logical_not(lane_lo), c_all, 0.0).astype(BF16)
            g_mats.append(lax.dot_general(cg, b_bf, (((1,), (1,)), ((), ())),
                                          preferred_element_type=F32).astype(BF16))
        return dict(
            tri=tri, g_mats=g_mats, c_bf=c_all.astype(BF16), bt_all=b_all.T,
            p2_col=p_col * LOG2E,
            q2_row=(p_row - jnp.log(dt_row)) * LOG2E,
            w_row=dt_row * jnp.exp(tot - p_row),
            etot=jnp.exp(tot),
            ep_col=jnp.exp(p_col))

    def pairs(d, s, x_bf):
        outs = []
        for i in range(M_PAIRS):
            g = (2 * i) // heads_per_group
            btg = s["bt_all"][g * M_STATE:(g + 1) * M_STATE, :]
            lhs_rows = []
            bw_rows = []
            for hh in range(2):
                h = 2 * i + hh
                col = d * M_HEADS + h
                pc = jnp.broadcast_to(s["p2_col"][:, col:col + 1], (L, L))
                dm = jnp.exp2(jnp.where(s["tri"], pc - s["q2_row"][h:h + 1, :], NEG_BIG))
                m_h = s["g_mats"][g] * dm.astype(BF16)
                ce_h = s["c_bf"] * jnp.broadcast_to(s["ep_col"][:, col:col + 1], (L, L)).astype(BF16)
                lhs_rows.append(jnp.concatenate([m_h, ce_h], axis=1))
                bw = (btg * s["w_row"][h:h + 1, :]).astype(BF16)
                bw_rows.append(jnp.concatenate([bw, zeros_s], axis=1))
            lhs = jnp.concatenate(lhs_rows + bw_rows, axis=0)
            x_pair = x_bf[:, i * 128:(i + 1) * 128]
            st_pair = st_scr[i]
            st_bf = st_pair.astype(BF16)
            rhs = jnp.concatenate([x_pair] + ([st_bf, zeros_s] if g == 0 else [zeros_s, st_bf]), axis=0)
            res = jnp.dot(lhs, rhs, preferred_element_type=F32)
            y_pair = jnp.where(lane_lo, res[0:L], res[L:2 * L])
            ds = jnp.where(lane_lo_s, res[2 * L:2 * L + M_STATE], res[2 * L + M_STATE:2 * L + 2 * M_STATE])
            e0 = jnp.broadcast_to(s["etot"][2 * i:2 * i + 1, :], (M_STATE, 2 * M_HEADDIM))
            e1 = jnp.broadcast_to(s["etot"][2 * i + 1:2 * i + 2, :], (M_STATE, 2 * M_HEADDIM))
            st_scr[i] = jnp.where(lane_lo_s, e0, e1) * st_pair + ds
            outs.append(y_pair)
        return jnp.concatenate(outs, axis=1)

    def block(d, x_bf, bc, dt_raw):
        order = range(cps) if d == 0 else range(cps - 1, -1, -1)
        su = {ci: setup(d, bc[ci * L:(ci + 1) * L], dt_raw[ci * L:(ci + 1) * L]) for ci in order}
        ys = {ci: pairs(d, su[ci], x_bf[ci * L:(ci + 1) * L]) for ci in order}
        return jnp.concatenate([ys[ci] for ci in range(cps)], axis=0)

    @pl.when(j == 0)
    def _():
        load_state(0)

    @pl.when(j == ns)
    def _():
        load_state(1)

    @pl.when(fwd)
    def _():
        x = xs_ref[...].astype(F32)
        prow = jnp.where(c > 0, xsp_ref[...].astype(F32)[15:16, :], 0.0)
        nrow = jnp.where(c < ns - 1, xsn_ref[...].astype(F32)[0:1, :], 0.0)
        xc = conv_silu(x, prow, nrow, cwx_ref, cbx_ref).astype(BF16)
        xc_scr[pl.ds(row0, LB), :] = xc
        m = mi_ref[...]
        bcx = m[:, 0:256]
        prow_m = jnp.where(c > 0, mip_ref[7:8, 0:256], 0.0)
        nrow_m = jnp.where(c < ns - 1, min_ref[0:1, 0:256], 0.0)
        bc = conv_silu(bcx, prow_m, nrow_m, cwm_ref, cbm_ref)
        bc_scr[pl.ds(row0, LB), :] = bc
        yf_scr[pl.ds(row0, LB), :] = block(0, xc, bc, m[:, 256:384])

    @pl.when(j == ns - 1)
    def _():
        store_state(0)

    @pl.when(jnp.logical_not(fwd))
    def _():
        xc = xc_scr[pl.ds(row0, LB), :]
        bc = bc_scr[pl.ds(row0, LB), :]
        yb = block(1, xc, bc, mi_ref[:, 256:384])
        y = yf_scr[pl.ds(row0, LB), :] + yb + dsum_ref[...] * xc.astype(F32)
        y = y * _silu(z_ref[...].astype(F32))
        ms = jnp.mean(y * y, axis=-1, keepdims=True)
        y_ref[...] = (y * lax.rsqrt(ms + EPS) * nw_ref[...]).astype(y_ref.dtype)

    @pl.when(j == 2 * ns - 1)
    def _():
        store_state(1)


def _ssd(main, misc, nseq, T, cwx, cbx, cwm, cbm, dtb, alog, dsum, nw, init, st_stack=None, layer=0, n_layers=1,
         cps=SSD_CHUNKS_PER_STEP):
    R = nseq * T
    nc = T // M_CHUNK
    L = cps * M_CHUNK
    ns = nc // cps
    nc, full_nc = ns, nc
    has_init = init is not None

    def cidx(j):
        return jnp.where(j < nc, j, 2 * nc - 1 - j)

    def oidx(j):
        return jnp.where(j < nc, nc - 1, 2 * nc - 1 - j)

    in_specs = [
        pl.BlockSpec((L, D_MODEL), lambda b, j: (b * nc + cidx(j), SEG_XS)),
        pl.BlockSpec((16, D_MODEL), lambda b, j: (jnp.maximum((b * nc + cidx(j)) * (L // 16) - 1, 0), SEG_XS)),
        pl.BlockSpec((16, D_MODEL),
                     lambda b, j: (jnp.minimum((b * nc + cidx(j) + 1) * (L // 16), R // 16 - 1), SEG_XS)),
        pl.BlockSpec((L, MISC_W), lambda b, j: (b * nc + cidx(j), 0)),
        pl.BlockSpec((8, MISC_W), lambda b, j: (jnp.maximum((b * nc + cidx(j)) * (L // 8) - 1, 0), 0)),
        pl.BlockSpec((8, MISC_W), lambda b, j: (jnp.minimum((b * nc + cidx(j) + 1) * (L // 8), R // 8 - 1), 0)),
        pl.BlockSpec((L, D_MODEL), lambda b, j: (b * nc + oidx(j), SEG_Z)),
        pl.BlockSpec((3, D_MODEL), lambda b, j: (0, 0)),
        pl.BlockSpec((1, D_MODEL), lambda b, j: (0, 0)),
        pl.BlockSpec((3, 256), lambda b, j: (0, 0)),
        pl.BlockSpec((1, 256), lambda b, j: (0, 0)),
        pl.BlockSpec((1, 128), lambda b, j: (0, 0)),
        pl.BlockSpec((1, 128), lambda b, j: (0, 0)),
        pl.BlockSpec((1, D_MODEL), lambda b, j: (0, 0)),
        pl.BlockSpec((1, D_MODEL), lambda b, j: (0, 0)),
    ]
    args = [main, main, main, misc, misc, misc, main, cwx, cbx, cwm, cbm, dtb, alog, dsum, nw]
    if has_init:
        init_arr, init_layer = init
        in_specs.append(pl.BlockSpec((None, None, 2, M_HEADS * M_HEADDIM, M_STATE),
                                     lambda b, j: (b, init_layer, 0, 0, 0)))
        args.append(init_arr)
    aliases = {}
    if st_stack is not None:
        in_specs.append(pl.BlockSpec(memory_space=pl.ANY))
        args.append(st_stack)
        aliases = {len(args) - 1: 1}
    return pl.pallas_call(
        functools.partial(_ssd_kernel, nc=full_nc, cps=cps, has_init=has_init, has_prev=st_stack is not None),
        out_shape=[jax.ShapeDtypeStruct((R, D_MODEL), BF16),
                   jax.ShapeDtypeStruct((nseq, n_layers, 2, M_HEADS * M_HEADDIM, M_STATE), F32)],
        grid=(nseq, 2 * nc),
        in_specs=in_specs,
        out_specs=[pl.BlockSpec((L, D_MODEL), lambda b, j: (b * nc + oidx(j), 0)),
                   pl.BlockSpec((None, None, 2, M_HEADS * M_HEADDIM, M_STATE), lambda b, j: (b, layer, 0, 0, 0))],
        scratch_shapes=[pltpu.VMEM((T, D_MODEL), BF16), pltpu.VMEM((T, 256), F32),
                        pltpu.VMEM((T, D_MODEL), F32), pltpu.VMEM((M_PAIRS, M_STATE, 128), F32)],
        input_output_aliases=aliases,
        compiler_params=_cparams(("parallel", "arbitrary")),
        name="ssd",
    )(*args)


def _attn_lat_kernel(q_ref, k_ref, v_ref, gb_ref, kc_ref, vc_ref, lv_ref, hw_ref, o_ref, vt_scr, vct_scr,
                     *, lam_init, kb):
    tq = q_ref.shape[0]
    T = k_ref.shape[0]
    lc = kc_ref.shape[0]
    qi = pl.program_id(2)

    @pl.when(qi == 0)
    def _():
        for c0 in range(0, T, 128):
            vt_scr[0:DA_VD, c0:c0 + 128] = v_ref[c0:c0 + 128, :].astype(F32).T.astype(BF16)
        vt_scr[DA_VD:DA_VD + 16, :] = jnp.ones((16, T), BF16)
        for c0 in range(0, lc, 128):
            vct_scr[0:DA_VD, c0:c0 + 128] = vc_ref[c0:c0 + 128, :].T.astype(BF16)
        vct_scr[DA_VD:DA_VD + 16, :] = jnp.ones((16, lc), BF16)

    lv = lv_ref[...]
    lam = (jnp.exp(jnp.sum(lv[0:1, :] * lv[1:2, :], axis=-1, keepdims=True))
           - jnp.exp(jnp.sum(lv[2:3, :] * lv[3:4, :], axis=-1, keepdims=True)) + lam_init)
    sq = ATT_SUBQ
    lane = lax.broadcasted_iota(jnp.int32, (sq, 2 * DA_QK), 1)
    dn_t = (((1,), (1,)), ((), ()))

    q2s = []
    for c in range(tq // sq):
        q = q_ref[c * sq:(c + 1) * sq, :].astype(F32) * (DA_QK ** -0.5 * math.log2(math.e))
        q2s.append(jnp.concatenate([jnp.where(lane < DA_QK, q, 0.0), jnp.where(lane < DA_QK, 0.0, q)],
                                   axis=0).astype(BF16))

    def scores(k_blk, q2):
        s = lax.dot_general(k_blk, q2, dn_t, preferred_element_type=F32)
        return s, jnp.max(s, axis=0, keepdims=True)

    def accumulate(s, bm, vt_blk, state):
        m_new = bm if state is None else jnp.maximum(state[0], bm)
        p = jnp.exp2(s - m_new).astype(BF16)
        pv = jnp.dot(vt_blk, p, preferred_element_type=F32)
        if state is None:
            return m_new, pv
        return m_new, jnp.exp2(state[0] - m_new) * state[1] + pv

    kcb = kc_ref[...].astype(BF16)
    cur = [scores(kcb, q2) for q2 in q2s]
    cur_vt = vct_scr[...]
    states = [None] * len(q2s)
    for k0 in range(0, T, kb):
        k_blk = k_ref[k0:k0 + kb, :]
        nxt = [scores(k_blk, q2) for q2 in q2s]
        states = [accumulate(cu[0], cu[1], cur_vt, st) for cu, st in zip(cur, states)]
        cur, cur_vt = nxt, vt_scr[:, k0:k0 + kb]
    states = [accumulate(cu[0], cu[1], cur_vt, st) for cu, st in zip(cur, states)]
    for c, (_, acc) in enumerate(states):
        rs = slice(c * sq, (c + 1) * sq)
        r = 1.0 / acc[DA_VD:DA_VD + 1, :]
        o_t = acc[0:DA_VD, 0:sq] * r[:, 0:sq] - acc[0:DA_VD, sq:2 * sq] * (r[:, sq:2 * sq] * lam)
        o = o_t.T
        ms = jnp.mean(o * o, axis=-1, keepdims=True)
        o = o * lax.rsqrt(ms + EPS) * hw_ref[...] * (1.0 - lam_init)
        o_ref[rs, :] = (o * _silu(gb_ref[rs, :].astype(F32))).astype(o_ref.dtype)


def _attn_ctx_kernel(q_ref, k_ref, v_ref, gb_ref, lv_ref, hw_ref, o_ref, *, lam_init):
    T = q_ref.shape[0]
    lv = lv_ref[...]
    lam = (jnp.exp(jnp.sum(lv[0:1, :] * lv[1:2, :], axis=-1, keepdims=True))
           - jnp.exp(jnp.sum(lv[2:3, :] * lv[3:4, :], axis=-1, keepdims=True)) + lam_init)
    lane = lax.broadcasted_iota(jnp.int32, (T, 2 * DA_QK), 1)
    dn_t = (((1,), (1,)), ((), ()))
    ones = jnp.ones((16, T), BF16)

    def scores(hh):
        cs = slice(hh * 128, (hh + 1) * 128)
        q = q_ref[:, cs].astype(F32) * (DA_QK ** -0.5 * math.log2(math.e))
        q2 = jnp.concatenate([jnp.where(lane < DA_QK, q, 0.0), jnp.where(lane < DA_QK, 0.0, q)],
                             axis=0).astype(BF16)
        v = v_ref[:, cs].astype(F32)
        vt = jnp.concatenate([v[c0:c0 + 128, :].T for c0 in range(0, T, 128)], axis=1).astype(BF16)
        vt1 = jnp.concatenate([vt, ones], axis=0)
        s = lax.dot_general(k_ref[:, cs], q2, dn_t, preferred_element_type=F32)
        return s, vt1

    nxt = scores(0)
    for hh in range(DA_HEADS):
        cs = slice(hh * 128, (hh + 1) * 128)
        s, vt1 = nxt
        if hh + 1 < DA_HEADS:
            nxt = scores(hh + 1)
        p = jnp.exp2(s - jnp.max(s, axis=0, keepdims=True)).astype(BF16)
        acc = jnp.dot(vt1, p, preferred_element_type=F32)
        r = 1.0 / acc[DA_VD:DA_VD + 1, :]
        o_t = acc[0:DA_VD, 0:T] * r[:, 0:T] - acc[0:DA_VD, T:2 * T] * (r[:, T:2 * T] * lam)
        o = jnp.concatenate([o_t[:, c0:c0 + 128].T for c0 in range(0, T, 128)], axis=0)
        ms = jnp.mean(o * o, axis=-1, keepdims=True)
        o = o * lax.rsqrt(ms + EPS) * hw_ref[...] * (1.0 - lam_init)
        o_ref[:, cs] = (o * _silu(gb_ref[:, cs].astype(F32))).astype(o_ref.dtype)


def _attention_ctx(main, nseq, T, lam_vecs, head_w, lam_init):
    return pl.pallas_call(
        functools.partial(_attn_ctx_kernel, lam_init=lam_init),
        out_shape=jax.ShapeDtypeStruct((nseq * T, D_MODEL), BF16),
        grid=(nseq,),
        in_specs=[
            pl.BlockSpec((T, D_MODEL), lambda b: (b, SEG_Q)),
            pl.BlockSpec((T, D_MODEL), lambda b: (b, SEG_K)),
            pl.BlockSpec((T, D_MODEL), lambda b: (b, SEG_V)),
            pl.BlockSpec((T, D_MODEL), lambda b: (b, SEG_GB)),
            pl.BlockSpec((4, DA_QK), lambda b: (0, 0)),
            pl.BlockSpec((1, DA_VD), lambda b: (0, 0)),
        ],
        out_specs=pl.BlockSpec((T, D_MODEL), lambda b: (b, 0)),
        compiler_params=_cparams(("parallel",)),
        name="diff_attn_ctx",
    )(main, main, main, main, lam_vecs, head_w)


def _attention_latent(hm, nseq, T, tq, kb, lam_vecs, head_w, lam_init, ck, cv, l):
    R = nseq * T
    nq = T // tq
    lc = ck.shape[2]
    return pl.pallas_call(
        functools.partial(_attn_lat_kernel, lam_init=lam_init, kb=kb),
        out_shape=jax.ShapeDtypeStruct((R, D_MODEL), BF16),
        grid=(nseq, DA_HEADS, nq),
        in_specs=[
            pl.BlockSpec((None, None, None, tq, 128), lambda b, g, qi: (0, b, g, qi, 0)),
            pl.BlockSpec((None, None, None, T, 128), lambda b, g, qi: (1, b, g, 0, 0)),
            pl.BlockSpec((None, None, None, T, 128), lambda b, g, qi: (2, b, g, 0, 0)),
            pl.BlockSpec((None, None, None, tq, 128), lambda b, g, qi: (3, b, g, qi, 0)),
            pl.BlockSpec((None, None, lc, 128), lambda b, g, qi: (b, l, 0, g)),
            pl.BlockSpec((None, None, lc, 128), lambda b, g, qi: (b, l, 0, g)),
            pl.BlockSpec((4, DA_QK), lambda b, g, qi: (0, 0)),
            pl.BlockSpec((1, DA_VD), lambda b, g, qi: (0, 0)),
        ],
        out_specs=pl.BlockSpec((tq, 128), lambda b, g, qi: (b * nq + qi, g)),
        scratch_shapes=[pltpu.VMEM((DA_VD + 16, T), BF16), pltpu.VMEM((DA_VD + 16, lc), BF16)],
        compiler_params=_cparams(("parallel", "parallel", "arbitrary")),
        name="diff_attn_lat",
    )(hm, hm, hm, hm, ck, cv, lam_vecs, head_w)


def _sgmlp_merge_kernel(ya_ref, yb_ref, u_ref, sv_ref, gc_ref, ga_ref, gb_ref, gm_ref, vw_ref, ws_ref, bias_ref,
                        wb_ref, wo_ref, pw_ref, mod_ref, x_ref, o_ref):
    tm = u_ref.shape[0]

    def gated(y, g_ref, i):
        p = jnp.dot(y, wb_ref[i], preferred_element_type=F32)
        return _sigmoid(g_ref[...].astype(F32)) * p

    u = _gelu_tanh(u_ref[...].astype(F32))
    v = _gelu_tanh(sv_ref[...].astype(F32))
    vc = v - jnp.mean(v, axis=-1, keepdims=True)
    vb = (vc * lax.rsqrt(jnp.mean(vc * vc, axis=-1, keepdims=True) + EPS) * vw_ref[...]).astype(BF16)
    ug = u * _silu(gc_ref[...].astype(F32))
    rows = []
    for ci in range(tm // SG_CHUNK):
        rs = slice(ci * SG_CHUNK, (ci + 1) * SG_CHUNK)
        cols = [jnp.dot(ws_ref[g], vb[rs, g * 128:(g + 1) * 128], preferred_element_type=F32)
                for g in range(SG_GROUPS)]
        rows.append(((jnp.concatenate(cols, axis=1) + bias_ref[...]) * ug[rs]).astype(BF16))
    yc = jnp.concatenate(rows, axis=0)

    merged = gated(ya_ref[...], ga_ref, 0) + gated(yb_ref[...], gb_ref, 1) + gated(yc, gm_ref, 2)
    o = jnp.dot(merged.astype(BF16), wo_ref[...], preferred_element_type=F32)
    ms = jnp.mean(o * o, axis=-1, keepdims=True)
    o = o * lax.rsqrt(ms + EPS) * pw_ref[...]
    o_ref[...] = x_ref[...] + mod_ref[2:3, :] * o


def _sgmlp_merge(ya, yb, main, seg_shift, vnorm_w, ws_bf, bias_exp, wb_bf, wo_bf, l, post_w, mod, rows_per_mod, x,
                 tm=512):
    R = x.shape[0]
    row = lambda i: (i, 0)
    seg = lambda s: pl.BlockSpec((tm, D_MODEL), lambda i: (i, s - seg_shift))
    const = dict(pipeline_mode=pl.Buffered(1))
    return pl.pallas_call(
        _sgmlp_merge_kernel,
        out_shape=jax.ShapeDtypeStruct((R, D_MODEL), F32),
        grid=(R // tm,),
        in_specs=[
            pl.BlockSpec((tm, D_MODEL), row),
            pl.BlockSpec((tm, D_MODEL), row),
            seg(SEG_U), seg(SEG_SV), seg(SEG_GC), seg(SEG_MGA), seg(SEG_MGB), seg(SEG_MGC),
            pl.BlockSpec((1, D_MODEL), lambda i: (0, 0)),
            pl.BlockSpec((None, SG_GROUPS, SG_CHUNK, SG_CHUNK), lambda i: (l, 0, 0, 0), **const),
            pl.BlockSpec((None, SG_CHUNK, D_MODEL), lambda i: (l, 0, 0), **const),
            pl.BlockSpec((None, 3, D_MODEL, D_MODEL), lambda i: (l, 0, 0, 0), **const),
            pl.BlockSpec((None, D_MODEL, D_MODEL), lambda i: (l, 0, 0), **const),
            pl.BlockSpec((1, D_MODEL), lambda i: (0, 0)),
            pl.BlockSpec((None, 3, D_MODEL), lambda i: ((i * tm) // rows_per_mod, 0, 0)),
            pl.BlockSpec((tm, D_MODEL), row),
        ],
        out_specs=pl.BlockSpec((tm, D_MODEL), row),
        compiler_params=_cparams(("parallel",)),
        name="sgmlp_merge",
    )(ya, yb, main, main, main, main, main, main, vnorm_w, ws_bf, bias_exp, wb_bf, wo_bf, post_w, mod, x)


def _rope_tables(n_tokens):
    n_rows = n_tokens // GRID_W
    rows = jnp.repeat(jnp.arange(n_rows, dtype=F32), GRID_W)
    cols = jnp.tile(jnp.arange(GRID_W, dtype=F32), n_rows)
    n_freq = DA_QK // 4
    inv = ROPE_BASE ** (-jnp.arange(n_freq, dtype=F32) / n_freq)
    ang = jnp.concatenate([rows[:, None] * inv, cols[:, None] * inv], -1)
    cos, sin = jnp.cos(ang), jnp.sin(ang)
    return jnp.tile(cos, (1, 4)), jnp.concatenate([-sin, sin, -sin, sin], axis=1)


def kernel(x_prompt, x_sample, cache_k, cache_v, state_ssm, c, c_ctx, pre_norm_w, post_norm_w, w_mod, b_mod, w_in,
           m_conv_w, m_conv_b, m_A_log, m_dt_bias, m_D, m_norm_w, da_lambda, da_head_norm_w, sg_vnorm_w,
           sg_spatial_w, sg_spatial_b, w_branch, w_out):
    depth = w_in.shape[0]
    nb, seq, _ = x_prompt.shape
    db, dseq, _ = x_sample.shape
    past = cache_k.shape[2]

    w_main, w_misc = _prep_in_weights(jnp.swapaxes(w_in, 1, 2))
    wb_bf = w_branch.astype(BF16)
    wo_bf = w_out.astype(BF16)
    ws_bf = sg_spatial_w.astype(BF16)
    bias_exp = jnp.repeat(jnp.swapaxes(sg_spatial_b, 1, 2), D_MODEL // SG_GROUPS, axis=2)
    cw = jnp.swapaxes(m_conv_w, 1, 2)
    dtb = jnp.pad(m_dt_bias.reshape(depth, 1, 2 * M_HEADS), ((0, 0), (0, 0), (0, 128 - 2 * M_HEADS)))
    alog = jnp.pad(m_A_log.reshape(depth, 1, 2 * M_HEADS), ((0, 0), (0, 0), (0, 128 - 2 * M_HEADS)))
    dsum = jnp.repeat(m_D[:, 0] + m_D[:, 1], M_HEADDIM, axis=1).reshape(depth, 1, D_MODEL)

    cvec = jnp.concatenate([c_ctx[None, :], c, jnp.zeros((8 - 1 - db, D_MODEL), F32)], axis=0)
    mods = _modulation(cvec, w_mod, b_mod).reshape(depth, 8, 3, D_MODEL)

    rope_tabs = _rope_tables(dseq)
    init_states = state_ssm.reshape(db, depth, 2, M_HEADS * M_HEADDIM, M_STATE)
    ck = cache_k.reshape(db, depth, past, D_MODEL)
    cv = cache_v.reshape(db, depth, past, D_MODEL)

    xp = x_prompt.reshape(nb * seq, D_MODEL)
    xs = x_sample.reshape(db * dseq, D_MODEL)

    def layer(x, l, nseq, T, mod, rows_per_mod, latent, stacks=(None, None, None)):
        lam_init = 0.8 - 0.6 * math.exp(-0.3 * l)
        k_stack, v_stack, st_stack = stacks
        outs = _inproj(x, mod, pre_norm_w[l][None], w_main, w_misc, l, rows_per_mod,
                       rope_tabs if latent else None, None if latent else (T, depth, k_stack, v_stack))
        main, misc = outs[0], outs[1]
        ya, st = _ssd(main, misc, nseq, T, cw[l, :, 0:D_MODEL], m_conv_b[l][None, 0:D_MODEL],
                      cw[l, :, D_MODEL:], m_conv_b[l][None, D_MODEL:], dtb[l], alog[l], dsum[l],
                      m_norm_w[l][None], (init_states, l) if latent else None,
                      st_stack=None if latent else st_stack, layer=0 if latent else l,
                      n_layers=1 if latent else depth, cps=min(SSD_CHUNKS_PER_STEP, T // M_CHUNK))
        if latent:
            yb = _attention_latent(outs[2], nseq, T, ATT_TQ, ATT_KB, da_lambda[l], da_head_norm_w[l][None],
                                   lam_init, ck, cv, l)
        else:
            yb = _attention_ctx(main, nseq, T, da_lambda[l], da_head_norm_w[l][None], lam_init)
        shift = LATENT_SEG_SHIFT if latent else 0
        x_new = _sgmlp_merge(ya, yb, main, shift, sg_vnorm_w[l][None], ws_bf, bias_exp, wb_bf, wo_bf, l,
                             post_norm_w[l][None], mod, rows_per_mod, x)
        return x_new, (None if latent else (outs[2], outs[3], st))

    stacks = (jnp.zeros((nb, depth, seq, D_MODEL), F32), jnp.zeros((nb, depth, seq, D_MODEL), F32),
              jnp.zeros((nb, depth, 2, M_HEADS * M_HEADDIM, M_STATE), F32))
    for l in range(depth):
        xp, stacks = layer(xp, l, nb, seq, mods[l, 0:1], nb * seq, False, stacks)
        xs, _ = layer(xs, l, db, dseq, mods[l, 1:1 + db], dseq, True)

    k_all, v_all, st_all = stacks
    return (xp.reshape(nb, seq, D_MODEL), xs.reshape(db, dseq, D_MODEL),
            k_all.reshape(nb, depth, seq, DA_HEADS, 2 * DA_QK), v_all.reshape(nb, depth, seq, DA_HEADS, DA_VD),
            st_all.reshape(nb, depth, 2, M_HEADS, M_HEADDIM, M_STATE))
```

```python
import functools
import math

import jax
import jax.numpy as jnp
from jax import lax
from jax.experimental import pallas as pl
from jax.experimental.pallas import tpu as pltpu

F32 = jnp.float32
BF16 = jnp.bfloat16

D_MODEL = 1024
EPS = 1e-6
GRID_W = 64
ROPE_BASE = 10000.0
M_HEADS = 16
M_HEADDIM = 64
M_STATE = 64
M_GROUPS = 2
M_CHUNK = 128
M_PAIRS = M_HEADS // 2
DA_HEADS = 8
DA_QK = 64
DA_VD = 128
SG_GROUPS = 8
SG_CHUNK = 128

SEG_Z, SEG_XS, SEG_Q, SEG_K, SEG_V, SEG_GB, SEG_U, SEG_SV, SEG_GC, SEG_MGA, SEG_MGB, SEG_MGC = range(12)
N_SEG = 12
N_WBLK = N_SEG + 2
ROPE_GROUP = 32
LATENT_SEG_SHIFT = 4
MISC_W = 384

VMEM_LIMIT = 56 * 1024 * 1024
NEG_BIG = -1e30
ATT_SUBQ = 128
ATT_TQ = 1024
ATT_KB = 512
SSD_CHUNKS_PER_STEP = 4
INPROJ_TM = 1024
INPROJ_TN = 2048
INPROJ_SUBN = 1024


_NEG_LOG2E = -math.log2(math.e)


def _silu(x):
    return x * _sigmoid(x)


def _sigmoid(x):
    return 1.0 / (1.0 + jnp.exp2(x * _NEG_LOG2E))


def _gelu_tanh(x):
    c = math.sqrt(2.0 / math.pi)
    hx = 0.5 * x
    return hx + hx * jnp.tanh(x * (c + (c * 0.044715) * (x * x)))


def _cparams(sem):
    return pltpu.CompilerParams(dimension_semantics=sem, vmem_limit_bytes=VMEM_LIMIT)


def _mod_kernel(c_ref, w_ref, b_ref, o_ref):
    c = c_ref[...]
    s = _silu(c).astype(BF16)
    o_ref[...] = jnp.dot(s, w_ref[...].astype(BF16), preferred_element_type=F32) + b_ref[...]


def _modulation(cvec, w_mod, b_mod):
    depth = w_mod.shape[0]
    nt = 3
    return pl.pallas_call(
        _mod_kernel,
        out_shape=jax.ShapeDtypeStruct((depth, 8, 3 * D_MODEL), F32),
        grid=(depth, nt),
        in_specs=[
            pl.BlockSpec((8, D_MODEL), lambda l, j: (0, 0)),
            pl.BlockSpec((None, D_MODEL, D_MODEL), lambda l, j: (l, 0, j)),
            pl.BlockSpec((None, 1, D_MODEL), lambda l, j: (l, 0, j)),
        ],
        out_specs=pl.BlockSpec((None, 8, D_MODEL), lambda l, j: (l, 0, j)),
        compiler_params=_cparams(("arbitrary", "arbitrary")),
        name="modulation",
    )(cvec, w_mod, b_mod.reshape(depth, 1, 3 * D_MODEL))


W_IN_COLS = 12576
W_XS_END = 2048
W_MISC_END = 2336


def _wprep_kernel(wa_ref, wb_ref, wm_ref, main_ref, misc_ref):
    j = pl.program_id(1)

    @pl.when(j < W_XS_END // D_MODEL)
    def _():
        main_ref[...] = wa_ref[...].T.astype(BF16)

    @pl.when(jnp.logical_and(j >= W_XS_END // D_MODEL, j < N_SEG))
    def _():
        main_ref[...] = wb_ref[...].T.astype(BF16)

    @pl.when(j >= N_SEG)
    def _():
        w = wb_ref[...]
        g = ROPE_GROUP
        rows = []
        for h0 in range(0, D_MODEL, 4 * g):
            rows += [w[h0:h0 + g], w[h0 + 2 * g:h0 + 3 * g], w[h0 + g:h0 + 2 * g], w[h0 + 3 * g:h0 + 4 * g]]
        main_ref[...] = jnp.concatenate(rows, axis=0).T.astype(BF16)

    @pl.when(j == 0)
    def _():
        n = W_MISC_END - W_XS_END
        wm = jnp.concatenate([wm_ref[...], jnp.zeros((MISC_W - n, D_MODEL), F32)], axis=0)
        misc_ref[...] = wm.T.astype(BF16)


def _prep_in_weights(w_in_t):
    depth = w_in_t.shape[0]
    n_head = W_XS_END // D_MODEL
    w2d = w_in_t.reshape(depth * W_IN_COLS, D_MODEL)
    skew = W_MISC_END - W_XS_END
    src = lambda j: jnp.where(j < N_SEG, j, j - N_SEG + SEG_Q)
    return pl.pallas_call(
        _wprep_kernel,
        out_shape=[jax.ShapeDtypeStruct((depth, D_MODEL, N_WBLK * D_MODEL), BF16),
                   jax.ShapeDtypeStruct((depth, D_MODEL, MISC_W), BF16)],
        grid=(depth, N_WBLK),
        in_specs=[pl.BlockSpec((None, D_MODEL, D_MODEL), lambda l, j: (l, jnp.minimum(j, n_head - 1), 0)),
                  pl.BlockSpec((pl.Element(D_MODEL), pl.Element(D_MODEL)),
                               lambda l, j: (pl.multiple_of(l * W_IN_COLS + skew + src(j) * D_MODEL, 32), 0)),
                  pl.BlockSpec((pl.Element(skew), pl.Element(D_MODEL)),
                               lambda l, j: (pl.multiple_of(l * W_IN_COLS + W_XS_END, 32), 0))],
        out_specs=[pl.BlockSpec((None, D_MODEL, D_MODEL), lambda l, j: (l, 0, j)),
                   pl.BlockSpec((None, D_MODEL, MISC_W), lambda l, j: (l, 0, 0))],
        compiler_params=_cparams(("parallel", "arbitrary")),
        name="w_prep",
    )(w_in_t, w2d, w2d)


def _inproj_kernel(*refs, rope, emit_kv, has_prev):
    it = iter(refs)
    x_ref, mod_ref, prew_ref, w_ref, wm_ref = next(it), next(it), next(it), next(it), next(it)
    if rope:
        cos_ref, sin_ref = next(it), next(it)
    if has_prev:
        next(it), next(it)
    main_ref, misc_ref = next(it), next(it)
    if emit_kv:
        kf_ref, vf_ref = next(it), next(it)
    if rope:
        hm_ref = next(it)
    h_scr = next(it)

    j = pl.program_id(1)
    tm = x_ref.shape[0]
    tn = w_ref.shape[1]
    spt = tn // D_MODEL

    @pl.when(j == 0)
    def _():
        x = x_ref[...]
        ms = jnp.mean(x * x, axis=-1, keepdims=True)
        y = x * lax.rsqrt(ms + EPS) * prew_ref[...]
        h = y * (1.0 + mod_ref[1:2, :]) + mod_ref[0:1, :]
        hb = h.astype(BF16)
        h_scr[...] = hb
        misc_ref[...] = jnp.dot(hb, wm_ref[...], preferred_element_type=F32)

    h = h_scr[...]
    sub = INPROJ_SUBN
    for c0 in range(0, tn, sub):
        cs = slice(c0, c0 + sub)
        seg = j * spt + c0 // D_MODEL
        lc = c0 % D_MODEL
        ls = slice(lc, lc + sub)
        acc = jnp.dot(h, w_ref[:, cs], preferred_element_type=F32)

        if rope:
            is_rope = jnp.logical_or(seg == SEG_Q, seg == SEG_K)
            is_plain_hm = jnp.logical_or(seg == SEG_V, seg == SEG_GB)

            def store_heads(val):
                vb = val.astype(hm_ref.dtype)
                for hh in range(sub // 128):
                    hm_ref[c0 // D_MODEL, lc // 128 + hh] = vb[:, hh * 128:(hh + 1) * 128]

            @pl.when(is_rope)
            def _():
                cos = jnp.concatenate([cos_ref[...]] * (sub // 128), axis=1)
                sin = jnp.concatenate([sin_ref[...]] * (sub // 128), axis=1)
                rot = jnp.concatenate([pltpu.roll(acc[:, c:c + 128], 2 * ROPE_GROUP, 1)
                                       for c in range(0, sub, 128)], axis=1)
                store_heads(acc * cos + rot * sin)

            @pl.when(is_plain_hm)
            def _():
                store_heads(acc)

            @pl.when(jnp.logical_not(jnp.logical_or(is_rope, is_plain_hm)))
            def _():
                main_ref[:, cs] = acc.astype(main_ref.dtype)
        else:
            main_ref[:, cs] = acc.astype(main_ref.dtype)

        if emit_kv:
            nsq, sq_len = kf_ref.shape[0], kf_ref.shape[1]

            @pl.when(seg == SEG_K)
            def _():
                kf_ref[:, :, ls] = acc.reshape(nsq, sq_len, sub)

            @pl.when(seg == SEG_V)
            def _():
                vf_ref[:, :, ls] = acc.reshape(nsq, sq_len, sub)


def _inproj(x, mod, pre_w, w_main, w_misc, l, rows_per_mod, rope_tabs, kv_cache=None, tm=INPROJ_TM, tn=INPROJ_TN):
    R = x.shape[0]
    emit_kv = kv_cache is not None
    ni = R // tm
    spt = tn // D_MODEL
    rope = rope_tabs is not None
    const = dict(pipeline_mode=pl.Buffered(1))
    in_specs = [
        pl.BlockSpec((tm, D_MODEL), lambda i, j: (i, 0), **const),
        pl.BlockSpec((None, 3, D_MODEL), lambda i, j: ((i * tm) // rows_per_mod, 0, 0)),
        pl.BlockSpec((1, D_MODEL), lambda i, j: (0, 0)),
        pl.BlockSpec((None, D_MODEL, tn),
                     (lambda i, j: (l, 0, jnp.where(j == SEG_Q // spt, N_SEG // spt, j))) if rope
                     else (lambda i, j: (l, 0, j))),
        pl.BlockSpec((None, D_MODEL, MISC_W), lambda i, j: (l, 0, 0), **const),
    ]
    args = [x, mod, pre_w, w_main, w_misc]
    if rope:
        cos, sin = rope_tabs
        nt = cos.shape[0] // tm
        in_specs += [pl.BlockSpec((tm, 128), lambda i, j: (i % nt, 0)),
                     pl.BlockSpec((tm, 128), lambda i, j: (i % nt, 0))]
        args += [cos, sin]
    if rope:
        n_main = N_SEG - LATENT_SEG_SHIFT
        main_idx = lambda i, j: (i, jnp.where(j < SEG_Q // spt, j,
                                              jnp.where(j <= SEG_GB // spt, SEG_XS // spt,
                                                        j - LATENT_SEG_SHIFT // spt)))
    else:
        n_main = N_SEG
        main_idx = lambda i, j: (i, j)
    out_shape = [jax.ShapeDtypeStruct((R, n_main * D_MODEL), BF16), jax.ShapeDtypeStruct((R, MISC_W), F32)]
    out_specs = [pl.BlockSpec((tm, tn), main_idx), pl.BlockSpec((tm, MISC_W), lambda i, j: (i, 0))]
    aliases = {}
    has_prev = False
    if emit_kv:
        seq_len, n_layers, k_prev, v_prev = kv_cache
        out_shape += [jax.ShapeDtypeStruct((R // seq_len, n_layers, seq_len, D_MODEL), F32)] * 2
        out_specs += [pl.BlockSpec((tm // seq_len, None, seq_len, D_MODEL), lambda i, j: (i, l, 0, 0))] * 2
        if k_prev is not None:
            has_prev = True
            in_specs += [pl.BlockSpec(memory_space=pl.ANY)] * 2
            args += [k_prev, v_prev]
            aliases = {len(args) - 2: 2, len(args) - 1: 3}
    if rope:
        T = cos.shape[0]
        out_shape.append(jax.ShapeDtypeStruct((4, R // T, DA_HEADS, T, 128), BF16))
        out_specs.append(pl.BlockSpec(
            (spt, None, DA_HEADS, tm, 128),
            lambda i, j: (jnp.clip(j - SEG_Q // spt, 0, 4 // spt - 1), (i * tm) // T, 0, ((i * tm) % T) // tm, 0)))
    return pl.pallas_call(
        functools.partial(_inproj_kernel, rope=rope, emit_kv=emit_kv, has_prev=has_prev),
        out_shape=out_shape,
        grid=(ni, N_SEG // spt),
        in_specs=in_specs,
        out_specs=out_specs,
        input_output_aliases=aliases,
        scratch_shapes=[pltpu.VMEM((tm, D_MODEL), BF16)],
        compiler_params=_cparams(("parallel", "arbitrary")),
        name="inproj",
    )(*args)


def _split3(a):
    hi = a.astype(BF16)
    r1 = a - hi.astype(F32)
    mid = r1.astype(BF16)
    lo = (r1 - mid.astype(F32)).astype(BF16)
    return hi, mid, lo


def _ssd_kernel(*refs, nc, cps, has_init, has_prev):
    it = iter(refs)
    xs_ref, xsp_ref, xsn_ref = next(it), next(it), next(it)
    mi_ref, mip_ref, min_ref = next(it), next(it), next(it)
    z_ref = next(it)
    cwx_ref, cbx_ref, cwm_ref, cbm_ref = next(it), next(it), next(it), next(it)
    dtb_ref, alog_ref, dsum_ref, nw_ref = next(it), next(it), next(it), next(it)
    if has_init:
        init_ref = next(it)
    if has_prev:
        next(it)
    y_ref, sto_ref = next(it), next(it)
    xc_scr, bc_scr, yf_scr, st_scr = next(it), next(it), next(it), next(it)

    L = M_CHUNK
    LB = cps * L
    ns = nc // cps
    j = pl.program_id(1)
    fwd = j < ns
    c = jnp.where(fwd, j, 2 * ns - 1 - j)
    row0 = pl.multiple_of(c * LB, LB)
    LOG2E = math.log2(math.e)

    rid = lax.broadcasted_iota(jnp.int32, (L, L), 0)
    cid = lax.broadcasted_iota(jnp.int32, (L, L), 1)

    def conv_silu(x, prow, nrow, w_ref, b_ref):
        n = x.shape[1]
        r = lax.broadcasted_iota(jnp.int32, (LB, n), 0)
        xm = jnp.where(r == 0, prow, pltpu.roll(x, 1, 0))
        xp = jnp.where(r == LB - 1, nrow, pltpu.roll(x, LB - 1, 0))
        y = w_ref[0:1, :] * xm + w_ref[1:2, :] * x + w_ref[2:3, :] * xp + b_ref[...]
        return _silu(y)

    def load_state(d):
        if has_init:
            pad = jnp.zeros((2 * M_HEADDIM, 2 * M_HEADDIM - M_STATE), F32)
            for i in range(M_PAIRS):
                blk = init_ref[d, i * 2 * M_HEADDIM:(i + 1) * 2 * M_HEADDIM, :]
                st_scr[i] = jnp.concatenate([blk, pad], axis=1).T[0:M_STATE, :]
        else:
            st_scr[...] = jnp.zeros_like(st_scr)

    def store_state(d):
        pad = jnp.zeros((2 * M_HEADDIM - M_STATE, 2 * M_HEADDIM), F32)
        for i in range(M_PAIRS):
            t = jnp.concatenate([st_scr[i], pad], axis=0).T
            sto_ref[d, i * 2 * M_HEADDIM:(i + 1) * 2 * M_HEADDIM, :] = t[:, 0:M_STATE]

    lane_lo = cid < M_HEADDIM
    lane_lo_s = lax.broadcasted_iota(jnp.int32, (M_STATE, 2 * M_HEADDIM), 1) < M_HEADDIM
    heads_per_group = M_HEADS // M_GROUPS
    zeros_s = jnp.zeros((M_STATE, 2 * M_HEADDIM), BF16)

    def setup(d, bc, dt_raw):
        tri = (cid <= rid) if d == 0 else (cid >= rid)
        tri_bf = jnp.where(tri, 1.0, 0.0).astype(BF16)
        dt = dt_raw + dtb_ref[...]
        dt = jnp.maximum(dt, 0.0) + jnp.log1p(jnp.exp(-jnp.abs(dt)))
        a = dt * (-jnp.exp(alog_ref[...]))
        a_hi, a_mid, a_lo = _split3(a)
        p_col = (jnp.dot(tri_bf, a_hi, preferred_element_type=F32)
                 + jnp.dot(tri_bf, a_mid, preferred_element_type=F32)
                 + jnp.dot(tri_bf, a_lo, preferred_element_type=F32))
        p_row = p_col.T[d * M_HEADS:(d + 1) * M_HEADS, :]
        dt_row = dt.T[d * M_HEADS:(d + 1) * M_HEADS, :]
        tot = p_row[:, L - 1:L] if d == 0 else p_row[:, 0:1]
        b_all = bc[:, 0:M_GROUPS * M_STATE]
        c_all = bc[:, M_GROUPS * M_STATE:2 * M_GROUPS * M_STATE]
        b_bf = b_all.astype(BF16)
        g_mats = []
        for g in range(M_GROUPS):
            cg = jnp.where(lane_lo if g == 0 else jnp.logical_not(lane_lo), c_all, 0.0).astype(BF16)
            g_mats.append(lax.dot_general(cg, b_bf, (((1,), (1,)), ((), ())),
                                          preferred_element_type=F32).astype(BF16))
        return dict(
            tri=tri, g_mats=g_mats, c_bf=c_all.astype(BF16), bt_all=b_all.T,
            p2_col=p_col * LOG2E,
            q2_row=(p_row - jnp.log(dt_row)) * LOG2E,
            w_row=dt_row * jnp.exp(tot - p_row),
            etot=jnp.exp(tot),
            ep_col=jnp.exp(p_col))

    def pairs(d, s, x_bf):
        outs = []
        for i in range(M_PAIRS):
            g = (2 * i) // heads_per_group
            btg = s["bt_all"][g * M_STATE:(g + 1) * M_STATE, :]
            lhs_rows = []
            bw_rows = []
            for hh in range(2):
                h = 2 * i + hh
                col = d * M_HEADS + h
                pc = jnp.broadcast_to(s["p2_col"][:, col:col + 1], (L, L))
                dm = jnp.exp2(jnp.where(s["tri"], pc - s["q2_row"][h:h + 1, :], NEG_BIG))
                m_h = s["g_mats"][g] * dm.astype(BF16)
                ce_h = s["c_bf"] * jnp.broadcast_to(s["ep_col"][:, col:col + 1], (L, L)).astype(BF16)
                lhs_rows.append(jnp.concatenate([m_h, ce_h], axis=1))
                bw = (btg * s["w_row"][h:h + 1, :]).astype(BF16)
                bw_rows.append(jnp.concatenate([bw, zeros_s], axis=1))
            lhs = jnp.concatenate(lhs_rows + bw_rows, axis=0)
            x_pair = x_bf[:, i * 128:(i + 1) * 128]
            st_pair = st_scr[i]
            st_bf = st_pair.astype(BF16)
            rhs = jnp.concatenate([x_pair] + ([st_bf, zeros_s] if g == 0 else [zeros_s, st_bf]), axis=0)
            res = jnp.dot(lhs, rhs, preferred_element_type=F32)
            y_pair = jnp.where(lane_lo, res[0:L], res[L:2 * L])
            ds = jnp.where(lane_lo_s, res[2 * L:2 * L + M_STATE], res[2 * L + M_STATE:2 * L + 2 * M_STATE])
            e0 = jnp.broadcast_to(s["etot"][2 * i:2 * i + 1, :], (M_STATE, 2 * M_HEADDIM))
            e1 = jnp.broadcast_to(s["etot"][2 * i + 1:2 * i + 2, :], (M_STATE, 2 * M_HEADDIM))
            st_scr[i] = jnp.where(lane_lo_s, e0, e1) * st_pair + ds
            outs.append(y_pair)
        return jnp.concatenate(outs, axis=1)

    def block(d, x_bf, bc, dt_raw):
        order = range(cps) if d == 0 else range(cps - 1, -1, -1)
        su = {ci: setup(d, bc[ci * L:(ci + 1) * L], dt_raw[ci * L:(ci + 1) * L]) for ci in order}
        ys = {ci: pairs(d, su[ci], x_bf[ci * L:(ci + 1) * L]) for ci in order}
        return jnp.concatenate([ys[ci] for ci in range(cps)], axis=0)

    @pl.when(j == 0)
    def _():
        load_state(0)

    @pl.when(j == ns)
    def _():
        load_state(1)

    @pl.when(fwd)
    def _():
        x = xs_ref[...].astype(F32)
        prow = jnp.where(c > 0, xsp_ref[...].astype(F32)[15:16, :], 0.0)
        nrow = jnp.where(c < ns - 1, xsn_ref[...].astype(F32)[0:1, :], 0.0)
        xc = conv_silu(x, prow, nrow, cwx_ref, cbx_ref).astype(BF16)
        xc_scr[pl.ds(row0, LB), :] = xc
        m = mi_ref[...]
        bcx = m[:, 0:256]
        prow_m = jnp.where(c > 0, mip_ref[7:8, 0:256], 0.0)
        nrow_m = jnp.where(c < ns - 1, min_ref[0:1, 0:256], 0.0)
        bc = conv_silu(bcx, prow_m, nrow_m, cwm_ref, cbm_ref)
        bc_scr[pl.ds(row0, LB), :] = bc
        yf_scr[pl.ds(row0, LB), :] = block(0, xc, bc, m[:, 256:384])

    @pl.when(j == ns - 1)
    def _():
        store_state(0)

    @pl.when(jnp.logical_not(fwd))
    def _():
        xc = xc_scr[pl.ds(row0, LB), :]
        bc = bc_scr[pl.ds(row0, LB), :]
        yb = block(1, xc, bc, mi_ref[:, 256:384])
        y = yf_scr[pl.ds(row0, LB), :] + yb + dsum_ref[...] * xc.astype(F32)
        y = y * _silu(z_ref[...].astype(F32))
        ms = jnp.mean(y * y, axis=-1, keepdims=True)
        y_ref[...] = (y * lax.rsqrt(ms + EPS) * nw_ref[...]).astype(y_ref.dtype)

    @pl.when(j == 2 * ns - 1)
    def _():
        store_state(1)


def _ssd(main, misc, nseq, T, cwx, cbx, cwm, cbm, dtb, alog, dsum, nw, init, st_stack=None, layer=0, n_layers=1,
         cps=SSD_CHUNKS_PER_STEP):
    R = nseq * T
    nc = T // M_CHUNK
    L = cps * M_CHUNK
    ns = nc // cps
    nc, full_nc = ns, nc
    has_init = init is not None

    def cidx(j):
        return jnp.where(j < nc, j, 2 * nc - 1 - j)

    def oidx(j):
        return jnp.where(j < nc, nc - 1, 2 * nc - 1 - j)

    in_specs = [
        pl.BlockSpec((L, D_MODEL), lambda b, j: (b * nc + cidx(j), SEG_XS)),
        pl.BlockSpec((16, D_MODEL), lambda b, j: (jnp.maximum((b * nc + cidx(j)) * (L // 16) - 1, 0), SEG_XS)),
        pl.BlockSpec((16, D_MODEL),
                     lambda b, j: (jnp.minimum((b * nc + cidx(j) + 1) * (L // 16), R // 16 - 1), SEG_XS)),
        pl.BlockSpec((L, MISC_W), lambda b, j: (b * nc + cidx(j), 0)),
        pl.BlockSpec((8, MISC_W), lambda b, j: (jnp.maximum((b * nc + cidx(j)) * (L // 8) - 1, 0), 0)),
        pl.BlockSpec((8, MISC_W), lambda b, j: (jnp.minimum((b * nc + cidx(j) + 1) * (L // 8), R // 8 - 1), 0)),
        pl.BlockSpec((L, D_MODEL), lambda b, j: (b * nc + oidx(j), SEG_Z)),
        pl.BlockSpec((3, D_MODEL), lambda b, j: (0, 0)),
        pl.BlockSpec((1, D_MODEL), lambda b, j: (0, 0)),
        pl.BlockSpec((3, 256), lambda b, j: (0, 0)),
        pl.BlockSpec((1, 256), lambda b, j: (0, 0)),
        pl.BlockSpec((1, 128), lambda b, j: (0, 0)),
        pl.BlockSpec((1, 128), lambda b, j: (0, 0)),
        pl.BlockSpec((1, D_MODEL), lambda b, j: (0, 0)),
        pl.BlockSpec((1, D_MODEL), lambda b, j: (0, 0)),
    ]
    args = [main, main, main, misc, misc, misc, main, cwx, cbx, cwm, cbm, dtb, alog, dsum, nw]
    if has_init:
        init_arr, init_layer = init
        in_specs.append(pl.BlockSpec((None, None, 2, M_HEADS * M_HEADDIM, M_STATE),
                                     lambda b, j: (b, init_layer, 0, 0, 0)))
        args.append(init_arr)
    aliases = {}
    if st_stack is not None:
        in_specs.append(pl.BlockSpec(memory_space=pl.ANY))
        args.append(st_stack)
        aliases = {len(args) - 1: 1}
    return pl.pallas_call(
        functools.partial(_ssd_kernel, nc=full_nc, cps=cps, has_init=has_init, has_prev=st_stack is not None),
        out_shape=[jax.ShapeDtypeStruct((R, D_MODEL), BF16),
                   jax.ShapeDtypeStruct((nseq, n_layers, 2, M_HEADS * M_HEADDIM, M_STATE), F32)],
        grid=(nseq, 2 * nc),
        in_specs=in_specs,
        out_specs=[pl.BlockSpec((L, D_MODEL), lambda b, j: (b * nc + oidx(j), 0)),
                   pl.BlockSpec((None, None, 2, M_HEADS * M_HEADDIM, M_STATE), lambda b, j: (b, layer, 0, 0, 0))],
        scratch_shapes=[pltpu.VMEM((T, D_MODEL), BF16), pltpu.VMEM((T, 256), F32),
                        pltpu.VMEM((T, D_MODEL), F32), pltpu.VMEM((M_PAIRS, M_STATE, 128), F32)],
        input_output_aliases=aliases,
        compiler_params=_cparams(("parallel", "arbitrary")),
        name="ssd",
    )(*args)


def _attn_lat_kernel(q_ref, k_ref, v_ref, gb_ref, kc_ref, vc_ref, lv_ref, hw_ref, o_ref, vt_scr, vct_scr,
                     *, lam_init, kb):
    tq = q_ref.shape[0]
    T = k_ref.shape[0]
    lc = kc_ref.shape[0]
    qi = pl.program_id(2)

    @pl.when(qi == 0)
    def _():
        for c0 in range(0, T, 128):
            vt_scr[0:DA_VD, c0:c0 + 128] = v_ref[c0:c0 + 128, :].astype(F32).T.astype(BF16)
        vt_scr[DA_VD:DA_VD + 16, :] = jnp.ones((16, T), BF16)
        for c0 in range(0, lc, 128):
            vct_scr[0:DA_VD, c0:c0 + 128] = vc_ref[c0:c0 + 128, :].T.astype(BF16)
        vct_scr[DA_VD:DA_VD + 16, :] = jnp.ones((16, lc), BF16)

    lv = lv_ref[...]
    lam = (jnp.exp(jnp.sum(lv[0:1, :] * lv[1:2, :], axis=-1, keepdims=True))
           - jnp.exp(jnp.sum(lv[2:3, :] * lv[3:4, :], axis=-1, keepdims=True)) + lam_init)
    sq = ATT_SUBQ
    map1 = (lax.broadcasted_iota(jnp.int32, (sq, 2 * DA_QK), 1) & ROPE_GROUP) == 0
    dn_t = (((1,), (1,)), ((), ()))

    q2s = []
    for c in range(tq // sq):
        q = q_ref[c * sq:(c + 1) * sq, :].astype(F32) * (DA_QK ** -0.5 * math.log2(math.e))
        q2s.append(jnp.concatenate([jnp.where(map1, q, 0.0), jnp.where(map1, 0.0, q)], axis=0).astype(BF16))

    def scores(k_blk, q2):
        s = lax.dot_general(k_blk, q2, dn_t, preferred_element_type=F32)
        return s, jnp.max(s, axis=0, keepdims=True)

    def accumulate(s, bm, vt_blk, state):
        m_new = bm if state is None else jnp.maximum(state[0], bm)
        p = jnp.exp2(s - m_new).astype(BF16)
        pv = jnp.dot(vt_blk, p, preferred_element_type=F32)
        if state is None:
            return m_new, pv
        return m_new, jnp.exp2(state[0] - m_new) * state[1] + pv

    kcb = kc_ref[...].astype(BF16)
    cur = [scores(kcb, q2) for q2 in q2s]
    cur_vt = vct_scr[...]
    states = [None] * len(q2s)
    for k0 in range(0, T, kb):
        k_blk = k_ref[k0:k0 + kb, :]
        nxt = [scores(k_blk, q2) for q2 in q2s]
        states = [accumulate(cu[0], cu[1], cur_vt, st) for cu, st in zip(cur, states)]
        cur, cur_vt = nxt, vt_scr[:, k0:k0 + kb]
    states = [accumulate(cu[0], cu[1], cur_vt, st) for cu, st in zip(cur, states)]
    for c, (_, acc) in enumerate(states):
        rs = slice(c * sq, (c + 1) * sq)
        r = 1.0 / acc[DA_VD:DA_VD + 1, :]
        o_t = acc[0:DA_VD, 0:sq] * r[:, 0:sq] - acc[0:DA_VD, sq:2 * sq] * (r[:, sq:2 * sq] * lam)
        o = o_t.T
        ms = jnp.mean(o * o, axis=-1, keepdims=True)
        o = o * lax.rsqrt(ms + EPS) * hw_ref[...] * (1.0 - lam_init)
        o_ref[rs, :] = (o * _silu(gb_ref[rs, :].astype(F32))).astype(o_ref.dtype)


def _attn_ctx_kernel(q_ref, k_ref, v_ref, gb_ref, lv_ref, hw_ref, o_ref, *, lam_init):
    T = q_ref.shape[0]
    lv = lv_ref[...]
    lam = (jnp.exp(jnp.sum(lv[0:1, :] * lv[1:2, :], axis=-1, keepdims=True))
           - jnp.exp(jnp.sum(lv[2:3, :] * lv[3:4, :], axis=-1, keepdims=True)) + lam_init)
    lane = lax.broadcasted_iota(jnp.int32, (T, 2 * DA_QK), 1)
    dn_t = (((1,), (1,)), ((), ()))
    ones = jnp.ones((16, T), BF16)

    def scores(hh):
        cs = slice(hh * 128, (hh + 1) * 128)
        q = q_ref[:, cs].astype(F32) * (DA_QK ** -0.5 * math.log2(math.e))
        q2 = jnp.concatenate([jnp.where(lane < DA_QK, q, 0.0), jnp.where(lane < DA_QK, 0.0, q)],
                             axis=0).astype(BF16)
        v = v_ref[:, cs].astype(F32)
        vt = jnp.concatenate([v[c0:c0 + 128, :].T for c0 in range(0, T, 128)], axis=1).astype(BF16)
        vt1 = jnp.concatenate([vt, ones], axis=0)
        s = lax.dot_general(k_ref[:, cs], q2, dn_t, preferred_element_type=F32)
        return s, vt1

    nxt = scores(0)
    for hh in range(DA_HEADS):
        cs = slice(hh * 128, (hh + 1) * 128)
        s, vt1 = nxt
        if hh + 1 < DA_HEADS:
            nxt = scores(hh + 1)
        p = jnp.exp2(s - jnp.max(s, axis=0, keepdims=True)).astype(BF16)
        acc = jnp.dot(vt1, p, preferred_element_type=F32)
        r = 1.0 / acc[DA_VD:DA_VD + 1, :]
        o_t = acc[0:DA_VD, 0:T] * r[:, 0:T] - acc[0:DA_VD, T:2 * T] * (r[:, T:2 * T] * lam)
        o = jnp.concatenate([o_t[:, c0:c0 + 128].T for c0 in range(0, T, 128)], axis=0)
        ms = jnp.mean(o * o, axis=-1, keepdims=True)
        o = o * lax.rsqrt(ms + EPS) * hw_ref[...] * (1.0 - lam_init)
        o_ref[:, cs] = (o * _silu(gb_ref[:, cs].astype(F32))).astype(o_ref.dtype)


def _attention_ctx(main, nseq, T, lam_vecs, head_w, lam_init):
    return pl.pallas_call(
        functools.partial(_attn_ctx_kernel, lam_init=lam_init),
        out_shape=jax.ShapeDtypeStruct((nseq * T, D_MODEL), BF16),
        grid=(nseq,),
        in_specs=[
            pl.BlockSpec((T, D_MODEL), lambda b: (b, SEG_Q)),
            pl.BlockSpec((T, D_MODEL), lambda b: (b, SEG_K)),
            pl.BlockSpec((T, D_MODEL), lambda b: (b, SEG_V)),
            pl.BlockSpec((T, D_MODEL), lambda b: (b, SEG_GB)),
            pl.BlockSpec((4, DA_QK), lambda b: (0, 0)),
            pl.BlockSpec((1, DA_VD), lambda b: (0, 0)),
        ],
        out_specs=pl.BlockSpec((T, D_MODEL), lambda b: (b, 0)),
        compiler_params=_cparams(("parallel",)),
        name="diff_attn_ctx",
    )(main, main, main, main, lam_vecs, head_w)


def _attention_latent(hm, nseq, T, tq, kb, lam_vecs, head_w, lam_init, ck, cv, l):
    R = nseq * T
    nq = T // tq
    lc = ck.shape[2]
    return pl.pallas_call(
        functools.partial(_attn_lat_kernel, lam_init=lam_init, kb=kb),
        out_shape=jax.ShapeDtypeStruct((R, D_MODEL), BF16),
        grid=(nseq, DA_HEADS, nq),
        in_specs=[
            pl.BlockSpec((None, None, None, tq, 128), lambda b, g, qi: (0, b, g, qi, 0)),
            pl.BlockSpec((None, None, None, T, 128), lambda b, g, qi: (1, b, g, 0, 0)),
            pl.BlockSpec((None, None, None, T, 128), lambda b, g, qi: (2, b, g, 0, 0)),
            pl.BlockSpec((None, None, None, tq, 128), lambda b, g, qi: (3, b, g, qi, 0)),
            pl.BlockSpec((None, None, lc, 128), lambda b, g, qi: (b, l, 0, g)),
            pl.BlockSpec((None, None, lc, 128), lambda b, g, qi: (b, l, 0, g)),
            pl.BlockSpec((4, DA_QK), lambda b, g, qi: (0, 0)),
            pl.BlockSpec((1, DA_VD), lambda b, g, qi: (0, 0)),
        ],
        out_specs=pl.BlockSpec((tq, 128), lambda b, g, qi: (b * nq + qi, g)),
        scratch_shapes=[pltpu.VMEM((DA_VD + 16, T), BF16), pltpu.VMEM((DA_VD + 16, lc), BF16)],
        compiler_params=_cparams(("parallel", "parallel", "arbitrary")),
        name="diff_attn_lat",
    )(hm, hm, hm, hm, ck, cv, lam_vecs, head_w)


def _sgmlp_merge_kernel(ya_ref, yb_ref, u_ref, sv_ref, gc_ref, ga_ref, gb_ref, gm_ref, vw_ref, ws_ref, bias_ref,
                        wb_ref, wo_ref, pw_ref, mod_ref, x_ref, o_ref):
    tm = u_ref.shape[0]

    def gated(y, g_ref, i):
        p = jnp.dot(y, wb_ref[i], preferred_element_type=F32)
        return _sigmoid(g_ref[...].astype(F32)) * p

    u = _gelu_tanh(u_ref[...].astype(F32))
    v = _gelu_tanh(sv_ref[...].astype(F32))
    vc = v - jnp.mean(v, axis=-1, keepdims=True)
    vb = (vc * lax.rsqrt(jnp.mean(vc * vc, axis=-1, keepdims=True) + EPS) * vw_ref[...]).astype(BF16)
    ug = u * _silu(gc_ref[...].astype(F32))
    rows = []
    for ci in range(tm // SG_CHUNK):
        rs = slice(ci * SG_CHUNK, (ci + 1) * SG_CHUNK)
        cols = [jnp.dot(ws_ref[g], vb[rs, g * 128:(g + 1) * 128], preferred_element_type=F32)
                for g in range(SG_GROUPS)]
        rows.append(((jnp.concatenate(cols, axis=1) + bias_ref[...]) * ug[rs]).astype(BF16))
    yc = jnp.concatenate(rows, axis=0)

    merged = gated(ya_ref[...], ga_ref, 0) + gated(yb_ref[...], gb_ref, 1) + gated(yc, gm_ref, 2)
    o = jnp.dot(merged.astype(BF16), wo_ref[...], preferred_element_type=F32)
    ms = jnp.mean(o * o, axis=-1, keepdims=True)
    o = o * lax.rsqrt(ms + EPS) * pw_ref[...]
    o_ref[...] = x_ref[...] + mod_ref[2:3, :] * o


def _sgmlp_merge(ya, yb, main, seg_shift, vnorm_w, ws_bf, bias_exp, wb_bf, wo_bf, l, post_w, mod, rows_per_mod, x,
                 tm=512):
    R = x.shape[0]
    row = lambda i: (i, 0)
    seg = lambda s: pl.BlockSpec((tm, D_MODEL), lambda i: (i, s - seg_shift))
    const = dict(pipeline_mode=pl.Buffered(1))
    return pl.pallas_call(
        _sgmlp_merge_kernel,
        out_shape=jax.ShapeDtypeStruct((R, D_MODEL), F32),
        grid=(R // tm,),
        in_specs=[
            pl.BlockSpec((tm, D_MODEL), row),
            pl.BlockSpec((tm, D_MODEL), row),
            seg(SEG_U), seg(SEG_SV), seg(SEG_GC), seg(SEG_MGA), seg(SEG_MGB), seg(SEG_MGC),
            pl.BlockSpec((1, D_MODEL), lambda i: (0, 0)),
            pl.BlockSpec((None, SG_GROUPS, SG_CHUNK, SG_CHUNK), lambda i: (l, 0, 0, 0), **const),
            pl.BlockSpec((None, SG_CHUNK, D_MODEL), lambda i: (l, 0, 0), **const),
            pl.BlockSpec((None, 3, D_MODEL, D_MODEL), lambda i: (l, 0, 0, 0), **const),
            pl.BlockSpec((None, D_MODEL, D_MODEL), lambda i: (l, 0, 0), **const),
            pl.BlockSpec((1, D_MODEL), lambda i: (0, 0)),
            pl.BlockSpec((None, 3, D_MODEL), lambda i: ((i * tm) // rows_per_mod, 0, 0)),
            pl.BlockSpec((tm, D_MODEL), row),
        ],
        out_specs=pl.BlockSpec((tm, D_MODEL), row),
        compiler_params=_cparams(("parallel",)),
        name="sgmlp_merge",
    )(ya, yb, main, main, main, main, main, main, vnorm_w, ws_bf, bias_exp, wb_bf, wo_bf, post_w, mod, x)


def _rope_tables(n_tokens):
    n_rows = n_tokens // GRID_W
    rows = jnp.repeat(jnp.arange(n_rows, dtype=F32), GRID_W)
    cols = jnp.tile(jnp.arange(GRID_W, dtype=F32), n_rows)
    n_freq = DA_QK // 4
    inv = ROPE_BASE ** (-jnp.arange(n_freq, dtype=F32) / n_freq)
    ang = jnp.concatenate([rows[:, None] * inv, cols[:, None] * inv], -1)
    cos, sin = jnp.cos(ang), jnp.sin(ang)
    return jnp.tile(cos, (1, 4)), jnp.concatenate([-sin, -sin, sin, sin], axis=1)


def kernel(x_prompt, x_sample, cache_k, cache_v, state_ssm, c, c_ctx, pre_norm_w, post_norm_w, w_mod, b_mod, w_in,
           m_conv_w, m_conv_b, m_A_log, m_dt_bias, m_D, m_norm_w, da_lambda, da_head_norm_w, sg_vnorm_w,
           sg_spatial_w, sg_spatial_b, w_branch, w_out):
    depth = w_in.shape[0]
    nb, seq, _ = x_prompt.shape
    db, dseq, _ = x_sample.shape
    past = cache_k.shape[2]

    w_main, w_misc = _prep_in_weights(jnp.swapaxes(w_in, 1, 2))
    wb_bf = w_branch.astype(BF16)
    wo_bf = w_out.astype(BF16)
    ws_bf = sg_spatial_w.astype(BF16)
    bias_exp = jnp.repeat(jnp.swapaxes(sg_spatial_b, 1, 2), D_MODEL // SG_GROUPS, axis=2)
    cw = jnp.swapaxes(m_conv_w, 1, 2)
    dtb = jnp.pad(m_dt_bias.reshape(depth, 1, 2 * M_HEADS), ((0, 0), (0, 0), (0, 128 - 2 * M_HEADS)))
    alog = jnp.pad(m_A_log.reshape(depth, 1, 2 * M_HEADS), ((0, 0), (0, 0), (0, 128 - 2 * M_HEADS)))
    dsum = jnp.repeat(m_D[:, 0] + m_D[:, 1], M_HEADDIM, axis=1).reshape(depth, 1, D_MODEL)

    cvec = jnp.concatenate([c_ctx[None, :], c, jnp.zeros((8 - 1 - db, D_MODEL), F32)], axis=0)
    mods = _modulation(cvec, w_mod, b_mod).reshape(depth, 8, 3, D_MODEL)

    rope_tabs = _rope_tables(dseq)
    init_states = state_ssm.reshape(db, depth, 2, M_HEADS * M_HEADDIM, M_STATE)
    ck = cache_k.reshape(db, depth, past, DA_HEADS, 2, 2, ROPE_GROUP)
    ck = jnp.swapaxes(ck, 4, 5).reshape(db, depth, past, D_MODEL)
    cv = cache_v.reshape(db, depth, past, D_MODEL)

    xp = x_prompt.reshape(nb * seq, D_MODEL)
    xs = x_sample.reshape(db * dseq, D_MODEL)

    def layer(x, l, nseq, T, mod, rows_per_mod, latent, stacks=(None, None, None)):
        lam_init = 0.8 - 0.6 * math.exp(-0.3 * l)
        k_stack, v_stack, st_stack = stacks
        outs = _inproj(x, mod, pre_norm_w[l][None], w_main, w_misc, l, rows_per_mod,
                       rope_tabs if latent else None, None if latent else (T, depth, k_stack, v_stack))
        main, misc = outs[0], outs[1]
        ya, st = _ssd(main, misc, nseq, T, cw[l, :, 0:D_MODEL], m_conv_b[l][None, 0:D_MODEL],
                      cw[l, :, D_MODEL:], m_conv_b[l][None, D_MODEL:], dtb[l], alog[l], dsum[l],
                      m_norm_w[l][None], (init_states, l) if latent else None,
                      st_stack=None if latent else st_stack, layer=0 if latent else l,
                      n_layers=1 if latent else depth, cps=min(SSD_CHUNKS_PER_STEP, T // M_CHUNK))
        if latent:
            yb = _attention_latent(outs[2], nseq, T, ATT_TQ, ATT_KB, da_lambda[l], da_head_norm_w[l][None],
                                   lam_init, ck, cv, l)
        else:
            yb = _attention_ctx(main, nseq, T, da_lambda[l], da_head_norm_w[l][None], lam_init)
        shift = LATENT_SEG_SHIFT if latent else 0
        x_new = _sgmlp_merge(ya, yb, main, shift, sg_vnorm_w[l][None], ws_bf, bias_exp, wb_bf, wo_bf, l,
                             post_norm_w[l][None], mod, rows_per_mod, x)
        return x_new, (None if latent else (outs[2], outs[3], st))

    stacks = (jnp.zeros((nb, depth, seq, D_MODEL), F32), jnp.zeros((nb, depth, seq, D_MODEL), F32),
              jnp.zeros((nb, depth, 2, M_HEADS * M_HEADDIM, M_STATE), F32))
    for l in range(depth):
        xp, stacks = layer(xp, l, nb, seq, mods[l, 0:1], nb * seq, False, stacks)
        xs, _ = layer(xs, l, db, dseq, mods[l, 1:1 + db], dseq, True)

    k_all, v_all, st_all = stacks
    return (xp.reshape(nb, seq, D_MODEL), xs.reshape(db, dseq, D_MODEL),
            k_all.reshape(nb, depth, seq, DA_HEADS, 2 * DA_QK), v_all.reshape(nb, depth, seq, DA_HEADS, DA_VD),
            st_all.reshape(nb, depth, 2, M_HEADS, M_HEADDIM, M_STATE))
```

```python
import functools
import math

import jax
import jax.numpy as jnp
from jax import lax
from jax.experimental import pallas as pl
from jax.experimental.pallas import tpu as pltpu

F32 = jnp.float32
BF16 = jnp.bfloat16

D_MODEL = 1024
EPS = 1e-6
GRID_W = 64
ROPE_BASE = 10000.0
M_HEADS = 16
M_HEADDIM = 64
M_STATE = 64
M_GROUPS = 2
M_CHUNK = 128
M_PAIRS = M_HEADS // 2
DA_HEADS = 8
DA_QK = 64
DA_VD = 128
SG_GROUPS = 8
SG_CHUNK = 128

SEG_Z, SEG_XS, SEG_Q, SEG_K, SEG_V, SEG_GB, SEG_U, SEG_SV, SEG_GC, SEG_MGA, SEG_MGB, SEG_MGC = range(12)
N_SEG = 12
N_WBLK = N_SEG + 2
ROPE_GROUP = 32
LATENT_SEG_SHIFT = 4
MISC_W = 384

VMEM_LIMIT = 56 * 1024 * 1024
NEG_BIG = -1e30
ATT_SUBQ = 128
ATT_TQ = 1024
ATT_KB = 1024
SSD_CHUNKS_PER_STEP = 8
INPROJ_TM = 1024
INPROJ_TN = 2048
INPROJ_SUBN = 1024


_NEG_LOG2E = -math.log2(math.e)


def _silu(x):
    return x * _sigmoid(x)


def _sigmoid(x):
    return 1.0 / (1.0 + jnp.exp2(x * _NEG_LOG2E))


def _gelu_tanh(x):
    c = math.sqrt(2.0 / math.pi)
    hx = 0.5 * x
    return hx + hx * jnp.tanh(x * (c + (c * 0.044715) * (x * x)))


def _cparams(sem):
    return pltpu.CompilerParams(dimension_semantics=sem, vmem_limit_bytes=VMEM_LIMIT)


def _mod_kernel(c_ref, w_ref, b_ref, o_ref):
    c = c_ref[...]
    s = _silu(c).astype(BF16)
    o_ref[...] = jnp.dot(s, w_ref[...].astype(BF16), preferred_element_type=F32) + b_ref[...]


def _modulation(cvec, w_mod, b_mod):
    depth = w_mod.shape[0]
    nt = 3
    return pl.pallas_call(
        _mod_kernel,
        out_shape=jax.ShapeDtypeStruct((depth, 8, 3 * D_MODEL), F32),
        grid=(depth, nt),
        in_specs=[
            pl.BlockSpec((8, D_MODEL), lambda l, j: (0, 0)),
            pl.BlockSpec((None, D_MODEL, D_MODEL), lambda l, j: (l, 0, j)),
            pl.BlockSpec((None, 1, D_MODEL), lambda l, j: (l, 0, j)),
        ],
        out_specs=pl.BlockSpec((None, 8, D_MODEL), lambda l, j: (l, 0, j)),
        compiler_params=_cparams(("arbitrary", "arbitrary")),
        name="modulation",
    )(cvec, w_mod, b_mod.reshape(depth, 1, 3 * D_MODEL))


W_IN_COLS = 12576
W_XS_END = 2048
W_MISC_END = 2336


def _wprep_kernel(wa_ref, wb_ref, wm_ref, main_ref, misc_ref):
    j = pl.program_id(1)

    @pl.when(j < W_XS_END // D_MODEL)
    def _():
        main_ref[...] = wa_ref[...].T.astype(BF16)

    @pl.when(jnp.logical_and(j >= W_XS_END // D_MODEL, j < N_SEG))
    def _():
        main_ref[...] = wb_ref[...].T.astype(BF16)

    @pl.when(j >= N_SEG)
    def _():
        w = wb_ref[...]
        g = ROPE_GROUP
        rows = []
        for h0 in range(0, D_MODEL, 4 * g):
            rows += [w[h0:h0 + g], w[h0 + 2 * g:h0 + 3 * g], w[h0 + g:h0 + 2 * g], w[h0 + 3 * g:h0 + 4 * g]]
        main_ref[...] = jnp.concatenate(rows, axis=0).T.astype(BF16)

    @pl.when(j == 0)
    def _():
        n = W_MISC_END - W_XS_END
        wm = jnp.concatenate([wm_ref[...], jnp.zeros((MISC_W - n, D_MODEL), F32)], axis=0)
        misc_ref[...] = wm.T.astype(BF16)


def _prep_in_weights(w_in_t):
    depth = w_in_t.shape[0]
    n_head = W_XS_END // D_MODEL
    w2d = w_in_t.reshape(depth * W_IN_COLS, D_MODEL)
    skew = W_MISC_END - W_XS_END
    src = lambda j: jnp.where(j < N_SEG, j, j - N_SEG + SEG_Q)
    return pl.pallas_call(
        _wprep_kernel,
        out_shape=[jax.ShapeDtypeStruct((depth, D_MODEL, N_WBLK * D_MODEL), BF16),
                   jax.ShapeDtypeStruct((depth, D_MODEL, MISC_W), BF16)],
        grid=(depth, N_WBLK),
        in_specs=[pl.BlockSpec((None, D_MODEL, D_MODEL), lambda l, j: (l, jnp.minimum(j, n_head - 1), 0)),
                  pl.BlockSpec((pl.Element(D_MODEL), pl.Element(D_MODEL)),
                               lambda l, j: (pl.multiple_of(l * W_IN_COLS + skew + src(j) * D_MODEL, 32), 0)),
                  pl.BlockSpec((pl.Element(skew), pl.Element(D_MODEL)),
                               lambda l, j: (pl.multiple_of(l * W_IN_COLS + W_XS_END, 32), 0))],
        out_specs=[pl.BlockSpec((None, D_MODEL, D_MODEL), lambda l, j: (l, 0, j)),
                   pl.BlockSpec((None, D_MODEL, MISC_W), lambda l, j: (l, 0, 0))],
        compiler_params=_cparams(("parallel", "arbitrary")),
        name="w_prep",
    )(w_in_t, w2d, w2d)


def _inproj_kernel(*refs, rope, emit_kv, has_prev):
    it = iter(refs)
    x_ref, mod_ref, prew_ref, w_ref, wm_ref = next(it), next(it), next(it), next(it), next(it)
    if rope:
        cos_ref, sin_ref = next(it), next(it)
    if has_prev:
        next(it), next(it)
    main_ref, misc_ref = next(it), next(it)
    if emit_kv:
        kf_ref, vf_ref = next(it), next(it)
    if rope:
        hm_ref = next(it)
    h_scr = next(it)

    j = pl.program_id(1)
    tm = x_ref.shape[0]
    tn = w_ref.shape[1]
    spt = tn // D_MODEL

    @pl.when(j == 0)
    def _():
        x = x_ref[...]
        ms = jnp.mean(x * x, axis=-1, keepdims=True)
        y = x * lax.rsqrt(ms + EPS) * prew_ref[...]
        h = y * (1.0 + mod_ref[1:2, :]) + mod_ref[0:1, :]
        hb = h.astype(BF16)
        h_scr[...] = hb
        misc_ref[...] = jnp.dot(hb, wm_ref[...], preferred_element_type=F32)

    h = h_scr[...]
    sub = INPROJ_SUBN
    for c0 in range(0, tn, sub):
        cs = slice(c0, c0 + sub)
        seg = j * spt + c0 // D_MODEL
        lc = c0 % D_MODEL
        ls = slice(lc, lc + sub)
        acc = jnp.dot(h, w_ref[:, cs], preferred_element_type=F32)

        if rope:
            is_rope = jnp.logical_or(seg == SEG_Q, seg == SEG_K)
            is_plain_hm = jnp.logical_or(seg == SEG_V, seg == SEG_GB)

            def store_heads(val):
                vb = val.astype(hm_ref.dtype)
                for hh in range(sub // 128):
                    hm_ref[c0 // D_MODEL, lc // 128 + hh] = vb[:, hh * 128:(hh + 1) * 128]

            @pl.when(is_rope)
            def _():
                cos = jnp.concatenate([cos_ref[...]] * (sub // 128), axis=1)
                sin = jnp.concatenate([sin_ref[...]] * (sub // 128), axis=1)
                rot = jnp.concatenate([pltpu.roll(acc[:, c:c + 128], 2 * ROPE_GROUP, 1)
                                       for c in range(0, sub, 128)], axis=1)
                store_heads(acc * cos + rot * sin)

            @pl.when(is_plain_hm)
            def _():
                store_heads(acc)

            @pl.when(jnp.logical_not(jnp.logical_or(is_rope, is_plain_hm)))
            def _():
                main_ref[:, cs] = acc.astype(main_ref.dtype)
        else:
            main_ref[:, cs] = acc.astype(main_ref.dtype)

        if emit_kv:
            nsq, sq_len = kf_ref.shape[0], kf_ref.shape[1]

            @pl.when(seg == SEG_K)
            def _():
                kf_ref[:, :, ls] = acc.reshape(nsq, sq_len, sub)

            @pl.when(seg == SEG_V)
            def _():
                vf_ref[:, :, ls] = acc.reshape(nsq, sq_len, sub)


def _inproj(x, mod, pre_w, w_main, w_misc, l, rows_per_mod, rope_tabs, kv_cache=None, tm=INPROJ_TM, tn=INPROJ_TN):
    R = x.shape[0]
    emit_kv = kv_cache is not None
    ni = R // tm
    spt = tn // D_MODEL
    rope = rope_tabs is not None
    const = dict(pipeline_mode=pl.Buffered(1))
    in_specs = [
        pl.BlockSpec((tm, D_MODEL), lambda i, j: (i, 0), **const),
        pl.BlockSpec((None, 3, D_MODEL), lambda i, j: ((i * tm) // rows_per_mod, 0, 0)),
        pl.BlockSpec((1, D_MODEL), lambda i, j: (0, 0)),
        pl.BlockSpec((None, D_MODEL, tn),
                     (lambda i, j: (l, 0, jnp.where(j == SEG_Q // spt, N_SEG // spt, j))) if rope
                     else (lambda i, j: (l, 0, j))),
        pl.BlockSpec((None, D_MODEL, MISC_W), lambda i, j: (l, 0, 0), **const),
    ]
    args = [x, mod, pre_w, w_main, w_misc]
    if rope:
        cos, sin = rope_tabs
        nt = cos.shape[0] // tm
        in_specs += [pl.BlockSpec((tm, 128), lambda i, j: (i % nt, 0)),
                     pl.BlockSpec((tm, 128), lambda i, j: (i % nt, 0))]
        args += [cos, sin]
    if rope:
        n_main = N_SEG - LATENT_SEG_SHIFT
        main_idx = lambda i, j: (i, jnp.where(j < SEG_Q // spt, j,
                                              jnp.where(j <= SEG_GB // spt, SEG_XS // spt,
                                                        j - LATENT_SEG_SHIFT // spt)))
    else:
        n_main = N_SEG
        main_idx = lambda i, j: (i, j)
    out_shape = [jax.ShapeDtypeStruct((R, n_main * D_MODEL), BF16), jax.ShapeDtypeStruct((R, MISC_W), F32)]
    out_specs = [pl.BlockSpec((tm, tn), main_idx), pl.BlockSpec((tm, MISC_W), lambda i, j: (i, 0))]
    aliases = {}
    has_prev = False
    if emit_kv:
        seq_len, n_layers, k_prev, v_prev = kv_cache
        out_shape += [jax.ShapeDtypeStruct((R // seq_len, n_layers, seq_len, D_MODEL), F32)] * 2
        out_specs += [pl.BlockSpec((tm // seq_len, None, seq_len, D_MODEL), lambda i, j: (i, l, 0, 0))] * 2
        if k_prev is not None:
            has_prev = True
            in_specs += [pl.BlockSpec(memory_space=pl.ANY)] * 2
            args += [k_prev, v_prev]
            aliases = {len(args) - 2: 2, len(args) - 1: 3}
    if rope:
        T = cos.shape[0]
        out_shape.append(jax.ShapeDtypeStruct((4, R // T, DA_HEADS, T, 128), BF16))
        out_specs.append(pl.BlockSpec(
            (spt, None, DA_HEADS, tm, 128),
            lambda i, j: (jnp.clip(j - SEG_Q // spt, 0, 4 // spt - 1), (i * tm) // T, 0, ((i * tm) % T) // tm, 0)))
    return pl.pallas_call(
        functools.partial(_inproj_kernel, rope=rope, emit_kv=emit_kv, has_prev=has_prev),
        out_shape=out_shape,
        grid=(ni, N_SEG // spt),
        in_specs=in_specs,
        out_specs=out_specs,
        input_output_aliases=aliases,
        scratch_shapes=[pltpu.VMEM((tm, D_MODEL), BF16)],
        compiler_params=_cparams(("parallel", "arbitrary")),
        name="inproj",
    )(*args)


def _split3(a):
    hi = a.astype(BF16)
    r1 = a - hi.astype(F32)
    mid = r1.astype(BF16)
    lo = (r1 - mid.astype(F32)).astype(BF16)
    return hi, mid, lo


def _ssd_kernel(*refs, nc, cps, has_init, has_prev):
    it = iter(refs)
    xs_ref, xsp_ref, xsn_ref = next(it), next(it), next(it)
    mi_ref, mip_ref, min_ref = next(it), next(it), next(it)
    z_ref = next(it)
    cwx_ref, cbx_ref, cwm_ref, cbm_ref = next(it), next(it), next(it), next(it)
    dtb_ref, alog_ref, dsum_ref, nw_ref = next(it), next(it), next(it), next(it)
    if has_init:
        init_ref = next(it)
    if has_prev:
        next(it)
    y_ref, sto_ref = next(it), next(it)
    xc_scr, bc_scr, yf_scr, st_scr = next(it), next(it), next(it), next(it)

    L = M_CHUNK
    LB = cps * L
    ns = nc // cps
    j = pl.program_id(1)
    fwd = j < ns
    c = jnp.where(fwd, j, 2 * ns - 1 - j)
    row0 = pl.multiple_of(c * LB, LB)
    LOG2E = math.log2(math.e)

    rid = lax.broadcasted_iota(jnp.int32, (L, L), 0)
    cid = lax.broadcasted_iota(jnp.int32, (L, L), 1)

    def conv_silu(x, prow, nrow, w_ref, b_ref):
        n = x.shape[1]
        r = lax.broadcasted_iota(jnp.int32, (LB, n), 0)
        xm = jnp.where(r == 0, prow, pltpu.roll(x, 1, 0))
        xp = jnp.where(r == LB - 1, nrow, pltpu.roll(x, LB - 1, 0))
        y = w_ref[0:1, :] * xm + w_ref[1:2, :] * x + w_ref[2:3, :] * xp + b_ref[...]
        return _silu(y)

    def load_state(d):
        if has_init:
            pad = jnp.zeros((2 * M_HEADDIM, 2 * M_HEADDIM - M_STATE), F32)
            for i in range(M_PAIRS):
                blk = init_ref[d, i * 2 * M_HEADDIM:(i + 1) * 2 * M_HEADDIM, :]
                st_scr[i] = jnp.concatenate([blk, pad], axis=1).T[0:M_STATE, :]
        else:
            st_scr[...] = jnp.zeros_like(st_scr)

    def store_state(d):
        pad = jnp.zeros((2 * M_HEADDIM - M_STATE, 2 * M_HEADDIM), F32)
        for i in range(M_PAIRS):
            t = jnp.concatenate([st_scr[i], pad], axis=0).T
            sto_ref[d, i * 2 * M_HEADDIM:(i + 1) * 2 * M_HEADDIM, :] = t[:, 0:M_STATE]

    lane_lo = cid < M_HEADDIM
    lane_lo_s = lax.broadcasted_iota(jnp.int32, (M_STATE, 2 * M_HEADDIM), 1) < M_HEADDIM
    heads_per_group = M_HEADS // M_GROUPS
    zeros_s = jnp.zeros((M_STATE, 2 * M_HEADDIM), BF16)

    def setup(d, bc, dt_raw):
        tri = (cid <= rid) if d == 0 else (cid >= rid)
        tri_bf = jnp.where(tri, 1.0, 0.0).astype(BF16)
        dt = dt_raw + dtb_ref[...]
        dt = jnp.maximum(dt, 0.0) + jnp.log1p(jnp.exp(-jnp.abs(dt)))
        a = dt * (-jnp.exp(alog_ref[...]))
        a_hi, a_mid, a_lo = _split3(a)
        p_col = (jnp.dot(tri_bf, a_hi, preferred_element_type=F32)
                 + jnp.dot(tri_bf, a_mid, preferred_element_type=F32)
                 + jnp.dot(tri_bf, a_lo, preferred_element_type=F32))
        p_row = p_col.T[d * M_HEADS:(d + 1) * M_HEADS, :]
        dt_row = dt.T[d * M_HEADS:(d + 1) * M_HEADS, :]
        tot = p_row[:, L - 1:L] if d == 0 else p_row[:, 0:1]
        b_all = bc[:, 0:M_GROUPS * M_STATE]
        c_all = bc[:, M_GROUPS * M_STATE:2 * M_GROUPS * M_STATE]
        b_bf = b_all.astype(BF16)
        g_mats = []
        for g in range(M_GROUPS):
            cg = jnp.where(lane_lo if g == 0 else jnp.logical_not(lane_lo), c_all, 0.0).astype(BF16)
            g_mats.append(lax.dot_general(cg, b_bf, (((1,), (1,)), ((), ())),
                                          preferred_element_type=F32).astype(BF16))
        return dict(
            tri=tri, g_mats=g_mats, c_bf=c_all.astype(BF16), bt_all=b_all.T,
            p2_col=p_col * LOG2E,
            q2_row=(p_row - jnp.log(dt_row)) * LOG2E,
            w_row=dt_row * jnp.exp(tot - p_row),
            etot=jnp.exp(tot),
            ep_col=jnp.exp(p_col))

    def pairs(d, s, x_bf):
        outs = []
        for i in range(M_PAIRS):
            g = (2 * i) // heads_per_group
            btg = s["bt_all"][g * M_STATE:(g + 1) * M_STATE, :]
            lhs_rows = []
            bw_rows = []
            for hh in range(2):
                h = 2 * i + hh
                col = d * M_HEADS + h
                pc = jnp.broadcast_to(s["p2_col"][:, col:col + 1], (L, L))
                dm = jnp.exp2(jnp.where(s["tri"], pc - s["q2_row"][h:h + 1, :], NEG_BIG))
                m_h = s["g_mats"][g] * dm.astype(BF16)
                ce_h = s["c_bf"] * jnp.broadcast_to(s["ep_col"][:, col:col + 1], (L, L)).astype(BF16)
                lhs_rows.append(jnp.concatenate([m_h, ce_h], axis=1))
                bw = (btg * s["w_row"][h:h + 1, :]).astype(BF16)
                bw_rows.append(jnp.concatenate([bw, zeros_s], axis=1))
            lhs = jnp.concatenate(lhs_rows + bw_rows, axis=0)
            x_pair = x_bf[:, i * 128:(i + 1) * 128]
            st_pair = st_scr[i]
            st_bf = st_pair.astype(BF16)
            rhs = jnp.concatenate([x_pair] + ([st_bf, zeros_s] if g == 0 else [zeros_s, st_bf]), axis=0)
            res = jnp.dot(lhs, rhs, preferred_element_type=F32)
            y_pair = jnp.where(lane_lo, res[0:L], res[L:2 * L])
            ds = jnp.where(lane_lo_s, res[2 * L:2 * L + M_STATE], res[2 * L + M_STATE:2 * L + 2 * M_STATE])
            e0 = jnp.broadcast_to(s["etot"][2 * i:2 * i + 1, :], (M_STATE, 2 * M_HEADDIM))
            e1 = jnp.broadcast_to(s["etot"][2 * i + 1:2 * i + 2, :], (M_STATE, 2 * M_HEADDIM))
            st_scr[i] = jnp.where(lane_lo_s, e0, e1) * st_pair + ds
            outs.append(y_pair)
        return jnp.concatenate(outs, axis=1)

    def block(d, x_bf, bc, dt_raw):
        order = range(cps) if d == 0 else range(cps - 1, -1, -1)
        su = {ci: setup(d, bc[ci * L:(ci + 1) * L], dt_raw[ci * L:(ci + 1) * L]) for ci in order}
        ys = {ci: pairs(d, su[ci], x_bf[ci * L:(ci + 1) * L]) for ci in order}
        return jnp.concatenate([ys[ci] for ci in range(cps)], axis=0)

    @pl.when(j == 0)
    def _():
        load_state(0)

    @pl.when(j == ns)
    def _():
        load_state(1)

    @pl.when(fwd)
    def _():
        x = xs_ref[...].astype(F32)
        prow = jnp.where(c > 0, xsp_ref[...].astype(F32)[15:16, :], 0.0)
        nrow = jnp.where(c < ns - 1, xsn_ref[...].astype(F32)[0:1, :], 0.0)
        xc = conv_silu(x, prow, nrow, cwx_ref, cbx_ref).astype(BF16)
        xc_scr[pl.ds(row0, LB), :] = xc
        m = mi_ref[...]
        bcx = m[:, 0:256]
        prow_m = jnp.where(c > 0, mip_ref[7:8, 0:256], 0.0)
        nrow_m = jnp.where(c < ns - 1, min_ref[0:1, 0:256], 0.0)
        bc = conv_silu(bcx, prow_m, nrow_m, cwm_ref, cbm_ref)
        bc_scr[pl.ds(row0, LB), :] = bc
        yf_scr[pl.ds(row0, LB), :] = block(0, xc, bc, m[:, 256:384])

    @pl.when(j == ns - 1)
    def _():
        store_state(0)

    @pl.when(jnp.logical_not(fwd))
    def _():
        xc = xc_scr[pl.ds(row0, LB), :]
        bc = bc_scr[pl.ds(row0, LB), :]
        yb = block(1, xc, bc, mi_ref[:, 256:384])
        y = yf_scr[pl.ds(row0, LB), :] + yb + dsum_ref[...] * xc.astype(F32)
        y = y * _silu(z_ref[...].astype(F32))
        ms = jnp.mean(y * y, axis=-1, keepdims=True)
        y_ref[...] = (y * lax.rsqrt(ms + EPS) * nw_ref[...]).astype(y_ref.dtype)

    @pl.when(j == 2 * ns - 1)
    def _():
        store_state(1)


def _ssd(main, misc, nseq, T, cwx, cbx, cwm, cbm, dtb, alog, dsum, nw, init, st_stack=None, layer=0, n_layers=1,
         cps=SSD_CHUNKS_PER_STEP):
    R = nseq * T
    nc = T // M_CHUNK
    L = cps * M_CHUNK
    ns = nc // cps
    nc, full_nc = ns, nc
    has_init = init is not None

    def cidx(j):
        return jnp.where(j < nc, j, 2 * nc - 1 - j)

    def oidx(j):
        return jnp.where(j < nc, nc - 1, 2 * nc - 1 - j)

    in_specs = [
        pl.BlockSpec((L, D_MODEL), lambda b, j: (b * nc + cidx(j), SEG_XS)),
        pl.BlockSpec((16, D_MODEL), lambda b, j: (jnp.maximum((b * nc + cidx(j)) * (L // 16) - 1, 0), SEG_XS)),
        pl.BlockSpec((16, D_MODEL),
                     lambda b, j: (jnp.minimum((b * nc + cidx(j) + 1) * (L // 16), R // 16 - 1), SEG_XS)),
        pl.BlockSpec((L, MISC_W), lambda b, j: (b * nc + cidx(j), 0)),
        pl.BlockSpec((8, MISC_W), lambda b, j: (jnp.maximum((b * nc + cidx(j)) * (L // 8) - 1, 0), 0)),
        pl.BlockSpec((8, MISC_W), lambda b, j: (jnp.minimum((b * nc + cidx(j) + 1) * (L // 8), R // 8 - 1), 0)),
        pl.BlockSpec((L, D_MODEL), lambda b, j: (b * nc + oidx(j), SEG_Z)),
        pl.BlockSpec((3, D_MODEL), lambda b, j: (0, 0)),
        pl.BlockSpec((1, D_MODEL), lambda b, j: (0, 0)),
        pl.BlockSpec((3, 256), lambda b, j: (0, 0)),
        pl.BlockSpec((1, 256), lambda b, j: (0, 0)),
        pl.BlockSpec((1, 128), lambda b, j: (0, 0)),
        pl.BlockSpec((1, 128), lambda b, j: (0, 0)),
        pl.BlockSpec((1, D_MODEL), lambda b, j: (0, 0)),
        pl.BlockSpec((1, D_MODEL), lambda b, j: (0, 0)),
    ]
    args = [main, main, main, misc, misc, misc, main, cwx, cbx, cwm, cbm, dtb, alog, dsum, nw]
    if has_init:
        init_arr, init_layer = init
        in_specs.append(pl.BlockSpec((None, None, 2, M_HEADS * M_HEADDIM, M_STATE),
                                     lambda b, j: (b, init_layer, 0, 0, 0)))
        args.append(init_arr)
    aliases = {}
    if st_stack is not None:
        in_specs.append(pl.BlockSpec(memory_space=pl.ANY))
        args.append(st_stack)
        aliases = {len(args) - 1: 1}
    return pl.pallas_call(
        functools.partial(_ssd_kernel, nc=full_nc, cps=cps, has_init=has_init, has_prev=st_stack is not None),
        out_shape=[jax.ShapeDtypeStruct((R, D_MODEL), BF16),
                   jax.ShapeDtypeStruct((nseq, n_layers, 2, M_HEADS * M_HEADDIM, M_STATE), F32)],
        grid=(nseq, 2 * nc),
        in_specs=in_specs,
        out_specs=[pl.BlockSpec((L, D_MODEL), lambda b, j: (b * nc + oidx(j), 0)),
                   pl.BlockSpec((None, None, 2, M_HEADS * M_HEADDIM, M_STATE), lambda b, j: (b, layer, 0, 0, 0))],
        scratch_shapes=[pltpu.VMEM((T, D_MODEL), BF16), pltpu.VMEM((T, 256), F32),
                        pltpu.VMEM((T, D_MODEL), F32), pltpu.VMEM((M_PAIRS, M_STATE, 128), F32)],
        input_output_aliases=aliases,
        compiler_params=_cparams(("parallel", "arbitrary")),
        name="ssd",
    )(*args)


def _attn_lat_kernel(q_ref, k_ref, v_ref, gb_ref, kc_ref, vc_ref, lv_ref, hw_ref, o_ref, vt_scr, vct_scr,
                     *, lam_init, kb):
    tq = q_ref.shape[0]
    T = k_ref.shape[0]
    lc = kc_ref.shape[0]
    qi = pl.program_id(2)

    @pl.when(qi == 0)
    def _():
        for c0 in range(0, T, 128):
            vt_scr[0:DA_VD, c0:c0 + 128] = v_ref[c0:c0 + 128, :].astype(F32).T.astype(BF16)
        vt_scr[DA_VD:DA_VD + 16, :] = jnp.ones((16, T), BF16)
        for c0 in range(0, lc, 128):
            vct_scr[0:DA_VD, c0:c0 + 128] = vc_ref[c0:c0 + 128, :].T.astype(BF16)
        vct_scr[DA_VD:DA_VD + 16, :] = jnp.ones((16, lc), BF16)

    lv = lv_ref[...]
    lam = (jnp.exp(jnp.sum(lv[0:1, :] * lv[1:2, :], axis=-1, keepdims=True))
           - jnp.exp(jnp.sum(lv[2:3, :] * lv[3:4, :], axis=-1, keepdims=True)) + lam_init)
    sq = ATT_SUBQ
    map1 = (lax.broadcasted_iota(jnp.int32, (sq, 2 * DA_QK), 1) & ROPE_GROUP) == 0
    dn_t = (((1,), (1,)), ((), ()))

    q2s = []
    for c in range(tq // sq):
        q = q_ref[c * sq:(c + 1) * sq, :].astype(F32) * (DA_QK ** -0.5 * math.log2(math.e))
        q2s.append(jnp.concatenate([jnp.where(map1, q, 0.0), jnp.where(map1, 0.0, q)], axis=0).astype(BF16))

    def scores(k_blk, q2):
        s = lax.dot_general(k_blk, q2, dn_t, preferred_element_type=F32)
        return s, jnp.max(s, axis=0, keepdims=True)

    def accumulate(s, bm, vt_blk, state):
        m_new = bm if state is None else jnp.maximum(state[0], bm)
        p = jnp.exp2(s - m_new).astype(BF16)
        pv = jnp.dot(vt_blk, p, preferred_element_type=F32)
        if state is None:
            return m_new, pv
        return m_new, jnp.exp2(state[0] - m_new) * state[1] + pv

    kcb = kc_ref[...].astype(BF16)
    cur = [scores(kcb, q2) for q2 in q2s]
    cur_vt = vct_scr[...]
    states = [None] * len(q2s)
    for k0 in range(0, T, kb):
        k_blk = k_ref[k0:k0 + kb, :]
        nxt = [scores(k_blk, q2) for q2 in q2s]
        states = [accumulate(cu[0], cu[1], cur_vt, st) for cu, st in zip(cur, states)]
        cur, cur_vt = nxt, vt_scr[:, k0:k0 + kb]
    states = [accumulate(cu[0], cu[1], cur_vt, st) for cu, st in zip(cur, states)]
    for c, (_, acc) in enumerate(states):
        rs = slice(c * sq, (c + 1) * sq)
        r = 1.0 / acc[DA_VD:DA_VD + 1, :]
        o_t = acc[0:DA_VD, 0:sq] * r[:, 0:sq] - acc[0:DA_VD, sq:2 * sq] * (r[:, sq:2 * sq] * lam)
        o = o_t.T
        ms = jnp.mean(o * o, axis=-1, keepdims=True)
        o = o * lax.rsqrt(ms + EPS) * hw_ref[...] * (1.0 - lam_init)
        o_ref[rs, :] = (o * _silu(gb_ref[rs, :].astype(F32))).astype(o_ref.dtype)


def _attn_ctx_kernel(q_ref, k_ref, v_ref, gb_ref, lv_ref, hw_ref, o_ref, *, lam_init):
    T = q_ref.shape[0]
    lv = lv_ref[...]
    lam = (jnp.exp(jnp.sum(lv[0:1, :] * lv[1:2, :], axis=-1, keepdims=True))
           - jnp.exp(jnp.sum(lv[2:3, :] * lv[3:4, :], axis=-1, keepdims=True)) + lam_init)
    lane = lax.broadcasted_iota(jnp.int32, (T, 2 * DA_QK), 1)
    dn_t = (((1,), (1,)), ((), ()))
    ones = jnp.ones((16, T), BF16)

    def scores(hh):
        cs = slice(hh * 128, (hh + 1) * 128)
        q = q_ref[:, cs].astype(F32) * (DA_QK ** -0.5 * math.log2(math.e))
        q2 = jnp.concatenate([jnp.where(lane < DA_QK, q, 0.0), jnp.where(lane < DA_QK, 0.0, q)],
                             axis=0).astype(BF16)
        v = v_ref[:, cs].astype(F32)
        vt = jnp.concatenate([v[c0:c0 + 128, :].T for c0 in range(0, T, 128)], axis=1).astype(BF16)
        vt1 = jnp.concatenate([vt, ones], axis=0)
        s = lax.dot_general(k_ref[:, cs], q2, dn_t, preferred_element_type=F32)
        return s, vt1

    nxt = scores(0)
    for hh in range(DA_HEADS):
        cs = slice(hh * 128, (hh + 1) * 128)
        s, vt1 = nxt
        if hh + 1 < DA_HEADS:
            nxt = scores(hh + 1)
        p = jnp.exp2(s - jnp.max(s, axis=0, keepdims=True)).astype(BF16)
        acc = jnp.dot(vt1, p, preferred_element_type=F32)
        r = 1.0 / acc[DA_VD:DA_VD + 1, :]
        o_t = acc[0:DA_VD, 0:T] * r[:, 0:T] - acc[0:DA_VD, T:2 * T] * (r[:, T:2 * T] * lam)
        o = jnp.concatenate([o_t[:, c0:c0 + 128].T for c0 in range(0, T, 128)], axis=0)
        ms = jnp.mean(o * o, axis=-1, keepdims=True)
        o = o * lax.rsqrt(ms + EPS) * hw_ref[...] * (1.0 - lam_init)
        o_ref[:, cs] = (o * _silu(gb_ref[:, cs].astype(F32))).astype(o_ref.dtype)


def _attention_ctx(main, nseq, T, lam_vecs, head_w, lam_init):
    return pl.pallas_call(
        functools.partial(_attn_ctx_kernel, lam_init=lam_init),
        out_shape=jax.ShapeDtypeStruct((nseq * T, D_MODEL), BF16),
        grid=(nseq,),
        in_specs=[
            pl.BlockSpec((T, D_MODEL), lambda b: (b, SEG_Q)),
            pl.BlockSpec((T, D_MODEL), lambda b: (b, SEG_K)),
            pl.BlockSpec((T, D_MODEL), lambda b: (b, SEG_V)),
            pl.BlockSpec((T, D_MODEL), lambda b: (b, SEG_GB)),
            pl.BlockSpec((4, DA_QK), lambda b: (0, 0)),
            pl.BlockSpec((1, DA_VD), lambda b: (0, 0)),
        ],
        out_specs=pl.BlockSpec((T, D_MODEL), lambda b: (b, 0)),
        compiler_params=_cparams(("parallel",)),
        name="diff_attn_ctx",
    )(main, main, main, main, lam_vecs, head_w)


def _attention_latent(hm, nseq, T, tq, kb, lam_vecs, head_w, lam_init, ck, cv, l):
    R = nseq * T
    nq = T // tq
    lc = ck.shape[2]
    return pl.pallas_call(
        functools.partial(_attn_lat_kernel, lam_init=lam_init, kb=kb),
        out_shape=jax.ShapeDtypeStruct((R, D_MODEL), BF16),
        grid=(nseq, DA_HEADS, nq),
        in_specs=[
            pl.BlockSpec((None, None, None, tq, 128), lambda b, g, qi: (0, b, g, qi, 0)),
            pl.BlockSpec((None, None, None, T, 128), lambda b, g, qi: (1, b, g, 0, 0)),
            pl.BlockSpec((None, None, None, T, 128), lambda b, g, qi: (2, b, g, 0, 0)),
            pl.BlockSpec((None, None, None, tq, 128), lambda b, g, qi: (3, b, g, qi, 0)),
            pl.BlockSpec((None, None, lc, 128), lambda b, g, qi: (b, l, 0, g)),
            pl.BlockSpec((None, None, lc, 128), lambda b, g, qi: (b, l, 0, g)),
            pl.BlockSpec((4, DA_QK), lambda b, g, qi: (0, 0)),
            pl.BlockSpec((1, DA_VD), lambda b, g, qi: (0, 0)),
        ],
        out_specs=pl.BlockSpec((tq, 128), lambda b, g, qi: (b * nq + qi, g)),
        scratch_shapes=[pltpu.VMEM((DA_VD + 16, T), BF16), pltpu.VMEM((DA_VD + 16, lc), BF16)],
        compiler_params=_cparams(("parallel", "parallel", "arbitrary")),
        name="diff_attn_lat",
    )(hm, hm, hm, hm, ck, cv, lam_vecs, head_w)


def _sgmlp_merge_kernel(ya_ref, yb_ref, u_ref, sv_ref, gc_ref, ga_ref, gb_ref, gm_ref, vw_ref, ws_ref, bias_ref,
                        wb_ref, wo_ref, pw_ref, mod_ref, x_ref, o_ref):
    tm = u_ref.shape[0]

    def gated(y, g_ref, i):
        p = jnp.dot(y, wb_ref[i], preferred_element_type=F32)
        return _sigmoid(g_ref[...].astype(F32)) * p

    u = _gelu_tanh(u_ref[...].astype(F32))
    v = _gelu_tanh(sv_ref[...].astype(F32))
    vc = v - jnp.mean(v, axis=-1, keepdims=True)
    vb = (vc * lax.rsqrt(jnp.mean(vc * vc, axis=-1, keepdims=True) + EPS) * vw_ref[...]).astype(BF16)
    ug = u * _silu(gc_ref[...].astype(F32))
    rows = []
    for ci in range(tm // SG_CHUNK):
        rs = slice(ci * SG_CHUNK, (ci + 1) * SG_CHUNK)
        cols = [jnp.dot(ws_ref[g], vb[rs, g * 128:(g + 1) * 128], preferred_element_type=F32)
                for g in range(SG_GROUPS)]
        rows.append(((jnp.concatenate(cols, axis=1) + bias_ref[...]) * ug[rs]).astype(BF16))
    yc = jnp.concatenate(rows, axis=0)

    merged = gated(ya_ref[...], ga_ref, 0) + gated(yb_ref[...], gb_ref, 1) + gated(yc, gm_ref, 2)
    o = jnp.dot(merged.astype(BF16), wo_ref[...], preferred_element_type=F32)
    ms = jnp.mean(o * o, axis=-1, keepdims=True)
    o = o * lax.rsqrt(ms + EPS) * pw_ref[...]
    o_ref[...] = x_ref[...] + mod_ref[2:3, :] * o


def _sgmlp_merge(ya, yb, main, seg_shift, vnorm_w, ws_bf, bias_exp, wb_bf, wo_bf, l, post_w, mod, rows_per_mod, x,
                 tm=512):
    R = x.shape[0]
    row = lambda i: (i, 0)
    seg = lambda s: pl.BlockSpec((tm, D_MODEL), lambda i: (i, s - seg_shift))
    const = dict(pipeline_mode=pl.Buffered(1))
    return pl.pallas_call(
        _sgmlp_merge_kernel,
        out_shape=jax.ShapeDtypeStruct((R, D_MODEL), F32),
        grid=(R // tm,),
        in_specs=[
            pl.BlockSpec((tm, D_MODEL), row),
            pl.BlockSpec((tm, D_MODEL), row),
            seg(SEG_U), seg(SEG_SV), seg(SEG_GC), seg(SEG_MGA), seg(SEG_MGB), seg(SEG_MGC),
            pl.BlockSpec((1, D_MODEL), lambda i: (0, 0)),
            pl.BlockSpec((None, SG_GROUPS, SG_CHUNK, SG_CHUNK), lambda i: (l, 0, 0, 0), **const),
            pl.BlockSpec((None, SG_CHUNK, D_MODEL), lambda i: (l, 0, 0), **const),
            pl.BlockSpec((None, 3, D_MODEL, D_MODEL), lambda i: (l, 0, 0, 0), **const),
            pl.BlockSpec((None, D_MODEL, D_MODEL), lambda i: (l, 0, 0), **const),
            pl.BlockSpec((1, D_MODEL), lambda i: (0, 0)),
            pl.BlockSpec((None, 3, D_MODEL), lambda i: ((i * tm) // rows_per_mod, 0, 0)),
            pl.BlockSpec((tm, D_MODEL), row),
        ],
        out_specs=pl.BlockSpec((tm, D_MODEL), row),
        compiler_params=_cparams(("parallel",)),
        name="sgmlp_merge",
    )(ya, yb, main, main, main, main, main, main, vnorm_w, ws_bf, bias_exp, wb_bf, wo_bf, post_w, mod, x)


def _rope_tables(n_tokens):
    n_rows = n_tokens // GRID_W
    rows = jnp.repeat(jnp.arange(n_rows, dtype=F32), GRID_W)
    cols = jnp.tile(jnp.arange(GRID_W, dtype=F32), n_rows)
    n_freq = DA_QK // 4
    inv = ROPE_BASE ** (-jnp.arange(n_freq, dtype=F32) / n_freq)
    ang = jnp.concatenate([rows[:, None] * inv, cols[:, None] * inv], -1)
    cos, sin = jnp.cos(ang), jnp.sin(ang)
    return jnp.tile(cos, (1, 4)), jnp.concatenate([-sin, -sin, sin, sin], axis=1)


def kernel(x_prompt, x_sample, cache_k, cache_v, state_ssm, c, c_ctx, pre_norm_w, post_norm_w, w_mod, b_mod, w_in,
           m_conv_w, m_conv_b, m_A_log, m_dt_bias, m_D, m_norm_w, da_lambda, da_head_norm_w, sg_vnorm_w,
           sg_spatial_w, sg_spatial_b, w_branch, w_out):
    depth = w_in.shape[0]
    nb, seq, _ = x_prompt.shape
    db, dseq, _ = x_sample.shape
    past = cache_k.shape[2]

    w_main, w_misc = _prep_in_weights(jnp.swapaxes(w_in, 1, 2))
    wb_bf = w_branch.astype(BF16)
    wo_bf = w_out.astype(BF16)
    ws_bf = sg_spatial_w.astype(BF16)
    bias_exp = jnp.repeat(jnp.swapaxes(sg_spatial_b, 1, 2), D_MODEL // SG_GROUPS, axis=2)
    cw = jnp.swapaxes(m_conv_w, 1, 2)
    dtb = jnp.pad(m_dt_bias.reshape(depth, 1, 2 * M_HEADS), ((0, 0), (0, 0), (0, 128 - 2 * M_HEADS)))
    alog = jnp.pad(m_A_log.reshape(depth, 1, 2 * M_HEADS), ((0, 0), (0, 0), (0, 128 - 2 * M_HEADS)))
    dsum = jnp.repeat(m_D[:, 0] + m_D[:, 1], M_HEADDIM, axis=1).reshape(depth, 1, D_MODEL)

    cvec = jnp.concatenate([c_ctx[None, :], c, jnp.zeros((8 - 1 - db, D_MODEL), F32)], axis=0)
    mods = _modulation(cvec, w_mod, b_mod).reshape(depth, 8, 3, D_MODEL)

    rope_tabs = _rope_tables(dseq)
    init_states = state_ssm.reshape(db, depth, 2, M_HEADS * M_HEADDIM, M_STATE)
    ck = cache_k.reshape(db, depth, past, DA_HEADS, 2, 2, ROPE_GROUP)
    ck = jnp.swapaxes(ck, 4, 5).reshape(db, depth, past, D_MODEL)
    cv = cache_v.reshape(db, depth, past, D_MODEL)

    xp = x_prompt.reshape(nb * seq, D_MODEL)
    xs = x_sample.reshape(db * dseq, D_MODEL)

    def layer(x, l, nseq, T, mod, rows_per_mod, latent, stacks=(None, None, None)):
        lam_init = 0.8 - 0.6 * math.exp(-0.3 * l)
        k_stack, v_stack, st_stack = stacks
        outs = _inproj(x, mod, pre_norm_w[l][None], w_main, w_misc, l, rows_per_mod,
                       rope_tabs if latent else None, None if latent else (T, depth, k_stack, v_stack))
        main, misc = outs[0], outs[1]
        ya, st = _ssd(main, misc, nseq, T, cw[l, :, 0:D_MODEL], m_conv_b[l][None, 0:D_MODEL],
                      cw[l, :, D_MODEL:], m_conv_b[l][None, D_MODEL:], dtb[l], alog[l], dsum[l],
                      m_norm_w[l][None], (init_states, l) if latent else None,
                      st_stack=None if latent else st_stack, layer=0 if latent else l,
                      n_layers=1 if latent else depth, cps=min(SSD_CHUNKS_PER_STEP, T // M_CHUNK))
        if latent:
            yb = _attention_latent(outs[2], nseq, T, ATT_TQ, ATT_KB, da_lambda[l], da_head_norm_w[l][None],
                                   lam_init, ck, cv, l)
        else:
            yb = _attention_ctx(main, nseq, T, da_lambda[l], da_head_norm_w[l][None], lam_init)
        shift = LATENT_SEG_SHIFT if latent else 0
        x_new = _sgmlp_merge(ya, yb, main, shift, sg_vnorm_w[l][None], ws_bf, bias_exp, wb_bf, wo_bf, l,
                             post_norm_w[l][None], mod, rows_per_mod, x)
        return x_new, (None if latent else (outs[2], outs[3], st))

    stacks = (jnp.zeros((nb, depth, seq, D_MODEL), F32), jnp.zeros((nb, depth, seq, D_MODEL), F32),
              jnp.zeros((nb, depth, 2, M_HEADS * M_HEADDIM, M_STATE), F32))
    for l in range(depth):
        xp, stacks = layer(xp, l, nb, seq, mods[l, 0:1], nb * seq, False, stacks)
        xs, _ = layer(xs, l, db, dseq, mods[l, 1:1 + db], dseq, True)

    k_all, v_all, st_all = stacks
    return (xp.reshape(nb, seq, D_MODEL), xs.reshape(db, dseq, D_MODEL),
            k_all.reshape(nb, depth, seq, DA_HEADS, 2 * DA_QK), v_all.reshape(nb, depth, seq, DA_HEADS, DA_VD),
            st_all.reshape(nb, depth, 2, M_HEADS, M_HEADDIM, M_STATE))
```

```python
import functools
import math

import jax
import jax.numpy as jnp
from jax import lax
from jax.experimental import pallas as pl
from jax.experimental.pallas import tpu as pltpu

F32 = jnp.float32
BF16 = jnp.bfloat16

D_MODEL = 1024
EPS = 1e-6
GRID_W = 64
ROPE_BASE = 10000.0
M_HEADS = 16
M_HEADDIM = 64
M_STATE = 64
M_GROUPS = 2
M_CHUNK = 128
M_PAIRS = M_HEADS // 2
DA_HEADS = 8
DA_QK = 64
DA_VD = 128
SG_GROUPS = 8
SG_CHUNK = 128

SEG_Z, SEG_XS, SEG_Q, SEG_K, SEG_V, SEG_GB, SEG_U, SEG_SV, SEG_GC, SEG_MGA, SEG_MGB, SEG_MGC = range(12)
N_SEG = 12
N_WBLK = N_SEG + 2
ROPE_GROUP = 32
LATENT_SEG_SHIFT = 4
MISC_W = 384

VMEM_LIMIT = 56 * 1024 * 1024
NEG_BIG = -1e30
ATT_SUBQ = 128
ATT_TQ = 1024
ATT_KB = 512
SSD_CHUNKS_PER_STEP = 8
INPROJ_TM = 1024
INPROJ_TN = 2048
INPROJ_SUBN = 1024


_NEG_LOG2E = -math.log2(math.e)


def _silu(x):
    return x * _sigmoid(x)


def _sigmoid(x):
    return 1.0 / (1.0 + jnp.exp2(x * _NEG_LOG2E))


def _gelu_tanh(x):
    c = math.sqrt(2.0 / math.pi)
    hx = 0.5 * x
    return hx + hx * jnp.tanh(x * (c + (c * 0.044715) * (x * x)))


def _cparams(sem):
    return pltpu.CompilerParams(dimension_semantics=sem, vmem_limit_bytes=VMEM_LIMIT)


def _mod_kernel(c_ref, w_ref, b_ref, o_ref):
    c = c_ref[...]
    s = _silu(c).astype(BF16)
    o_ref[...] = jnp.dot(s, w_ref[...].astype(BF16), preferred_element_type=F32) + b_ref[...]


def _modulation(cvec, w_mod, b_mod):
    depth = w_mod.shape[0]
    nt = 3
    return pl.pallas_call(
        _mod_kernel,
        out_shape=jax.ShapeDtypeStruct((depth, 8, 3 * D_MODEL), F32),
        grid=(depth, nt),
        in_specs=[
            pl.BlockSpec((8, D_MODEL), lambda l, j: (0, 0)),
            pl.BlockSpec((None, D_MODEL, D_MODEL), lambda l, j: (l, 0, j)),
            pl.BlockSpec((None, 1, D_MODEL), lambda l, j: (l, 0, j)),
        ],
        out_specs=pl.BlockSpec((None, 8, D_MODEL), lambda l, j: (l, 0, j)),
        compiler_params=_cparams(("arbitrary", "arbitrary")),
        name="modulation",
    )(cvec, w_mod, b_mod.reshape(depth, 1, 3 * D_MODEL))


W_IN_COLS = 12576
W_XS_END = 2048
W_MISC_END = 2336


def _wprep_kernel(wa_ref, wb_ref, wm_ref, main_ref, misc_ref):
    j = pl.program_id(1)

    @pl.when(j < W_XS_END // D_MODEL)
    def _():
        main_ref[...] = wa_ref[...].T.astype(BF16)

    @pl.when(jnp.logical_and(j >= W_XS_END // D_MODEL, j < N_SEG))
    def _():
        main_ref[...] = wb_ref[...].T.astype(BF16)

    @pl.when(j >= N_SEG)
    def _():
        w = wb_ref[...]
        g = ROPE_GROUP
        rows = []
        for h0 in range(0, D_MODEL, 4 * g):
            rows += [w[h0:h0 + g], w[h0 + 2 * g:h0 + 3 * g], w[h0 + g:h0 + 2 * g], w[h0 + 3 * g:h0 + 4 * g]]
        main_ref[...] = jnp.concatenate(rows, axis=0).T.astype(BF16)

    @pl.when(j == 0)
    def _():
        n = W_MISC_END - W_XS_END
        wm = jnp.concatenate([wm_ref[...], jnp.zeros((MISC_W - n, D_MODEL), F32)], axis=0)
        misc_ref[...] = wm.T.astype(BF16)


def _prep_in_weights(w_in_t):
    depth = w_in_t.shape[0]
    n_head = W_XS_END // D_MODEL
    w2d = w_in_t.reshape(depth * W_IN_COLS, D_MODEL)
    skew = W_MISC_END - W_XS_END
    src = lambda j: jnp.where(j < N_SEG, j, j - N_SEG + SEG_Q)
    return pl.pallas_call(
        _wprep_kernel,
        out_shape=[jax.ShapeDtypeStruct((depth, D_MODEL, N_WBLK * D_MODEL), BF16),
                   jax.ShapeDtypeStruct((depth, D_MODEL, MISC_W), BF16)],
        grid=(depth, N_WBLK),
        in_specs=[pl.BlockSpec((None, D_MODEL, D_MODEL), lambda l, j: (l, jnp.minimum(j, n_head - 1), 0)),
                  pl.BlockSpec((pl.Element(D_MODEL), pl.Element(D_MODEL)),
                               lambda l, j: (pl.multiple_of(l * W_IN_COLS + skew + src(j) * D_MODEL, 32), 0)),
                  pl.BlockSpec((pl.Element(skew), pl.Element(D_MODEL)),
                               lambda l, j: (pl.multiple_of(l * W_IN_COLS + W_XS_END, 32), 0))],
        out_specs=[pl.BlockSpec((None, D_MODEL, D_MODEL), lambda l, j: (l, 0, j)),
                   pl.BlockSpec((None, D_MODEL, MISC_W), lambda l, j: (l, 0, 0))],
        compiler_params=_cparams(("parallel", "arbitrary")),
        name="w_prep",
    )(w_in_t, w2d, w2d)


def _inproj_kernel(*refs, rope, emit_kv, has_prev):
    it = iter(refs)
    x_ref, mod_ref, prew_ref, w_ref, wm_ref = next(it), next(it), next(it), next(it), next(it)
    if rope:
        cos_ref, sin_ref = next(it), next(it)
    if has_prev:
        next(it), next(it)
    main_ref, misc_ref = next(it), next(it)
    if emit_kv:
        kf_ref, vf_ref = next(it), next(it)
    if rope:
        hm_ref = next(it)
    h_scr = next(it)

    j = pl.program_id(1)
    tm = x_ref.shape[0]
    tn = w_ref.shape[1]
    spt = tn // D_MODEL

    @pl.when(j == 0)
    def _():
        x = x_ref[...]
        ms = jnp.mean(x * x, axis=-1, keepdims=True)
        y = x * lax.rsqrt(ms + EPS) * prew_ref[...]
        h = y * (1.0 + mod_ref[1:2, :]) + mod_ref[0:1, :]
        hb = h.astype(BF16)
        h_scr[...] = hb
        misc_ref[...] = jnp.dot(hb, wm_ref[...], preferred_element_type=F32)

    h = h_scr[...]
    sub = INPROJ_SUBN
    for c0 in range(0, tn, sub):
        cs = slice(c0, c0 + sub)
        seg = j * spt + c0 // D_MODEL
        lc = c0 % D_MODEL
        ls = slice(lc, lc + sub)
        acc = jnp.dot(h, w_ref[:, cs], preferred_element_type=F32)

        if rope:
            is_rope = jnp.logical_or(seg == SEG_Q, seg == SEG_K)
            is_plain_hm = jnp.logical_or(seg == SEG_V, seg == SEG_GB)

            def store_heads(val):
                vb = val.astype(hm_ref.dtype)
                for hh in range(sub // 128):
                    hm_ref[c0 // D_MODEL, lc // 128 + hh] = vb[:, hh * 128:(hh + 1) * 128]

            @pl.when(is_rope)
            def _():
                cos = jnp.concatenate([cos_ref[...]] * (sub // 128), axis=1)
                sin = jnp.concatenate([sin_ref[...]] * (sub // 128), axis=1)
                rot = jnp.concatenate([pltpu.roll(acc[:, c:c + 128], 2 * ROPE_GROUP, 1)
                                       for c in range(0, sub, 128)], axis=1)
                store_heads(acc * cos + rot * sin)

            @pl.when(is_plain_hm)
            def _():
                store_heads(acc)

            @pl.when(jnp.logical_not(jnp.logical_or(is_rope, is_plain_hm)))
            def _():
                main_ref[:, cs] = acc.astype(main_ref.dtype)
        else:
            main_ref[:, cs] = acc.astype(main_ref.dtype)

        if emit_kv:
            nsq, sq_len = kf_ref.shape[0], kf_ref.shape[1]

            @pl.when(seg == SEG_K)
            def _():
                kf_ref[:, :, ls] = acc.reshape(nsq, sq_len, sub)

            @pl.when(seg == SEG_V)
            def _():
                vf_ref[:, :, ls] = acc.reshape(nsq, sq_len, sub)


def _inproj(x, mod, pre_w, w_main, w_misc, l, rows_per_mod, rope_tabs, kv_cache=None, tm=INPROJ_TM, tn=INPROJ_TN):
    R = x.shape[0]
    emit_kv = kv_cache is not None
    ni = R // tm
    spt = tn // D_MODEL
    rope = rope_tabs is not None
    const = dict(pipeline_mode=pl.Buffered(1))
    in_specs = [
        pl.BlockSpec((tm, D_MODEL), lambda i, j: (i, 0), **const),
        pl.BlockSpec((None, 3, D_MODEL), lambda i, j: ((i * tm) // rows_per_mod, 0, 0)),
        pl.BlockSpec((1, D_MODEL), lambda i, j: (0, 0)),
        pl.BlockSpec((None, D_MODEL, tn),
                     (lambda i, j: (l, 0, jnp.where(j == SEG_Q // spt, N_SEG // spt, j))) if rope
                     else (lambda i, j: (l, 0, j))),
        pl.BlockSpec((None, D_MODEL, MISC_W), lambda i, j: (l, 0, 0), **const),
    ]
    args = [x, mod, pre_w, w_main, w_misc]
    if rope:
        cos, sin = rope_tabs
        nt = cos.shape[0] // tm
        in_specs += [pl.BlockSpec((tm, 128), lambda i, j: (i % nt, 0)),
                     pl.BlockSpec((tm, 128), lambda i, j: (i % nt, 0))]
        args += [cos, sin]
    if rope:
        n_main = N_SEG - LATENT_SEG_SHIFT
        main_idx = lambda i, j: (i, jnp.where(j < SEG_Q // spt, j,
                                              jnp.where(j <= SEG_GB // spt, SEG_XS // spt,
                                                        j - LATENT_SEG_SHIFT // spt)))
    else:
        n_main = N_SEG
        main_idx = lambda i, j: (i, j)
    out_shape = [jax.ShapeDtypeStruct((R, n_main * D_MODEL), BF16), jax.ShapeDtypeStruct((R, MISC_W), F32)]
    out_specs = [pl.BlockSpec((tm, tn), main_idx), pl.BlockSpec((tm, MISC_W), lambda i, j: (i, 0))]
    aliases = {}
    has_prev = False
    if emit_kv:
        seq_len, n_layers, k_prev, v_prev = kv_cache
        out_shape += [jax.ShapeDtypeStruct((R // seq_len, n_layers, seq_len, D_MODEL), F32)] * 2
        out_specs += [pl.BlockSpec((tm // seq_len, None, seq_len, D_MODEL), lambda i, j: (i, l, 0, 0))] * 2
        if k_prev is not None:
            has_prev = True
            in_specs += [pl.BlockSpec(memory_space=pl.ANY)] * 2
            args += [k_prev, v_prev]
            aliases = {len(args) - 2: 2, len(args) - 1: 3}
    if rope:
        T = cos.shape[0]
        out_shape.append(jax.ShapeDtypeStruct((4, R // T, DA_HEADS, T, 128), BF16))
        out_specs.append(pl.BlockSpec(
            (spt, None, DA_HEADS, tm, 128),
            lambda i, j: (jnp.clip(j - SEG_Q // spt, 0, 4 // spt - 1), (i * tm) // T, 0, ((i * tm) % T) // tm, 0)))
    return pl.pallas_call(
        functools.partial(_inproj_kernel, rope=rope, emit_kv=emit_kv, has_prev=has_prev),
        out_shape=out_shape,
        grid=(ni, N_SEG // spt),
        in_specs=in_specs,
        out_specs=out_specs,
        input_output_aliases=aliases,
        scratch_shapes=[pltpu.VMEM((tm, D_MODEL), BF16)],
        compiler_params=_cparams(("parallel", "arbitrary")),
        name="inproj",
    )(*args)


def _split3(a):
    hi = a.astype(BF16)
    r1 = a - hi.astype(F32)
    mid = r1.astype(BF16)
    lo = (r1 - mid.astype(F32)).astype(BF16)
    return hi, mid, lo


def _ssd_kernel(*refs, nc, cps, has_init, has_prev):
    it = iter(refs)
    xs_ref, xsp_ref, xsn_ref = next(it), next(it), next(it)
    mi_ref, mip_ref, min_ref = next(it), next(it), next(it)
    z_ref = next(it)
    cwx_ref, cbx_ref, cwm_ref, cbm_ref = next(it), next(it), next(it), next(it)
    dtb_ref, alog_ref, dsum_ref, nw_ref = next(it), next(it), next(it), next(it)
    if has_init:
        init_ref = next(it)
    if has_prev:
        next(it)
    y_ref, sto_ref = next(it), next(it)
    xc_scr, bc_scr, yf_scr, st_scr = next(it), next(it), next(it), next(it)

    L = M_CHUNK
    LB = cps * L
    ns = nc // cps
    j = pl.program_id(1)
    fwd = j < ns
    c = jnp.where(fwd, j, 2 * ns - 1 - j)
    row0 = pl.multiple_of(c * LB, LB)
    LOG2E = math.log2(math.e)

    rid = lax.broadcasted_iota(jnp.int32, (L, L), 0)
    cid = lax.broadcasted_iota(jnp.int32, (L, L), 1)

    def conv_silu(x, prow, nrow, w_ref, b_ref):
        n = x.shape[1]
        r = lax.broadcasted_iota(jnp.int32, (LB, n), 0)
        xm = jnp.where(r == 0, prow, pltpu.roll(x, 1, 0))
        xp = jnp.where(r == LB - 1, nrow, pltpu.roll(x, LB - 1, 0))
        y = w_ref[0:1, :] * xm + w_ref[1:2, :] * x + w_ref[2:3, :] * xp + b_ref[...]
        return _silu(y)

    def load_state(d):
        if has_init:
            pad = jnp.zeros((2 * M_HEADDIM, 2 * M_HEADDIM - M_STATE), F32)
            for i in range(M_PAIRS):
                blk = init_ref[d, i * 2 * M_HEADDIM:(i + 1) * 2 * M_HEADDIM, :]
                st_scr[i] = jnp.concatenate([blk, pad], axis=1).T[0:M_STATE, :]
        else:
            st_scr[...] = jnp.zeros_like(st_scr)

    def store_state(d):
        pad = jnp.zeros((2 * M_HEADDIM - M_STATE, 2 * M_HEADDIM), F32)
        for i in range(M_PAIRS):
            t = jnp.concatenate([st_scr[i], pad], axis=0).T
            sto_ref[d, i * 2 * M_HEADDIM:(i + 1) * 2 * M_HEADDIM, :] = t[:, 0:M_STATE]

    lane_lo = cid < M_HEADDIM
    lane_lo_s = lax.broadcasted_iota(jnp.int32, (M_STATE, 2 * M_HEADDIM), 1) < M_HEADDIM
    heads_per_group = M_HEADS // M_GROUPS
    zeros_s = jnp.zeros((M_STATE, 2 * M_HEADDIM), BF16)

    def setup(d, bc, dt_raw):
        tri = (cid <= rid) if d == 0 else (cid >= rid)
        tri_bf = jnp.where(tri, 1.0, 0.0).astype(BF16)
        dt = dt_raw + dtb_ref[...]
        dt = jnp.maximum(dt, 0.0) + jnp.log1p(jnp.exp(-jnp.abs(dt)))
        a = dt * (-jnp.exp(alog_ref[...]))
        a_hi, a_mid, a_lo = _split3(a)
        p_col = (jnp.dot(tri_bf, a_hi, preferred_element_type=F32)
                 + jnp.dot(tri_bf, a_mid, preferred_element_type=F32)
                 + jnp.dot(tri_bf, a_lo, preferred_element_type=F32))
        p_row = p_col.T[d * M_HEADS:(d + 1) * M_HEADS, :]
        dt_row = dt.T[d * M_HEADS:(d + 1) * M_HEADS, :]
        tot = p_row[:, L - 1:L] if d == 0 else p_row[:, 0:1]
        b_all = bc[:, 0:M_GROUPS * M_STATE]
        c_all = bc[:, M_GROUPS * M_STATE:2 * M_GROUPS * M_STATE]
        b_bf = b_all.astype(BF16)
        g_mats = []
        for g in range(M_GROUPS):
            cg = jnp.where(lane_lo if g == 0 else jnp.logical_not(lane_lo), c_all, 0.0).astype(BF16)
            g_mats.append(lax.dot_general(cg, b_bf, (((1,), (1,)), ((), ())),
                                          preferred_element_type=F32).astype(BF16))
        return dict(
            tri=tri, g_mats=g_mats, c_bf=c_all.astype(BF16), bt_all=b_all.T,
            p2_col=p_col * LOG2E,
            q2_row=(p_row - jnp.log(dt_row)) * LOG2E,
            w_row=dt_row * jnp.exp(tot - p_row),
            etot=jnp.exp(tot),
            ep_col=jnp.exp(p_col))

    def pairs(d, s, x_bf):
        outs = []
        for i in range(M_PAIRS):
            g = (2 * i) // heads_per_group
            btg = s["bt_all"][g * M_STATE:(g + 1) * M_STATE, :]
            lhs_rows = []
            bw_rows = []
            for hh in range(2):
                h = 2 * i + hh
                col = d * M_HEADS + h
                pc = jnp.broadcast_to(s["p2_col"][:, col:col + 1], (L, L))
                dm = jnp.exp2(jnp.where(s["tri"], pc - s["q2_row"][h:h + 1, :], NEG_BIG))
                m_h = s["g_mats"][g] * dm.astype(BF16)
                ce_h = s["c_bf"] * jnp.broadcast_to(s["ep_col"][:, col:col + 1], (L, L)).astype(BF16)
                lhs_rows.append(jnp.concatenate([m_h, ce_h], axis=1))
                bw = (btg * s["w_row"][h:h + 1, :]).astype(BF16)
                bw_rows.append(jnp.concatenate([bw, zeros_s], axis=1))
            lhs = jnp.concatenate(lhs_rows + bw_rows, axis=0)
            x_pair = x_bf[:, i * 128:(i + 1) * 128]
            st_pair = st_scr[i]
            st_bf = st_pair.astype(BF16)
            rhs = jnp.concatenate([x_pair] + ([st_bf, zeros_s] if g == 0 else [zeros_s, st_bf]), axis=0)
            res = jnp.dot(lhs, rhs, preferred_element_type=F32)
            y_pair = jnp.where(lane_lo, res[0:L], res[L:2 * L])
            ds = jnp.where(lane_lo_s, res[2 * L:2 * L + M_STATE], res[2 * L + M_STATE:2 * L + 2 * M_STATE])
            e0 = jnp.broadcast_to(s["etot"][2 * i:2 * i + 1, :], (M_STATE, 2 * M_HEADDIM))
            e1 = jnp.broadcast_to(s["etot"][2 * i + 1:2 * i + 2, :], (M_STATE, 2 * M_HEADDIM))
            st_scr[i] = jnp.where(lane_lo_s, e0, e1) * st_pair + ds
            outs.append(y_pair)
        return jnp.concatenate(outs, axis=1)

    def block(d, x_bf, bc, dt_raw):
        order = range(cps) if d == 0 else range(cps - 1, -1, -1)
        su = {ci: setup(d, bc[ci * L:(ci + 1) * L], dt_raw[ci * L:(ci + 1) * L]) for ci in order}
        ys = {ci: pairs(d, su[ci], x_bf[ci * L:(ci + 1) * L]) for ci in order}
        return jnp.concatenate([ys[ci] for ci in range(cps)], axis=0)

    @pl.when(j == 0)
    def _():
        load_state(0)

    @pl.when(j == ns)
    def _():
        load_state(1)

    @pl.when(fwd)
    def _():
        x = xs_ref[...].astype(F32)
        prow = jnp.where(c > 0, xsp_ref[...].astype(F32)[15:16, :], 0.0)
        nrow = jnp.where(c < ns - 1, xsn_ref[...].astype(F32)[0:1, :], 0.0)
        xc = conv_silu(x, prow, nrow, cwx_ref, cbx_ref).astype(BF16)
        xc_scr[pl.ds(row0, LB), :] = xc
        m = mi_ref[...]
        bcx = m[:, 0:256]
        prow_m = jnp.where(c > 0, mip_ref[7:8, 0:256], 0.0)
        nrow_m = jnp.where(c < ns - 1, min_ref[0:1, 0:256], 0.0)
        bc = conv_silu(bcx, prow_m, nrow_m, cwm_ref, cbm_ref)
        bc_scr[pl.ds(row0, LB), :] = bc
        yf_scr[pl.ds(row0, LB), :] = block(0, xc, bc, m[:, 256:384])

    @pl.when(j == ns - 1)
    def _():
        store_state(0)

    @pl.when(jnp.logical_not(fwd))
    def _():
        xc = xc_scr[pl.ds(row0, LB), :]
        bc = bc_scr[pl.ds(row0, LB), :]
        yb = block(1, xc, bc, mi_ref[:, 256:384])
        y = yf_scr[pl.ds(row0, LB), :] + yb + dsum_ref[...] * xc.astype(F32)
        y = y * _silu(z_ref[...].astype(F32))
        ms = jnp.mean(y * y, axis=-1, keepdims=True)
        y_ref[...] = (y * lax.rsqrt(ms + EPS) * nw_ref[...]).astype(y_ref.dtype)

    @pl.when(j == 2 * ns - 1)
    def _():
        store_state(1)


def _ssd(main, misc, nseq, T, cwx, cbx, cwm, cbm, dtb, alog, dsum, nw, init, st_stack=None, layer=0, n_layers=1,
         cps=SSD_CHUNKS_PER_STEP):
    R = nseq * T
    nc = T // M_CHUNK
    L = cps * M_CHUNK
    ns = nc // cps
    nc, full_nc = ns, nc
    has_init = init is not None

    def cidx(j):
        return jnp.where(j < nc, j, 2 * nc - 1 - j)

    def oidx(j):
        return jnp.where(j < nc, nc - 1, 2 * nc - 1 - j)

    in_specs = [
        pl.BlockSpec((L, D_MODEL), lambda b, j: (b * nc + cidx(j), SEG_XS)),
        pl.BlockSpec((16, D_MODEL), lambda b, j: (jnp.maximum((b * nc + cidx(j)) * (L // 16) - 1, 0), SEG_XS)),
        pl.BlockSpec((16, D_MODEL),
                     lambda b, j: (jnp.minimum((b * nc + cidx(j) + 1) * (L // 16), R // 16 - 1), SEG_XS)),
        pl.BlockSpec((L, MISC_W), lambda b, j: (b * nc + cidx(j), 0)),
        pl.BlockSpec((8, MISC_W), lambda b, j: (jnp.maximum((b * nc + cidx(j)) * (L // 8) - 1, 0), 0)),
        pl.BlockSpec((8, MISC_W), lambda b, j: (jnp.minimum((b * nc + cidx(j) + 1) * (L // 8), R // 8 - 1), 0)),
        pl.BlockSpec((L, D_MODEL), lambda b, j: (b * nc + oidx(j), SEG_Z)),
        pl.BlockSpec((3, D_MODEL), lambda b, j: (0, 0)),
        pl.BlockSpec((1, D_MODEL), lambda b, j: (0, 0)),
        pl.BlockSpec((3, 256), lambda b, j: (0, 0)),
        pl.BlockSpec((1, 256), lambda b, j: (0, 0)),
        pl.BlockSpec((1, 128), lambda b, j: (0, 0)),
        pl.BlockSpec((1, 128), lambda b, j: (0, 0)),
        pl.BlockSpec((1, D_MODEL), lambda b, j: (0, 0)),
        pl.BlockSpec((1, D_MODEL), lambda b, j: (0, 0)),
    ]
    args = [main, main, main, misc, misc, misc, main, cwx, cbx, cwm, cbm, dtb, alog, dsum, nw]
    if has_init:
        init_arr, init_layer = init
        in_specs.append(pl.BlockSpec((None, None, 2, M_HEADS * M_HEADDIM, M_STATE),
                                     lambda b, j: (b, init_layer, 0, 0, 0)))
        args.append(init_arr)
    aliases = {}
    if st_stack is not None:
        in_specs.append(pl.BlockSpec(memory_space=pl.ANY))
        args.append(st_stack)
        aliases = {len(args) - 1: 1}
    return pl.pallas_call(
        functools.partial(_ssd_kernel, nc=full_nc, cps=cps, has_init=has_init, has_prev=st_stack is not None),
        out_shape=[jax.ShapeDtypeStruct((R, D_MODEL), BF16),
                   jax.ShapeDtypeStruct((nseq, n_layers, 2, M_HEADS * M_HEADDIM, M_STATE), F32)],
        grid=(nseq, 2 * nc),
        in_specs=in_specs,
        out_specs=[pl.BlockSpec((L, D_MODEL), lambda b, j: (b * nc + oidx(j), 0)),
                   pl.BlockSpec((None, None, 2, M_HEADS * M_HEADDIM, M_STATE), lambda b, j: (b, layer, 0, 0, 0))],
        scratch_shapes=[pltpu.VMEM((T, D_MODEL), BF16), pltpu.VMEM((T, 256), F32),
                        pltpu.VMEM((T, D_MODEL), F32), pltpu.VMEM((M_PAIRS, M_STATE, 128), F32)],
        input_output_aliases=aliases,
        compiler_params=_cparams(("parallel", "arbitrary")),
        name="ssd",
    )(*args)


def _attn_lat_kernel(q_ref, k_ref, v_ref, gb_ref, kc_ref, vc_ref, lv_ref, hw_ref, o_ref, vt_scr, vct_scr,
                     *, lam_init, kb):
    tq = q_ref.shape[0]
    T = k_ref.shape[0]
    lc = kc_ref.shape[0]
    qi = pl.program_id(2)

    @pl.when(qi == 0)
    def _():
        for c0 in range(0, T, 128):
            vt_scr[0:DA_VD, c0:c0 + 128] = v_ref[c0:c0 + 128, :].astype(F32).T.astype(BF16)
        vt_scr[DA_VD:DA_VD + 16, :] = jnp.ones((16, T), BF16)
        for c0 in range(0, lc, 128):
            vct_scr[0:DA_VD, c0:c0 + 128] = vc_ref[c0:c0 + 128, :].T.astype(BF16)
        vct_scr[DA_VD:DA_VD + 16, :] = jnp.ones((16, lc), BF16)

    lv = lv_ref[...]
    lam = (jnp.exp(jnp.sum(lv[0:1, :] * lv[1:2, :], axis=-1, keepdims=True))
           - jnp.exp(jnp.sum(lv[2:3, :] * lv[3:4, :], axis=-1, keepdims=True)) + lam_init)
    sq = ATT_SUBQ
    map1 = (lax.broadcasted_iota(jnp.int32, (sq, 2 * DA_QK), 1) & ROPE_GROUP) == 0
    dn_t = (((1,), (1,)), ((), ()))

    q2s = []
    for c in range(tq // sq):
        q = q_ref[c * sq:(c + 1) * sq, :].astype(F32) * (DA_QK ** -0.5 * math.log2(math.e))
        q2s.append(jnp.concatenate([jnp.where(map1, q, 0.0), jnp.where(map1, 0.0, q)], axis=0).astype(BF16))

    def scores(k_blk, q2):
        s = lax.dot_general(k_blk, q2, dn_t, preferred_element_type=F32)
        return s, jnp.max(s, axis=0, keepdims=True)

    def accumulate(s, bm, vt_blk, state):
        m_new = bm if state is None else jnp.maximum(state[0], bm)
        p = jnp.exp2(s - m_new).astype(BF16)
        pv = jnp.dot(vt_blk, p, preferred_element_type=F32)
        if state is None:
            return m_new, pv
        return m_new, jnp.exp2(state[0] - m_new) * state[1] + pv

    kcb = kc_ref[...].astype(BF16)
    cur = [scores(kcb, q2) for q2 in q2s]
    cur_vt = vct_scr[...]
    states = [None] * len(q2s)
    for k0 in range(0, T, kb):
        k_blk = k_ref[k0:k0 + kb, :]
        nxt = [scores(k_blk, q2) for q2 in q2s]
        states = [accumulate(cu[0], cu[1], cur_vt, st) for cu, st in zip(cur, states)]
        cur, cur_vt = nxt, vt_scr[:, k0:k0 + kb]
    states = [accumulate(cu[0], cu[1], cur_vt, st) for cu, st in zip(cur, states)]
    for c, (_, acc) in enumerate(states):
        rs = slice(c * sq, (c + 1) * sq)
        r = 1.0 / acc[DA_VD:DA_VD + 1, :]
        o_t = acc[0:DA_VD, 0:sq] * r[:, 0:sq] - acc[0:DA_VD, sq:2 * sq] * (r[:, sq:2 * sq] * lam)
        o = o_t.T
        ms = jnp.mean(o * o, axis=-1, keepdims=True)
        o = o * lax.rsqrt(ms + EPS) * hw_ref[...] * (1.0 - lam_init)
        o_ref[rs, :] = (o * _silu(gb_ref[rs, :].astype(F32))).astype(o_ref.dtype)


def _attn_ctx_kernel(q_ref, k_ref, v_ref, gb_ref, lv_ref, hw_ref, o_ref, *, lam_init):
    T = q_ref.shape[0]
    lv = lv_ref[...]
    lam = (jnp.exp(jnp.sum(lv[0:1, :] * lv[1:2, :], axis=-1, keepdims=True))
           - jnp.exp(jnp.sum(lv[2:3, :] * lv[3:4, :], axis=-1, keepdims=True)) + lam_init)
    lane = lax.broadcasted_iota(jnp.int32, (T, 2 * DA_QK), 1)
    dn_t = (((1,), (1,)), ((), ()))
    ones = jnp.ones((16, T), BF16)

    def scores(hh):
        cs = slice(hh * 128, (hh + 1) * 128)
        q = q_ref[:, cs].astype(F32) * (DA_QK ** -0.5 * math.log2(math.e))
        q2 = jnp.concatenate([jnp.where(lane < DA_QK, q, 0.0), jnp.where(lane < DA_QK, 0.0, q)],
                             axis=0).astype(BF16)
        v = v_ref[:, cs].astype(F32)
        vt = jnp.concatenate([v[c0:c0 + 128, :].T for c0 in range(0, T, 128)], axis=1).astype(BF16)
        vt1 = jnp.concatenate([vt, ones], axis=0)
        s = lax.dot_general(k_ref[:, cs], q2, dn_t, preferred_element_type=F32)
        return s, vt1

    nxt = scores(0)
    for hh in range(DA_HEADS):
        cs = slice(hh * 128, (hh + 1) * 128)
        s, vt1 = nxt
        if hh + 1 < DA_HEADS:
            nxt = scores(hh + 1)
        p = jnp.exp2(s - jnp.max(s, axis=0, keepdims=True)).astype(BF16)
        acc = jnp.dot(vt1, p, preferred_element_type=F32)
        r = 1.0 / acc[DA_VD:DA_VD + 1, :]
        o_t = acc[0:DA_VD, 0:T] * r[:, 0:T] - acc[0:DA_VD, T:2 * T] * (r[:, T:2 * T] * lam)
        o = jnp.concatenate([o_t[:, c0:c0 + 128].T for c0 in range(0, T, 128)], axis=0)
        ms = jnp.mean(o * o, axis=-1, keepdims=True)
        o = o * lax.rsqrt(ms + EPS) * hw_ref[...] * (1.0 - lam_init)
        o_ref[:, cs] = (o * _silu(gb_ref[:, cs].astype(F32))).astype(o_ref.dtype)


def _attention_ctx(main, nseq, T, lam_vecs, head_w, lam_init):
    return pl.pallas_call(
        functools.partial(_attn_ctx_kernel, lam_init=lam_init),
        out_shape=jax.ShapeDtypeStruct((nseq * T, D_MODEL), BF16),
        grid=(nseq,),
        in_specs=[
            pl.BlockSpec((T, D_MODEL), lambda b: (b, SEG_Q)),
            pl.BlockSpec((T, D_MODEL), lambda b: (b, SEG_K)),
            pl.BlockSpec((T, D_MODEL), lambda b: (b, SEG_V)),
            pl.BlockSpec((T, D_MODEL), lambda b: (b, SEG_GB)),
            pl.BlockSpec((4, DA_QK), lambda b: (0, 0)),
            pl.BlockSpec((1, DA_VD), lambda b: (0, 0)),
        ],
        out_specs=pl.BlockSpec((T, D_MODEL), lambda b: (b, 0)),
        compiler_params=_cparams(("parallel",)),
        name="diff_attn_ctx",
    )(main, main, main, main, lam_vecs, head_w)


def _attention_latent(hm, nseq, T, tq, kb, lam_vecs, head_w, lam_init, ck, cv, l):
    R = nseq * T
    nq = T // tq
    lc = ck.shape[2]
    return pl.pallas_call(
        functools.partial(_attn_lat_kernel, lam_init=lam_init, kb=kb),
        out_shape=jax.ShapeDtypeStruct((R, D_MODEL), BF16),
        grid=(nseq, DA_HEADS, nq),
        in_specs=[
            pl.BlockSpec((None, None, None, tq, 128), lambda b, g, qi: (0, b, g, qi, 0)),
            pl.BlockSpec((None, None, None, T, 128), lambda b, g, qi: (1, b, g, 0, 0)),
            pl.BlockSpec((None, None, None, T, 128), lambda b, g, qi: (2, b, g, 0, 0)),
            pl.BlockSpec((None, None, None, tq, 128), lambda b, g, qi: (3, b, g, qi, 0)),
            pl.BlockSpec((None, None, lc, 128), lambda b, g, qi: (b, l, 0, g)),
            pl.BlockSpec((None, None, lc, 128), lambda b, g, qi: (b, l, 0, g)),
            pl.BlockSpec((4, DA_QK), lambda b, g, qi: (0, 0)),
            pl.BlockSpec((1, DA_VD), lambda b, g, qi: (0, 0)),
        ],
        out_specs=pl.BlockSpec((tq, 128), lambda b, g, qi: (b * nq + qi, g)),
        scratch_shapes=[pltpu.VMEM((DA_VD + 16, T), BF16), pltpu.VMEM((DA_VD + 16, lc), BF16)],
        compiler_params=_cparams(("parallel", "parallel", "arbitrary")),
        name="diff_attn_lat",
    )(hm, hm, hm, hm, ck, cv, lam_vecs, head_w)


def _sgmlp_merge_kernel(ya_ref, yb_ref, u_ref, sv_ref, gc_ref, ga_ref, gb_ref, gm_ref, vw_ref, ws_ref, bias_ref,
                        wb_ref, wo_ref, pw_ref, mod_ref, x_ref, o_ref):
    tm = u_ref.shape[0]

    def gated(y, g_ref, i):
        p = jnp.dot(y, wb_ref[i], preferred_element_type=F32)
        return _sigmoid(g_ref[...].astype(F32)) * p

    u = _gelu_tanh(u_ref[...].astype(F32))
    v = _gelu_tanh(sv_ref[...].astype(F32))
    vc = v - jnp.mean(v, axis=-1, keepdims=True)
    vb = (vc * lax.rsqrt(jnp.mean(vc * vc, axis=-1, keepdims=True) + EPS) * vw_ref[...]).astype(BF16)
    ug = u * _silu(gc_ref[...].astype(F32))
    rows = []
    for ci in range(tm // SG_CHUNK):
        rs = slice(ci * SG_CHUNK, (ci + 1) * SG_CHUNK)
        cols = [jnp.dot(ws_ref[g], vb[rs, g * 128:(g + 1) * 128], preferred_element_type=F32)
                for g in range(SG_GROUPS)]
        rows.append(((jnp.concatenate(cols, axis=1) + bias_ref[...]) * ug[rs]).astype(BF16))
    yc = jnp.concatenate(rows, axis=0)

    merged = gated(ya_ref[...], ga_ref, 0) + gated(yb_ref[...], gb_ref, 1) + gated(yc, gm_ref, 2)
    o = jnp.dot(merged.astype(BF16), wo_ref[...], preferred_element_type=F32)
    ms = jnp.mean(o * o, axis=-1, keepdims=True)
    o = o * lax.rsqrt(ms + EPS) * pw_ref[...]
    o_ref[...] = x_ref[...] + mod_ref[2:3, :] * o


def _sgmlp_merge(ya, yb, main, seg_shift, vnorm_w, ws_bf, bias_exp, wb_bf, wo_bf, l, post_w, mod, rows_per_mod, x,
                 tm=512):
    R = x.shape[0]
    row = lambda i: (i, 0)
    seg = lambda s: pl.BlockSpec((tm, D_MODEL), lambda i: (i, s - seg_shift))
    const = dict(pipeline_mode=pl.Buffered(1))
    return pl.pallas_call(
        _sgmlp_merge_kernel,
        out_shape=jax.ShapeDtypeStruct((R, D_MODEL), F32),
        grid=(R // tm,),
        in_specs=[
            pl.BlockSpec((tm, D_MODEL), row),
            pl.BlockSpec((tm, D_MODEL), row),
            seg(SEG_U), seg(SEG_SV), seg(SEG_GC), seg(SEG_MGA), seg(SEG_MGB), seg(SEG_MGC),
            pl.BlockSpec((1, D_MODEL), lambda i: (0, 0)),
            pl.BlockSpec((None, SG_GROUPS, SG_CHUNK, SG_CHUNK), lambda i: (l, 0, 0, 0), **const),
            pl.BlockSpec((None, SG_CHUNK, D_MODEL), lambda i: (l, 0, 0), **const),
            pl.BlockSpec((None, 3, D_MODEL, D_MODEL), lambda i: (l, 0, 0, 0), **const),
            pl.BlockSpec((None, D_MODEL, D_MODEL), lambda i: (l, 0, 0), **const),
            pl.BlockSpec((1, D_MODEL), lambda i: (0, 0)),
            pl.BlockSpec((None, 3, D_MODEL), lambda i: ((i * tm) // rows_per_mod, 0, 0)),
            pl.BlockSpec((tm, D_MODEL), row),
        ],
        out_specs=pl.BlockSpec((tm, D_MODEL), row),
        compiler_params=_cparams(("parallel",)),
        name="sgmlp_merge",
    )(ya, yb, main, main, main, main, main, main, vnorm_w, ws_bf, bias_exp, wb_bf, wo_bf, post_w, mod, x)


def _rope_tables(n_tokens):
    n_rows = n_tokens // GRID_W
    rows = jnp.repeat(jnp.arange(n_rows, dtype=F32), GRID_W)
    cols = jnp.tile(jnp.arange(GRID_W, dtype=F32), n_rows)
    n_freq = DA_QK // 4
    inv = ROPE_BASE ** (-jnp.arange(n_freq, dtype=F32) / n_freq)
    ang = jnp.concatenate([rows[:, None] * inv, cols[:, None] * inv], -1)
    cos, sin = jnp.cos(ang), jnp.sin(ang)
    return jnp.tile(cos, (1, 4)), jnp.concatenate([-sin, -sin, sin, sin], axis=1)


def kernel(x_prompt, x_sample, cache_k, cache_v, state_ssm, c, c_ctx, pre_norm_w, post_norm_w, w_mod, b_mod, w_in,
           m_conv_w, m_conv_b, m_A_log, m_dt_bias, m_D, m_norm_w, da_lambda, da_head_norm_w, sg_vnorm_w,
           sg_spatial_w, sg_spatial_b, w_branch, w_out):
    depth = w_in.shape[0]
    nb, seq, _ = x_prompt.shape
    db, dseq, _ = x_sample.shape
    past = cache_k.shape[2]

    w_main, w_misc = _prep_in_weights(jnp.swapaxes(w_in, 1, 2))
    wb_bf = w_branch.astype(BF16)
    wo_bf = w_out.astype(BF16)
    ws_bf = sg_spatial_w.astype(BF16)
    bias_exp = jnp.repeat(jnp.swapaxes(sg_spatial_b, 1, 2), D_MODEL // SG_GROUPS, axis=2)
    cw = jnp.swapaxes(m_conv_w, 1, 2)
    dtb = jnp.pad(m_dt_bias.reshape(depth, 1, 2 * M_HEADS), ((0, 0), (0, 0), (0, 128 - 2 * M_HEADS)))
    alog = jnp.pad(m_A_log.reshape(depth, 1, 2 * M_HEADS), ((0, 0), (0, 0), (0, 128 - 2 * M_HEADS)))
    dsum = jnp.repeat(m_D[:, 0] + m_D[:, 1], M_HEADDIM, axis=1).reshape(depth, 1, D_MODEL)

    cvec = jnp.concatenate([c_ctx[None, :], c, jnp.zeros((8 - 1 - db, D_MODEL), F32)], axis=0)
    mods = _modulation(cvec, w_mod, b_mod).reshape(depth, 8, 3, D_MODEL)

    rope_tabs = _rope_tables(dseq)
    init_states = state_ssm.reshape(db, depth, 2, M_HEADS * M_HEADDIM, M_STATE)
    ck = cache_k.reshape(db, depth, past, DA_HEADS, 2, 2, ROPE_GROUP)
    ck = jnp.swapaxes(ck, 4, 5).reshape(db, depth, past, D_MODEL)
    cv = cache_v.reshape(db, depth, past, D_MODEL)

    xp = x_prompt.reshape(nb * seq, D_MODEL)
    xs = x_sample.reshape(db * dseq, D_MODEL)

    def layer(x, l, nseq, T, mod, rows_per_mod, latent, stacks=(None, None, None)):
        lam_init = 0.8 - 0.6 * math.exp(-0.3 * l)
        k_stack, v_stack, st_stack = stacks
        outs = _inproj(x, mod, pre_norm_w[l][None], w_main, w_misc, l, rows_per_mod,
                       rope_tabs if latent else None, None if latent else (T, depth, k_stack, v_stack))
        main, misc = outs[0], outs[1]
        ya, st = _ssd(main, misc, nseq, T, cw[l, :, 0:D_MODEL], m_conv_b[l][None, 0:D_MODEL],
                      cw[l, :, D_MODEL:], m_conv_b[l][None, D_MODEL:], dtb[l], alog[l], dsum[l],
                      m_norm_w[l][None], (init_states, l) if latent else None,
                      st_stack=None if latent else st_stack, layer=0 if latent else l,
                      n_layers=1 if latent else depth, cps=min(SSD_CHUNKS_PER_STEP, T // M_CHUNK))
        if latent:
            yb = _attention_latent(outs[2], nseq, T, ATT_TQ, ATT_KB, da_lambda[l], da_head_norm_w[l][None],
                                   lam_init, ck, cv, l)
        else:
            yb = _attention_ctx(main, nseq, T, da_lambda[l], da_head_norm_w[l][None], lam_init)
        shift = LATENT_SEG_SHIFT if latent else 0
        x_new = _sgmlp_merge(ya, yb, main, shift, sg_vnorm_w[l][None], ws_bf, bias_exp, wb_bf, wo_bf, l,
                             post_norm_w[l][None], mod, rows_per_mod, x)
        return x_new, (None if latent else (outs[2], outs[3], st))

    stacks = (jnp.zeros((nb, depth, seq, D_MODEL), F32), jnp.zeros((nb, depth, seq, D_MODEL), F32),
              jnp.zeros((nb, depth, 2, M_HEADS * M_HEADDIM, M_STATE), F32))
    for l in range(depth):
        xp, stacks = layer(xp, l, nb, seq, mods[l, 0:1], nb * seq, False, stacks)
        xs, _ = layer(xs, l, db, dseq, mods[l, 1:1 + db], dseq, True)

    k_all, v_all, st_all = stacks
    return (xp.reshape(nb, seq, D_MODEL), xs.reshape(db, dseq, D_MODEL),
            k_all.reshape(nb, depth, seq, DA_HEADS, 2 * DA_QK), v_all.reshape(nb, depth, seq, DA_HEADS, DA_VD),
            st_all.reshape(nb, depth, 2, M_HEADS, M_HEADDIM, M_STATE))
```

```python
import functools
import math

import jax
import jax.numpy as jnp
from jax import lax
from jax.experimental import pallas as pl
from jax.experimental.pallas import tpu as pltpu

F32 = jnp.float32
BF16 = jnp.bfloat16

D_MODEL = 1024
EPS = 1e-6
GRID_W = 64
ROPE_BASE = 10000.0
M_HEADS = 16
M_HEADDIM = 64
M_STATE = 64
M_GROUPS = 2
M_CHUNK = 128
M_PAIRS = M_HEADS // 2
DA_HEADS = 8
DA_QK = 64
DA_VD = 128
SG_GROUPS = 8
SG_CHUNK = 128

SEG_Z, SEG_XS, SEG_Q, SEG_K, SEG_V, SEG_GB, SEG_U, SEG_SV, SEG_GC, SEG_MGA, SEG_MGB, SEG_MGC = range(12)
N_SEG = 12
N_WBLK = N_SEG + 2
ROPE_GROUP = 32
LATENT_SEG_SHIFT = 4
MISC_W = 384

VMEM_LIMIT = 56 * 1024 * 1024
NEG_BIG = -1e30
ATT_SUBQ = 128
ATT_TQ = 2048
ATT_KB = 512
SSD_CHUNKS_PER_STEP = 8
INPROJ_TM = 1024
INPROJ_TN = 2048
INPROJ_SUBN = 1024


_NEG_LOG2E = -math.log2(math.e)


def _silu(x):
    return x * _sigmoid(x)


def _sigmoid(x):
    return 1.0 / (1.0 + jnp.exp2(x * _NEG_LOG2E))


def _gelu_tanh(x):
    c = math.sqrt(2.0 / math.pi)
    hx = 0.5 * x
    return hx + hx * jnp.tanh(x * (c + (c * 0.044715) * (x * x)))


def _cparams(sem):
    return pltpu.CompilerParams(dimension_semantics=sem, vmem_limit_bytes=VMEM_LIMIT)


def _mod_kernel(c_ref, w_ref, b_ref, o_ref):
    c = c_ref[...]
    s = _silu(c).astype(BF16)
    o_ref[...] = jnp.dot(s, w_ref[...].astype(BF16), preferred_element_type=F32) + b_ref[...]


def _modulation(cvec, w_mod, b_mod):
    depth = w_mod.shape[0]
    nt = 3
    return pl.pallas_call(
        _mod_kernel,
        out_shape=jax.ShapeDtypeStruct((depth, 8, 3 * D_MODEL), F32),
        grid=(depth, nt),
        in_specs=[
            pl.BlockSpec((8, D_MODEL), lambda l, j: (0, 0)),
            pl.BlockSpec((None, D_MODEL, D_MODEL), lambda l, j: (l, 0, j)),
            pl.BlockSpec((None, 1, D_MODEL), lambda l, j: (l, 0, j)),
        ],
        out_specs=pl.BlockSpec((None, 8, D_MODEL), lambda l, j: (l, 0, j)),
        compiler_params=_cparams(("arbitrary", "arbitrary")),
        name="modulation",
    )(cvec, w_mod, b_mod.reshape(depth, 1, 3 * D_MODEL))


W_IN_COLS = 12576
W_XS_END = 2048
W_MISC_END = 2336


def _wprep_kernel(wa_ref, wb_ref, wm_ref, main_ref, misc_ref):
    j = pl.program_id(1)

    @pl.when(j < W_XS_END // D_MODEL)
    def _():
        main_ref[...] = wa_ref[...].T.astype(BF16)

    @pl.when(jnp.logical_and(j >= W_XS_END // D_MODEL, j < N_SEG))
    def _():
        main_ref[...] = wb_ref[...].T.astype(BF16)

    @pl.when(j >= N_SEG)
    def _():
        w = wb_ref[...]
        g = ROPE_GROUP
        rows = []
        for h0 in range(0, D_MODEL, 4 * g):
            rows += [w[h0:h0 + g], w[h0 + 2 * g:h0 + 3 * g], w[h0 + g:h0 + 2 * g], w[h0 + 3 * g:h0 + 4 * g]]
        main_ref[...] = jnp.concatenate(rows, axis=0).T.astype(BF16)

    @pl.when(j == 0)
    def _():
        n = W_MISC_END - W_XS_END
        wm = jnp.concatenate([wm_ref[...], jnp.zeros((MISC_W - n, D_MODEL), F32)], axis=0)
        misc_ref[...] = wm.T.astype(BF16)


def _prep_in_weights(w_in_t):
    depth = w_in_t.shape[0]
    n_head = W_XS_END // D_MODEL
    w2d = w_in_t.reshape(depth * W_IN_COLS, D_MODEL)
    skew = W_MISC_END - W_XS_END
    src = lambda j: jnp.where(j < N_SEG, j, j - N_SEG + SEG_Q)
    return pl.pallas_call(
        _wprep_kernel,
        out_shape=[jax.ShapeDtypeStruct((depth, D_MODEL, N_WBLK * D_MODEL), BF16),
                   jax.ShapeDtypeStruct((depth, D_MODEL, MISC_W), BF16)],
        grid=(depth, N_WBLK),
        in_specs=[pl.BlockSpec((None, D_MODEL, D_MODEL), lambda l, j: (l, jnp.minimum(j, n_head - 1), 0)),
                  pl.BlockSpec((pl.Element(D_MODEL), pl.Element(D_MODEL)),
                               lambda l, j: (pl.multiple_of(l * W_IN_COLS + skew + src(j) * D_MODEL, 32), 0)),
                  pl.BlockSpec((pl.Element(skew), pl.Element(D_MODEL)),
                               lambda l, j: (pl.multiple_of(l * W_IN_COLS + W_XS_END, 32), 0))],
        out_specs=[pl.BlockSpec((None, D_MODEL, D_MODEL), lambda l, j: (l, 0, j)),
                   pl.BlockSpec((None, D_MODEL, MISC_W), lambda l, j: (l, 0, 0))],
        compiler_params=_cparams(("parallel", "arbitrary")),
        name="w_prep",
    )(w_in_t, w2d, w2d)


def _inproj_kernel(*refs, rope, emit_kv, has_prev):
    it = iter(refs)
    x_ref, mod_ref, prew_ref, w_ref, wm_ref = next(it), next(it), next(it), next(it), next(it)
    if rope:
        cos_ref, sin_ref = next(it), next(it)
    if has_prev:
        next(it), next(it)
    main_ref, misc_ref = next(it), next(it)
    if emit_kv:
        kf_ref, vf_ref = next(it), next(it)
    if rope:
        hm_ref = next(it)
    h_scr = next(it)

    j = pl.program_id(1)
    tm = x_ref.shape[0]
    tn = w_ref.shape[1]
    spt = tn // D_MODEL

    @pl.when(j == 0)
    def _():
        x = x_ref[...]
        ms = jnp.mean(x * x, axis=-1, keepdims=True)
        y = x * lax.rsqrt(ms + EPS) * prew_ref[...]
        h = y * (1.0 + mod_ref[1:2, :]) + mod_ref[0:1, :]
        hb = h.astype(BF16)
        h_scr[...] = hb
        misc_ref[...] = jnp.dot(hb, wm_ref[...], preferred_element_type=F32)

    h = h_scr[...]
    sub = INPROJ_SUBN
    for c0 in range(0, tn, sub):
        cs = slice(c0, c0 + sub)
        seg = j * spt + c0 // D_MODEL
        lc = c0 % D_MODEL
        ls = slice(lc, lc + sub)
        acc = jnp.dot(h, w_ref[:, cs], preferred_element_type=F32)

        if rope:
            is_rope = jnp.logical_or(seg == SEG_Q, seg == SEG_K)
            is_plain_hm = jnp.logical_or(seg == SEG_V, seg == SEG_GB)

            def store_heads(val):
                vb = val.astype(hm_ref.dtype)
                for hh in range(sub // 128):
                    hm_ref[c0 // D_MODEL, lc // 128 + hh] = vb[:, hh * 128:(hh + 1) * 128]

            @pl.when(is_rope)
            def _():
                cos = jnp.concatenate([cos_ref[...]] * (sub // 128), axis=1)
                sin = jnp.concatenate([sin_ref[...]] * (sub // 128), axis=1)
                rot = jnp.concatenate([pltpu.roll(acc[:, c:c + 128], 2 * ROPE_GROUP, 1)
                                       for c in range(0, sub, 128)], axis=1)
                store_heads(acc * cos + rot * sin)

            @pl.when(is_plain_hm)
            def _():
                store_heads(acc)

            @pl.when(jnp.logical_not(jnp.logical_or(is_rope, is_plain_hm)))
            def _():
                main_ref[:, cs] = acc.astype(main_ref.dtype)
        else:
            main_ref[:, cs] = acc.astype(main_ref.dtype)

        if emit_kv:
            nsq, sq_len = kf_ref.shape[0], kf_ref.shape[1]

            @pl.when(seg == SEG_K)
            def _():
                kf_ref[:, :, ls] = acc.reshape(nsq, sq_len, sub)

            @pl.when(seg == SEG_V)
            def _():
                vf_ref[:, :, ls] = acc.reshape(nsq, sq_len, sub)


def _inproj(x, mod, pre_w, w_main, w_misc, l, rows_per_mod, rope_tabs, kv_cache=None, tm=INPROJ_TM, tn=INPROJ_TN):
    R = x.shape[0]
    emit_kv = kv_cache is not None
    ni = R // tm
    spt = tn // D_MODEL
    rope = rope_tabs is not None
    const = dict(pipeline_mode=pl.Buffered(1))
    in_specs = [
        pl.BlockSpec((tm, D_MODEL), lambda i, j: (i, 0), **const),
        pl.BlockSpec((None, 3, D_MODEL), lambda i, j: ((i * tm) // rows_per_mod, 0, 0)),
        pl.BlockSpec((1, D_MODEL), lambda i, j: (0, 0)),
        pl.BlockSpec((None, D_MODEL, tn),
                     (lambda i, j: (l, 0, jnp.where(j == SEG_Q // spt, N_SEG // spt, j))) if rope
                     else (lambda i, j: (l, 0, j))),
        pl.BlockSpec((None, D_MODEL, MISC_W), lambda i, j: (l, 0, 0), **const),
    ]
    args = [x, mod, pre_w, w_main, w_misc]
    if rope:
        cos, sin = rope_tabs
        nt = cos.shape[0] // tm
        in_specs += [pl.BlockSpec((tm, 128), lambda i, j: (i % nt, 0)),
                     pl.BlockSpec((tm, 128), lambda i, j: (i % nt, 0))]
        args += [cos, sin]
    if rope:
        n_main = N_SEG - LATENT_SEG_SHIFT
        main_idx = lambda i, j: (i, jnp.where(j < SEG_Q // spt, j,
                                              jnp.where(j <= SEG_GB // spt, SEG_XS // spt,
                                                        j - LATENT_SEG_SHIFT // spt)))
    else:
        n_main = N_SEG
        main_idx = lambda i, j: (i, j)
    out_shape = [jax.ShapeDtypeStruct((R, n_main * D_MODEL), BF16), jax.ShapeDtypeStruct((R, MISC_W), F32)]
    out_specs = [pl.BlockSpec((tm, tn), main_idx), pl.BlockSpec((tm, MISC_W), lambda i, j: (i, 0))]
    aliases = {}
    has_prev = False
    if emit_kv:
        seq_len, n_layers, k_prev, v_prev = kv_cache
        out_shape += [jax.ShapeDtypeStruct((R // seq_len, n_layers, seq_len, D_MODEL), F32)] * 2
        out_specs += [pl.BlockSpec((tm // seq_len, None, seq_len, D_MODEL), lambda i, j: (i, l, 0, 0))] * 2
        if k_prev is not None:
            has_prev = True
            in_specs += [pl.BlockSpec(memory_space=pl.ANY)] * 2
            args += [k_prev, v_prev]
            aliases = {len(args) - 2: 2, len(args) - 1: 3}
    if rope:
        T = cos.shape[0]
        out_shape.append(jax.ShapeDtypeStruct((4, R // T, DA_HEADS, T, 128), BF16))
        out_specs.append(pl.BlockSpec(
            (spt, None, DA_HEADS, tm, 128),
            lambda i, j: (jnp.clip(j - SEG_Q // spt, 0, 4 // spt - 1), (i * tm) // T, 0, ((i * tm) % T) // tm, 0)))
    return pl.pallas_call(
        functools.partial(_inproj_kernel, rope=rope, emit_kv=emit_kv, has_prev=has_prev),
        out_shape=out_shape,
        grid=(ni, N_SEG // spt),
        in_specs=in_specs,
        out_specs=out_specs,
        input_output_aliases=aliases,
        scratch_shapes=[pltpu.VMEM((tm, D_MODEL), BF16)],
        compiler_params=_cparams(("parallel", "arbitrary")),
        name="inproj",
    )(*args)


def _split3(a):
    hi = a.astype(BF16)
    r1 = a - hi.astype(F32)
    mid = r1.astype(BF16)
    lo = (r1 - mid.astype(F32)).astype(BF16)
    return hi, mid, lo


def _ssd_kernel(*refs, nc, cps, has_init, has_prev):
    it = iter(refs)
    xs_ref, xsp_ref, xsn_ref = next(it), next(it), next(it)
    mi_ref, mip_ref, min_ref = next(it), next(it), next(it)
    z_ref = next(it)
    cwx_ref, cbx_ref, cwm_ref, cbm_ref = next(it), next(it), next(it), next(it)
    dtb_ref, alog_ref, dsum_ref, nw_ref = next(it), next(it), next(it), next(it)
    if has_init:
        init_ref = next(it)
    if has_prev:
        next(it)
    y_ref, sto_ref = next(it), next(it)
    xc_scr, bc_scr, yf_scr, st_scr = next(it), next(it), next(it), next(it)

    L = M_CHUNK
    LB = cps * L
    ns = nc // cps
    j = pl.program_id(1)
    fwd = j < ns
    c = jnp.where(fwd, j, 2 * ns - 1 - j)
    row0 = pl.multiple_of(c * LB, LB)
    LOG2E = math.log2(math.e)

    rid = lax.broadcasted_iota(jnp.int32, (L, L), 0)
    cid = lax.broadcasted_iota(jnp.int32, (L, L), 1)

    def conv_silu(x, prow, nrow, w_ref, b_ref):
        n = x.shape[1]
        r = lax.broadcasted_iota(jnp.int32, (LB, n), 0)
        xm = jnp.where(r == 0, prow, pltpu.roll(x, 1, 0))
        xp = jnp.where(r == LB - 1, nrow, pltpu.roll(x, LB - 1, 0))
        y = w_ref[0:1, :] * xm + w_ref[1:2, :] * x + w_ref[2:3, :] * xp + b_ref[...]
        return _silu(y)

    def load_state(d):
        if has_init:
            pad = jnp.zeros((2 * M_HEADDIM, 2 * M_HEADDIM - M_STATE), F32)
            for i in range(M_PAIRS):
                blk = init_ref[d, i * 2 * M_HEADDIM:(i + 1) * 2 * M_HEADDIM, :]
                st_scr[i] = jnp.concatenate([blk, pad], axis=1).T[0:M_STATE, :]
        else:
            st_scr[...] = jnp.zeros_like(st_scr)

    def store_state(d):
        pad = jnp.zeros((2 * M_HEADDIM - M_STATE, 2 * M_HEADDIM), F32)
        for i in range(M_PAIRS):
            t = jnp.concatenate([st_scr[i], pad], axis=0).T
            sto_ref[d, i * 2 * M_HEADDIM:(i + 1) * 2 * M_HEADDIM, :] = t[:, 0:M_STATE]

    lane_lo = cid < M_HEADDIM
    lane_lo_s = lax.broadcasted_iota(jnp.int32, (M_STATE, 2 * M_HEADDIM), 1) < M_HEADDIM
    heads_per_group = M_HEADS // M_GROUPS
    zeros_s = jnp.zeros((M_STATE, 2 * M_HEADDIM), BF16)

    def setup(d, bc, dt_raw):
        tri = (cid <= rid) if d == 0 else (cid >= rid)
        tri_bf = jnp.where(tri, 1.0, 0.0).astype(BF16)
        dt = dt_raw + dtb_ref[...]
        dt = jnp.maximum(dt, 0.0) + jnp.log1p(jnp.exp(-jnp.abs(dt)))
        a = dt * (-jnp.exp(alog_ref[...]))
        a_hi, a_mid, a_lo = _split3(a)
        p_col = (jnp.dot(tri_bf, a_hi, preferred_element_type=F32)
                 + jnp.dot(tri_bf, a_mid, preferred_element_type=F32)
                 + jnp.dot(tri_bf, a_lo, preferred_element_type=F32))
        p_row = p_col.T[d * M_HEADS:(d + 1) * M_HEADS, :]
        dt_row = dt.T[d * M_HEADS:(d + 1) * M_HEADS, :]
        tot = p_row[:, L - 1:L] if d == 0 else p_row[:, 0:1]
        b_all = bc[:, 0:M_GROUPS * M_STATE]
        c_all = bc[:, M_GROUPS * M_STATE:2 * M_GROUPS * M_STATE]
        b_bf = b_all.astype(BF16)
        g_mats = []
        for g in range(M_GROUPS):
            cg = jnp.where(lane_lo if g == 0 else jnp.logical_not(lane_lo), c_all, 0.0).astype(BF16)
            g_mats.append(lax.dot_general(cg, b_bf, (((1,), (1,)), ((), ())),
                                          preferred_element_type=F32).astype(BF16))
        return dict(
            tri=tri, g_mats=g_mats, c_bf=c_all.astype(BF16), bt_all=b_all.T,
            p2_col=p_col * LOG2E,
            q2_row=(p_row - jnp.log(dt_row)) * LOG2E,
            w_row=dt_row * jnp.exp(tot - p_row),
            etot=jnp.exp(tot),
            ep_col=jnp.exp(p_col))

    def pairs(d, s, x_bf):
        outs = []
        for i in range(M_PAIRS):
            g = (2 * i) // heads_per_group
            btg = s["bt_all"][g * M_STATE:(g + 1) * M_STATE, :]
            lhs_rows = []
            bw_rows = []
            for hh in range(2):
                h = 2 * i + hh
                col = d * M_HEADS + h
                pc = jnp.broadcast_to(s["p2_col"][:, col:col + 1], (L, L))
                dm = jnp.exp2(jnp.where(s["tri"], pc - s["q2_row"][h:h + 1, :], NEG_BIG))
                m_h = s["g_mats"][g] * dm.astype(BF16)
                ce_h = s["c_bf"] * jnp.broadcast_to(s["ep_col"][:, col:col + 1], (L, L)).astype(BF16)
                lhs_rows.append(jnp.concatenate([m_h, ce_h], axis=1))
                bw = (btg * s["w_row"][h:h + 1, :]).astype(BF16)
                bw_rows.append(jnp.concatenate([bw, zeros_s], axis=1))
            lhs = jnp.concatenate(lhs_rows + bw_rows, axis=0)
            x_pair = x_bf[:, i * 128:(i + 1) * 128]
            st_pair = st_scr[i]
            st_bf = st_pair.astype(BF16)
            rhs = jnp.concatenate([x_pair] + ([st_bf, zeros_s] if g == 0 else [zeros_s, st_bf]), axis=0)
            res = jnp.dot(lhs, rhs, preferred_element_type=F32)
            y_pair = jnp.where(lane_lo, res[0:L], res[L:2 * L])
            ds = jnp.where(lane_lo_s, res[2 * L:2 * L + M_STATE], res[2 * L + M_STATE:2 * L + 2 * M_STATE])
            e0 = jnp.broadcast_to(s["etot"][2 * i:2 * i + 1, :], (M_STATE, 2 * M_HEADDIM))
            e1 = jnp.broadcast_to(s["etot"][2 * i + 1:2 * i + 2, :], (M_STATE, 2 * M_HEADDIM))
            st_scr[i] = jnp.where(lane_lo_s, e0, e1) * st_pair + ds
            outs.append(y_pair)
        return jnp.concatenate(outs, axis=1)

    def block(d, x_bf, bc, dt_raw):
        order = range(cps) if d == 0 else range(cps - 1, -1, -1)
        su = {ci: setup(d, bc[ci * L:(ci + 1) * L], dt_raw[ci * L:(ci + 1) * L]) for ci in order}
        ys = {ci: pairs(d, su[ci], x_bf[ci * L:(ci + 1) * L]) for ci in order}
        return jnp.concatenate([ys[ci] for ci in range(cps)], axis=0)

    @pl.when(j == 0)
    def _():
        load_state(0)

    @pl.when(j == ns)
    def _():
        load_state(1)

    @pl.when(fwd)
    def _():
        x = xs_ref[...].astype(F32)
        prow = jnp.where(c > 0, xsp_ref[...].astype(F32)[15:16, :], 0.0)
        nrow = jnp.where(c < ns - 1, xsn_ref[...].astype(F32)[0:1, :], 0.0)
        xc = conv_silu(x, prow, nrow, cwx_ref, cbx_ref).astype(BF16)
        xc_scr[pl.ds(row0, LB), :] = xc
        m = mi_ref[...]
        bcx = m[:, 0:256]
        prow_m = jnp.where(c > 0, mip_ref[7:8, 0:256], 0.0)
        nrow_m = jnp.where(c < ns - 1, min_ref[0:1, 0:256], 0.0)
        bc = conv_silu(bcx, prow_m, nrow_m, cwm_ref, cbm_ref)
        bc_scr[pl.ds(row0, LB), :] = bc
        yf_scr[pl.ds(row0, LB), :] = block(0, xc, bc, m[:, 256:384])

    @pl.when(j == ns - 1)
    def _():
        store_state(0)

    @pl.when(jnp.logical_not(fwd))
    def _():
        xc = xc_scr[pl.ds(row0, LB), :]
        bc = bc_scr[pl.ds(row0, LB), :]
        yb = block(1, xc, bc, mi_ref[:, 256:384])
        y = yf_scr[pl.ds(row0, LB), :] + yb + dsum_ref[...] * xc.astype(F32)
        y = y * _silu(z_ref[...].astype(F32))
        ms = jnp.mean(y * y, axis=-1, keepdims=True)
        y_ref[...] = (y * lax.rsqrt(ms + EPS) * nw_ref[...]).astype(y_ref.dtype)

    @pl.when(j == 2 * ns - 1)
    def _():
        store_state(1)


def _ssd(main, misc, nseq, T, cwx, cbx, cwm, cbm, dtb, alog, dsum, nw, init, st_stack=None, layer=0, n_layers=1,
         cps=SSD_CHUNKS_PER_STEP):
    R = nseq * T
    nc = T // M_CHUNK
    L = cps * M_CHUNK
    ns = nc // cps
    nc, full_nc = ns, nc
    has_init = init is not None

    def cidx(j):
        return jnp.where(j < nc, j, 2 * nc - 1 - j)

    def oidx(j):
        return jnp.where(j < nc, nc - 1, 2 * nc - 1 - j)

    in_specs = [
        pl.BlockSpec((L, D_MODEL), lambda b, j: (b * nc + cidx(j), SEG_XS)),
        pl.BlockSpec((16, D_MODEL), lambda b, j: (jnp.maximum((b * nc + cidx(j)) * (L // 16) - 1, 0), SEG_XS)),
        pl.BlockSpec((16, D_MODEL),
                     lambda b, j: (jnp.minimum((b * nc + cidx(j) + 1) * (L // 16), R // 16 - 1), SEG_XS)),
        pl.BlockSpec((L, MISC_W), lambda b, j: (b * nc + cidx(j), 0)),
        pl.BlockSpec((8, MISC_W), lambda b, j: (jnp.maximum((b * nc + cidx(j)) * (L // 8) - 1, 0), 0)),
        pl.BlockSpec((8, MISC_W), lambda b, j: (jnp.minimum((b * nc + cidx(j) + 1) * (L // 8), R // 8 - 1), 0)),
        pl.BlockSpec((L, D_MODEL), lambda b, j: (b * nc + oidx(j), SEG_Z)),
        pl.BlockSpec((3, D_MODEL), lambda b, j: (0, 0)),
        pl.BlockSpec((1, D_MODEL), lambda b, j: (0, 0)),
        pl.BlockSpec((3, 256), lambda b, j: (0, 0)),
        pl.BlockSpec((1, 256), lambda b, j: (0, 0)),
        pl.BlockSpec((1, 128), lambda b, j: (0, 0)),
        pl.BlockSpec((1, 128), lambda b, j: (0, 0)),
        pl.BlockSpec((1, D_MODEL), lambda b, j: (0, 0)),
        pl.BlockSpec((1, D_MODEL), lambda b, j: (0, 0)),
    ]
    args = [main, main, main, misc, misc, misc, main, cwx, cbx, cwm, cbm, dtb, alog, dsum, nw]
    if has_init:
        init_arr, init_layer = init
        in_specs.append(pl.BlockSpec((None, None, 2, M_HEADS * M_HEADDIM, M_STATE),
                                     lambda b, j: (b, init_layer, 0, 0, 0)))
        args.append(init_arr)
    aliases = {}
    if st_stack is not None:
        in_specs.append(pl.BlockSpec(memory_space=pl.ANY))
        args.append(st_stack)
        aliases = {len(args) - 1: 1}
    return pl.pallas_call(
        functools.partial(_ssd_kernel, nc=full_nc, cps=cps, has_init=has_init, has_prev=st_stack is not None),
        out_shape=[jax.ShapeDtypeStruct((R, D_MODEL), BF16),
                   jax.ShapeDtypeStruct((nseq, n_layers, 2, M_HEADS * M_HEADDIM, M_STATE), F32)],
        grid=(nseq, 2 * nc),
        in_specs=in_specs,
        out_specs=[pl.BlockSpec((L, D_MODEL), lambda b, j: (b * nc + oidx(j), 0)),
                   pl.BlockSpec((None, None, 2, M_HEADS * M_HEADDIM, M_STATE), lambda b, j: (b, layer, 0, 0, 0))],
        scratch_shapes=[pltpu.VMEM((T, D_MODEL), BF16), pltpu.VMEM((T, 256), F32),
                        pltpu.VMEM((T, D_MODEL), F32), pltpu.VMEM((M_PAIRS, M_STATE, 128), F32)],
        input_output_aliases=aliases,
        compiler_params=_cparams(("parallel", "arbitrary")),
        name="ssd",
    )(*args)


def _attn_lat_kernel(q_ref, k_ref, v_ref, gb_ref, kc_ref, vc_ref, lv_ref, hw_ref, o_ref, vt_scr, vct_scr,
                     *, lam_init, kb):
    tq = q_ref.shape[0]
    T = k_ref.shape[0]
    lc = kc_ref.shape[0]
    qi = pl.program_id(2)

    @pl.when(qi == 0)
    def _():
        for c0 in range(0, T, 128):
            vt_scr[0:DA_VD, c0:c0 + 128] = v_ref[c0:c0 + 128, :].astype(F32).T.astype(BF16)
        vt_scr[DA_VD:DA_VD + 16, :] = jnp.ones((16, T), BF16)
        for c0 in range(0, lc, 128):
            vct_scr[0:DA_VD, c0:c0 + 128] = vc_ref[c0:c0 + 128, :].T.astype(BF16)
        vct_scr[DA_VD:DA_VD + 16, :] = jnp.ones((16, lc), BF16)

    lv = lv_ref[...]
    lam = (jnp.exp(jnp.sum(lv[0:1, :] * lv[1:2, :], axis=-1, keepdims=True))
           - jnp.exp(jnp.sum(lv[2:3, :] * lv[3:4, :], axis=-1, keepdims=True)) + lam_init)
    sq = ATT_SUBQ
    map1 = (lax.broadcasted_iota(jnp.int32, (sq, 2 * DA_QK), 1) & ROPE_GROUP) == 0
    dn_t = (((1,), (1,)), ((), ()))

    q2s = []
    for c in range(tq // sq):
        q = q_ref[c * sq:(c + 1) * sq, :].astype(F32) * (DA_QK ** -0.5 * math.log2(math.e))
        q2s.append(jnp.concatenate([jnp.where(map1, q, 0.0), jnp.where(map1, 0.0, q)], axis=0).astype(BF16))

    def scores(k_blk, q2):
        s = lax.dot_general(k_blk, q2, dn_t, preferred_element_type=F32)
        return s, jnp.max(s, axis=0, keepdims=True)

    def accumulate(s, bm, vt_blk, state):
        m_new = bm if state is None else jnp.maximum(state[0], bm)
        p = jnp.exp2(s - m_new).astype(BF16)
        pv = jnp.dot(vt_blk, p, preferred_element_type=F32)
        if state is None:
            return m_new, pv
        return m_new, jnp.exp2(state[0] - m_new) * state[1] + pv

    kcb = kc_ref[...].astype(BF16)
    cur = [scores(kcb, q2) for q2 in q2s]
    cur_vt = vct_scr[...]
    states = [None] * len(q2s)
    for k0 in range(0, T, kb):
        k_blk = k_ref[k0:k0 + kb, :]
        nxt = [scores(k_blk, q2) for q2 in q2s]
        states = [accumulate(cu[0], cu[1], cur_vt, st) for cu, st in zip(cur, states)]
        cur, cur_vt = nxt, vt_scr[:, k0:k0 + kb]
    states = [accumulate(cu[0], cu[1], cur_vt, st) for cu, st in zip(cur, states)]
    for c, (_, acc) in enumerate(states):
        rs = slice(c * sq, (c + 1) * sq)
        r = 1.0 / acc[DA_VD:DA_VD + 1, :]
        o_t = acc[0:DA_VD, 0:sq] * r[:, 0:sq] - acc[0:DA_VD, sq:2 * sq] * (r[:, sq:2 * sq] * lam)
        o = o_t.T
        ms = jnp.mean(o * o, axis=-1, keepdims=True)
        o = o * lax.rsqrt(ms + EPS) * hw_ref[...] * (1.0 - lam_init)
        o_ref[rs, :] = (o * _silu(gb_ref[rs, :].astype(F32))).astype(o_ref.dtype)


def _attn_ctx_kernel(q_ref, k_ref, v_ref, gb_ref, lv_ref, hw_ref, o_ref, *, lam_init):
    T = q_ref.shape[0]
    lv = lv_ref[...]
    lam = (jnp.exp(jnp.sum(lv[0:1, :] * lv[1:2, :], axis=-1, keepdims=True))
           - jnp.exp(jnp.sum(lv[2:3, :] * lv[3:4, :], axis=-1, keepdims=True)) + lam_init)
    lane = lax.broadcasted_iota(jnp.int32, (T, 2 * DA_QK), 1)
    dn_t = (((1,), (1,)), ((), ()))
    ones = jnp.ones((16, T), BF16)

    def scores(hh):
        cs = slice(hh * 128, (hh + 1) * 128)
        q = q_ref[:, cs].astype(F32) * (DA_QK ** -0.5 * math.log2(math.e))
        q2 = jnp.concatenate([jnp.where(lane < DA_QK, q, 0.0), jnp.where(lane < DA_QK, 0.0, q)],
                             axis=0).astype(BF16)
        v = v_ref[:, cs].astype(F32)
        vt = jnp.concatenate([v[c0:c0 + 128, :].T for c0 in range(0, T, 128)], axis=1).astype(BF16)
        vt1 = jnp.concatenate([vt, ones], axis=0)
        s = lax.dot_general(k_ref[:, cs], q2, dn_t, preferred_element_type=F32)
        return s, vt1

    nxt = scores(0)
    for hh in range(DA_HEADS):
        cs = slice(hh * 128, (hh + 1) * 128)
        s, vt1 = nxt
        if hh + 1 < DA_HEADS:
            nxt = scores(hh + 1)
        p = jnp.exp2(s - jnp.max(s, axis=0, keepdims=True)).astype(BF16)
        acc = jnp.dot(vt1, p, preferred_element_type=F32)
        r = 1.0 / acc[DA_VD:DA_VD + 1, :]
        o_t = acc[0:DA_VD, 0:T] * r[:, 0:T] - acc[0:DA_VD, T:2 * T] * (r[:, T:2 * T] * lam)
        o = jnp.concatenate([o_t[:, c0:c0 + 128].T for c0 in range(0, T, 128)], axis=0)
        ms = jnp.mean(o * o, axis=-1, keepdims=True)
        o = o * lax.rsqrt(ms + EPS) * hw_ref[...] * (1.0 - lam_init)
        o_ref[:, cs] = (o * _silu(gb_ref[:, cs].astype(F32))).astype(o_ref.dtype)


def _attention_ctx(main, nseq, T, lam_vecs, head_w, lam_init):
    return pl.pallas_call(
        functools.partial(_attn_ctx_kernel, lam_init=lam_init),
        out_shape=jax.ShapeDtypeStruct((nseq * T, D_MODEL), BF16),
        grid=(nseq,),
        in_specs=[
            pl.BlockSpec((T, D_MODEL), lambda b: (b, SEG_Q)),
            pl.BlockSpec((T, D_MODEL), lambda b: (b, SEG_K)),
            pl.BlockSpec((T, D_MODEL), lambda b: (b, SEG_V)),
            pl.BlockSpec((T, D_MODEL), lambda b: (b, SEG_GB)),
            pl.BlockSpec((4, DA_QK), lambda b: (0, 0)),
            pl.BlockSpec((1, DA_VD), lambda b: (0, 0)),
        ],
        out_specs=pl.BlockSpec((T, D_MODEL), lambda b: (b, 0)),
        compiler_params=_cparams(("parallel",)),
        name="diff_attn_ctx",
    )(main, main, main, main, lam_vecs, head_w)


def _attention_latent(hm, nseq, T, tq, kb, lam_vecs, head_w, lam_init, ck, cv, l):
    R = nseq * T
    nq = T // tq
    lc = ck.shape[2]
    return pl.pallas_call(
        functools.partial(_attn_lat_kernel, lam_init=lam_init, kb=kb),
        out_shape=jax.ShapeDtypeStruct((R, D_MODEL), BF16),
        grid=(nseq, DA_HEADS, nq),
        in_specs=[
            pl.BlockSpec((None, None, None, tq, 128), lambda b, g, qi: (0, b, g, qi, 0)),
            pl.BlockSpec((None, None, None, T, 128), lambda b, g, qi: (1, b, g, 0, 0)),
            pl.BlockSpec((None, None, None, T, 128), lambda b, g, qi: (2, b, g, 0, 0)),
            pl.BlockSpec((None, None, None, tq, 128), lambda b, g, qi: (3, b, g, qi, 0)),
            pl.BlockSpec((None, None, lc, 128), lambda b, g, qi: (b, l, 0, g)),
            pl.BlockSpec((None, None, lc, 128), lambda b, g, qi: (b, l, 0, g)),
            pl.BlockSpec((4, DA_QK), lambda b, g, qi: (0, 0)),
            pl.BlockSpec((1, DA_VD), lambda b, g, qi: (0, 0)),
        ],
        out_specs=pl.BlockSpec((tq, 128), lambda b, g, qi: (b * nq + qi, g)),
        scratch_shapes=[pltpu.VMEM((DA_VD + 16, T), BF16), pltpu.VMEM((DA_VD + 16, lc), BF16)],
        compiler_params=_cparams(("parallel", "parallel", "arbitrary")),
        name="diff_attn_lat",
    )(hm, hm, hm, hm, ck, cv, lam_vecs, head_w)


def _sgmlp_merge_kernel(ya_ref, yb_ref, u_ref, sv_ref, gc_ref, ga_ref, gb_ref, gm_ref, vw_ref, ws_ref, bias_ref,
                        wb_ref, wo_ref, pw_ref, mod_ref, x_ref, o_ref):
    tm = u_ref.shape[0]

    def gated(y, g_ref, i):
        p = jnp.dot(y, wb_ref[i], preferred_element_type=F32)
        return _sigmoid(g_ref[...].astype(F32)) * p

    u = _gelu_tanh(u_ref[...].astype(F32))
    v = _gelu_tanh(sv_ref[...].astype(F32))
    vc = v - jnp.mean(v, axis=-1, keepdims=True)
    vb = (vc * lax.rsqrt(jnp.mean(vc * vc, axis=-1, keepdims=True) + EPS) * vw_ref[...]).astype(BF16)
    ug = u * _silu(gc_ref[...].astype(F32))
    rows = []
    for ci in range(tm // SG_CHUNK):
        rs = slice(ci * SG_CHUNK, (ci + 1) * SG_CHUNK)
        cols = [jnp.dot(ws_ref[g], vb[rs, g * 128:(g + 1) * 128], preferred_element_type=F32)
                for g in range(SG_GROUPS)]
        rows.append(((jnp.concatenate(cols, axis=1) + bias_ref[...]) * ug[rs]).astype(BF16))
    yc = jnp.concatenate(rows, axis=0)

    merged = gated(ya_ref[...], ga_ref, 0) + gated(yb_ref[...], gb_ref, 1) + gated(yc, gm_ref, 2)
    o = jnp.dot(merged.astype(BF16), wo_ref[...], preferred_element_type=F32)
    ms = jnp.mean(o * o, axis=-1, keepdims=True)
    o = o * lax.rsqrt(ms + EPS) * pw_ref[...]
    o_ref[...] = x_ref[...] + mod_ref[2:3, :] * o


def _sgmlp_merge(ya, yb, main, seg_shift, vnorm_w, ws_bf, bias_exp, wb_bf, wo_bf, l, post_w, mod, rows_per_mod, x,
                 tm=512):
    R = x.shape[0]
    row = lambda i: (i, 0)
    seg = lambda s: pl.BlockSpec((tm, D_MODEL), lambda i: (i, s - seg_shift))
    const = dict(pipeline_mode=pl.Buffered(1))
    return pl.pallas_call(
        _sgmlp_merge_kernel,
        out_shape=jax.ShapeDtypeStruct((R, D_MODEL), F32),
        grid=(R // tm,),
        in_specs=[
            pl.BlockSpec((tm, D_MODEL), row),
            pl.BlockSpec((tm, D_MODEL), row),
            seg(SEG_U), seg(SEG_SV), seg(SEG_GC), seg(SEG_MGA), seg(SEG_MGB), seg(SEG_MGC),
            pl.BlockSpec((1, D_MODEL), lambda i: (0, 0)),
            pl.BlockSpec((None, SG_GROUPS, SG_CHUNK, SG_CHUNK), lambda i: (l, 0, 0, 0), **const),
            pl.BlockSpec((None, SG_CHUNK, D_MODEL), lambda i: (l, 0, 0), **const),
            pl.BlockSpec((None, 3, D_MODEL, D_MODEL), lambda i: (l, 0, 0, 0), **const),
            pl.BlockSpec((None, D_MODEL, D_MODEL), lambda i: (l, 0, 0), **const),
            pl.BlockSpec((1, D_MODEL), lambda i: (0, 0)),
            pl.BlockSpec((None, 3, D_MODEL), lambda i: ((i * tm) // rows_per_mod, 0, 0)),
            pl.BlockSpec((tm, D_MODEL), row),
        ],
        out_specs=pl.BlockSpec((tm, D_MODEL), row),
        compiler_params=_cparams(("parallel",)),
        name="sgmlp_merge",
    )(ya, yb, main, main, main, main, main, main, vnorm_w, ws_bf, bias_exp, wb_bf, wo_bf, post_w, mod, x)


def _rope_tables(n_tokens):
    n_rows = n_tokens // GRID_W
    rows = jnp.repeat(jnp.arange(n_rows, dtype=F32), GRID_W)
    cols = jnp.tile(jnp.arange(GRID_W, dtype=F32), n_rows)
    n_freq = DA_QK // 4
    inv = ROPE_BASE ** (-jnp.arange(n_freq, dtype=F32) / n_freq)
    ang = jnp.concatenate([rows[:, None] * inv, cols[:, None] * inv], -1)
    cos, sin = jnp.cos(ang), jnp.sin(ang)
    return jnp.tile(cos, (1, 4)), jnp.concatenate([-sin, -sin, sin, sin], axis=1)


def kernel(x_prompt, x_sample, cache_k, cache_v, state_ssm, c, c_ctx, pre_norm_w, post_norm_w, w_mod, b_mod, w_in,
           m_conv_w, m_conv_b, m_A_log, m_dt_bias, m_D, m_norm_w, da_lambda, da_head_norm_w, sg_vnorm_w,
           sg_spatial_w, sg_spatial_b, w_branch, w_out):
    depth = w_in.shape[0]
    nb, seq, _ = x_prompt.shape
    db, dseq, _ = x_sample.shape
    past = cache_k.shape[2]

    w_main, w_misc = _prep_in_weights(jnp.swapaxes(w_in, 1, 2))
    wb_bf = w_branch.astype(BF16)
    wo_bf = w_out.astype(BF16)
    ws_bf = sg_spatial_w.astype(BF16)
    bias_exp = jnp.repeat(jnp.swapaxes(sg_spatial_b, 1, 2), D_MODEL // SG_GROUPS, axis=2)
    cw = jnp.swapaxes(m_conv_w, 1, 2)
    dtb = jnp.pad(m_dt_bias.reshape(depth, 1, 2 * M_HEADS), ((0, 0), (0, 0), (0, 128 - 2 * M_HEADS)))
    alog = jnp.pad(m_A_log.reshape(depth, 1, 2 * M_HEADS), ((0, 0), (0, 0), (0, 128 - 2 * M_HEADS)))
    dsum = jnp.repeat(m_D[:, 0] + m_D[:, 1], M_HEADDIM, axis=1).reshape(depth, 1, D_MODEL)

    cvec = jnp.concatenate([c_ctx[None, :], c, jnp.zeros((8 - 1 - db, D_MODEL), F32)], axis=0)
    mods = _modulation(cvec, w_mod, b_mod).reshape(depth, 8, 3, D_MODEL)

    rope_tabs = _rope_tables(dseq)
    init_states = state_ssm.reshape(db, depth, 2, M_HEADS * M_HEADDIM, M_STATE)
    ck = cache_k.reshape(db, depth, past, DA_HEADS, 2, 2, ROPE_GROUP)
    ck = jnp.swapaxes(ck, 4, 5).reshape(db, depth, past, D_MODEL)
    cv = cache_v.reshape(db, depth, past, D_MODEL)

    xp = x_prompt.reshape(nb * seq, D_MODEL)
    xs = x_sample.reshape(db * dseq, D_MODEL)

    def layer(x, l, nseq, T, mod, rows_per_mod, latent, stacks=(None, None, None)):
        lam_init = 0.8 - 0.6 * math.exp(-0.3 * l)
        k_stack, v_stack, st_stack = stacks
        outs = _inproj(x, mod, pre_norm_w[l][None], w_main, w_misc, l, rows_per_mod,
                       rope_tabs if latent else None, None if latent else (T, depth, k_stack, v_stack))
        main, misc = outs[0], outs[1]
        ya, st = _ssd(main, misc, nseq, T, cw[l, :, 0:D_MODEL], m_conv_b[l][None, 0:D_MODEL],
                      cw[l, :, D_MODEL:], m_conv_b[l][None, D_MODEL:], dtb[l], alog[l], dsum[l],
                      m_norm_w[l][None], (init_states, l) if latent else None,
                      st_stack=None if latent else st_stack, layer=0 if latent else l,
                      n_layers=1 if latent else depth, cps=min(SSD_CHUNKS_PER_STEP, T // M_CHUNK))
        if latent:
            yb = _attention_latent(outs[2], nseq, T, ATT_TQ, ATT_KB, da_lambda[l], da_head_norm_w[l][None],
                                   lam_init, ck, cv, l)
        else:
            yb = _attention_ctx(main, nseq, T, da_lambda[l], da_head_norm_w[l][None], lam_init)
        shift = LATENT_SEG_SHIFT if latent else 0
        x_new = _sgmlp_merge(ya, yb, main, shift, sg_vnorm_w[l][None], ws_bf, bias_exp, wb_bf, wo_bf, l,
                             post_norm_w[l][None], mod, rows_per_mod, x)
        return x_new, (None if latent else (outs[2], outs[3], st))

    stacks = (jnp.zeros((nb, depth, seq, D_MODEL), F32), jnp.zeros((nb, depth, seq, D_MODEL), F32),
              jnp.zeros((nb, depth, 2, M_HEADS * M_HEADDIM, M_STATE), F32))
    for l in range(depth):
        xp, stacks = layer(xp, l, nb, seq, mods[l, 0:1], nb * seq, False, stacks)
        xs, _ = layer(xs, l, db, dseq, mods[l, 1:1 + db], dseq, True)

    k_all, v_all, st_all = stacks
    return (xp.reshape(nb, seq, D_MODEL), xs.reshape(db, dseq, D_MODEL),
            k_all.reshape(nb, depth, seq, DA_HEADS, 2 * DA_QK), v_all.reshape(nb, depth, seq, DA_HEADS, DA_VD),
            st_all.reshape(nb, depth, 2, M_HEADS, M_HEADDIM, M_STATE))
```

```python
import functools
import math

import jax
import jax.numpy as jnp
from jax import lax
from jax.experimental import pallas as pl
from jax.experimental.pallas import tpu as pltpu

F32 = jnp.float32
BF16 = jnp.bfloat16

D_MODEL = 1024
EPS = 1e-6
GRID_W = 64
ROPE_BASE = 10000.0
M_HEADS = 16
M_HEADDIM = 64
M_STATE = 64
M_GROUPS = 2
M_CHUNK = 128
M_PAIRS = M_HEADS // 2
DA_HEADS = 8
DA_QK = 64
DA_VD = 128
SG_GROUPS = 8
SG_CHUNK = 128

SEG_Z, SEG_XS, SEG_Q, SEG_K, SEG_V, SEG_GB, SEG_U, SEG_SV, SEG_GC, SEG_MGA, SEG_MGB, SEG_MGC = range(12)
N_SEG = 12
N_WBLK = N_SEG + 2
ROPE_GROUP = 32
LATENT_SEG_SHIFT = 4
MISC_W = 384

VMEM_LIMIT = 56 * 1024 * 1024
NEG_BIG = -1e30
ATT_SUBQ = 128
ATT_TQ = 2048
ATT_KB = 512
SSD_CHUNKS_PER_STEP = 8
INPROJ_TM = 1024
INPROJ_TN = 2048
INPROJ_SUBN = 1024


_NEG_LOG2E = -math.log2(math.e)


def _silu(x):
    return x * _sigmoid(x)


def _sigmoid(x):
    return 1.0 / (1.0 + jnp.exp2(x * _NEG_LOG2E))


def _gelu_tanh(x):
    c = math.sqrt(2.0 / math.pi)
    hx = 0.5 * x
    return hx + hx * jnp.tanh(x * (c + (c * 0.044715) * (x * x)))


def _cparams(sem):
    return pltpu.CompilerParams(dimension_semantics=sem, vmem_limit_bytes=VMEM_LIMIT)


def _mod_kernel(c_ref, w_ref, b_ref, o_ref):
    c = c_ref[...]
    s = _silu(c).astype(BF16)
    o_ref[...] = jnp.dot(s, w_ref[...].astype(BF16), preferred_element_type=F32) + b_ref[...]


def _modulation(cvec, w_mod, b_mod):
    depth = w_mod.shape[0]
    nt = 3
    return pl.pallas_call(
        _mod_kernel,
        out_shape=jax.ShapeDtypeStruct((depth, 8, 3 * D_MODEL), F32),
        grid=(depth, nt),
        in_specs=[
            pl.BlockSpec((8, D_MODEL), lambda l, j: (0, 0)),
            pl.BlockSpec((None, D_MODEL, D_MODEL), lambda l, j: (l, 0, j)),
            pl.BlockSpec((None, 1, D_MODEL), lambda l, j: (l, 0, j)),
        ],
        out_specs=pl.BlockSpec((None, 8, D_MODEL), lambda l, j: (l, 0, j)),
        compiler_params=_cparams(("arbitrary", "arbitrary")),
        name="modulation",
    )(cvec, w_mod, b_mod.reshape(depth, 1, 3 * D_MODEL))


W_IN_COLS = 12576
W_XS_END = 2048
W_MISC_END = 2336


def _wprep_kernel(wa_ref, wb_ref, wm_ref, main_ref, misc_ref):
    j = pl.program_id(1)

    @pl.when(j < W_XS_END // D_MODEL)
    def _():
        main_ref[...] = wa_ref[...].T.astype(BF16)

    @pl.when(jnp.logical_and(j >= W_XS_END // D_MODEL, j < N_SEG))
    def _():
        main_ref[...] = wb_ref[...].T.astype(BF16)

    @pl.when(j >= N_SEG)
    def _():
        w = wb_ref[...]
        g = ROPE_GROUP
        rows = []
        for h0 in range(0, D_MODEL, 4 * g):
            rows += [w[h0:h0 + g], w[h0 + 2 * g:h0 + 3 * g], w[h0 + g:h0 + 2 * g], w[h0 + 3 * g:h0 + 4 * g]]
        main_ref[...] = jnp.concatenate(rows, axis=0).T.astype(BF16)

    @pl.when(j == 0)
    def _():
        n = W_MISC_END - W_XS_END
        wm = jnp.concatenate([wm_ref[...], jnp.zeros((MISC_W - n, D_MODEL), F32)], axis=0)
        misc_ref[...] = wm.T.astype(BF16)


def _prep_in_weights(w_in_t):
    depth = w_in_t.shape[0]
    n_head = W_XS_END // D_MODEL
    w2d = w_in_t.reshape(depth * W_IN_COLS, D_MODEL)
    skew = W_MISC_END - W_XS_END
    src = lambda j: jnp.where(j < N_SEG, j, j - N_SEG + SEG_Q)
    return pl.pallas_call(
        _wprep_kernel,
        out_shape=[jax.ShapeDtypeStruct((depth, D_MODEL, N_WBLK * D_MODEL), BF16),
                   jax.ShapeDtypeStruct((depth, D_MODEL, MISC_W), BF16)],
        grid=(depth, N_WBLK),
        in_specs=[pl.BlockSpec((None, D_MODEL, D_MODEL), lambda l, j: (l, jnp.minimum(j, n_head - 1), 0)),
                  pl.BlockSpec((pl.Element(D_MODEL), pl.Element(D_MODEL)),
                               lambda l, j: (pl.multiple_of(l * W_IN_COLS + skew + src(j) * D_MODEL, 32), 0)),
                  pl.BlockSpec((pl.Element(skew), pl.Element(D_MODEL)),
                               lambda l, j: (pl.multiple_of(l * W_IN_COLS + W_XS_END, 32), 0))],
        out_specs=[pl.BlockSpec((None, D_MODEL, D_MODEL), lambda l, j: (l, 0, j)),
                   pl.BlockSpec((None, D_MODEL, MISC_W), lambda l, j: (l, 0, 0))],
        compiler_params=_cparams(("parallel", "arbitrary")),
        name="w_prep",
    )(w_in_t, w2d, w2d)


def _inproj_kernel(*refs, rope, emit_kv, has_prev):
    it = iter(refs)
    x_ref, mod_ref, prew_ref, w_ref, wm_ref = next(it), next(it), next(it), next(it), next(it)
    if rope:
        cos_ref, sin_ref = next(it), next(it)
    if has_prev:
        next(it), next(it)
    main_ref, misc_ref = next(it), next(it)
    if emit_kv:
        kf_ref, vf_ref = next(it), next(it)
    if rope:
        hm_ref = next(it)
    h_scr = next(it)

    j = pl.program_id(1)
    tm = x_ref.shape[0]
    tn = w_ref.shape[1]
    spt = tn // D_MODEL

    @pl.when(j == 0)
    def _():
        x = x_ref[...]
        ms = jnp.mean(x * x, axis=-1, keepdims=True)
        y = x * lax.rsqrt(ms + EPS) * prew_ref[...]
        h = y * (1.0 + mod_ref[1:2, :]) + mod_ref[0:1, :]
        hb = h.astype(BF16)
        h_scr[...] = hb
        misc_ref[...] = jnp.dot(hb, wm_ref[...], preferred_element_type=F32)

    h = h_scr[...]
    sub = INPROJ_SUBN
    for c0 in range(0, tn, sub):
        cs = slice(c0, c0 + sub)
        seg = j * spt + c0 // D_MODEL
        lc = c0 % D_MODEL
        ls = slice(lc, lc + sub)
        acc = jnp.dot(h, w_ref[:, cs], preferred_element_type=F32)

        if rope:
            is_rope = jnp.logical_or(seg == SEG_Q, seg == SEG_K)
            is_plain_hm = jnp.logical_or(seg == SEG_V, seg == SEG_GB)

            def store_heads(val):
                vb = val.astype(hm_ref.dtype)
                for hh in range(sub // 128):
                    hm_ref[c0 // D_MODEL, lc // 128 + hh] = vb[:, hh * 128:(hh + 1) * 128]

            @pl.when(is_rope)
            def _():
                cos = jnp.concatenate([cos_ref[...]] * (sub // 128), axis=1)
                sin = jnp.concatenate([sin_ref[...]] * (sub // 128), axis=1)
                rot = jnp.concatenate([pltpu.roll(acc[:, c:c + 128], 2 * ROPE_GROUP, 1)
                                       for c in range(0, sub, 128)], axis=1)
                store_heads(acc * cos + rot * sin)

            @pl.when(is_plain_hm)
            def _():
                store_heads(acc)

            @pl.when(jnp.logical_not(jnp.logical_or(is_rope, is_plain_hm)))
            def _():
                main_ref[:, cs] = acc.astype(main_ref.dtype)
        else:
            main_ref[:, cs] = acc.astype(main_ref.dtype)

        if emit_kv:
            nsq, sq_len = kf_ref.shape[0], kf_ref.shape[1]

            @pl.when(seg == SEG_K)
            def _():
                kf_ref[:, :, ls] = acc.reshape(nsq, sq_len, sub)

            @pl.when(seg == SEG_V)
            def _():
                vf_ref[:, :, ls] = acc.reshape(nsq, sq_len, sub)


def _inproj(x, mod, pre_w, w_main, w_misc, l, rows_per_mod, rope_tabs, kv_cache=None, tm=INPROJ_TM, tn=INPROJ_TN):
    R = x.shape[0]
    emit_kv = kv_cache is not None
    ni = R // tm
    spt = tn // D_MODEL
    rope = rope_tabs is not None
    const = dict(pipeline_mode=pl.Buffered(1))
    in_specs = [
        pl.BlockSpec((tm, D_MODEL), lambda i, j: (i, 0), **const),
        pl.BlockSpec((None, 3, D_MODEL), lambda i, j: ((i * tm) // rows_per_mod, 0, 0)),
        pl.BlockSpec((1, D_MODEL), lambda i, j: (0, 0)),
        pl.BlockSpec((None, D_MODEL, tn),
                     (lambda i, j: (l, 0, jnp.where(j == SEG_Q // spt, N_SEG // spt, j))) if rope
                     else (lambda i, j: (l, 0, j))),
        pl.BlockSpec((None, D_MODEL, MISC_W), lambda i, j: (l, 0, 0), **const),
    ]
    args = [x, mod, pre_w, w_main, w_misc]
    if rope:
        cos, sin = rope_tabs
        nt = cos.shape[0] // tm
        in_specs += [pl.BlockSpec((tm, 128), lambda i, j: (i % nt, 0)),
                     pl.BlockSpec((tm, 128), lambda i, j: (i % nt, 0))]
        args += [cos, sin]
    if rope:
        n_main = N_SEG - LATENT_SEG_SHIFT
        main_idx = lambda i, j: (i, jnp.where(j < SEG_Q // spt, j,
                                              jnp.where(j <= SEG_GB // spt, SEG_XS // spt,
                                                        j - LATENT_SEG_SHIFT // spt)))
    else:
        n_main = N_SEG
        main_idx = lambda i, j: (i, j)
    out_shape = [jax.ShapeDtypeStruct((R, n_main * D_MODEL), BF16), jax.ShapeDtypeStruct((R, MISC_W), F32)]
    out_specs = [pl.BlockSpec((tm, tn), main_idx), pl.BlockSpec((tm, MISC_W), lambda i, j: (i, 0))]
    aliases = {}
    has_prev = False
    if emit_kv:
        seq_len, n_layers, k_prev, v_prev = kv_cache
        out_shape += [jax.ShapeDtypeStruct((R // seq_len, n_layers, seq_len, D_MODEL), F32)] * 2
        out_specs += [pl.BlockSpec((tm // seq_len, None, seq_len, D_MODEL), lambda i, j: (i, l, 0, 0))] * 2
        if k_prev is not None:
            has_prev = True
            in_specs += [pl.BlockSpec(memory_space=pl.ANY)] * 2
            args += [k_prev, v_prev]
            aliases = {len(args) - 2: 2, len(args) - 1: 3}
    if rope:
        T = cos.shape[0]
        out_shape.append(jax.ShapeDtypeStruct((4, R // T, DA_HEADS, T, 128), BF16))
        out_specs.append(pl.BlockSpec(
            (spt, None, DA_HEADS, tm, 128),
            lambda i, j: (jnp.clip(j - SEG_Q // spt, 0, 4 // spt - 1), (i * tm) // T, 0, ((i * tm) % T) // tm, 0)))
    return pl.pallas_call(
        functools.partial(_inproj_kernel, rope=rope, emit_kv=emit_kv, has_prev=has_prev),
        out_shape=out_shape,
        grid=(ni, N_SEG // spt),
        in_specs=in_specs,
        out_specs=out_specs,
        input_output_aliases=aliases,
        scratch_shapes=[pltpu.VMEM((tm, D_MODEL), BF16)],
        compiler_params=_cparams(("parallel", "arbitrary")),
        name="inproj",
    )(*args)


def _split3(a):
    hi = a.astype(BF16)
    r1 = a - hi.astype(F32)
    mid = r1.astype(BF16)
    lo = (r1 - mid.astype(F32)).astype(BF16)
    return hi, mid, lo


def _ssd_kernel(*refs, nc, cps, has_init, has_prev, st_slab):
    it = iter(refs)
    xs_ref, xsp_ref, xsn_ref = next(it), next(it), next(it)
    mi_ref, mip_ref, min_ref = next(it), next(it), next(it)
    z_ref = next(it)
    cwx_ref, cbx_ref, cwm_ref, cbm_ref = next(it), next(it), next(it), next(it)
    dtb_ref, alog_ref, dsum_ref, nw_ref = next(it), next(it), next(it), next(it)
    if has_init:
        init_ref = next(it)
    if has_prev:
        next(it)
    y_ref, sto_ref = next(it), next(it)
    xc_scr, bc_scr, yf_scr, st_scr = next(it), next(it), next(it), next(it)

    L = M_CHUNK
    LB = cps * L
    ns = nc // cps
    j = pl.program_id(1)
    fwd = j < ns
    c = jnp.where(fwd, j, 2 * ns - 1 - j)
    row0 = pl.multiple_of(c * LB, LB)
    LOG2E = math.log2(math.e)

    rid = lax.broadcasted_iota(jnp.int32, (L, L), 0)
    cid = lax.broadcasted_iota(jnp.int32, (L, L), 1)

    def conv_silu(x, prow, nrow, w_ref, b_ref):
        n = x.shape[1]
        r = lax.broadcasted_iota(jnp.int32, (LB, n), 0)
        xm = jnp.where(r == 0, prow, pltpu.roll(x, 1, 0))
        xp = jnp.where(r == LB - 1, nrow, pltpu.roll(x, LB - 1, 0))
        y = w_ref[0:1, :] * xm + w_ref[1:2, :] * x + w_ref[2:3, :] * xp + b_ref[...]
        return _silu(y)

    def load_state(d):
        if has_init:
            pad = jnp.zeros((2 * M_HEADDIM, 2 * M_HEADDIM - M_STATE), F32)
            for i in range(M_PAIRS):
                blk = init_ref[d, i * 2 * M_HEADDIM:(i + 1) * 2 * M_HEADDIM, :]
                st_scr[i] = jnp.concatenate([blk, pad], axis=1).T[0:M_STATE, :]
        else:
            st_scr[...] = jnp.zeros_like(st_scr)

    def store_state(d):
        pad = jnp.zeros((2 * M_HEADDIM - M_STATE, 2 * M_HEADDIM), F32)
        for i in range(M_PAIRS):
            t = jnp.concatenate([st_scr[i], pad], axis=0).T
            sto_ref[st_slab, d, i * 2 * M_HEADDIM:(i + 1) * 2 * M_HEADDIM, :] = t[:, 0:M_STATE]

    lane_lo = cid < M_HEADDIM
    lane_lo_s = lax.broadcasted_iota(jnp.int32, (M_STATE, 2 * M_HEADDIM), 1) < M_HEADDIM
    heads_per_group = M_HEADS // M_GROUPS
    zeros_s = jnp.zeros((M_STATE, 2 * M_HEADDIM), BF16)

    def setup(d, bc, dt_raw):
        tri = (cid <= rid) if d == 0 else (cid >= rid)
        tri_bf = jnp.where(tri, 1.0, 0.0).astype(BF16)
        dt = dt_raw + dtb_ref[...]
        dt = jnp.maximum(dt, 0.0) + jnp.log1p(jnp.exp(-jnp.abs(dt)))
        a = dt * (-jnp.exp(alog_ref[...]))
        a_hi, a_mid, a_lo = _split3(a)
        p_col = (jnp.dot(tri_bf, a_hi, preferred_element_type=F32)
                 + jnp.dot(tri_bf, a_mid, preferred_element_type=F32)
                 + jnp.dot(tri_bf, a_lo, preferred_element_type=F32))
        p_row = p_col.T[d * M_HEADS:(d + 1) * M_HEADS, :]
        dt_row = dt.T[d * M_HEADS:(d + 1) * M_HEADS, :]
        tot = p_row[:, L - 1:L] if d == 0 else p_row[:, 0:1]
        b_all = bc[:, 0:M_GROUPS * M_STATE]
        c_all = bc[:, M_GROUPS * M_STATE:2 * M_GROUPS * M_STATE]
        b_bf = b_all.astype(BF16)
        g_mats = []
        for g in range(M_GROUPS):
            cg = jnp.where(lane_lo if g == 0 else jnp.logical_not(lane_lo), c_all, 0.0).astype(BF16)
            g_mats.append(lax.dot_general(cg, b_bf, (((1,), (1,)), ((), ())),
                                          preferred_element_type=F32).astype(BF16))
        return dict(
            tri=tri, g_mats=g_mats, c_bf=c_all.astype(BF16), bt_all=b_all.T,
            p2_col=p_col * LOG2E,
            q2_row=(p_row - jnp.log(dt_row)) * LOG2E,
            w_row=dt_row * jnp.exp(tot - p_row),
            etot=jnp.exp(tot),
            ep_col=jnp.exp(p_col))

    def pairs(d, s, x_bf):
        outs = []
        for i in range(M_PAIRS):
            g = (2 * i) // heads_per_group
            btg = s["bt_all"][g * M_STATE:(g + 1) * M_STATE, :]
            lhs_rows = []
            bw_rows = []
            for hh in range(2):
                h = 2 * i + hh
                col = d * M_HEADS + h
                pc = jnp.broadcast_to(s["p2_col"][:, col:col + 1], (L, L))
                dm = jnp.exp2(jnp.where(s["tri"], pc - s["q2_row"][h:h + 1, :], NEG_BIG))
                m_h = s["g_mats"][g] * dm.astype(BF16)
                ce_h = s["c_bf"] * jnp.broadcast_to(s["ep_col"][:, col:col + 1], (L, L)).astype(BF16)
                lhs_rows.append(jnp.concatenate([m_h, ce_h], axis=1))
                bw = (btg * s["w_row"][h:h + 1, :]).astype(BF16)
                bw_rows.append(jnp.concatenate([bw, zeros_s], axis=1))
            lhs = jnp.concatenate(lhs_rows + bw_rows, axis=0)
            x_pair = x_bf[:, i * 128:(i + 1) * 128]
            st_pair = st_scr[i]
            st_bf = st_pair.astype(BF16)
            rhs = jnp.concatenate([x_pair] + ([st_bf, zeros_s] if g == 0 else [zeros_s, st_bf]), axis=0)
            res = jnp.dot(lhs, rhs, preferred_element_type=F32)
            y_pair = jnp.where(lane_lo, res[0:L], res[L:2 * L])
            ds = jnp.where(lane_lo_s, res[2 * L:2 * L + M_STATE], res[2 * L + M_STATE:2 * L + 2 * M_STATE])
            e0 = jnp.broadcast_to(s["etot"][2 * i:2 * i + 1, :], (M_STATE, 2 * M_HEADDIM))
            e1 = jnp.broadcast_to(s["etot"][2 * i + 1:2 * i + 2, :], (M_STATE, 2 * M_HEADDIM))
            st_scr[i] = jnp.where(lane_lo_s, e0, e1) * st_pair + ds
            outs.append(y_pair)
        return jnp.concatenate(outs, axis=1)

    def block(d, x_bf, bc, dt_raw):
        order = range(cps) if d == 0 else range(cps - 1, -1, -1)
        su = {ci: setup(d, bc[ci * L:(ci + 1) * L], dt_raw[ci * L:(ci + 1) * L]) for ci in order}
        ys = {ci: pairs(d, su[ci], x_bf[ci * L:(ci + 1) * L]) for ci in order}
        return jnp.concatenate([ys[ci] for ci in range(cps)], axis=0)

    @pl.when(j == 0)
    def _():
        load_state(0)
        for s_other in range(sto_ref.shape[0]):
            if s_other != st_slab:
                sto_ref[s_other] = jnp.zeros(sto_ref.shape[1:], F32)

    @pl.when(j == ns)
    def _():
        load_state(1)

    @pl.when(fwd)
    def _():
        x = xs_ref[...].astype(F32)
        prow = jnp.where(c > 0, xsp_ref[...].astype(F32)[15:16, :], 0.0)
        nrow = jnp.where(c < ns - 1, xsn_ref[...].astype(F32)[0:1, :], 0.0)
        xc = conv_silu(x, prow, nrow, cwx_ref, cbx_ref).astype(BF16)
        xc_scr[pl.ds(row0, LB), :] = xc
        m = mi_ref[...]
        bcx = m[:, 0:256]
        prow_m = jnp.where(c > 0, mip_ref[7:8, 0:256], 0.0)
        nrow_m = jnp.where(c < ns - 1, min_ref[0:1, 0:256], 0.0)
        bc = conv_silu(bcx, prow_m, nrow_m, cwm_ref, cbm_ref)
        bc_scr[pl.ds(row0, LB), :] = bc
        yf_scr[pl.ds(row0, LB), :] = block(0, xc, bc, m[:, 256:384])

    @pl.when(j == ns - 1)
    def _():
        store_state(0)

    @pl.when(jnp.logical_not(fwd))
    def _():
        xc = xc_scr[pl.ds(row0, LB), :]
        bc = bc_scr[pl.ds(row0, LB), :]
        yb = block(1, xc, bc, mi_ref[:, 256:384])
        y = yf_scr[pl.ds(row0, LB), :] + yb + dsum_ref[...] * xc.astype(F32)
        y = y * _silu(z_ref[...].astype(F32))
        ms = jnp.mean(y * y, axis=-1, keepdims=True)
        y_ref[...] = (y * lax.rsqrt(ms + EPS) * nw_ref[...]).astype(y_ref.dtype)

    @pl.when(j == 2 * ns - 1)
    def _():
        store_state(1)


def _ssd(main, misc, nseq, T, cwx, cbx, cwm, cbm, dtb, alog, dsum, nw, init, st_stack=None, layer=0, n_layers=1,
         cps=SSD_CHUNKS_PER_STEP):
    R = nseq * T
    nc = T // M_CHUNK
    L = cps * M_CHUNK
    ns = nc // cps
    nc, full_nc = ns, nc
    has_init = init is not None

    def cidx(j):
        return jnp.where(j < nc, j, 2 * nc - 1 - j)

    def oidx(j):
        return jnp.where(j < nc, nc - 1, 2 * nc - 1 - j)

    in_specs = [
        pl.BlockSpec((L, D_MODEL), lambda b, j: (b * nc + cidx(j), SEG_XS)),
        pl.BlockSpec((16, D_MODEL), lambda b, j: (jnp.maximum((b * nc + cidx(j)) * (L // 16) - 1, 0), SEG_XS)),
        pl.BlockSpec((16, D_MODEL),
                     lambda b, j: (jnp.minimum((b * nc + cidx(j) + 1) * (L // 16), R // 16 - 1), SEG_XS)),
        pl.BlockSpec((L, MISC_W), lambda b, j: (b * nc + cidx(j), 0)),
        pl.BlockSpec((8, MISC_W), lambda b, j: (jnp.maximum((b * nc + cidx(j)) * (L // 8) - 1, 0), 0)),
        pl.BlockSpec((8, MISC_W), lambda b, j: (jnp.minimum((b * nc + cidx(j) + 1) * (L // 8), R // 8 - 1), 0)),
        pl.BlockSpec((L, D_MODEL), lambda b, j: (b * nc + oidx(j), SEG_Z)),
        pl.BlockSpec((3, D_MODEL), lambda b, j: (0, 0)),
        pl.BlockSpec((1, D_MODEL), lambda b, j: (0, 0)),
        pl.BlockSpec((3, 256), lambda b, j: (0, 0)),
        pl.BlockSpec((1, 256), lambda b, j: (0, 0)),
        pl.BlockSpec((1, 128), lambda b, j: (0, 0)),
        pl.BlockSpec((1, 128), lambda b, j: (0, 0)),
        pl.BlockSpec((1, D_MODEL), lambda b, j: (0, 0)),
        pl.BlockSpec((1, D_MODEL), lambda b, j: (0, 0)),
    ]
    args = [main, main, main, misc, misc, misc, main, cwx, cbx, cwm, cbm, dtb, alog, dsum, nw]
    if has_init:
        init_arr, init_layer = init
        in_specs.append(pl.BlockSpec((None, None, 2, M_HEADS * M_HEADDIM, M_STATE),
                                     lambda b, j: (b, init_layer, 0, 0, 0)))
        args.append(init_arr)
    aliases = {}
    if st_stack is not None:
        in_specs.append(pl.BlockSpec(memory_space=pl.ANY))
        args.append(st_stack)
        aliases = {len(args) - 1: 1}
    creates = st_stack is None
    st_block = (None, n_layers if creates else 1, 2, M_HEADS * M_HEADDIM, M_STATE)
    st_index = (lambda b, j: (b, 0, 0, 0, 0)) if creates else (lambda b, j: (b, layer, 0, 0, 0))
    return pl.pallas_call(
        functools.partial(_ssd_kernel, nc=full_nc, cps=cps, has_init=has_init, has_prev=st_stack is not None,
                          st_slab=layer if creates else 0),
        out_shape=[jax.ShapeDtypeStruct((R, D_MODEL), BF16),
                   jax.ShapeDtypeStruct((nseq, n_layers, 2, M_HEADS * M_HEADDIM, M_STATE), F32)],
        grid=(nseq, 2 * nc),
        in_specs=in_specs,
        out_specs=[pl.BlockSpec((L, D_MODEL), lambda b, j: (b * nc + oidx(j), 0)),
                   pl.BlockSpec(st_block, st_index)],
        scratch_shapes=[pltpu.VMEM((T, D_MODEL), BF16), pltpu.VMEM((T, 256), F32),
                        pltpu.VMEM((T, D_MODEL), F32), pltpu.VMEM((M_PAIRS, M_STATE, 128), F32)],
        input_output_aliases=aliases,
        compiler_params=_cparams(("parallel", "arbitrary")),
        name="ssd",
    )(*args)


def _attn_lat_kernel(q_ref, k_ref, v_ref, gb_ref, kc_ref, vc_ref, lv_ref, hw_ref, o_ref, vt_scr, vct_scr,
                     *, lam_init, kb):
    tq = q_ref.shape[0]
    T = k_ref.shape[0]
    lc = kc_ref.shape[0]
    qi = pl.program_id(2)

    @pl.when(qi == 0)
    def _():
        for c0 in range(0, T, 128):
            vt_scr[0:DA_VD, c0:c0 + 128] = v_ref[c0:c0 + 128, :].astype(F32).T.astype(BF16)
        vt_scr[DA_VD:DA_VD + 16, :] = jnp.ones((16, T), BF16)
        for c0 in range(0, lc, 128):
            vct_scr[0:DA_VD, c0:c0 + 128] = vc_ref[c0:c0 + 128, :].T.astype(BF16)
        vct_scr[DA_VD:DA_VD + 16, :] = jnp.ones((16, lc), BF16)

    lv = lv_ref[...]
    lam = (jnp.exp(jnp.sum(lv[0:1, :] * lv[1:2, :], axis=-1, keepdims=True))
           - jnp.exp(jnp.sum(lv[2:3, :] * lv[3:4, :], axis=-1, keepdims=True)) + lam_init)
    sq = ATT_SUBQ
    map1 = (lax.broadcasted_iota(jnp.int32, (sq, 2 * DA_QK), 1) & ROPE_GROUP) == 0
    dn_t = (((1,), (1,)), ((), ()))

    q2s = []
    for c in range(tq // sq):
        q = q_ref[c * sq:(c + 1) * sq, :].astype(F32) * (DA_QK ** -0.5 * math.log2(math.e))
        q2s.append(jnp.concatenate([jnp.where(map1, q, 0.0), jnp.where(map1, 0.0, q)], axis=0).astype(BF16))

    def scores(k_blk, q2):
        s = lax.dot_general(k_blk, q2, dn_t, preferred_element_type=F32)
        return s, jnp.max(s, axis=0, keepdims=True)

    def accumulate(s, bm, vt_blk, state):
        m_new = bm if state is None else jnp.maximum(state[0], bm)
        p = jnp.exp2(s - m_new).astype(BF16)
        pv = jnp.dot(vt_blk, p, preferred_element_type=F32)
        if state is None:
            return m_new, pv
        return m_new, jnp.exp2(state[0] - m_new) * state[1] + pv

    kcb = kc_ref[...].astype(BF16)
    cur = [scores(kcb, q2) for q2 in q2s]
    cur_vt = vct_scr[...]
    states = [None] * len(q2s)
    for k0 in range(0, T, kb):
        k_blk = k_ref[k0:k0 + kb, :]
        nxt = [scores(k_blk, q2) for q2 in q2s]
        states = [accumulate(cu[0], cu[1], cur_vt, st) for cu, st in zip(cur, states)]
        cur, cur_vt = nxt, vt_scr[:, k0:k0 + kb]
    states = [accumulate(cu[0], cu[1], cur_vt, st) for cu, st in zip(cur, states)]
    for c, (_, acc) in enumerate(states):
        rs = slice(c * sq, (c + 1) * sq)
        r = 1.0 / acc[DA_VD:DA_VD + 1, :]
        o_t = acc[0:DA_VD, 0:sq] * r[:, 0:sq] - acc[0:DA_VD, sq:2 * sq] * (r[:, sq:2 * sq] * lam)
        o = o_t.T
        ms = jnp.mean(o * o, axis=-1, keepdims=True)
        o = o * lax.rsqrt(ms + EPS) * hw_ref[...] * (1.0 - lam_init)
        o_ref[rs, :] = (o * _silu(gb_ref[rs, :].astype(F32))).astype(o_ref.dtype)


def _attn_ctx_kernel(q_ref, k_ref, v_ref, gb_ref, lv_ref, hw_ref, o_ref, *, lam_init):
    T = q_ref.shape[0]
    lv = lv_ref[...]
    lam = (jnp.exp(jnp.sum(lv[0:1, :] * lv[1:2, :], axis=-1, keepdims=True))
           - jnp.exp(jnp.sum(lv[2:3, :] * lv[3:4, :], axis=-1, keepdims=True)) + lam_init)
    lane = lax.broadcasted_iota(jnp.int32, (T, 2 * DA_QK), 1)
    dn_t = (((1,), (1,)), ((), ()))
    ones = jnp.ones((16, T), BF16)

    def scores(hh):
        cs = slice(hh * 128, (hh + 1) * 128)
        q = q_ref[:, cs].astype(F32) * (DA_QK ** -0.5 * math.log2(math.e))
        q2 = jnp.concatenate([jnp.where(lane < DA_QK, q, 0.0), jnp.where(lane < DA_QK, 0.0, q)],
                             axis=0).astype(BF16)
        v = v_ref[:, cs].astype(F32)
        vt = jnp.concatenate([v[c0:c0 + 128, :].T for c0 in range(0, T, 128)], axis=1).astype(BF16)
        vt1 = jnp.concatenate([vt, ones], axis=0)
        s = lax.dot_general(k_ref[:, cs], q2, dn_t, preferred_element_type=F32)
        return s, vt1

    nxt = scores(0)
    for hh in range(DA_HEADS):
        cs = slice(hh * 128, (hh + 1) * 128)
        s, vt1 = nxt
        if hh + 1 < DA_HEADS:
            nxt = scores(hh + 1)
        p = jnp.exp2(s - jnp.max(s, axis=0, keepdims=True)).astype(BF16)
        acc = jnp.dot(vt1, p, preferred_element_type=F32)
        r = 1.0 / acc[DA_VD:DA_VD + 1, :]
        o_t = acc[0:DA_VD, 0:T] * r[:, 0:T] - acc[0:DA_VD, T:2 * T] * (r[:, T:2 * T] * lam)
        o = jnp.concatenate([o_t[:, c0:c0 + 128].T for c0 in range(0, T, 128)], axis=0)
        ms = jnp.mean(o * o, axis=-1, keepdims=True)
        o = o * lax.rsqrt(ms + EPS) * hw_ref[...] * (1.0 - lam_init)
        o_ref[:, cs] = (o * _silu(gb_ref[:, cs].astype(F32))).astype(o_ref.dtype)


def _attention_ctx(main, nseq, T, lam_vecs, head_w, lam_init):
    return pl.pallas_call(
        functools.partial(_attn_ctx_kernel, lam_init=lam_init),
        out_shape=jax.ShapeDtypeStruct((nseq * T, D_MODEL), BF16),
        grid=(nseq,),
        in_specs=[
            pl.BlockSpec((T, D_MODEL), lambda b: (b, SEG_Q)),
            pl.BlockSpec((T, D_MODEL), lambda b: (b, SEG_K)),
            pl.BlockSpec((T, D_MODEL), lambda b: (b, SEG_V)),
            pl.BlockSpec((T, D_MODEL), lambda b: (b, SEG_GB)),
            pl.BlockSpec((4, DA_QK), lambda b: (0, 0)),
            pl.BlockSpec((1, DA_VD), lambda b: (0, 0)),
        ],
        out_specs=pl.BlockSpec((T, D_MODEL), lambda b: (b, 0)),
        compiler_params=_cparams(("parallel",)),
        name="diff_attn_ctx",
    )(main, main, main, main, lam_vecs, head_w)


def _attention_latent(hm, nseq, T, tq, kb, lam_vecs, head_w, lam_init, ck, cv, l):
    R = nseq * T
    nq = T // tq
    lc = ck.shape[2]
    return pl.pallas_call(
        functools.partial(_attn_lat_kernel, lam_init=lam_init, kb=kb),
        out_shape=jax.ShapeDtypeStruct((R, D_MODEL), BF16),
        grid=(nseq, DA_HEADS, nq),
        in_specs=[
            pl.BlockSpec((None, None, None, tq, 128), lambda b, g, qi: (0, b, g, qi, 0)),
            pl.BlockSpec((None, None, None, T, 128), lambda b, g, qi: (1, b, g, 0, 0)),
            pl.BlockSpec((None, None, None, T, 128), lambda b, g, qi: (2, b, g, 0, 0)),
            pl.BlockSpec((None, None, None, tq, 128), lambda b, g, qi: (3, b, g, qi, 0)),
            pl.BlockSpec((None, None, lc, 128), lambda b, g, qi: (b, l, 0, g)),
            pl.BlockSpec((None, None, lc, 128), lambda b, g, qi: (b, l, 0, g)),
            pl.BlockSpec((4, DA_QK), lambda b, g, qi: (0, 0)),
            pl.BlockSpec((1, DA_VD), lambda b, g, qi: (0, 0)),
        ],
        out_specs=pl.BlockSpec((tq, 128), lambda b, g, qi: (b * nq + qi, g)),
        scratch_shapes=[pltpu.VMEM((DA_VD + 16, T), BF16), pltpu.VMEM((DA_VD + 16, lc), BF16)],
        compiler_params=_cparams(("parallel", "parallel", "arbitrary")),
        name="diff_attn_lat",
    )(hm, hm, hm, hm, ck, cv, lam_vecs, head_w)


def _sgmlp_merge_kernel(ya_ref, yb_ref, u_ref, sv_ref, gc_ref, ga_ref, gb_ref, gm_ref, vw_ref, ws_ref, bias_ref,
                        wb_ref, wo_ref, pw_ref, mod_ref, x_ref, o_ref):
    tm = u_ref.shape[0]

    def gated(y, g_ref, i):
        p = jnp.dot(y, wb_ref[i], preferred_element_type=F32)
        return _sigmoid(g_ref[...].astype(F32)) * p

    u = _gelu_tanh(u_ref[...].astype(F32))
    v = _gelu_tanh(sv_ref[...].astype(F32))
    vc = v - jnp.mean(v, axis=-1, keepdims=True)
    vb = (vc * lax.rsqrt(jnp.mean(vc * vc, axis=-1, keepdims=True) + EPS) * vw_ref[...]).astype(BF16)
    ug = u * _silu(gc_ref[...].astype(F32))
    rows = []
    for ci in range(tm // SG_CHUNK):
        rs = slice(ci * SG_CHUNK, (ci + 1) * SG_CHUNK)
        cols = [jnp.dot(ws_ref[g], vb[rs, g * 128:(g + 1) * 128], preferred_element_type=F32)
                for g in range(SG_GROUPS)]
        rows.append(((jnp.concatenate(cols, axis=1) + bias_ref[...]) * ug[rs]).astype(BF16))
    yc = jnp.concatenate(rows, axis=0)

    merged = gated(ya_ref[...], ga_ref, 0) + gated(yb_ref[...], gb_ref, 1) + gated(yc, gm_ref, 2)
    o = jnp.dot(merged.astype(BF16), wo_ref[...], preferred_element_type=F32)
    ms = jnp.mean(o * o, axis=-1, keepdims=True)
    o = o * lax.rsqrt(ms + EPS) * pw_ref[...]
    o_ref[...] = x_ref[...] + mod_ref[2:3, :] * o


def _sgmlp_merge(ya, yb, main, seg_shift, vnorm_w, ws_bf, bias_exp, wb_bf, wo_bf, l, post_w, mod, rows_per_mod, x,
                 tm=512):
    R = x.shape[0]
    row = lambda i: (i, 0)
    seg = lambda s: pl.BlockSpec((tm, D_MODEL), lambda i: (i, s - seg_shift))
    const = dict(pipeline_mode=pl.Buffered(1))
    return pl.pallas_call(
        _sgmlp_merge_kernel,
        out_shape=jax.ShapeDtypeStruct((R, D_MODEL), F32),
        grid=(R // tm,),
        in_specs=[
            pl.BlockSpec((tm, D_MODEL), row),
            pl.BlockSpec((tm, D_MODEL), row),
            seg(SEG_U), seg(SEG_SV), seg(SEG_GC), seg(SEG_MGA), seg(SEG_MGB), seg(SEG_MGC),
            pl.BlockSpec((1, D_MODEL), lambda i: (0, 0)),
            pl.BlockSpec((None, SG_GROUPS, SG_CHUNK, SG_CHUNK), lambda i: (l, 0, 0, 0), **const),
            pl.BlockSpec((None, SG_CHUNK, D_MODEL), lambda i: (l, 0, 0), **const),
            pl.BlockSpec((None, 3, D_MODEL, D_MODEL), lambda i: (l, 0, 0, 0), **const),
            pl.BlockSpec((None, D_MODEL, D_MODEL), lambda i: (l, 0, 0), **const),
            pl.BlockSpec((1, D_MODEL), lambda i: (0, 0)),
            pl.BlockSpec((None, 3, D_MODEL), lambda i: ((i * tm) // rows_per_mod, 0, 0)),
            pl.BlockSpec((tm, D_MODEL), row),
        ],
        out_specs=pl.BlockSpec((tm, D_MODEL), row),
        compiler_params=_cparams(("parallel",)),
        name="sgmlp_merge",
    )(ya, yb, main, main, main, main, main, main, vnorm_w, ws_bf, bias_exp, wb_bf, wo_bf, post_w, mod, x)


def _rope_tables(n_tokens):
    n_rows = n_tokens // GRID_W
    rows = jnp.repeat(jnp.arange(n_rows, dtype=F32), GRID_W)
    cols = jnp.tile(jnp.arange(GRID_W, dtype=F32), n_rows)
    n_freq = DA_QK // 4
    inv = ROPE_BASE ** (-jnp.arange(n_freq, dtype=F32) / n_freq)
    ang = jnp.concatenate([rows[:, None] * inv, cols[:, None] * inv], -1)
    cos, sin = jnp.cos(ang), jnp.sin(ang)
    return jnp.tile(cos, (1, 4)), jnp.concatenate([-sin, -sin, sin, sin], axis=1)


def kernel(x_prompt, x_sample, cache_k, cache_v, state_ssm, c, c_ctx, pre_norm_w, post_norm_w, w_mod, b_mod, w_in,
           m_conv_w, m_conv_b, m_A_log, m_dt_bias, m_D, m_norm_w, da_lambda, da_head_norm_w, sg_vnorm_w,
           sg_spatial_w, sg_spatial_b, w_branch, w_out):
    depth = w_in.shape[0]
    nb, seq, _ = x_prompt.shape
    db, dseq, _ = x_sample.shape
    past = cache_k.shape[2]

    w_main, w_misc = _prep_in_weights(jnp.swapaxes(w_in, 1, 2))
    wb_bf = w_branch.astype(BF16)
    wo_bf = w_out.astype(BF16)
    ws_bf = sg_spatial_w.astype(BF16)
    bias_exp = jnp.repeat(jnp.swapaxes(sg_spatial_b, 1, 2), D_MODEL // SG_GROUPS, axis=2)
    cw = jnp.swapaxes(m_conv_w, 1, 2)
    dtb = jnp.pad(m_dt_bias.reshape(depth, 1, 2 * M_HEADS), ((0, 0), (0, 0), (0, 128 - 2 * M_HEADS)))
    alog = jnp.pad(m_A_log.reshape(depth, 1, 2 * M_HEADS), ((0, 0), (0, 0), (0, 128 - 2 * M_HEADS)))
    dsum = jnp.repeat(m_D[:, 0] + m_D[:, 1], M_HEADDIM, axis=1).reshape(depth, 1, D_MODEL)

    cvec = jnp.concatenate([c_ctx[None, :], c, jnp.zeros((8 - 1 - db, D_MODEL), F32)], axis=0)
    mods = _modulation(cvec, w_mod, b_mod).reshape(depth, 8, 3, D_MODEL)

    rope_tabs = _rope_tables(dseq)
    init_states = state_ssm.reshape(db, depth, 2, M_HEADS * M_HEADDIM, M_STATE)
    ck = cache_k.reshape(db, depth, past, DA_HEADS, 2, 2, ROPE_GROUP)
    ck = jnp.swapaxes(ck, 4, 5).reshape(db, depth, past, D_MODEL)
    cv = cache_v.reshape(db, depth, past, D_MODEL)

    xp = x_prompt.reshape(nb * seq, D_MODEL)
    xs = x_sample.reshape(db * dseq, D_MODEL)

    def layer(x, l, nseq, T, mod, rows_per_mod, latent, stacks=(None, None, None)):
        lam_init = 0.8 - 0.6 * math.exp(-0.3 * l)
        k_stack, v_stack, st_stack = stacks
        outs = _inproj(x, mod, pre_norm_w[l][None], w_main, w_misc, l, rows_per_mod,
                       rope_tabs if latent else None, None if latent else (T, depth, k_stack, v_stack))
        main, misc = outs[0], outs[1]
        ya, st = _ssd(main, misc, nseq, T, cw[l, :, 0:D_MODEL], m_conv_b[l][None, 0:D_MODEL],
                      cw[l, :, D_MODEL:], m_conv_b[l][None, D_MODEL:], dtb[l], alog[l], dsum[l],
                      m_norm_w[l][None], (init_states, l) if latent else None,
                      st_stack=None if latent else st_stack, layer=0 if latent else l,
                      n_layers=1 if latent else depth, cps=min(SSD_CHUNKS_PER_STEP, T // M_CHUNK))
        if latent:
            yb = _attention_latent(outs[2], nseq, T, ATT_TQ, ATT_KB, da_lambda[l], da_head_norm_w[l][None],
                                   lam_init, ck, cv, l)
        else:
            yb = _attention_ctx(main, nseq, T, da_lambda[l], da_head_norm_w[l][None], lam_init)
        shift = LATENT_SEG_SHIFT if latent else 0
        x_new = _sgmlp_merge(ya, yb, main, shift, sg_vnorm_w[l][None], ws_bf, bias_exp, wb_bf, wo_bf, l,
                             post_norm_w[l][None], mod, rows_per_mod, x)
        return x_new, (None if latent else (outs[2], outs[3], st))

    stacks = (jnp.zeros((nb, depth, seq, D_MODEL), F32), jnp.zeros((nb, depth, seq, D_MODEL), F32), None)
    for l in range(depth):
        xp, stacks = layer(xp, l, nb, seq, mods[l, 0:1], nb * seq, False, stacks)
        xs, _ = layer(xs, l, db, dseq, mods[l, 1:1 + db], dseq, True)

    k_all, v_all, st_all = stacks
    return (xp.reshape(nb, seq, D_MODEL), xs.reshape(db, dseq, D_MODEL),
            k_all.reshape(nb, depth, seq, DA_HEADS, 2 * DA_QK), v_all.reshape(nb, depth, seq, DA_HEADS, DA_VD),
            st_all.reshape(nb, depth, 2, M_HEADS, M_HEADDIM, M_STATE))
```
